```python
import math
import jax, jax.numpy as jnp
from jax import lax
import numpy as np


D_MODEL = 1024
BATCH = 16
SEQ = 2048
DEPTH = 2

MIX_WIDTH = D_MODEL
MLSTM_WIDTH = MIX_WIDTH // 2
MLSTM_DV = 128
N_MLSTM_HEADS = MLSTM_WIDTH // MLSTM_DV
MLSTM_DK = MLSTM_DV // 2
MLSTM_CONV_W = 4
CHUNK = 64
DIFF_WIDTH = MIX_WIDTH - MLSTM_WIDTH
DIFF_HEAD_DIM = 64
DIFF_V_DIM = 2 * DIFF_HEAD_DIM
N_DIFF_HEADS = DIFF_WIDTH // DIFF_V_DIM
Q_BLOCK = 128
N_BUCKETS = 32
MAX_DISTANCE = 128
D_FF = 2816
FFN_CONV_W = 3
PLE_DIM = 256
EPS = 1e-6

COL_SIZES = [
    N_MLSTM_HEADS * MLSTM_DK,
    N_MLSTM_HEADS * MLSTM_DK,
    N_MLSTM_HEADS * MLSTM_DV,
    N_MLSTM_HEADS * MLSTM_DV,
    N_MLSTM_HEADS,
    N_MLSTM_HEADS,
    N_DIFF_HEADS * 2 * DIFF_HEAD_DIM,
    N_DIFF_HEADS * 2 * DIFF_HEAD_DIM,
    N_DIFF_HEADS * DIFF_V_DIM,
]
IN_COLS = sum(COL_SIZES)

kernel_name = 'hybrid_mlstm_diffattn_block'


def rms_norm(x, g):
    xf = x.astype(jnp.float32)
    y = xf * lax.rsqrt(jnp.mean(xf * xf, axis=-1, keepdims=True) + EPS)
    return (y * g.astype(jnp.float32)).astype(x.dtype)


def causal_dwconv(x, w):
    width, ch = w.shape
    return lax.conv_general_dilated(
        x, w.astype(x.dtype)[:, None, :], window_strides=(1,),
        padding=[(width - 1, 0)], dimension_numbers=('NWC', 'WIO', 'NWC'),
        feature_group_count=ch)


def t5_causal_bucket(dist):
    n = jnp.maximum(dist, 0)
    max_exact = N_BUCKETS // 2
    nf = jnp.maximum(n, 1).astype(jnp.float32)
    large = max_exact + (jnp.log(nf / max_exact) / math.log(MAX_DISTANCE / max_exact)
                         * (N_BUCKETS - max_exact)).astype(jnp.int32)
    large = jnp.minimum(large, N_BUCKETS - 1)
    return jnp.where(n < max_exact, n, large)


def mlstm_chunkwise(q, k, v, i_pre, f_pre):
    B, S, H, DK = q.shape
    DV = v.shape[-1]
    NC = S // CHUNK
    f32 = jnp.float32

    def to_chunks(t):
        t = t.astype(f32).reshape((B, NC, CHUNK, H) + t.shape[3:])
        return jnp.moveaxis(t, 3, 1)

    q = to_chunks(q)
    k = to_chunks(k) * (DK ** -0.5)
    v = to_chunks(v)
    ig = to_chunks(i_pre)
    logf = jax.nn.log_sigmoid(to_chunks(f_pre))
    b = jnp.cumsum(logf, axis=-1)
    g = b[..., -1]
    a = g[..., None] - b + ig

    def step(carry, inp):
        C, n, m = carry
        k_c, v_c, a_c, g_c = inp
        m_new = jnp.maximum(g_c + m, jnp.max(a_c, axis=-1))
        decay = jnp.exp(g_c + m - m_new)
        w = jnp.exp(a_c - m_new[..., None])
        C_new = decay[..., None, None] * C + jnp.einsum('bhlv,bhlk->bhvk', v_c * w[..., None], k_c)
        n_new = decay[..., None] * n + jnp.einsum('bhl,bhlk->bhk', w, k_c)
        return (C_new, n_new, m_new), (C, n, m)

    init = (jnp.zeros((B, H, DV, DK), f32), jnp.zeros((B, H, DK), f32), jnp.zeros((B, H), f32))
    xs = (jnp.moveaxis(k, 2, 0), jnp.moveaxis(v, 2, 0), jnp.moveaxis(a, 2, 0), jnp.moveaxis(g, 2, 0))
    _, (C_prev, n_prev, m_prev) = lax.scan(step, init, xs)
    C_prev = jnp.moveaxis(C_prev, 0, 2)
    n_prev = jnp.moveaxis(n_prev, 0, 2)
    m_prev = jnp.moveaxis(m_prev, 0, 2)

    e = b + m_prev[..., None]
    causal = jnp.tril(jnp.ones((CHUNK, CHUNK), dtype=bool))
    D = jnp.where(causal, b[..., :, None] - b[..., None, :] + ig[..., None, :], -jnp.inf)
    m_out = jnp.maximum(e, jnp.max(D, axis=-1))
    s = jnp.einsum('bhcld,bhcsd->bhcls', q, k) * jnp.exp(D - m_out[..., None])
    inter = jnp.exp(e - m_out)
    num = inter[..., None] * jnp.einsum('bhcld,bhcvd->bhclv', q, C_prev) \
        + jnp.einsum('bhcls,bhcsv->bhclv', s, v)
    den = inter * jnp.einsum('bhcld,bhcd->bhcl', q, n_prev) + jnp.sum(s, axis=-1)
    h = num / jnp.maximum(jnp.abs(den), jnp.exp(-m_out))[..., None]
    return jnp.moveaxis(h, 1, 3).reshape(B, S, H, DV)


def diff_attention(q, k, v, positions, rel_bias, lam, lam_init, subln_g):
    B, S, H, _, d = q.shape
    scale = d ** -0.5
    qh = jnp.transpose(q, (0, 2, 3, 1, 4))
    kh = jnp.transpose(k, (0, 2, 3, 1, 4))
    vh = jnp.transpose(v, (0, 2, 1, 3))
    outs = []
    for blk in range(S // Q_BLOCK):
        s0 = blk * Q_BLOCK
        s1 = s0 + Q_BLOCK
        qb = qh[:, :, :, s0:s1]
        kb = kh[:, :, :, :s1]
        vb = vh[:, :, :s1]
        logits = jnp.einsum('bhmqd,bhmkd->bhmqk', qb, kb).astype(jnp.float32) * scale
        bucket = t5_causal_bucket(positions[s0:s1, None] - positions[None, :s1])
        bias = jnp.transpose(rel_bias[bucket], (2, 3, 0, 1)).astype(jnp.float32)
        causal = jnp.arange(s0, s1)[:, None] >= jnp.arange(s1)[None, :]
        logits = jnp.where(causal, logits + bias, -jnp.inf)
        probs = jax.nn.softmax(logits, axis=-1)
        attn = probs[:, :, 0] - lam * probs[:, :, 1]
        outs.append(jnp.einsum('bhqk,bhkv->bhqv', attn, vb.astype(jnp.float32)))
    o = jnp.concatenate(outs, axis=2)
    o = rms_norm(o, subln_g) * (1.0 - lam_init)
    return jnp.transpose(o, (0, 2, 1, 3)).reshape(B, S, H * v.shape[-1])


def setup_inputs(seed: int = 0) -> dict:
    key = jax.random.key(seed)
    ks = jax.random.split(key, 26)
    f32 = jnp.float32

    def nrm(k, shape, scale):
        return jax.random.normal(k, shape, f32) * scale

    def gain(k, shape):
        return 1.0 + 0.05 * jax.random.normal(k, shape, f32)

    qk_cols = 2 * N_MLSTM_HEADS * MLSTM_DK
    return {
        'x': nrm(ks[0], (BATCH, SEQ, D_MODEL), 1.0),
        'p': nrm(ks[1], (DEPTH, BATCH, SEQ, PLE_DIM), 1.0),
        'positions': jnp.arange(SEQ, dtype=jnp.int32),
        'rel_bias': nrm(ks[2], (N_BUCKETS, N_DIFF_HEADS, 2), 0.2),
        'ln_mix_g': gain(ks[3], (DEPTH, D_MODEL)),
        'w_in': nrm(ks[4], (DEPTH, D_MODEL, IN_COLS), D_MODEL ** -0.5),
        'mlstm_conv_w': nrm(ks[5], (DEPTH, MLSTM_CONV_W, qk_cols), MLSTM_CONV_W ** -0.5),
        'b_igate': nrm(ks[6], (DEPTH, N_MLSTM_HEADS), 0.1),
        'b_fgate': 3.0 + nrm(ks[7], (DEPTH, N_MLSTM_HEADS), 0.5),
        'mlstm_norm_g': gain(ks[8], (DEPTH, MLSTM_DV)),
        'q_norm_g': gain(ks[9], (DEPTH, DIFF_HEAD_DIM)),
        'k_norm_g': gain(ks[10], (DEPTH, DIFF_HEAD_DIM)),
        'lam_q1': nrm(ks[11], (DEPTH, DIFF_HEAD_DIM), 0.1),
        'lam_k1': nrm(ks[12], (DEPTH, DIFF_HEAD_DIM), 0.1),
        'lam_q2': nrm(ks[13], (DEPTH, DIFF_HEAD_DIM), 0.1),
        'lam_k2': nrm(ks[14], (DEPTH, DIFF_HEAD_DIM), 0.1),
        'diff_subln_g': gain(ks[15], (DEPTH, DIFF_V_DIM)),
        'w_out': nrm(ks[16], (DEPTH, MIX_WIDTH, D_MODEL), MIX_WIDTH ** -0.5),
        'ln_ffn_g': gain(ks[17], (DEPTH, D_MODEL)),
        'w_up': nrm(ks[18], (DEPTH, D_MODEL, 2 * D_FF), D_MODEL ** -0.5),
        'ffn_conv_w': nrm(ks[19], (DEPTH, FFN_CONV_W, 2 * D_FF), FFN_CONV_W ** -0.5),
        'ffn_conv_b': nrm(ks[20], (DEPTH, 2 * D_FF), 0.02),
        'w_down': nrm(ks[21], (DEPTH, D_FF, D_MODEL), D_FF ** -0.5),
        'ln_ple_g': gain(ks[22], (DEPTH, D_MODEL)),
        'w_ple_gate': nrm(ks[23], (DEPTH, D_MODEL, D_MODEL), D_MODEL ** -0.5),
        'w_ple_proj': nrm(ks[24], (DEPTH, PLE_DIM, D_MODEL), PLE_DIM ** -0.5),
    }


def reference(x, p, positions, rel_bias, ln_mix_g, w_in, mlstm_conv_w, b_igate, b_fgate,
              mlstm_norm_g, q_norm_g, k_norm_g, lam_q1, lam_k1, lam_q2, lam_k2,
              diff_subln_g, w_out, ln_ffn_g, w_up, ffn_conv_w, ffn_conv_b, w_down,
              ln_ple_g, w_ple_gate, w_ple_proj):
    B, S, _ = x.shape
    split_idx = [int(c) for c in np.cumsum(COL_SIZES)[:-1]]
    qk_cols = N_MLSTM_HEADS * MLSTM_DK
    h = x
    for i in range(DEPTH):
        u = rms_norm(h, ln_mix_g[i])
        z = u @ w_in[i]
        qm, km, vm, om, im, fm, qd, kd, vd = jnp.split(z, split_idx, axis=-1)

        qk = jax.nn.silu(causal_dwconv(jnp.concatenate([qm, km], axis=-1), mlstm_conv_w[i]))
        qm, km = qk[..., :qk_cols], qk[..., qk_cols:]
        hm = mlstm_chunkwise(
            qm.reshape(B, S, N_MLSTM_HEADS, MLSTM_DK),
            km.reshape(B, S, N_MLSTM_HEADS, MLSTM_DK),
            vm.reshape(B, S, N_MLSTM_HEADS, MLSTM_DV),
            im + b_igate[i], fm + b_fgate[i])
        hm = rms_norm(hm, mlstm_norm_g[i]).astype(h.dtype)
        hm = hm.reshape(B, S, MLSTM_WIDTH) * jax.nn.sigmoid(om)

        lam_init = 0.8 - 0.6 * math.exp(-0.3 * i)
        lam = (jnp.exp(jnp.sum(lam_q1[i].astype(jnp.float32) * lam_k1[i].astype(jnp.float32)))
               - jnp.exp(jnp.sum(lam_q2[i].astype(jnp.float32) * lam_k2[i].astype(jnp.float32)))
               + lam_init)
        qd = rms_norm(qd.reshape(B, S, N_DIFF_HEADS, 2, DIFF_HEAD_DIM), q_norm_g[i])
        kd = rms_norm(kd.reshape(B, S, N_DIFF_HEADS, 2, DIFF_HEAD_DIM), k_norm_g[i])
        hd = diff_attention(qd, kd, vd.reshape(B, S, N_DIFF_HEADS, DIFF_V_DIM), positions,
                            rel_bias, lam, lam_init, diff_subln_g[i]).astype(h.dtype)

        h = h + jnp.concatenate([hm, hd], axis=-1) @ w_out[i]

        u = rms_norm(h, ln_ffn_g[i])
        up = causal_dwconv(u @ w_up[i], ffn_conv_w[i]) + ffn_conv_b[i]
        gate, val = up[..., :D_FF], up[..., D_FF:]
        h = h + (jax.nn.gelu(gate, approximate=False) * val) @ w_down[i]

        ple_gate = jax.nn.sigmoid(rms_norm(h, ln_ple_g[i]) @ w_ple_gate[i])
        h = h + ple_gate * (p[i] @ w_ple_proj[i])
    return h
```

```python
import functools
import math

import numpy as np
import jax
import jax.numpy as jnp
from jax import lax
from jax.experimental import pallas as pl
from jax.experimental.pallas import tpu as pltpu

F32 = jnp.float32
BF16 = jnp.bfloat16

D_MODEL = 1024
N_HEADS = 4
MLSTM_DK = 64
MLSTM_DV = 128
MLSTM_CONV_W = 4
CHUNK = 64
DIFF_D = 64
DIFF_DV = 128
N_BUCKETS = 32
MAX_DISTANCE = 128
D_FF = 2816
FFN_CONV_W = 3
PLE_DIM = 256
EPS = 1e-6
GROUP_W = N_HEADS * 128
N_GATE_ROWS = 16

TM_IN = 512
TM_FFN = 512
FF_CHUNK = 256
N_FF_CHUNKS = D_FF // FF_CHUNK
TQ = 256
TK = 256
NEG_BIG = -1e30
VMEM_LIMIT = 56 * 1024 * 1024


def _dot(a, b):
    return jnp.dot(a, b, preferred_element_type=F32)


def _dot_nt(a, b):
    return lax.dot_general(a, b, (((1,), (1,)), ((), ())), preferred_element_type=F32)


def _dot_tn(a, b):
    return lax.dot_general(a, b, (((0,), (0,)), ((), ())), preferred_element_type=F32)


def _sigmoid(x):
    return 1.0 / (1.0 + jnp.exp(-x))


def _log_sigmoid(x):
    return jnp.minimum(x, 0.0) - jnp.log1p(jnp.exp(-jnp.abs(x)))


def _bucket_thresholds():
    max_exact = N_BUCKETS // 2
    thr = []
    for v in range(1, N_BUCKETS):
        if v <= max_exact:
            thr.append(v)
            continue
        edge = max_exact * (MAX_DISTANCE / max_exact) ** ((v - max_exact) / (N_BUCKETS - max_exact))
        assert abs(edge - round(edge)) > 1e-3, edge
        thr.append(int(math.ceil(edge)))
    assert all(a < b for a, b in zip(thr, thr[1:])), thr
    return tuple(thr)


_BUCKET_THR = _bucket_thresholds()


def _group_mean_square(z, gsum_ref):
    sq = z * z
    hi = sq.astype(BF16)
    lo = (sq - hi.astype(F32)).astype(BF16)
    return (_dot(hi, gsum_ref[...]) + _dot(lo, gsum_ref[...])) * (1.0 / DIFF_D)


def _inproj_kernel(h_ref, g_ref, wm_ref, wg_ref, bcol_ref, brow_ref, gsum_ref, qg_ref, kg_ref,
                   qk_ref, v_ref, o_ref, gc_ref, gr_ref, qd_ref, kd_ref, vd_ref):
    x = h_ref[...]
    ms = jnp.mean(x * x, axis=-1, keepdims=True)
    u = (x * lax.rsqrt(ms + EPS) * g_ref[...]).astype(BF16)
    w = GROUP_W
    qk_ref[...] = _dot(u, wm_ref[:, 0:w])
    v_ref[...] = _dot(u, wm_ref[:, w:2 * w]).astype(BF16)
    o_ref[...] = _dot(u, wm_ref[:, 2 * w:3 * w])
    zq = _dot(u, wm_ref[:, 3 * w:4 * w])
    qd_ref[...] = (zq * lax.rsqrt(_group_mean_square(zq, gsum_ref) + EPS) * qg_ref[...]).astype(BF16)
    zk = _dot(u, wm_ref[:, 4 * w:5 * w])
    kd_ref[...] = (zk * lax.rsqrt(_group_mean_square(zk, gsum_ref) + EPS) * kg_ref[...]).astype(BF16)
    vd_ref[...] = _dot(u, wm_ref[:, 5 * w:6 * w]).astype(BF16)
    gc_ref[...] = _dot_nt(u, wg_ref[...]) + bcol_ref[...]
    gr = _dot_nt(wg_ref[...], u) + brow_ref[...]
    for ci in range(TM_IN // CHUNK):
        gr_ref[ci] = gr[:, ci * CHUNK:(ci + 1) * CHUNK]


def _inproj(h, g, wm, wg, bcol, brow, gsum, qg, kg):
    n_tok = h.shape[0]
    grid = (n_tok // TM_IN,)
    const = lambda shape: pl.BlockSpec(shape, lambda i: (0,) * len(shape))
    tile = lambda width: pl.BlockSpec((TM_IN, width), lambda i: (i, 0))
    out_shape = (
        jax.ShapeDtypeStruct((n_tok, GROUP_W), F32),
        jax.ShapeDtypeStruct((n_tok, GROUP_W), BF16),
        jax.ShapeDtypeStruct((n_tok, GROUP_W), F32),
        jax.ShapeDtypeStruct((n_tok, N_GATE_ROWS), F32),
        jax.ShapeDtypeStruct((n_tok // CHUNK, N_GATE_ROWS, CHUNK), F32),
        jax.ShapeDtypeStruct((n_tok, GROUP_W), BF16),
        jax.ShapeDtypeStruct((n_tok, GROUP_W), BF16),
        jax.ShapeDtypeStruct((n_tok, GROUP_W), BF16),
    )
    out_specs = (
        tile(GROUP_W), tile(GROUP_W), tile(GROUP_W), tile(N_GATE_ROWS),
        pl.BlockSpec((TM_IN // CHUNK, N_GATE_ROWS, CHUNK), lambda i: (i, 0, 0)),
        tile(GROUP_W), tile(GROUP_W), tile(GROUP_W),
    )
    in_specs = [
        tile(D_MODEL), const((1, D_MODEL)), const(wm.shape), const(wg.shape),
        const((1, N_GATE_ROWS)), const((N_GATE_ROWS, 1)), const(gsum.shape),
        const((1, GROUP_W)), const((1, GROUP_W)),
    ]
    return pl.pallas_call(
        _inproj_kernel, grid=grid, in_specs=in_specs, out_specs=out_specs, out_shape=out_shape,
        name="inproj",
        compiler_params=pltpu.CompilerParams(dimension_semantics=("arbitrary",), vmem_limit_bytes=VMEM_LIMIT),
    )(h, g, wm, wg, bcol, brow, gsum, qg, kg)


def _mlstm_kernel(qk_ref, v_ref, o_ref, gc_ref, gr_ref, cw_ref, ng_ref, out_ref, xpad_ref, ct_ref, n_ref):
    seq = qk_ref.shape[0]
    n_chunks = seq // CHUNK
    halo = 8
    xpad_ref[0:halo, :] = jnp.zeros((halo, GROUP_W), F32)
    xpad_ref[halo:, :] = qk_ref[...]
    ct_ref[...] = jnp.zeros_like(ct_ref)
    n_ref[...] = jnp.zeros_like(n_ref)

    row = lax.broadcasted_iota(jnp.int32, (CHUNK, CHUNK), 0)
    col = lax.broadcasted_iota(jnp.int32, (CHUNK, CHUNK), 1)
    tri = col <= row
    cw = cw_ref[...]
    ng = ng_ref[...]

    def chunk_step(c, m_state):
        r0 = pl.multiple_of(c * CHUNK, CHUNK)
        win = xpad_ref[pl.ds(r0, CHUNK + halo), :]
        conv = cw[0:1, :] * win[halo - 3:halo - 3 + CHUNK, :]
        for j in range(1, MLSTM_CONV_W):
            conv = conv + cw[j:j + 1, :] * win[halo - 3 + j:halo - 3 + j + CHUNK, :]
        x = conv * _sigmoid(conv)
        gcc = gc_ref[pl.ds(r0, CHUNK), :]
        grr = gr_ref[c]
        m_next = []
        for hd in range(N_HEADS):
            lane0 = hd * 128
            q = x[:, lane0:lane0 + MLSTM_DK]
            k = x[:, lane0 + MLSTM_DK:lane0 + 2 * MLSTM_DK] * (MLSTM_DK ** -0.5)
            qb = q.astype(BF16)
            kb = k.astype(BF16)
            vb = v_ref[pl.ds(r0, CHUNK), lane0:lane0 + 128]
            i_col = gcc[:, hd:hd + 1]
            lf_col = _log_sigmoid(gcc[:, N_HEADS + hd:N_HEADS + hd + 1])
            i_row = grr[hd:hd + 1, :]
            lf_row = _log_sigmoid(grr[N_HEADS + hd:N_HEADS + hd + 1, :])
            b_col = jnp.sum(jnp.where(tri, lf_row, 0.0), axis=1, keepdims=True)
            b_row = jnp.sum(jnp.where(row <= col, lf_col, 0.0), axis=0, keepdims=True)
            g = jnp.sum(lf_row, axis=1, keepdims=True)
            a_col = g - b_col + i_col
            a_row = g - b_row + i_row
            m_prev = m_state[hd]
            m_new = jnp.maximum(g + m_prev, jnp.max(a_row, axis=1, keepdims=True))
            decay = jnp.exp(g + m_prev - m_new)
            w_col = jnp.exp(a_col - m_new)
            e_col = b_col + m_prev
            dmat = jnp.where(tri, b_col - b_row + i_row, -jnp.inf)
            m_out = jnp.maximum(e_col, jnp.max(dmat, axis=1, keepdims=True))
            s = _dot_nt(qb, kb) * jnp.exp(dmat - m_out)
            inter = jnp.exp(e_col - m_out)
            ct = ct_ref[hd]
            n_row = n_ref[hd]
            num = inter * _dot(qb, ct.astype(BF16)) + _dot(s.astype(BF16), vb)
            den = inter * jnp.sum(q * n_row, axis=1, keepdims=True) + jnp.sum(s, axis=1, keepdims=True)
            hh = num / jnp.maximum(jnp.abs(den), jnp.exp(-m_out))
            hn = hh * lax.rsqrt(jnp.mean(hh * hh, axis=1, keepdims=True) + EPS) * ng
            og = o_ref[pl.ds(r0, CHUNK), lane0:lane0 + 128]
            out_ref[pl.ds(r0, CHUNK), lane0:lane0 + 128] = (hn * _sigmoid(og)).astype(BF16)
            wk = w_col * k
            ct_ref[hd] = decay * ct + _dot_tn(wk.astype(BF16), vb)
            n_ref[hd] = decay * n_row + jnp.sum(wk, axis=0, keepdims=True)
            m_next.append(m_new)
        return tuple(m_next)

    m0 = tuple(jnp.zeros((1, 1), F32) for _ in range(N_HEADS))
    lax.fori_loop(0, n_chunks, chunk_step, m0)


def _mlstm(qk, v, o, gc, gr, cw, ng, batch, seq):
    n_chunks = seq // CHUNK
    blk = lambda: pl.BlockSpec((seq, GROUP_W), lambda b: (b, 0))
    in_specs = [
        blk(), blk(), blk(),
        pl.BlockSpec((seq, N_GATE_ROWS), lambda b: (b, 0)),
        pl.BlockSpec((n_chunks, N_GATE_ROWS, CHUNK), lambda b: (b, 0, 0)),
        pl.BlockSpec((MLSTM_CONV_W, GROUP_W), lambda b: (0, 0)),
        pl.BlockSpec((1, MLSTM_DV), lambda b: (0, 0)),
    ]
    return pl.pallas_call(
        _mlstm_kernel, grid=(batch,), in_specs=in_specs, out_specs=blk(),
        out_shape=jax.ShapeDtypeStruct((batch * seq, GROUP_W), BF16),
        scratch_shapes=[
            pltpu.VMEM((seq + 8, GROUP_W), F32),
            pltpu.VMEM((N_HEADS, MLSTM_DK, MLSTM_DV), F32),
            pltpu.VMEM((N_HEADS, 1, MLSTM_DK), F32),
        ],
        name="mlstm",
        compiler_params=pltpu.CompilerParams(dimension_semantics=("arbitrary",), vmem_limit_bytes=VMEM_LIMIT),
    )(qk, v, o, gc, gr, cw, ng)


def _attn_kernel(lam_init, tbl_ref, q_ref, k_ref, v_ref, pcol_ref, prow_ref, lq1_ref, lk1_ref, lq2_ref, lk2_ref,
                 sg_ref, out_ref, bias_ref, m_ref, l_ref, acc_ref):
    hd = pl.program_id(0)
    qi = pl.program_id(1)
    b = pl.program_id(2)
    n_kv = qi + 1

    @pl.when(b == 0)
    def _build_bias():
        rows = 32
        rowi = lax.broadcasted_iota(jnp.int32, (rows, TK), 0)
        coli = lax.broadcasted_iota(jnp.int32, (rows, TK), 1)
        base = hd * (2 * N_BUCKETS)

        def build(t, _):
            kj = t // (TQ // rows)
            r0 = pl.multiple_of((t % (TQ // rows)) * rows, rows)
            k0 = pl.multiple_of(kj * TK, TK)
            dist = jnp.maximum(pcol_ref[pl.ds(r0, rows), :] - prow_ref[:, pl.ds(k0, TK)], 0)
            causal = rowi + (qi * TQ + r0) >= coli + k0
            b0 = jnp.full((rows, TK), tbl_ref[base], F32)
            b1 = jnp.full((rows, TK), tbl_ref[base + N_BUCKETS], F32)
            for v, thr in enumerate(_BUCKET_THR, start=1):
                ge = dist >= thr
                b0 = jnp.where(ge, tbl_ref[base + v], b0)
                b1 = jnp.where(ge, tbl_ref[base + N_BUCKETS + v], b1)
            bias_ref[0, pl.ds(r0, rows), pl.ds(k0, TK)] = jnp.where(causal, b0, NEG_BIG)
            bias_ref[1, pl.ds(r0, rows), pl.ds(k0, TK)] = jnp.where(causal, b1, NEG_BIG)
            return 0

        lax.fori_loop(0, n_kv * (TQ // rows), build, 0)

    lane = lax.broadcasted_iota(jnp.int32, (TQ, 128), 1)
    q = q_ref[...]
    zero = jnp.zeros_like(q)
    q_comp = (jnp.where(lane < DIFF_D, q, zero), jnp.where(lane >= DIFF_D, q, zero))
    m_ref[...] = jnp.full(m_ref.shape, NEG_BIG, F32)
    l_ref[...] = jnp.zeros_like(l_ref)
    acc_ref[...] = jnp.zeros_like(acc_ref)

    def kv_step(kj, _):
        k0 = pl.multiple_of(kj * TK, TK)
        kb = k_ref[pl.ds(k0, TK), :]
        vb = v_ref[pl.ds(k0, TK), :]
        for c in range(2):
            s = _dot_nt(q_comp[c], kb) + bias_ref[c, :, pl.ds(k0, TK)]
            m_prev = m_ref[c]
            m_new = jnp.maximum(m_prev, jnp.max(s, axis=1, keepdims=True))
            alpha = jnp.exp(m_prev - m_new)
            p = jnp.exp(s - m_new)
            l_ref[c] = alpha * l_ref[c] + jnp.sum(p, axis=1, keepdims=True)
            acc_ref[c] = alpha * acc_ref[c] + _dot(p.astype(BF16), vb)
            m_ref[c] = m_new
        return 0

    lax.fori_loop(0, n_kv, kv_step, 0)

    lam = (jnp.exp(jnp.sum(lq1_ref[...] * lk1_ref[...], axis=1, keepdims=True))
           - jnp.exp(jnp.sum(lq2_ref[...] * lk2_ref[...], axis=1, keepdims=True)) + lam_init)
    o = acc_ref[0] / l_ref[0] - lam * (acc_ref[1] / l_ref[1])
    on = o * lax.rsqrt(jnp.mean(o * o, axis=1, keepdims=True) + EPS) * sg_ref[...]
    out_ref[...] = (on * (1.0 - lam_init)).astype(BF16)


def _attention(lam_init, tbl, qd, kd, vd, pcol, prow, lq1, lk1, lq2, lk2, sg, batch, seq):
    q3, k3, v3 = (t.reshape(batch, seq, GROUP_W) for t in (qd, kd, vd))
    vec = lambda n: pl.BlockSpec((1, n), lambda h, i, b: (0, 0))
    in_specs = [
        pl.BlockSpec(memory_space=pltpu.SMEM),
        pl.BlockSpec((None, TQ, 128), lambda h, i, b: (b, i, h)),
        pl.BlockSpec((None, seq, 128), lambda h, i, b: (b, 0, h)),
        pl.BlockSpec((None, seq, 128), lambda h, i, b: (b, 0, h)),
        pl.BlockSpec((TQ, 1), lambda h, i, b: (i, 0)),
        pl.BlockSpec((1, seq), lambda h, i, b: (0, 0)),
        vec(DIFF_D), vec(DIFF_D), vec(DIFF_D), vec(DIFF_D), vec(DIFF_DV),
    ]
    out = pl.pallas_call(
        functools.partial(_attn_kernel, lam_init),
        grid=(N_HEADS, seq // TQ, batch), in_specs=in_specs,
        out_specs=pl.BlockSpec((None, TQ, 128), lambda h, i, b: (b, i, h)),
        out_shape=jax.ShapeDtypeStruct((batch, seq, GROUP_W), BF16),
        scratch_shapes=[
            pltpu.VMEM((2, TQ, seq), F32),
            pltpu.VMEM((2, TQ, 1), F32),
            pltpu.VMEM((2, TQ, 1), F32),
            pltpu.VMEM((2, TQ, DIFF_DV), F32),
        ],
        name="diff_attn",
        compiler_params=pltpu.CompilerParams(
            dimension_semantics=("arbitrary", "arbitrary", "arbitrary"), vmem_limit_bytes=VMEM_LIMIT),
    )(tbl, q3, k3, v3, pcol, prow, lq1, lk1, lq2, lk2, sg)
    return out.reshape(batch * seq, GROUP_W)


def _gelu(x):
    return 0.5 * x * (1.0 + lax.erf(x * (2.0 ** -0.5)))


def _rms(x, g):
    return x * lax.rsqrt(jnp.mean(x * x, axis=-1, keepdims=True) + EPS) * g


def _mixer_kernel(h_ref, hm_ref, hd_ref, p_ref, woa_ref, wob_ref, gf_ref, wug_ref, wuv_ref, cwg_ref, cwv_ref,
                  cbg_ref, cbv_ref, wd_ref, gp_ref, wpg_ref, wpp_ref, out_ref,
                  sg_ref, sv_ref, cg_ref, cv_ref, acc_ref):
    tm = h_ref.shape[0]
    halo = 8

    @pl.when(pl.program_id(1) == 0)
    def _reset_conv_history():
        cg_ref[...] = jnp.zeros_like(cg_ref)
        cv_ref[...] = jnp.zeros_like(cv_ref)

    h1 = h_ref[...] + _dot(hm_ref[...], woa_ref[...]) + _dot(hd_ref[...], wob_ref[...])
    u = _rms(h1, gf_ref[...]).astype(BF16)
    acc_ref[...] = jnp.zeros_like(acc_ref)

    def conv_branch(j, w_ref, cw_ref, cb_ref, stage_ref, hist_ref):
        up = _dot(u, w_ref[j])
        stage_ref[0:halo, :] = hist_ref[j]
        stage_ref[halo:, :] = up
        hist_ref[j] = up[tm - halo:, :]
        cw = cw_ref[j]
        return (cw[0:1, :] * stage_ref[halo - 2:halo - 2 + tm, :]
                + cw[1:2, :] * stage_ref[halo - 1:halo - 1 + tm, :]
                + cw[2:3, :] * up + cb_ref[j])

    def ff_step(j, _):
        gate = conv_branch(j, wug_ref, cwg_ref, cbg_ref, sg_ref, cg_ref)
        val = conv_branch(j, wuv_ref, cwv_ref, cbv_ref, sv_ref, cv_ref)
        act = (_gelu(gate) * val).astype(BF16)
        acc_ref[...] += _dot(act, wd_ref[j])
        return 0

    lax.fori_loop(0, N_FF_CHUNKS, ff_step, 0)

    h2 = h1 + acc_ref[...]
    u3 = _rms(h2, gp_ref[...]).astype(BF16)
    ple_gate = _sigmoid(_dot(u3, wpg_ref[...]))
    out_ref[...] = h2 + ple_gate * _dot(p_ref[...].astype(BF16), wpp_ref[...])


def _mixer(h, hm, hd, p, woa, wob, gf, wug, wuv, cwg, cwv, cbg, cbv, wd, gp, wpg, wpp, batch, seq):
    tiles = seq // TM_FFN
    tile = lambda width: pl.BlockSpec((TM_FFN, width), lambda b, t: (b * tiles + t, 0))

    def const(arr):
        nd = arr.ndim
        return pl.BlockSpec(arr.shape, lambda b, t: (0,) * nd, pipeline_mode=pl.Buffered(1))

    weights = (woa, wob, gf, wug, wuv, cwg, cwv, cbg, cbv, wd, gp, wpg, wpp)
    in_specs = [tile(D_MODEL), tile(GROUP_W), tile(GROUP_W), tile(PLE_DIM)] + [const(w) for w in weights]
    return pl.pallas_call(
        _mixer_kernel, grid=(batch, tiles), in_specs=in_specs, out_specs=tile(D_MODEL),
        out_shape=jax.ShapeDtypeStruct(h.shape, F32),
        scratch_shapes=[
            pltpu.VMEM((TM_FFN + 8, FF_CHUNK), F32),
            pltpu.VMEM((TM_FFN + 8, FF_CHUNK), F32),
            pltpu.VMEM((N_FF_CHUNKS, 8, FF_CHUNK), F32),
            pltpu.VMEM((N_FF_CHUNKS, 8, FF_CHUNK), F32),
            pltpu.VMEM((TM_FFN, D_MODEL), F32),
        ],
        name="mixer",
        compiler_params=pltpu.CompilerParams(
            dimension_semantics=("arbitrary", "arbitrary"), vmem_limit_bytes=VMEM_LIMIT),
    )(h, hm, hd, p, *weights)


def _head_interleave(qcols, kcols):
    lead = qcols.shape[:-1]
    qh = qcols.reshape(lead + (N_HEADS, MLSTM_DK))
    kh = kcols.reshape(lead + (N_HEADS, MLSTM_DK))
    return jnp.concatenate([qh, kh], axis=-1).reshape(lead + (GROUP_W,))


def _ff_chunks(w):
    return jnp.transpose(w.reshape(w.shape[0], N_FF_CHUNKS, FF_CHUNK), (1, 0, 2))


def kernel(x, p, positions, rel_bias, ln_mix_g, w_in, mlstm_conv_w, b_igate, b_fgate, mlstm_norm_g, q_norm_g, k_norm_g, lam_q1, lam_k1, lam_q2, lam_k2, diff_subln_g, w_out, ln_ffn_g, w_up, ffn_conv_w, ffn_conv_b, w_down, ln_ple_g, w_ple_gate, w_ple_proj):
    batch, seq, _ = x.shape
    depth = w_in.shape[0]
    n_tok = batch * seq
    qk_cols = N_HEADS * MLSTM_DK
    col_sizes = [qk_cols, qk_cols, GROUP_W, GROUP_W, N_HEADS, N_HEADS, GROUP_W, GROUP_W, GROUP_W]
    offs = np.concatenate([[0], np.cumsum(col_sizes)])
    sl = lambda a, j: a[..., int(offs[j]):int(offs[j + 1])]

    tbl = jnp.transpose(rel_bias.astype(F32), (1, 2, 0)).reshape(-1)
    pcol = positions.astype(jnp.int32).reshape(seq, 1)
    prow = positions.astype(jnp.int32).reshape(1, seq)
    gsum = jnp.asarray(np.kron(np.eye(GROUP_W // DIFF_D), np.ones((DIFF_D, DIFF_D))), BF16)

    h = x.reshape(n_tok, D_MODEL)
    for i in range(depth):
        wi = w_in[i]
        wm = jnp.concatenate(
            [_head_interleave(sl(wi, 0), sl(wi, 1)), sl(wi, 2), sl(wi, 3), sl(wi, 6), sl(wi, 7), sl(wi, 8)],
            axis=-1).astype(BF16)
        wg = jnp.zeros((N_GATE_ROWS, D_MODEL), F32).at[:2 * N_HEADS].set(
            jnp.concatenate([sl(wi, 4), sl(wi, 5)], axis=-1).T).astype(BF16)
        gate_bias = jnp.zeros((N_GATE_ROWS,), F32).at[:2 * N_HEADS].set(
            jnp.concatenate([b_igate[i], b_fgate[i]]).astype(F32))
        qg = jnp.tile(q_norm_g[i].astype(F32), GROUP_W // DIFF_D).reshape(1, GROUP_W) * (DIFF_D ** -0.5)
        kg = jnp.tile(k_norm_g[i].astype(F32), GROUP_W // DIFF_D).reshape(1, GROUP_W)
        qk, vm, om, gc, gr, qd, kd, vd = _inproj(
            h, ln_mix_g[i].reshape(1, D_MODEL), wm, wg, gate_bias.reshape(1, -1), gate_bias.reshape(-1, 1),
            gsum, qg, kg)

        cw = _head_interleave(mlstm_conv_w[i][:, :qk_cols], mlstm_conv_w[i][:, qk_cols:]).astype(F32)
        hm = _mlstm(qk, vm, om, gc, gr, cw, mlstm_norm_g[i].reshape(1, MLSTM_DV).astype(F32), batch, seq)

        lam_init = 0.8 - 0.6 * math.exp(-0.3 * i)
        row64 = lambda a: a[i].reshape(1, DIFF_D).astype(F32)
        hd = _attention(lam_init, tbl, qd, kd, vd, pcol, prow, row64(lam_q1), row64(lam_k1), row64(lam_q2),
                        row64(lam_k2), diff_subln_g[i].reshape(1, DIFF_DV).astype(F32), batch, seq)

        wo = w_out[i].astype(BF16)
        wu = w_up[i].astype(BF16)
        cwf = ffn_conv_w[i].astype(F32)
        cbf = ffn_conv_b[i].astype(F32).reshape(1, 2 * D_FF)
        h = _mixer(
            h, hm, hd, p[i].reshape(n_tok, PLE_DIM),
            wo[:GROUP_W], wo[GROUP_W:], ln_ffn_g[i].reshape(1, D_MODEL),
            _ff_chunks(wu[:, :D_FF]), _ff_chunks(wu[:, D_FF:]),
            _ff_chunks(cwf[:, :D_FF]), _ff_chunks(cwf[:, D_FF:]),
            _ff_chunks(cbf[:, :D_FF]), _ff_chunks(cbf[:, D_FF:]),
            w_down[i].astype(BF16).reshape(N_FF_CHUNKS, FF_CHUNK, D_MODEL),
            ln_ple_g[i].reshape(1, D_MODEL), w_ple_gate[i].astype(BF16), w_ple_proj[i].astype(BF16),
            batch, seq)
    return h.reshape(batch, seq, D_MODEL)
```

```python
import functools
import math

import numpy as np
import jax
import jax.numpy as jnp
from jax import lax
from jax.experimental import pallas as pl
from jax.experimental.pallas import tpu as pltpu

F32 = jnp.float32
BF16 = jnp.bfloat16

D_MODEL = 1024
N_HEADS = 4
MLSTM_DK = 64
MLSTM_DV = 128
MLSTM_CONV_W = 4
CHUNK = 64
DIFF_D = 64
DIFF_DV = 128
N_BUCKETS = 32
MAX_DISTANCE = 128
D_FF = 2816
FFN_CONV_W = 3
PLE_DIM = 256
EPS = 1e-6
GROUP_W = N_HEADS * 128
N_GATE_ROWS = 16

TM_IN = 512
TM_FFN = 512
FF_CHUNK = 256
N_FF_CHUNKS = D_FF // FF_CHUNK
TQ = 512
TK = 256
NEG_BIG = -1e30
M_INIT = -1e20
ONES_ROWS = 16
V_ROWS = DIFF_DV + ONES_ROWS
LOG2E = math.log2(math.e)
VMEM_LIMIT = 56 * 1024 * 1024


def _dot(a, b):
    return jnp.dot(a, b, preferred_element_type=F32)


def _dot_nt(a, b):
    return lax.dot_general(a, b, (((1,), (1,)), ((), ())), preferred_element_type=F32)


def _dot_tn(a, b):
    return lax.dot_general(a, b, (((0,), (0,)), ((), ())), preferred_element_type=F32)


def _sigmoid(x):
    return 1.0 / (1.0 + jnp.exp(-x))


def _log_sigmoid(x):
    return jnp.minimum(x, 0.0) - jnp.log1p(jnp.exp(-jnp.abs(x)))


def _bucket_thresholds():
    max_exact = N_BUCKETS // 2
    thr = []
    for v in range(1, N_BUCKETS):
        if v <= max_exact:
            thr.append(v)
            continue
        edge = max_exact * (MAX_DISTANCE / max_exact) ** ((v - max_exact) / (N_BUCKETS - max_exact))
        assert abs(edge - round(edge)) > 1e-3, edge
        thr.append(int(math.ceil(edge)))
    assert all(a < b for a, b in zip(thr, thr[1:])), thr
    return tuple(thr)


_BUCKET_THR = _bucket_thresholds()


def _group_mean_square(z, gsum_ref):
    sq = z * z
    hi = sq.astype(BF16)
    lo = (sq - hi.astype(F32)).astype(BF16)
    return (_dot(hi, gsum_ref[...]) + _dot(lo, gsum_ref[...])) * (1.0 / DIFF_D)


def _inproj_kernel(h_ref, g_ref, wm_ref, wvt_ref, wg_ref, bcol_ref, brow_ref, gsum_ref, qg_ref, kg_ref,
                   qk_ref, v_ref, o_ref, gc_ref, gr_ref, qd_ref, kd_ref, vdt_ref):
    x = h_ref[...]
    ms = jnp.mean(x * x, axis=-1, keepdims=True)
    u = (x * lax.rsqrt(ms + EPS) * g_ref[...]).astype(BF16)
    w = GROUP_W
    qk_ref[...] = _dot(u, wm_ref[:, 0:w])
    v_ref[...] = _dot(u, wm_ref[:, w:2 * w]).astype(BF16)
    o_ref[...] = _dot(u, wm_ref[:, 2 * w:3 * w])
    zq = _dot(u, wm_ref[:, 3 * w:4 * w])
    qd_ref[...] = (zq * lax.rsqrt(_group_mean_square(zq, gsum_ref) + EPS) * qg_ref[...]).astype(BF16)
    zk = _dot(u, wm_ref[:, 4 * w:5 * w])
    kd_ref[...] = (zk * lax.rsqrt(_group_mean_square(zk, gsum_ref) + EPS) * kg_ref[...]).astype(BF16)
    vt = _dot_nt(wvt_ref[...], u).astype(BF16)
    for hh in range(N_HEADS):
        vdt_ref[hh * V_ROWS:hh * V_ROWS + DIFF_DV, :] = vt[hh * DIFF_DV:(hh + 1) * DIFF_DV, :]
        vdt_ref[hh * V_ROWS + DIFF_DV:(hh + 1) * V_ROWS, :] = jnp.ones((ONES_ROWS, TM_IN), BF16)
    gc_ref[...] = _dot_nt(u, wg_ref[...]) + bcol_ref[...]
    gr = _dot_nt(wg_ref[...], u) + brow_ref[...]
    for ci in range(TM_IN // CHUNK):
        gr_ref[ci] = gr[:, ci * CHUNK:(ci + 1) * CHUNK]


def _inproj(h, g, wm, wvt, wg, bcol, brow, gsum, qg, kg):
    n_tok = h.shape[0]
    grid = (n_tok // TM_IN,)
    const = lambda shape: pl.BlockSpec(shape, lambda i: (0,) * len(shape))
    tile = lambda width: pl.BlockSpec((TM_IN, width), lambda i: (i, 0))
    out_shape = (
        jax.ShapeDtypeStruct((n_tok, GROUP_W), F32),
        jax.ShapeDtypeStruct((n_tok, GROUP_W), BF16),
        jax.ShapeDtypeStruct((n_tok, GROUP_W), F32),
        jax.ShapeDtypeStruct((n_tok, N_GATE_ROWS), F32),
        jax.ShapeDtypeStruct((n_tok // CHUNK, N_GATE_ROWS, CHUNK), F32),
        jax.ShapeDtypeStruct((n_tok, GROUP_W), BF16),
        jax.ShapeDtypeStruct((n_tok, GROUP_W), BF16),
        jax.ShapeDtypeStruct((N_HEADS * V_ROWS, n_tok), BF16),
    )
    out_specs = (
        tile(GROUP_W), tile(GROUP_W), tile(GROUP_W), tile(N_GATE_ROWS),
        pl.BlockSpec((TM_IN // CHUNK, N_GATE_ROWS, CHUNK), lambda i: (i, 0, 0)),
        tile(GROUP_W), tile(GROUP_W), pl.BlockSpec((N_HEADS * V_ROWS, TM_IN), lambda i: (0, i)),
    )
    in_specs = [
        tile(D_MODEL), const((1, D_MODEL)), const(wm.shape), const(wvt.shape), const(wg.shape),
        const((1, N_GATE_ROWS)), const((N_GATE_ROWS, 1)), const(gsum.shape),
        const((1, GROUP_W)), const((1, GROUP_W)),
    ]
    return pl.pallas_call(
        _inproj_kernel, grid=grid, in_specs=in_specs, out_specs=out_specs, out_shape=out_shape,
        name="inproj",
        compiler_params=pltpu.CompilerParams(dimension_semantics=("arbitrary",), vmem_limit_bytes=VMEM_LIMIT),
    )(h, g, wm, wvt, wg, bcol, brow, gsum, qg, kg)


def _mlstm_kernel(qk_ref, v_ref, o_ref, gc_ref, gr_ref, cw_ref, ng_ref, out_ref, xpad_ref, ct_ref, n_ref):
    seq = qk_ref.shape[0]
    n_chunks = seq // CHUNK
    halo = 8
    xpad_ref[0:halo, :] = jnp.zeros((halo, GROUP_W), F32)
    xpad_ref[halo:, :] = qk_ref[...]
    ct_ref[...] = jnp.zeros_like(ct_ref)
    n_ref[...] = jnp.zeros_like(n_ref)

    row = lax.broadcasted_iota(jnp.int32, (CHUNK, CHUNK), 0)
    col = lax.broadcasted_iota(jnp.int32, (CHUNK, CHUNK), 1)
    tri = col <= row
    cw = cw_ref[...]
    ng = ng_ref[...]

    def chunk_step(c, m_state):
        r0 = pl.multiple_of(c * CHUNK, CHUNK)
        win = xpad_ref[pl.ds(r0, CHUNK + halo), :]
        conv = cw[0:1, :] * win[halo - 3:halo - 3 + CHUNK, :]
        for j in range(1, MLSTM_CONV_W):
            conv = conv + cw[j:j + 1, :] * win[halo - 3 + j:halo - 3 + j + CHUNK, :]
        x = conv * _sigmoid(conv)
        gcc = gc_ref[pl.ds(r0, CHUNK), :]
        grr = gr_ref[c]
        m_next = []
        for hd in range(N_HEADS):
            lane0 = hd * 128
            q = x[:, lane0:lane0 + MLSTM_DK]
            k = x[:, lane0 + MLSTM_DK:lane0 + 2 * MLSTM_DK] * (MLSTM_DK ** -0.5)
            qb = q.astype(BF16)
            kb = k.astype(BF16)
            vb = v_ref[pl.ds(r0, CHUNK), lane0:lane0 + 128]
            i_col = gcc[:, hd:hd + 1]
            lf_col = _log_sigmoid(gcc[:, N_HEADS + hd:N_HEADS + hd + 1])
            i_row = grr[hd:hd + 1, :]
            lf_row = _log_sigmoid(grr[N_HEADS + hd:N_HEADS + hd + 1, :])
            b_col = jnp.sum(jnp.where(tri, lf_row, 0.0), axis=1, keepdims=True)
            b_row = jnp.sum(jnp.where(row <= col, lf_col, 0.0), axis=0, keepdims=True)
            g = jnp.sum(lf_row, axis=1, keepdims=True)
            a_col = g - b_col + i_col
            a_row = g - b_row + i_row
            m_prev = m_state[hd]
            m_new = jnp.maximum(g + m_prev, jnp.max(a_row, axis=1, keepdims=True))
            decay = jnp.exp(g + m_prev - m_new)
            w_col = jnp.exp(a_col - m_new)
            e_col = b_col + m_prev
            dmat = jnp.where(tri, b_col - b_row + i_row, -jnp.inf)
            m_out = jnp.maximum(e_col, jnp.max(dmat, axis=1, keepdims=True))
            s = _dot_nt(qb, kb) * jnp.exp(dmat - m_out)
            inter = jnp.exp(e_col - m_out)
            ct = ct_ref[hd]
            n_row = n_ref[hd]
            num = inter * _dot(qb, ct.astype(BF16)) + _dot(s.astype(BF16), vb)
            den = inter * jnp.sum(q * n_row, axis=1, keepdims=True) + jnp.sum(s, axis=1, keepdims=True)
            hh = num / jnp.maximum(jnp.abs(den), jnp.exp(-m_out))
            hn = hh * lax.rsqrt(jnp.mean(hh * hh, axis=1, keepdims=True) + EPS) * ng
            og = o_ref[pl.ds(r0, CHUNK), lane0:lane0 + 128]
            out_ref[pl.ds(r0, CHUNK), lane0:lane0 + 128] = (hn * _sigmoid(og)).astype(BF16)
            wk = w_col * k
            ct_ref[hd] = decay * ct + _dot_tn(wk.astype(BF16), vb)
            n_ref[hd] = decay * n_row + jnp.sum(wk, axis=0, keepdims=True)
            m_next.append(m_new)
        return tuple(m_next)

    m0 = tuple(jnp.zeros((1, 1), F32) for _ in range(N_HEADS))
    lax.fori_loop(0, n_chunks, chunk_step, m0)


def _mlstm(qk, v, o, gc, gr, cw, ng, batch, seq):
    n_chunks = seq // CHUNK
    blk = lambda: pl.BlockSpec((seq, GROUP_W), lambda b: (b, 0))
    in_specs = [
        blk(), blk(), blk(),
        pl.BlockSpec((seq, N_GATE_ROWS), lambda b: (b, 0)),
        pl.BlockSpec((n_chunks, N_GATE_ROWS, CHUNK), lambda b: (b, 0, 0)),
        pl.BlockSpec((MLSTM_CONV_W, GROUP_W), lambda b: (0, 0)),
        pl.BlockSpec((1, MLSTM_DV), lambda b: (0, 0)),
    ]
    return pl.pallas_call(
        _mlstm_kernel, grid=(batch,), in_specs=in_specs, out_specs=blk(),
        out_shape=jax.ShapeDtypeStruct((batch * seq, GROUP_W), BF16),
        scratch_shapes=[
            pltpu.VMEM((seq + 8, GROUP_W), F32),
            pltpu.VMEM((N_HEADS, MLSTM_DK, MLSTM_DV), F32),
            pltpu.VMEM((N_HEADS, 1, MLSTM_DK), F32),
        ],
        name="mlstm",
        compiler_params=pltpu.CompilerParams(dimension_semantics=("arbitrary",), vmem_limit_bytes=VMEM_LIMIT),
    )(qk, v, o, gc, gr, cw, ng)


def _attn_kernel(lam_init, tbl_ref, q_ref, k_ref, vt_ref, pcol_ref, prow_ref, lq1_ref, lk1_ref, lq2_ref, lk2_ref,
                 sg_ref, out_ref, bias_ref, qc_ref, s_ref, p_ref, acc_ref):
    hd = pl.program_id(0)
    qi = pl.program_id(1)
    b = pl.program_id(2)
    n_kv = (qi + 1) * (TQ // TK)
    dead_tile = k_ref.shape[0] // TK

    @pl.when(b == 0)
    def _build_bias():
        rows = 32
        rowi = lax.broadcasted_iota(jnp.int32, (rows, TQ), 0)
        coli = lax.broadcasted_iota(jnp.int32, (rows, TQ), 1) + qi * TQ
        base = hd * (2 * N_BUCKETS)
        pq = prow_ref[...]

        def build(t, _):
            r0 = pl.multiple_of(t * rows, rows)
            dist = jnp.maximum(pq - pcol_ref[pl.ds(r0, rows), :], 0)
            causal = rowi + r0 <= coli
            b0 = jnp.full((rows, TQ), tbl_ref[base], F32)
            b1 = jnp.full((rows, TQ), tbl_ref[base + N_BUCKETS], F32)
            for v, thr in enumerate(_BUCKET_THR, start=1):
                ge = dist >= thr
                b0 = jnp.where(ge, tbl_ref[base + v], b0)
                b1 = jnp.where(ge, tbl_ref[base + N_BUCKETS + v], b1)
            bias_ref[0, pl.ds(r0, rows), :] = jnp.where(causal, b0, NEG_BIG)
            bias_ref[1, pl.ds(r0, rows), :] = jnp.where(causal, b1, NEG_BIG)
            return 0

        lax.fori_loop(0, n_kv * (TK // rows), build, 0)
        bias_ref[:, pl.ds(dead_tile * TK, TK), :] = jnp.full((2, TK, TQ), NEG_BIG, F32)

    lane = lax.broadcasted_iota(jnp.int32, (TQ, 128), 1)
    q = q_ref[...]
    zero = jnp.zeros_like(q)
    qc_ref[0] = jnp.where(lane < DIFF_D, q, zero)
    qc_ref[1] = jnp.where(lane >= DIFF_D, q, zero)
    acc_ref[...] = jnp.zeros_like(acc_ref)
    s_ref[...] = jnp.full(s_ref.shape, NEG_BIG, F32)
    p_ref[...] = jnp.zeros_like(p_ref)

    def kv_step(t, carry):
        m_state, alpha_state = carry
        v0 = pl.multiple_of(jnp.maximum(t - 2, 0) * TK, TK)
        vt = vt_ref[:, pl.ds(v0, TK)]
        for c in range(2):
            acc_ref[c] = alpha_state[c] * acc_ref[c] + _dot(vt, p_ref[c])
        m_next, alpha_next = [], []
        for c in range(2):
            s = s_ref[c]
            m_new = jnp.maximum(m_state[c], jnp.max(s, axis=0, keepdims=True))
            alpha_next.append(jnp.exp2(m_state[c] - m_new))
            p_ref[c] = jnp.exp2(s - m_new).astype(BF16)
            m_next.append(m_new)
        k0 = pl.multiple_of(jnp.minimum(t, n_kv - 1) * TK, TK)
        b0 = pl.multiple_of(jnp.where(t < n_kv, t, dead_tile) * TK, TK)
        kb = k_ref[pl.ds(k0, TK), :]
        for c in range(2):
            s_ref[c] = _dot_nt(kb, qc_ref[c]) + bias_ref[c, pl.ds(b0, TK), :]
        return tuple(m_next), tuple(alpha_next)

    m0 = tuple(jnp.full((1, TQ), M_INIT, F32) for _ in range(2))
    alpha0 = tuple(jnp.ones((1, TQ), F32) for _ in range(2))
    lax.fori_loop(0, n_kv + 2, kv_step, (m0, alpha0))

    lam = (jnp.exp(jnp.sum(lq1_ref[...] * lk1_ref[...], axis=1, keepdims=True))
           - jnp.exp(jnp.sum(lq2_ref[...] * lk2_ref[...], axis=1, keepdims=True)) + lam_init)
    a0 = acc_ref[0]
    a1 = acc_ref[1]
    o = a0[0:DIFF_DV] / a0[DIFF_DV:DIFF_DV + 1] - lam * (a1[0:DIFF_DV] / a1[DIFF_DV:DIFF_DV + 1])
    on = o * lax.rsqrt(jnp.mean(o * o, axis=0, keepdims=True) + EPS) * sg_ref[...]
    out_ref[...] = (on * (1.0 - lam_init)).T.astype(BF16)


def _attention(lam_init, tbl, qd, kd, vdt, pcol, prow, lq1, lk1, lq2, lk2, sg, batch, seq):
    q3, k3 = (t.reshape(batch, seq, GROUP_W) for t in (qd, kd))
    vec = lambda n: pl.BlockSpec((1, n), lambda h, i, b: (0, 0))
    in_specs = [
        pl.BlockSpec(memory_space=pltpu.SMEM),
        pl.BlockSpec((None, TQ, 128), lambda h, i, b: (b, i, h)),
        pl.BlockSpec((None, seq, 128), lambda h, i, b: (b, 0, h)),
        pl.BlockSpec((V_ROWS, seq), lambda h, i, b: (h, b)),
        pl.BlockSpec((seq, 1), lambda h, i, b: (0, 0)),
        pl.BlockSpec((1, TQ), lambda h, i, b: (0, i)),
        vec(DIFF_D), vec(DIFF_D), vec(DIFF_D), vec(DIFF_D),
        pl.BlockSpec((DIFF_DV, TQ), lambda h, i, b: (0, 0)),
    ]
    out = pl.pallas_call(
        functools.partial(_attn_kernel, lam_init),
        grid=(N_HEADS, seq // TQ, batch), in_specs=in_specs,
        out_specs=pl.BlockSpec((None, TQ, 128), lambda h, i, b: (b, i, h)),
        out_shape=jax.ShapeDtypeStruct((batch, seq, GROUP_W), BF16),
        scratch_shapes=[
            pltpu.VMEM((2, seq + TK, TQ), F32),
            pltpu.VMEM((2, TQ, 128), BF16),
            pltpu.VMEM((2, TK, TQ), F32),
            pltpu.VMEM((2, TK, TQ), BF16),
            pltpu.VMEM((2, V_ROWS, TQ), F32),
        ],
        name="diff_attn",
        compiler_params=pltpu.CompilerParams(
            dimension_semantics=("arbitrary", "arbitrary", "arbitrary"), vmem_limit_bytes=VMEM_LIMIT),
    )(tbl, q3, k3, vdt, pcol, prow, lq1, lk1, lq2, lk2, jnp.broadcast_to(sg.reshape(DIFF_DV, 1), (DIFF_DV, TQ)))
    return out.reshape(batch * seq, GROUP_W)


def _gelu(x):
    return 0.5 * x * (1.0 + lax.erf(x * (2.0 ** -0.5)))


def _rms(x, g):
    return x * lax.rsqrt(jnp.mean(x * x, axis=-1, keepdims=True) + EPS) * g


def _mixer_kernel(h_ref, hm_ref, hd_ref, p_ref, woa_ref, wob_ref, gf_ref, wug_ref, wuv_ref, cwg_ref, cwv_ref,
                  cbg_ref, cbv_ref, wd_ref, gp_ref, wpg_ref, wpp_ref, out_ref,
                  sg_ref, sv_ref, cg_ref, cv_ref, acc_ref):
    tm = h_ref.shape[0]
    halo = 8

    @pl.when(pl.program_id(1) == 0)
    def _reset_conv_history():
        cg_ref[...] = jnp.zeros_like(cg_ref)
        cv_ref[...] = jnp.zeros_like(cv_ref)

    h1 = h_ref[...] + _dot(hm_ref[...], woa_ref[...]) + _dot(hd_ref[...], wob_ref[...])
    u = _rms(h1, gf_ref[...]).astype(BF16)
    acc_ref[...] = jnp.zeros_like(acc_ref)

    def conv_branch(j, w_ref, cw_ref, cb_ref, stage_ref, hist_ref):
        up = _dot(u, w_ref[j])
        stage_ref[0:halo, :] = hist_ref[j]
        stage_ref[halo:, :] = up
        hist_ref[j] = up[tm - halo:, :]
        cw = cw_ref[j]
        return (cw[0:1, :] * stage_ref[halo - 2:halo - 2 + tm, :]
                + cw[1:2, :] * stage_ref[halo - 1:halo - 1 + tm, :]
                + cw[2:3, :] * up + cb_ref[j])

    def ff_step(j, _):
        gate = conv_branch(j, wug_ref, cwg_ref, cbg_ref, sg_ref, cg_ref)
        val = conv_branch(j, wuv_ref, cwv_ref, cbv_ref, sv_ref, cv_ref)
        act = (_gelu(gate) * val).astype(BF16)
        acc_ref[...] += _dot(act, wd_ref[j])
        return 0

    lax.fori_loop(0, N_FF_CHUNKS, ff_step, 0)

    h2 = h1 + acc_ref[...]
    u3 = _rms(h2, gp_ref[...]).astype(BF16)
    ple_gate = _sigmoid(_dot(u3, wpg_ref[...]))
    out_ref[...] = h2 + ple_gate * _dot(p_ref[...].astype(BF16), wpp_ref[...])


def _mixer(h, hm, hd, p, woa, wob, gf, wug, wuv, cwg, cwv, cbg, cbv, wd, gp, wpg, wpp, batch, seq):
    tiles = seq // TM_FFN
    tile = lambda width: pl.BlockSpec((TM_FFN, width), lambda b, t: (b * tiles + t, 0))

    def const(arr):
        nd = arr.ndim
        return pl.BlockSpec(arr.shape, lambda b, t: (0,) * nd, pipeline_mode=pl.Buffered(1))

    weights = (woa, wob, gf, wug, wuv, cwg, cwv, cbg, cbv, wd, gp, wpg, wpp)
    in_specs = [tile(D_MODEL), tile(GROUP_W), tile(GROUP_W), tile(PLE_DIM)] + [const(w) for w in weights]
    return pl.pallas_call(
        _mixer_kernel, grid=(batch, tiles), in_specs=in_specs, out_specs=tile(D_MODEL),
        out_shape=jax.ShapeDtypeStruct(h.shape, F32),
        scratch_shapes=[
            pltpu.VMEM((TM_FFN + 8, FF_CHUNK), F32),
            pltpu.VMEM((TM_FFN + 8, FF_CHUNK), F32),
            pltpu.VMEM((N_FF_CHUNKS, 8, FF_CHUNK), F32),
            pltpu.VMEM((N_FF_CHUNKS, 8, FF_CHUNK), F32),
            pltpu.VMEM((TM_FFN, D_MODEL), F32),
        ],
        name="mixer",
        compiler_params=pltpu.CompilerParams(
            dimension_semantics=("arbitrary", "arbitrary"), vmem_limit_bytes=VMEM_LIMIT),
    )(h, hm, hd, p, *weights)


def _head_interleave(qcols, kcols):
    lead = qcols.shape[:-1]
    qh = qcols.reshape(lead + (N_HEADS, MLSTM_DK))
    kh = kcols.reshape(lead + (N_HEADS, MLSTM_DK))
    return jnp.concatenate([qh, kh], axis=-1).reshape(lead + (GROUP_W,))


def _ff_chunks(w):
    return jnp.transpose(w.reshape(w.shape[0], N_FF_CHUNKS, FF_CHUNK), (1, 0, 2))


def kernel(x, p, positions, rel_bias, ln_mix_g, w_in, mlstm_conv_w, b_igate, b_fgate, mlstm_norm_g, q_norm_g, k_norm_g, lam_q1, lam_k1, lam_q2, lam_k2, diff_subln_g, w_out, ln_ffn_g, w_up, ffn_conv_w, ffn_conv_b, w_down, ln_ple_g, w_ple_gate, w_ple_proj):
    batch, seq, _ = x.shape
    depth = w_in.shape[0]
    n_tok = batch * seq
    qk_cols = N_HEADS * MLSTM_DK
    col_sizes = [qk_cols, qk_cols, GROUP_W, GROUP_W, N_HEADS, N_HEADS, GROUP_W, GROUP_W, GROUP_W]
    offs = np.concatenate([[0], np.cumsum(col_sizes)])
    sl = lambda a, j: a[..., int(offs[j]):int(offs[j + 1])]

    tbl = jnp.transpose(rel_bias.astype(F32), (1, 2, 0)).reshape(-1) * LOG2E
    pcol = positions.astype(jnp.int32).reshape(seq, 1)
    prow = positions.astype(jnp.int32).reshape(1, seq)
    gsum = jnp.asarray(np.kron(np.eye(GROUP_W // DIFF_D), np.ones((DIFF_D, DIFF_D))), BF16)

    h = x.reshape(n_tok, D_MODEL)
    for i in range(depth):
        wi = w_in[i]
        wm = jnp.concatenate(
            [_head_interleave(sl(wi, 0), sl(wi, 1)), sl(wi, 2), sl(wi, 3), sl(wi, 6), sl(wi, 7)],
            axis=-1).astype(BF16)
        wvt = sl(wi, 8).T.astype(BF16)
        wg = jnp.zeros((N_GATE_ROWS, D_MODEL), F32).at[:2 * N_HEADS].set(
            jnp.concatenate([sl(wi, 4), sl(wi, 5)], axis=-1).T).astype(BF16)
        gate_bias = jnp.zeros((N_GATE_ROWS,), F32).at[:2 * N_HEADS].set(
            jnp.concatenate([b_igate[i], b_fgate[i]]).astype(F32))
        qg = jnp.tile(q_norm_g[i].astype(F32), GROUP_W // DIFF_D).reshape(1, GROUP_W) * (DIFF_D ** -0.5 * LOG2E)
        kg = jnp.tile(k_norm_g[i].astype(F32), GROUP_W // DIFF_D).reshape(1, GROUP_W)
        qk, vm, om, gc, gr, qd, kd, vdt = _inproj(
            h, ln_mix_g[i].reshape(1, D_MODEL), wm, wvt, wg, gate_bias.reshape(1, -1), gate_bias.reshape(-1, 1),
            gsum, qg, kg)

        cw = _head_interleave(mlstm_conv_w[i][:, :qk_cols], mlstm_conv_w[i][:, qk_cols:]).astype(F32)
        hm = _mlstm(qk, vm, om, gc, gr, cw, mlstm_norm_g[i].reshape(1, MLSTM_DV).astype(F32), batch, seq)

        lam_init = 0.8 - 0.6 * math.exp(-0.3 * i)
        row64 = lambda a: a[i].reshape(1, DIFF_D).astype(F32)
        hd = _attention(lam_init, tbl, qd, kd, vdt, pcol, prow, row64(lam_q1), row64(lam_k1), row64(lam_q2),
                        row64(lam_k2), diff_subln_g[i].reshape(1, DIFF_DV).astype(F32), batch, seq)

        wo = w_out[i].astype(BF16)
        wu = w_up[i].astype(BF16)
        cwf = ffn_conv_w[i].astype(F32)
        cbf = ffn_conv_b[i].astype(F32).reshape(1, 2 * D_FF)
        h = _mixer(
            h, hm, hd, p[i].reshape(n_tok, PLE_DIM),
            wo[:GROUP_W], wo[GROUP_W:], ln_ffn_g[i].reshape(1, D_MODEL),
            _ff_chunks(wu[:, :D_FF]), _ff_chunks(wu[:, D_FF:]),
            _ff_chunks(cwf[:, :D_FF]), _ff_chunks(cwf[:, D_FF:]),
            _ff_chunks(cbf[:, :D_FF]), _ff_chunks(cbf[:, D_FF:]),
            w_down[i].astype(BF16).reshape(N_FF_CHUNKS, FF_CHUNK, D_MODEL),
            ln_ple_g[i].reshape(1, D_MODEL), w_ple_gate[i].astype(BF16), w_ple_proj[i].astype(BF16),
            batch, seq)
    return h.reshape(batch, seq, D_MODEL)
```

```python
import functools
import math

import numpy as np
import jax
import jax.numpy as jnp
from jax import lax
from jax.experimental import pallas as pl
from jax.experimental.pallas import tpu as pltpu

F32 = jnp.float32
BF16 = jnp.bfloat16

D_MODEL = 1024
N_HEADS = 4
MLSTM_DK = 64
MLSTM_DV = 128
MLSTM_CONV_W = 4
CHUNK = 64
DIFF_D = 64
DIFF_DV = 128
N_BUCKETS = 32
MAX_DISTANCE = 128
D_FF = 2816
FFN_CONV_W = 3
PLE_DIM = 256
EPS = 1e-6
GROUP_W = N_HEADS * 128
N_GATE_ROWS = 16

TM_IN = 512
TM_FFN = 512
FF_CHUNK = 256
N_FF_CHUNKS = D_FF // FF_CHUNK
TQ = 512
TK = 256
NEG_BIG = -1e30
M_INIT = -1e20
ONES_ROWS = 16
V_ROWS = DIFF_DV + ONES_ROWS
LOG2E = math.log2(math.e)
VMEM_LIMIT = 56 * 1024 * 1024


def _dot(a, b):
    return jnp.dot(a, b, preferred_element_type=F32)


def _dot_nt(a, b):
    return lax.dot_general(a, b, (((1,), (1,)), ((), ())), preferred_element_type=F32)


def _dot_tn(a, b):
    return lax.dot_general(a, b, (((0,), (0,)), ((), ())), preferred_element_type=F32)


def _sigmoid(x):
    return 1.0 / (1.0 + jnp.exp(-x))


def _log_sigmoid(x):
    return jnp.minimum(x, 0.0) - jnp.log1p(jnp.exp(-jnp.abs(x)))


def _bucket_thresholds():
    max_exact = N_BUCKETS // 2
    thr = []
    for v in range(1, N_BUCKETS):
        if v <= max_exact:
            thr.append(v)
            continue
        edge = max_exact * (MAX_DISTANCE / max_exact) ** ((v - max_exact) / (N_BUCKETS - max_exact))
        assert abs(edge - round(edge)) > 1e-3, edge
        thr.append(int(math.ceil(edge)))
    assert all(a < b for a, b in zip(thr, thr[1:])), thr
    return tuple(thr)


_BUCKET_THR = _bucket_thresholds()


def _group_mean_square(z, gsum_ref):
    sq = z * z
    hi = sq.astype(BF16)
    lo = (sq - hi.astype(F32)).astype(BF16)
    return (_dot(hi, gsum_ref[...]) + _dot(lo, gsum_ref[...])) * (1.0 / DIFF_D)


def _inproj_kernel(h_ref, g_ref, wm_ref, wvt_ref, wg_ref, bcol_ref, brow_ref, gsum_ref, qg_ref, kg_ref,
                   qk_ref, v_ref, o_ref, gc_ref, gr_ref, qd_ref, kd_ref, vdt_ref):
    x = h_ref[...]
    ms = jnp.mean(x * x, axis=-1, keepdims=True)
    u = (x * lax.rsqrt(ms + EPS) * g_ref[...]).astype(BF16)
    w = GROUP_W
    qk_ref[...] = _dot(u, wm_ref[:, 0:w])
    v_ref[...] = _dot(u, wm_ref[:, w:2 * w]).astype(BF16)
    o_ref[...] = _dot(u, wm_ref[:, 2 * w:3 * w])
    zq = _dot(u, wm_ref[:, 3 * w:4 * w])
    qd_ref[...] = (zq * lax.rsqrt(_group_mean_square(zq, gsum_ref) + EPS) * qg_ref[...]).astype(BF16)
    zk = _dot(u, wm_ref[:, 4 * w:5 * w])
    kd_ref[...] = (zk * lax.rsqrt(_group_mean_square(zk, gsum_ref) + EPS) * kg_ref[...]).astype(BF16)
    vt = _dot_nt(wvt_ref[...], u).astype(BF16)
    for hh in range(N_HEADS):
        vdt_ref[hh * V_ROWS:hh * V_ROWS + DIFF_DV, :] = vt[hh * DIFF_DV:(hh + 1) * DIFF_DV, :]
        vdt_ref[hh * V_ROWS + DIFF_DV:(hh + 1) * V_ROWS, :] = jnp.ones((ONES_ROWS, TM_IN), BF16)
    gc_ref[...] = _dot_nt(u, wg_ref[...]) + bcol_ref[...]
    gr = _dot_nt(wg_ref[...], u) + brow_ref[...]
    for ci in range(TM_IN // CHUNK):
        gr_ref[ci] = gr[:, ci * CHUNK:(ci + 1) * CHUNK]


def _inproj(h, g, wm, wvt, wg, bcol, brow, gsum, qg, kg):
    n_tok = h.shape[0]
    grid = (n_tok // TM_IN,)
    const = lambda shape: pl.BlockSpec(shape, lambda i: (0,) * len(shape))
    tile = lambda width: pl.BlockSpec((TM_IN, width), lambda i: (i, 0))
    out_shape = (
        jax.ShapeDtypeStruct((n_tok, GROUP_W), F32),
        jax.ShapeDtypeStruct((n_tok, GROUP_W), BF16),
        jax.ShapeDtypeStruct((n_tok, GROUP_W), F32),
        jax.ShapeDtypeStruct((n_tok, N_GATE_ROWS), F32),
        jax.ShapeDtypeStruct((n_tok // CHUNK, N_GATE_ROWS, CHUNK), F32),
        jax.ShapeDtypeStruct((n_tok, GROUP_W), BF16),
        jax.ShapeDtypeStruct((n_tok, GROUP_W), BF16),
        jax.ShapeDtypeStruct((N_HEADS * V_ROWS, n_tok), BF16),
    )
    out_specs = (
        tile(GROUP_W), tile(GROUP_W), tile(GROUP_W), tile(N_GATE_ROWS),
        pl.BlockSpec((TM_IN // CHUNK, N_GATE_ROWS, CHUNK), lambda i: (i, 0, 0)),
        tile(GROUP_W), tile(GROUP_W), pl.BlockSpec((N_HEADS * V_ROWS, TM_IN), lambda i: (0, i)),
    )
    in_specs = [
        tile(D_MODEL), const((1, D_MODEL)), const(wm.shape), const(wvt.shape), const(wg.shape),
        const((1, N_GATE_ROWS)), const((N_GATE_ROWS, 1)), const(gsum.shape),
        const((1, GROUP_W)), const((1, GROUP_W)),
    ]
    return pl.pallas_call(
        _inproj_kernel, grid=grid, in_specs=in_specs, out_specs=out_specs, out_shape=out_shape,
        name="inproj",
        compiler_params=pltpu.CompilerParams(dimension_semantics=("arbitrary",), vmem_limit_bytes=VMEM_LIMIT),
    )(h, g, wm, wvt, wg, bcol, brow, gsum, qg, kg)


def _mlstm_kernel(qk_ref, v_ref, o_ref, gc_ref, gr_ref, cw_ref, ng_ref, out_ref, xpad_ref, ct_ref, n_ref):
    seq = qk_ref.shape[0]
    n_chunks = seq // CHUNK
    halo = 8
    xpad_ref[0:halo, :] = jnp.zeros((halo, GROUP_W), F32)
    xpad_ref[halo:, :] = qk_ref[...]
    ct_ref[...] = jnp.zeros_like(ct_ref)
    n_ref[...] = jnp.zeros_like(n_ref)

    row = lax.broadcasted_iota(jnp.int32, (CHUNK, CHUNK), 0)
    col = lax.broadcasted_iota(jnp.int32, (CHUNK, CHUNK), 1)
    tri = col <= row
    cw = cw_ref[...]
    ng = ng_ref[...]

    def chunk_step(c, m_state):
        r0 = pl.multiple_of(c * CHUNK, CHUNK)
        win = xpad_ref[pl.ds(r0, CHUNK + halo), :]
        conv = cw[0:1, :] * win[halo - 3:halo - 3 + CHUNK, :]
        for j in range(1, MLSTM_CONV_W):
            conv = conv + cw[j:j + 1, :] * win[halo - 3 + j:halo - 3 + j + CHUNK, :]
        x = conv * _sigmoid(conv)
        gcc = gc_ref[pl.ds(r0, CHUNK), :]
        grr = gr_ref[c]
        m_next = []
        for hd in range(N_HEADS):
            lane0 = hd * 128
            q = x[:, lane0:lane0 + MLSTM_DK]
            k = x[:, lane0 + MLSTM_DK:lane0 + 2 * MLSTM_DK] * (MLSTM_DK ** -0.5)
            qb = q.astype(BF16)
            kb = k.astype(BF16)
            vb = v_ref[pl.ds(r0, CHUNK), lane0:lane0 + 128]
            i_col = gcc[:, hd:hd + 1]
            lf_col = _log_sigmoid(gcc[:, N_HEADS + hd:N_HEADS + hd + 1])
            i_row = grr[hd:hd + 1, :]
            lf_row = _log_sigmoid(grr[N_HEADS + hd:N_HEADS + hd + 1, :])
            b_col = jnp.sum(jnp.where(tri, lf_row, 0.0), axis=1, keepdims=True)
            b_row = jnp.sum(jnp.where(row <= col, lf_col, 0.0), axis=0, keepdims=True)
            g = jnp.sum(lf_row, axis=1, keepdims=True)
            a_col = g - b_col + i_col
            a_row = g - b_row + i_row
            m_prev = m_state[hd]
            m_new = jnp.maximum(g + m_prev, jnp.max(a_row, axis=1, keepdims=True))
            decay = jnp.exp(g + m_prev - m_new)
            w_col = jnp.exp(a_col - m_new)
            e_col = b_col + m_prev
            dmat = jnp.where(tri, b_col - b_row + i_row, -jnp.inf)
            m_out = jnp.maximum(e_col, jnp.max(dmat, axis=1, keepdims=True))
            s = _dot_nt(qb, kb) * jnp.exp(dmat - m_out)
            inter = jnp.exp(e_col - m_out)
            ct = ct_ref[hd]
            n_row = n_ref[hd]
            num = inter * _dot(qb, ct.astype(BF16)) + _dot(s.astype(BF16), vb)
            den = inter * jnp.sum(q * n_row, axis=1, keepdims=True) + jnp.sum(s, axis=1, keepdims=True)
            hh = num / jnp.maximum(jnp.abs(den), jnp.exp(-m_out))
            hn = hh * lax.rsqrt(jnp.mean(hh * hh, axis=1, keepdims=True) + EPS) * ng
            og = o_ref[pl.ds(r0, CHUNK), lane0:lane0 + 128]
            out_ref[pl.ds(r0, CHUNK), lane0:lane0 + 128] = (hn * _sigmoid(og)).astype(BF16)
            wk = w_col * k
            ct_ref[hd] = decay * ct + _dot_tn(wk.astype(BF16), vb)
            n_ref[hd] = decay * n_row + jnp.sum(wk, axis=0, keepdims=True)
            m_next.append(m_new)
        return tuple(m_next)

    m0 = tuple(jnp.zeros((1, 1), F32) for _ in range(N_HEADS))
    lax.fori_loop(0, n_chunks, chunk_step, m0)


def _mlstm(qk, v, o, gc, gr, cw, ng, batch, seq):
    n_chunks = seq // CHUNK
    blk = lambda: pl.BlockSpec((seq, GROUP_W), lambda b: (b, 0))
    in_specs = [
        blk(), blk(), blk(),
        pl.BlockSpec((seq, N_GATE_ROWS), lambda b: (b, 0)),
        pl.BlockSpec((n_chunks, N_GATE_ROWS, CHUNK), lambda b: (b, 0, 0)),
        pl.BlockSpec((MLSTM_CONV_W, GROUP_W), lambda b: (0, 0)),
        pl.BlockSpec((1, MLSTM_DV), lambda b: (0, 0)),
    ]
    return pl.pallas_call(
        _mlstm_kernel, grid=(batch,), in_specs=in_specs, out_specs=blk(),
        out_shape=jax.ShapeDtypeStruct((batch * seq, GROUP_W), BF16),
        scratch_shapes=[
            pltpu.VMEM((seq + 8, GROUP_W), F32),
            pltpu.VMEM((N_HEADS, MLSTM_DK, MLSTM_DV), F32),
            pltpu.VMEM((N_HEADS, 1, MLSTM_DK), F32),
        ],
        name="mlstm",
        compiler_params=pltpu.CompilerParams(dimension_semantics=("arbitrary",), vmem_limit_bytes=VMEM_LIMIT),
    )(qk, v, o, gc, gr, cw, ng)


def _attn_kernel(lam_init, tbl_ref, q_ref, k_ref, vt_ref, pcol_ref, prow_ref, lq1_ref, lk1_ref, lq2_ref, lk2_ref,
                 sg_ref, out_ref, bias_ref, qc_ref, s_ref, p_ref, acc_ref):
    hd = pl.program_id(0)
    qi = pl.program_id(1)
    b = pl.program_id(2)
    n_kv = (qi + 1) * (TQ // TK)
    dead_tile = k_ref.shape[0] // TK

    @pl.when(b == 0)
    def _build_bias():
        rows = 32
        rowi = lax.broadcasted_iota(jnp.int32, (rows, TQ), 0)
        coli = lax.broadcasted_iota(jnp.int32, (rows, TQ), 1) + qi * TQ
        base = hd * (2 * N_BUCKETS)
        pq = prow_ref[...]

        def build(t, _):
            r0 = pl.multiple_of(t * rows, rows)
            dist = jnp.maximum(pq - pcol_ref[pl.ds(r0, rows), :], 0)
            causal = rowi + r0 <= coli
            b0 = jnp.full((rows, TQ), tbl_ref[base], F32)
            b1 = jnp.full((rows, TQ), tbl_ref[base + N_BUCKETS], F32)
            for v, thr in enumerate(_BUCKET_THR, start=1):
                ge = dist >= thr
                b0 = jnp.where(ge, tbl_ref[base + v], b0)
                b1 = jnp.where(ge, tbl_ref[base + N_BUCKETS + v], b1)
            bias_ref[0, pl.ds(r0, rows), :] = jnp.where(causal, b0, NEG_BIG)
            bias_ref[1, pl.ds(r0, rows), :] = jnp.where(causal, b1, NEG_BIG)
            return 0

        lax.fori_loop(0, n_kv * (TK // rows), build, 0)
        bias_ref[:, pl.ds(dead_tile * TK, TK), :] = jnp.full((2, TK, TQ), NEG_BIG, F32)

    lane = lax.broadcasted_iota(jnp.int32, (TQ, 128), 1)
    q = q_ref[...]
    zero = jnp.zeros_like(q)
    qc_ref[0] = jnp.where(lane < DIFF_D, q, zero)
    qc_ref[1] = jnp.where(lane >= DIFF_D, q, zero)
    acc_ref[...] = jnp.zeros_like(acc_ref)
    s_ref[...] = jnp.full(s_ref.shape, NEG_BIG, F32)
    p_ref[...] = jnp.zeros_like(p_ref)

    def kv_step(t, carry):
        m_state, alpha_state = carry
        v0 = pl.multiple_of(jnp.maximum(t - 2, 0) * TK, TK)
        vt = vt_ref[:, pl.ds(v0, TK)]
        for c in range(2):
            acc_ref[c] = alpha_state[c] * acc_ref[c] + _dot(vt, p_ref[c])
        m_next, alpha_next = [], []
        for c in range(2):
            s = s_ref[c]
            m_new = jnp.maximum(m_state[c], jnp.max(s, axis=0, keepdims=True))
            alpha_next.append(jnp.exp2(m_state[c] - m_new))
            p_ref[c] = jnp.exp2(s - m_new).astype(BF16)
            m_next.append(m_new)
        k0 = pl.multiple_of(jnp.minimum(t, n_kv - 1) * TK, TK)
        b0 = pl.multiple_of(jnp.where(t < n_kv, t, dead_tile) * TK, TK)
        kb = k_ref[pl.ds(k0, TK), :]
        for c in range(2):
            s_ref[c] = _dot_nt(kb, qc_ref[c]) + bias_ref[c, pl.ds(b0, TK), :]
        return tuple(m_next), tuple(alpha_next)

    m0 = tuple(jnp.full((1, TQ), M_INIT, F32) for _ in range(2))
    alpha0 = tuple(jnp.ones((1, TQ), F32) for _ in range(2))
    lax.fori_loop(0, n_kv + 2, kv_step, (m0, alpha0))

    lam = (jnp.exp(jnp.sum(lq1_ref[...] * lk1_ref[...], axis=1, keepdims=True))
           - jnp.exp(jnp.sum(lq2_ref[...] * lk2_ref[...], axis=1, keepdims=True)) + lam_init)
    a0 = acc_ref[0]
    a1 = acc_ref[1]
    o = a0[0:DIFF_DV] / a0[DIFF_DV:DIFF_DV + 1] - lam * (a1[0:DIFF_DV] / a1[DIFF_DV:DIFF_DV + 1])
    on = o * lax.rsqrt(jnp.mean(o * o, axis=0, keepdims=True) + EPS) * sg_ref[...]
    out_ref[...] = (on * (1.0 - lam_init)).T.astype(BF16)


def _attention(lam_init, tbl, qd, kd, vdt, pcol, prow, lq1, lk1, lq2, lk2, sg, batch, seq):
    q3, k3 = (t.reshape(batch, seq, GROUP_W) for t in (qd, kd))
    vec = lambda n: pl.BlockSpec((1, n), lambda h, i, b: (0, 0))
    in_specs = [
        pl.BlockSpec(memory_space=pltpu.SMEM),
        pl.BlockSpec((None, TQ, 128), lambda h, i, b: (b, i, h)),
        pl.BlockSpec((None, seq, 128), lambda h, i, b: (b, 0, h)),
        pl.BlockSpec((V_ROWS, seq), lambda h, i, b: (h, b)),
        pl.BlockSpec((seq, 1), lambda h, i, b: (0, 0)),
        pl.BlockSpec((1, TQ), lambda h, i, b: (0, i)),
        vec(DIFF_D), vec(DIFF_D), vec(DIFF_D), vec(DIFF_D),
        pl.BlockSpec((DIFF_DV, TQ), lambda h, i, b: (0, 0)),
    ]
    out = pl.pallas_call(
        functools.partial(_attn_kernel, lam_init),
        grid=(N_HEADS, seq // TQ, batch), in_specs=in_specs,
        out_specs=pl.BlockSpec((None, TQ, 128), lambda h, i, b: (b, i, h)),
        out_shape=jax.ShapeDtypeStruct((batch, seq, GROUP_W), BF16),
        scratch_shapes=[
            pltpu.VMEM((2, seq + TK, TQ), F32),
            pltpu.VMEM((2, TQ, 128), BF16),
            pltpu.VMEM((2, TK, TQ), F32),
            pltpu.VMEM((2, TK, TQ), BF16),
            pltpu.VMEM((2, V_ROWS, TQ), F32),
        ],
        name="diff_attn",
        compiler_params=pltpu.CompilerParams(
            dimension_semantics=("arbitrary", "arbitrary", "arbitrary"), vmem_limit_bytes=VMEM_LIMIT),
    )(tbl, q3, k3, vdt, pcol, prow, lq1, lk1, lq2, lk2, jnp.broadcast_to(sg.reshape(DIFF_DV, 1), (DIFF_DV, TQ)))
    return out.reshape(batch * seq, GROUP_W)


def _gelu(x):
    return 0.5 * x * (1.0 + lax.erf(x * (2.0 ** -0.5)))


def _rms(x, g):
    return x * lax.rsqrt(jnp.mean(x * x, axis=-1, keepdims=True) + EPS) * g


def _mixer_kernel(h_ref, hm_ref, hd_ref, p_ref, woa_ref, wob_ref, gf_ref, wug_ref, wuv_ref, cwg_ref, cwv_ref,
                  cbg_ref, cbv_ref, wd_ref, gp_ref, wpg_ref, wpp_ref, out_ref,
                  sg_ref, sv_ref, cg_ref, cv_ref, acc_ref, u_ref):
    tm = h_ref.shape[0]
    halo = 8

    @pl.when(pl.program_id(1) == 0)
    def _reset_conv_history():
        cg_ref[...] = jnp.zeros_like(cg_ref)
        cv_ref[...] = jnp.zeros_like(cv_ref)

    h1 = h_ref[...] + _dot(hm_ref[...], woa_ref[...]) + _dot(hd_ref[...], wob_ref[...])
    u_ref[...] = _rms(h1, gf_ref[...]).astype(BF16)
    acc_ref[...] = jnp.zeros_like(acc_ref)

    def up_stage(j, slot):
        sg_ref[slot, halo:, :] = _dot(u_ref[...], wug_ref[j])
        sv_ref[slot, halo:, :] = _dot(u_ref[...], wuv_ref[j])

    def conv_branch(j, slot, cw_ref, cb_ref, stage_ref, hist_ref):
        stage_ref[slot, 0:halo, :] = hist_ref[j]
        hist_ref[j] = stage_ref[slot, tm:tm + halo, :]
        cw = cw_ref[j]
        return (cw[0:1, :] * stage_ref[slot, halo - 2:halo - 2 + tm, :]
                + cw[1:2, :] * stage_ref[slot, halo - 1:halo - 1 + tm, :]
                + cw[2:3, :] * stage_ref[slot, halo:, :] + cb_ref[j])

    def act_stage(j, slot):
        gate = conv_branch(j, slot, cwg_ref, cbg_ref, sg_ref, cg_ref)
        val = conv_branch(j, slot, cwv_ref, cbv_ref, sv_ref, cv_ref)
        act = (_gelu(gate) * val).astype(BF16)
        acc_ref[...] += _dot(act, wd_ref[j])

    def chunk_pair(i, _):
        j = 2 * i
        up_stage(j + 1, 1)
        act_stage(j, 0)
        up_stage(j + 2, 0)
        act_stage(j + 1, 1)
        return 0

    assert N_FF_CHUNKS % 2 == 1
    up_stage(0, 0)
    lax.fori_loop(0, N_FF_CHUNKS // 2, chunk_pair, 0)
    act_stage(N_FF_CHUNKS - 1, 0)

    h2 = h1 + acc_ref[...]
    u3 = _rms(h2, gp_ref[...]).astype(BF16)
    ple_gate = _sigmoid(_dot(u3, wpg_ref[...]))
    out_ref[...] = h2 + ple_gate * _dot(p_ref[...].astype(BF16), wpp_ref[...])


def _mixer(h, hm, hd, p, woa, wob, gf, wug, wuv, cwg, cwv, cbg, cbv, wd, gp, wpg, wpp, batch, seq):
    tiles = seq // TM_FFN
    tile = lambda width: pl.BlockSpec((TM_FFN, width), lambda b, t: (b * tiles + t, 0))

    def const(arr):
        nd = arr.ndim
        return pl.BlockSpec(arr.shape, lambda b, t: (0,) * nd, pipeline_mode=pl.Buffered(1))

    weights = (woa, wob, gf, wug, wuv, cwg, cwv, cbg, cbv, wd, gp, wpg, wpp)
    in_specs = [tile(D_MODEL), tile(GROUP_W), tile(GROUP_W), tile(PLE_DIM)] + [const(w) for w in weights]
    return pl.pallas_call(
        _mixer_kernel, grid=(batch, tiles), in_specs=in_specs, out_specs=tile(D_MODEL),
        out_shape=jax.ShapeDtypeStruct(h.shape, F32),
        scratch_shapes=[
            pltpu.VMEM((2, TM_FFN + 8, FF_CHUNK), F32),
            pltpu.VMEM((2, TM_FFN + 8, FF_CHUNK), F32),
            pltpu.VMEM((N_FF_CHUNKS, 8, FF_CHUNK), F32),
            pltpu.VMEM((N_FF_CHUNKS, 8, FF_CHUNK), F32),
            pltpu.VMEM((TM_FFN, D_MODEL), F32),
            pltpu.VMEM((TM_FFN, D_MODEL), BF16),
        ],
        name="mixer",
        compiler_params=pltpu.CompilerParams(
            dimension_semantics=("arbitrary", "arbitrary"), vmem_limit_bytes=VMEM_LIMIT),
    )(h, hm, hd, p, *weights)


def _head_interleave(qcols, kcols):
    lead = qcols.shape[:-1]
    qh = qcols.reshape(lead + (N_HEADS, MLSTM_DK))
    kh = kcols.reshape(lead + (N_HEADS, MLSTM_DK))
    return jnp.concatenate([qh, kh], axis=-1).reshape(lead + (GROUP_W,))


def _ff_chunks(w):
    return jnp.transpose(w.reshape(w.shape[0], N_FF_CHUNKS, FF_CHUNK), (1, 0, 2))


def kernel(x, p, positions, rel_bias, ln_mix_g, w_in, mlstm_conv_w, b_igate, b_fgate, mlstm_norm_g, q_norm_g, k_norm_g, lam_q1, lam_k1, lam_q2, lam_k2, diff_subln_g, w_out, ln_ffn_g, w_up, ffn_conv_w, ffn_conv_b, w_down, ln_ple_g, w_ple_gate, w_ple_proj):
    batch, seq, _ = x.shape
    depth = w_in.shape[0]
    n_tok = batch * seq
    qk_cols = N_HEADS * MLSTM_DK
    col_sizes = [qk_cols, qk_cols, GROUP_W, GROUP_W, N_HEADS, N_HEADS, GROUP_W, GROUP_W, GROUP_W]
    offs = np.concatenate([[0], np.cumsum(col_sizes)])
    sl = lambda a, j: a[..., int(offs[j]):int(offs[j + 1])]

    tbl = jnp.transpose(rel_bias.astype(F32), (1, 2, 0)).reshape(-1) * LOG2E
    pcol = positions.astype(jnp.int32).reshape(seq, 1)
    prow = positions.astype(jnp.int32).reshape(1, seq)
    gsum = jnp.asarray(np.kron(np.eye(GROUP_W // DIFF_D), np.ones((DIFF_D, DIFF_D))), BF16)

    h = x.reshape(n_tok, D_MODEL)
    for i in range(depth):
        wi = w_in[i]
        wm = jnp.concatenate(
            [_head_interleave(sl(wi, 0), sl(wi, 1)), sl(wi, 2), sl(wi, 3), sl(wi, 6), sl(wi, 7)],
            axis=-1).astype(BF16)
        wvt = sl(wi, 8).T.astype(BF16)
        wg = jnp.zeros((N_GATE_ROWS, D_MODEL), F32).at[:2 * N_HEADS].set(
            jnp.concatenate([sl(wi, 4), sl(wi, 5)], axis=-1).T).astype(BF16)
        gate_bias = jnp.zeros((N_GATE_ROWS,), F32).at[:2 * N_HEADS].set(
            jnp.concatenate([b_igate[i], b_fgate[i]]).astype(F32))
        qg = jnp.tile(q_norm_g[i].astype(F32), GROUP_W // DIFF_D).reshape(1, GROUP_W) * (DIFF_D ** -0.5 * LOG2E)
        kg = jnp.tile(k_norm_g[i].astype(F32), GROUP_W // DIFF_D).reshape(1, GROUP_W)
        qk, vm, om, gc, gr, qd, kd, vdt = _inproj(
            h, ln_mix_g[i].reshape(1, D_MODEL), wm, wvt, wg, gate_bias.reshape(1, -1), gate_bias.reshape(-1, 1),
            gsum, qg, kg)

        cw = _head_interleave(mlstm_conv_w[i][:, :qk_cols], mlstm_conv_w[i][:, qk_cols:]).astype(F32)
        hm = _mlstm(qk, vm, om, gc, gr, cw, mlstm_norm_g[i].reshape(1, MLSTM_DV).astype(F32), batch, seq)

        lam_init = 0.8 - 0.6 * math.exp(-0.3 * i)
        row64 = lambda a: a[i].reshape(1, DIFF_D).astype(F32)
        hd = _attention(lam_init, tbl, qd, kd, vdt, pcol, prow, row64(lam_q1), row64(lam_k1), row64(lam_q2),
                        row64(lam_k2), diff_subln_g[i].reshape(1, DIFF_DV).astype(F32), batch, seq)

        wo = w_out[i].astype(BF16)
        wu = w_up[i].astype(BF16)
        cwf = ffn_conv_w[i].astype(F32)
        cbf = ffn_conv_b[i].astype(F32).reshape(1, 2 * D_FF)
        h = _mixer(
            h, hm, hd, p[i].reshape(n_tok, PLE_DIM),
            wo[:GROUP_W], wo[GROUP_W:], ln_ffn_g[i].reshape(1, D_MODEL),
            _ff_chunks(wu[:, :D_FF]), _ff_chunks(wu[:, D_FF:]),
            _ff_chunks(cwf[:, :D_FF]), _ff_chunks(cwf[:, D_FF:]),
            _ff_chunks(cbf[:, :D_FF]), _ff_chunks(cbf[:, D_FF:]),
            w_down[i].astype(BF16).reshape(N_FF_CHUNKS, FF_CHUNK, D_MODEL),
            ln_ple_g[i].reshape(1, D_MODEL), w_ple_gate[i].astype(BF16), w_ple_proj[i].astype(BF16),
            batch, seq)
    return h.reshape(batch, seq, D_MODEL)
```

```python
import functools
import math

import numpy as np
import jax
import jax.numpy as jnp
from jax import lax
from jax.experimental import pallas as pl
from jax.experimental.pallas import tpu as pltpu

F32 = jnp.float32
BF16 = jnp.bfloat16

D_MODEL = 1024
N_HEADS = 4
MLSTM_DK = 64
MLSTM_DV = 128
MLSTM_CONV_W = 4
CHUNK = 64
PAIR = 2 * CHUNK
DIFF_D = 64
DIFF_DV = 128
N_BUCKETS = 32
MAX_DISTANCE = 128
D_FF = 2816
FFN_CONV_W = 3
PLE_DIM = 256
EPS = 1e-6
GROUP_W = N_HEADS * 128
N_GATE_ROWS = 16

TM_IN = 512
TM_FFN = 512
FF_CHUNK = 256
N_FF_CHUNKS = D_FF // FF_CHUNK
TQ = 512
TK = 256
NEG_BIG = -1e30
M_INIT = -1e20
ONES_ROWS = 16
V_ROWS = DIFF_DV + ONES_ROWS
LOG2E = math.log2(math.e)
VMEM_LIMIT = 56 * 1024 * 1024


def _dot(a, b):
    return jnp.dot(a, b, preferred_element_type=F32)


def _dot_nt(a, b):
    return lax.dot_general(a, b, (((1,), (1,)), ((), ())), preferred_element_type=F32)


def _dot_tn(a, b):
    return lax.dot_general(a, b, (((0,), (0,)), ((), ())), preferred_element_type=F32)


def _sigmoid(x):
    return 1.0 / (1.0 + jnp.exp(-x))


def _log_sigmoid(x):
    return jnp.minimum(x, 0.0) - jnp.log1p(jnp.exp(-jnp.abs(x)))


def _bucket_thresholds():
    max_exact = N_BUCKETS // 2
    thr = []
    for v in range(1, N_BUCKETS):
        if v <= max_exact:
            thr.append(v)
            continue
        edge = max_exact * (MAX_DISTANCE / max_exact) ** ((v - max_exact) / (N_BUCKETS - max_exact))
        assert abs(edge - round(edge)) > 1e-3, edge
        thr.append(int(math.ceil(edge)))
    assert all(a < b for a, b in zip(thr, thr[1:])), thr
    return tuple(thr)


_BUCKET_THR = _bucket_thresholds()


def _group_mean_square(z, gsum_ref):
    sq = z * z
    hi = sq.astype(BF16)
    lo = (sq - hi.astype(F32)).astype(BF16)
    return (_dot(hi, gsum_ref[...]) + _dot(lo, gsum_ref[...])) * (1.0 / DIFF_D)


def _with_ones_rows(dst_ref, vt):
    for hh in range(N_HEADS):
        dst_ref[hh * V_ROWS:hh * V_ROWS + DIFF_DV, :] = vt[hh * DIFF_DV:(hh + 1) * DIFF_DV, :]
        dst_ref[hh * V_ROWS + DIFF_DV:(hh + 1) * V_ROWS, :] = jnp.ones((ONES_ROWS, vt.shape[1]), BF16)


def _inproj_kernel(h_ref, g_ref, wm_ref, wt_ref, wg_ref, bcol_ref, brow_ref, gsum_ref, qg_ref, kg_ref,
                   qk_ref, vmt_ref, omt_ref, gri_ref, grf_ref, gci_ref, gcf_ref, qd_ref, kd_ref, vdt_ref):
    x = h_ref[...]
    ms = jnp.mean(x * x, axis=-1, keepdims=True)
    u = (x * lax.rsqrt(ms + EPS) * g_ref[...]).astype(BF16)
    w = GROUP_W
    qk_ref[...] = _dot(u, wm_ref[:, 0:w])
    zq = _dot(u, wm_ref[:, w:2 * w])
    qd_ref[...] = (zq * lax.rsqrt(_group_mean_square(zq, gsum_ref) + EPS) * qg_ref[...]).astype(BF16)
    zk = _dot(u, wm_ref[:, 2 * w:3 * w])
    kd_ref[...] = (zk * lax.rsqrt(_group_mean_square(zk, gsum_ref) + EPS) * kg_ref[...]).astype(BF16)
    _with_ones_rows(vmt_ref, _dot_nt(wt_ref[0:w, :], u).astype(BF16))
    omt_ref[...] = _dot_nt(wt_ref[w:2 * w, :], u)
    _with_ones_rows(vdt_ref, _dot_nt(wt_ref[2 * w:3 * w, :], u).astype(BF16))
    gc = _dot_nt(u, wg_ref[...]) + bcol_ref[...]
    gci_ref[...] = gc[:, 0:N_GATE_ROWS]
    gcf_ref[...] = gc[:, N_GATE_ROWS:2 * N_GATE_ROWS]
    gr = _dot_nt(wg_ref[...], u) + brow_ref[...]
    for ci in range(TM_IN // PAIR):
        gri_ref[ci] = gr[0:8, ci * PAIR:(ci + 1) * PAIR]
        grf_ref[ci] = gr[N_GATE_ROWS:N_GATE_ROWS + 8, ci * PAIR:(ci + 1) * PAIR]


def _inproj(h, g, wm, wt, wg, bcol, brow, gsum, qg, kg):
    n_tok = h.shape[0]
    grid = (n_tok // TM_IN,)
    const = lambda shape: pl.BlockSpec(shape, lambda i: (0,) * len(shape))
    tile = lambda width: pl.BlockSpec((TM_IN, width), lambda i: (i, 0))
    tile_t = lambda rows: pl.BlockSpec((rows, TM_IN), lambda i: (0, i))
    gate_rows = pl.BlockSpec((TM_IN // PAIR, 8, PAIR), lambda i: (i, 0, 0))
    out_shape = (
        jax.ShapeDtypeStruct((n_tok, GROUP_W), F32),
        jax.ShapeDtypeStruct((N_HEADS * V_ROWS, n_tok), BF16),
        jax.ShapeDtypeStruct((GROUP_W, n_tok), F32),
        jax.ShapeDtypeStruct((n_tok // PAIR, 8, PAIR), F32),
        jax.ShapeDtypeStruct((n_tok // PAIR, 8, PAIR), F32),
        jax.ShapeDtypeStruct((n_tok, N_GATE_ROWS), F32),
        jax.ShapeDtypeStruct((n_tok, N_GATE_ROWS), F32),
        jax.ShapeDtypeStruct((n_tok, GROUP_W), BF16),
        jax.ShapeDtypeStruct((n_tok, GROUP_W), BF16),
        jax.ShapeDtypeStruct((N_HEADS * V_ROWS, n_tok), BF16),
    )
    out_specs = (
        tile(GROUP_W), tile_t(N_HEADS * V_ROWS), tile_t(GROUP_W), gate_rows, gate_rows,
        tile(N_GATE_ROWS), tile(N_GATE_ROWS), tile(GROUP_W), tile(GROUP_W), tile_t(N_HEADS * V_ROWS),
    )
    in_specs = [
        tile(D_MODEL), const((1, D_MODEL)), const(wm.shape), const(wt.shape), const(wg.shape),
        const(bcol.shape), const(brow.shape), const(gsum.shape),
        const((1, GROUP_W)), const((1, GROUP_W)),
    ]
    return pl.pallas_call(
        _inproj_kernel, grid=grid, in_specs=in_specs, out_specs=out_specs, out_shape=out_shape,
        name="inproj",
        compiler_params=pltpu.CompilerParams(dimension_semantics=("arbitrary",), vmem_limit_bytes=VMEM_LIMIT),
    )(h, g, wm, wt, wg, bcol, brow, gsum, qg, kg)


def _split_hi_lo(x):
    hi = x.astype(BF16)
    return hi, (x - hi.astype(F32)).astype(BF16)


def _mlstm_kernel(qk_ref, vt_ref, ot_ref, gri_ref, grf_ref, gci_ref, gcf_ref, cw_ref, ng_ref, ug_ref, l2_ref,
                  out_ref, xpad_ref, cn_ref):
    seq = qk_ref.shape[0]
    halo = 8
    xpad_ref[0:halo, :] = jnp.zeros((halo, GROUP_W), F32)
    xpad_ref[halo:, :] = qk_ref[...]
    cn_ref[...] = jnp.zeros_like(cn_ref)

    first8 = lax.broadcasted_iota(jnp.int32, (8, PAIR), 1) < CHUNK
    first = lax.broadcasted_iota(jnp.int32, (1, PAIR), 1) < CHUNK
    upper = lax.broadcasted_iota(jnp.int32, (PAIR, PAIR), 1) >= MLSTM_DK
    key_t = lax.broadcasted_iota(jnp.int32, (PAIR, PAIR), 0)
    qry_t = lax.broadcasted_iota(jnp.int32, (PAIR, PAIR), 1)
    chunk_start = jnp.where(qry_t < CHUNK, 0, CHUNK)
    cw = cw_ref[...]
    neg_inf = jnp.float32(-jnp.inf)

    def pair_step(c2, m_prev):
        r0 = pl.multiple_of(c2 * PAIR, PAIR)
        win = xpad_ref[pl.ds(r0, PAIR + halo), :]
        conv = cw[0:1, :] * win[halo - 3:halo - 3 + PAIR, :]
        for j in range(1, MLSTM_CONV_W):
            conv = conv + cw[j:j + 1, :] * win[halo - 3 + j:halo - 3 + j + PAIR, :]
        x = conv * _sigmoid(conv)

        gi = gri_ref[c2]
        lf_hi, lf_lo = _split_hi_lo(_log_sigmoid(grf_ref[c2]))
        bg = _dot(lf_hi, ug_ref[...]) + _dot(lf_lo, ug_ref[...])
        b_r = bg[:, 0:PAIR]
        g_a = bg[:, PAIR:2 * PAIR]
        g_b = bg[:, 2 * PAIR:3 * PAIR]
        a_r = jnp.where(first8, g_a, g_b) - b_r + gi
        max_a = jnp.max(jnp.where(first8, a_r, neg_inf), axis=1, keepdims=True)
        max_b = jnp.max(jnp.where(first8, neg_inf, a_r), axis=1, keepdims=True)
        m_a = jnp.maximum(g_a + m_prev, max_a)
        m_b = jnp.maximum(g_b + m_a, max_b)
        dec_a = jnp.exp(g_a + m_prev - m_a)
        dec_b = jnp.exp(g_b + m_a - m_b)
        w_r = jnp.exp(a_r - jnp.where(first8, m_a, m_b))
        e_r = b_r + jnp.where(first8, m_prev, m_a)
        lc_hi, lc_lo = _split_hi_lo(_log_sigmoid(gcf_ref[pl.ds(r0, PAIR), :]))
        x_c = _dot(l2_ref[...], lc_hi) + _dot(l2_ref[...], lc_lo) - gci_ref[pl.ds(r0, PAIR), :]

        for hd in range(N_HEADS):
            rows = slice(hd * MLSTM_DV, (hd + 1) * MLSTM_DV)
            xh = x[:, hd * 128:(hd + 1) * 128]
            xs = pltpu.roll(xh, MLSTM_DK, 1)
            xk = jnp.where(upper, xh * (MLSTM_DK ** -0.5), 0.0).astype(BF16)
            xq = jnp.where(upper, xs, 0.0).astype(BF16)
            vt = vt_ref[hd * V_ROWS:(hd + 1) * V_ROWS, pl.ds(r0, PAIR)]
            vt32 = vt.astype(F32)
            w_h = w_r[hd:hd + 1, :]
            cn0 = cn_ref[hd]
            up_a = _dot((vt32 * jnp.where(first, w_h, 0.0)).astype(BF16), xk)
            cn1 = dec_a[hd:hd + 1, :] * cn0 + up_a
            up_b = _dot((vt32 * jnp.where(first, 0.0, w_h)).astype(BF16), xk)
            cn_ref[hd] = dec_b[hd:hd + 1, :] * cn1 + up_b
            dmat = b_r[hd:hd + 1, :] - x_c[:, hd:hd + 1]
            dmat = jnp.where(key_t <= qry_t, jnp.where(key_t >= chunk_start, dmat, neg_inf), neg_inf)
            e_h = e_r[hd:hd + 1, :]
            m_out = jnp.maximum(e_h, jnp.max(dmat, axis=0, keepdims=True))
            s_t = (_dot_nt(xk, xq) * jnp.exp(dmat - m_out)).astype(BF16)
            inter = jnp.exp(e_h - m_out)
            cq = jnp.where(first, _dot_nt(cn0.astype(BF16), xq), _dot_nt(cn1.astype(BF16), xq))
            tot = inter * cq + _dot(vt, s_t)
            den = tot[MLSTM_DV:MLSTM_DV + 1, :]
            hh = tot[0:MLSTM_DV, :] / jnp.maximum(jnp.abs(den), jnp.exp(-m_out))
            hn = hh * lax.rsqrt(jnp.mean(hh * hh, axis=0, keepdims=True) + EPS) * ng_ref[...]
            og = ot_ref[rows, pl.ds(r0, PAIR)]
            out_ref[rows, pl.ds(r0, PAIR)] = (hn * _sigmoid(og)).astype(BF16)
        return m_b

    lax.fori_loop(0, seq // PAIR, pair_step, jnp.zeros((8, PAIR), F32))


def _mlstm(qk, vmt, omt, gri, grf, gci, gcf, cw, ng, ug, l2, batch, seq):
    const = lambda arr: pl.BlockSpec(arr.shape, lambda b: (0,) * arr.ndim)
    in_specs = [
        pl.BlockSpec((seq, GROUP_W), lambda b: (b, 0)),
        pl.BlockSpec((N_HEADS * V_ROWS, seq), lambda b: (0, b)),
        pl.BlockSpec((GROUP_W, seq), lambda b: (0, b)),
        pl.BlockSpec((seq // PAIR, 8, PAIR), lambda b: (b, 0, 0)),
        pl.BlockSpec((seq // PAIR, 8, PAIR), lambda b: (b, 0, 0)),
        pl.BlockSpec((seq, N_GATE_ROWS), lambda b: (b, 0)),
        pl.BlockSpec((seq, N_GATE_ROWS), lambda b: (b, 0)),
        const(cw), const(ng), const(ug), const(l2),
    ]
    return pl.pallas_call(
        _mlstm_kernel, grid=(batch,), in_specs=in_specs,
        out_specs=pl.BlockSpec((GROUP_W, seq), lambda b: (0, b)),
        out_shape=jax.ShapeDtypeStruct((GROUP_W, batch * seq), BF16),
        scratch_shapes=[
            pltpu.VMEM((seq + 8, GROUP_W), F32),
            pltpu.VMEM((N_HEADS, V_ROWS, 128), F32),
        ],
        name="mlstm",
        compiler_params=pltpu.CompilerParams(dimension_semantics=("arbitrary",), vmem_limit_bytes=VMEM_LIMIT),
    )(qk, vmt, omt, gri, grf, gci, gcf, cw, ng, ug, l2)


def _attn_kernel(lam_init, tbl_ref, q_ref, k_ref, vt_ref, pcol_ref, prow_ref, lq1_ref, lk1_ref, lq2_ref, lk2_ref,
                 sg_ref, out_ref, bias_ref, qc_ref, s_ref, p_ref, acc_ref):
    hd = pl.program_id(0)
    qi = pl.program_id(1)
    b = pl.program_id(2)
    n_kv = (qi + 1) * (TQ // TK)
    dead_tile = k_ref.shape[0] // TK

    @pl.when(b == 0)
    def _build_bias():
        rows = 32
        rowi = lax.broadcasted_iota(jnp.int32, (rows, TQ), 0)
        coli = lax.broadcasted_iota(jnp.int32, (rows, TQ), 1) + qi * TQ
        base = hd * (2 * N_BUCKETS)
        pq = prow_ref[...]

        def build(t, _):
            r0 = pl.multiple_of(t * rows, rows)
            dist = jnp.maximum(pq - pcol_ref[pl.ds(r0, rows), :], 0)
            causal = rowi + r0 <= coli
            b0 = jnp.full((rows, TQ), tbl_ref[base], F32)
            b1 = jnp.full((rows, TQ), tbl_ref[base + N_BUCKETS], F32)
            for v, thr in enumerate(_BUCKET_THR, start=1):
                ge = dist >= thr
                b0 = jnp.where(ge, tbl_ref[base + v], b0)
                b1 = jnp.where(ge, tbl_ref[base + N_BUCKETS + v], b1)
            bias_ref[0, pl.ds(r0, rows), :] = jnp.where(causal, b0, NEG_BIG)
            bias_ref[1, pl.ds(r0, rows), :] = jnp.where(causal, b1, NEG_BIG)
            return 0

        lax.fori_loop(0, n_kv * (TK // rows), build, 0)
        bias_ref[:, pl.ds(dead_tile * TK, TK), :] = jnp.full((2, TK, TQ), NEG_BIG, F32)

    lane = lax.broadcasted_iota(jnp.int32, (TQ, 128), 1)
    q = q_ref[...]
    zero = jnp.zeros_like(q)
    qc_ref[0] = jnp.where(lane < DIFF_D, q, zero)
    qc_ref[1] = jnp.where(lane >= DIFF_D, q, zero)
    acc_ref[...] = jnp.zeros_like(acc_ref)
    s_ref[...] = jnp.full(s_ref.shape, NEG_BIG, F32)
    p_ref[...] = jnp.zeros_like(p_ref)

    def kv_step(t, carry):
        m_state, alpha_state = carry
        v0 = pl.multiple_of(jnp.maximum(t - 2, 0) * TK, TK)
        vt = vt_ref[:, pl.ds(v0, TK)]
        for c in range(2):
            acc_ref[c] = alpha_state[c] * acc_ref[c] + _dot(vt, p_ref[c])
        m_next, alpha_next = [], []
        for c in range(2):
            s = s_ref[c]
            m_new = jnp.maximum(m_state[c], jnp.max(s, axis=0, keepdims=True))
            alpha_next.append(jnp.exp2(m_state[c] - m_new))
            p_ref[c] = jnp.exp2(s - m_new).astype(BF16)
            m_next.append(m_new)
        k0 = pl.multiple_of(jnp.minimum(t, n_kv - 1) * TK, TK)
        b0 = pl.multiple_of(jnp.where(t < n_kv, t, dead_tile) * TK, TK)
        kb = k_ref[pl.ds(k0, TK), :]
        for c in range(2):
            s_ref[c] = _dot_nt(kb, qc_ref[c]) + bias_ref[c, pl.ds(b0, TK), :]
        return tuple(m_next), tuple(alpha_next)

    m0 = tuple(jnp.full((1, TQ), M_INIT, F32) for _ in range(2))
    alpha0 = tuple(jnp.ones((1, TQ), F32) for _ in range(2))
    lax.fori_loop(0, n_kv + 2, kv_step, (m0, alpha0))

    lam = (jnp.exp(jnp.sum(lq1_ref[...] * lk1_ref[...], axis=1, keepdims=True))
           - jnp.exp(jnp.sum(lq2_ref[...] * lk2_ref[...], axis=1, keepdims=True)) + lam_init)
    a0 = acc_ref[0]
    a1 = acc_ref[1]
    o = a0[0:DIFF_DV] / a0[DIFF_DV:DIFF_DV + 1] - lam * (a1[0:DIFF_DV] / a1[DIFF_DV:DIFF_DV + 1])
    on = o * lax.rsqrt(jnp.mean(o * o, axis=0, keepdims=True) + EPS) * sg_ref[...]
    out_ref[...] = (on * (1.0 - lam_init)).T.astype(BF16)


def _attention(lam_init, tbl, qd, kd, vdt, pcol, prow, lq1, lk1, lq2, lk2, sg, batch, seq):
    q3, k3 = (t.reshape(batch, seq, GROUP_W) for t in (qd, kd))
    vec = lambda n: pl.BlockSpec((1, n), lambda h, i, b: (0, 0))
    in_specs = [
        pl.BlockSpec(memory_space=pltpu.SMEM),
        pl.BlockSpec((None, TQ, 128), lambda h, i, b: (b, i, h)),
        pl.BlockSpec((None, seq, 128), lambda h, i, b: (b, 0, h)),
        pl.BlockSpec((V_ROWS, seq), lambda h, i, b: (h, b)),
        pl.BlockSpec((seq, 1), lambda h, i, b: (0, 0)),
        pl.BlockSpec((1, TQ), lambda h, i, b: (0, i)),
        vec(DIFF_D), vec(DIFF_D), vec(DIFF_D), vec(DIFF_D),
        pl.BlockSpec((DIFF_DV, TQ), lambda h, i, b: (0, 0)),
    ]
    out = pl.pallas_call(
        functools.partial(_attn_kernel, lam_init),
        grid=(N_HEADS, seq // TQ, batch), in_specs=in_specs,
        out_specs=pl.BlockSpec((None, TQ, 128), lambda h, i, b: (b, i, h)),
        out_shape=jax.ShapeDtypeStruct((batch, seq, GROUP_W), BF16),
        scratch_shapes=[
            pltpu.VMEM((2, seq + TK, TQ), F32),
            pltpu.VMEM((2, TQ, 128), BF16),
            pltpu.VMEM((2, TK, TQ), F32),
            pltpu.VMEM((2, TK, TQ), BF16),
            pltpu.VMEM((2, V_ROWS, TQ), F32),
        ],
        name="diff_attn",
        compiler_params=pltpu.CompilerParams(
            dimension_semantics=("arbitrary", "arbitrary", "arbitrary"), vmem_limit_bytes=VMEM_LIMIT),
    )(tbl, q3, k3, vdt, pcol, prow, lq1, lk1, lq2, lk2, jnp.broadcast_to(sg.reshape(DIFF_DV, 1), (DIFF_DV, TQ)))
    return out.reshape(batch * seq, GROUP_W)


def _gelu(x):
    return 0.5 * x * (1.0 + lax.erf(x * (2.0 ** -0.5)))


def _rms(x, g):
    return x * lax.rsqrt(jnp.mean(x * x, axis=-1, keepdims=True) + EPS) * g


def _mixer_kernel(h_ref, hmt_ref, hd_ref, p_ref, woa_ref, wob_ref, gf_ref, wug_ref, wuv_ref, cwg_ref, cwv_ref,
                  cbg_ref, cbv_ref, wd_ref, gp_ref, wpg_ref, wpp_ref, out_ref,
                  sg_ref, sv_ref, cg_ref, cv_ref, acc_ref, u_ref):
    tm = h_ref.shape[0]
    halo = 8

    @pl.when(pl.program_id(1) == 0)
    def _reset_conv_history():
        cg_ref[...] = jnp.zeros_like(cg_ref)
        cv_ref[...] = jnp.zeros_like(cv_ref)

    h1 = h_ref[...] + _dot_tn(hmt_ref[...], woa_ref[...]) + _dot(hd_ref[...], wob_ref[...])
    u_ref[...] = _rms(h1, gf_ref[...]).astype(BF16)
    acc_ref[...] = jnp.zeros_like(acc_ref)

    def up_stage(j, slot):
        sg_ref[slot, halo:, :] = _dot(u_ref[...], wug_ref[j])
        sv_ref[slot, halo:, :] = _dot(u_ref[...], wuv_ref[j])

    def conv_branch(j, slot, cw_ref, cb_ref, stage_ref, hist_ref):
        stage_ref[slot, 0:halo, :] = hist_ref[j]
        hist_ref[j] = stage_ref[slot, tm:tm + halo, :]
        cw = cw_ref[j]
        return (cw[0:1, :] * stage_ref[slot, halo - 2:halo - 2 + tm, :]
                + cw[1:2, :] * stage_ref[slot, halo - 1:halo - 1 + tm, :]
                + cw[2:3, :] * stage_ref[slot, halo:, :] + cb_ref[j])

    def act_stage(j, slot):
        gate = conv_branch(j, slot, cwg_ref, cbg_ref, sg_ref, cg_ref)
        val = conv_branch(j, slot, cwv_ref, cbv_ref, sv_ref, cv_ref)
        act = (_gelu(gate) * val).astype(BF16)
        acc_ref[...] += _dot(act, wd_ref[j])

    def chunk_pair(i, _):
        j = 2 * i
        up_stage(j + 1, 1)
        act_stage(j, 0)
        up_stage(j + 2, 0)
        act_stage(j + 1, 1)
        return 0

    assert N_FF_CHUNKS % 2 == 1
    up_stage(0, 0)
    lax.fori_loop(0, N_FF_CHUNKS // 2, chunk_pair, 0)
    act_stage(N_FF_CHUNKS - 1, 0)

    h2 = h1 + acc_ref[...]
    u3 = _rms(h2, gp_ref[...]).astype(BF16)
    ple_gate = _sigmoid(_dot(u3, wpg_ref[...]))
    out_ref[...] = h2 + ple_gate * _dot(p_ref[...].astype(BF16), wpp_ref[...])


def _mixer(h, hmt, hd, p, woa, wob, gf, wug, wuv, cwg, cwv, cbg, cbv, wd, gp, wpg, wpp, batch, seq):
    tiles = seq // TM_FFN
    tile = lambda width: pl.BlockSpec((TM_FFN, width), lambda b, t: (b * tiles + t, 0))

    def const(arr):
        nd = arr.ndim
        return pl.BlockSpec(arr.shape, lambda b, t: (0,) * nd, pipeline_mode=pl.Buffered(1))

    weights = (woa, wob, gf, wug, wuv, cwg, cwv, cbg, cbv, wd, gp, wpg, wpp)
    tile_t = pl.BlockSpec((GROUP_W, TM_FFN), lambda b, t: (0, b * tiles + t))
    in_specs = [tile(D_MODEL), tile_t, tile(GROUP_W), tile(PLE_DIM)] + [const(w) for w in weights]
    return pl.pallas_call(
        _mixer_kernel, grid=(batch, tiles), in_specs=in_specs, out_specs=tile(D_MODEL),
        out_shape=jax.ShapeDtypeStruct(h.shape, F32),
        scratch_shapes=[
            pltpu.VMEM((2, TM_FFN + 8, FF_CHUNK), F32),
            pltpu.VMEM((2, TM_FFN + 8, FF_CHUNK), F32),
            pltpu.VMEM((N_FF_CHUNKS, 8, FF_CHUNK), F32),
            pltpu.VMEM((N_FF_CHUNKS, 8, FF_CHUNK), F32),
            pltpu.VMEM((TM_FFN, D_MODEL), F32),
            pltpu.VMEM((TM_FFN, D_MODEL), BF16),
        ],
        name="mixer",
        compiler_params=pltpu.CompilerParams(
            dimension_semantics=("arbitrary", "arbitrary"), vmem_limit_bytes=VMEM_LIMIT),
    )(h, hmt, hd, p, *weights)


def _head_interleave(qcols, kcols):
    lead = qcols.shape[:-1]
    qh = qcols.reshape(lead + (N_HEADS, MLSTM_DK))
    kh = kcols.reshape(lead + (N_HEADS, MLSTM_DK))
    return jnp.concatenate([qh, kh], axis=-1).reshape(lead + (GROUP_W,))


def _ff_chunks(w):
    return jnp.transpose(w.reshape(w.shape[0], N_FF_CHUNKS, FF_CHUNK), (1, 0, 2))


def kernel(x, p, positions, rel_bias, ln_mix_g, w_in, mlstm_conv_w, b_igate, b_fgate, mlstm_norm_g, q_norm_g, k_norm_g, lam_q1, lam_k1, lam_q2, lam_k2, diff_subln_g, w_out, ln_ffn_g, w_up, ffn_conv_w, ffn_conv_b, w_down, ln_ple_g, w_ple_gate, w_ple_proj):
    batch, seq, _ = x.shape
    depth = w_in.shape[0]
    n_tok = batch * seq
    qk_cols = N_HEADS * MLSTM_DK
    col_sizes = [qk_cols, qk_cols, GROUP_W, GROUP_W, N_HEADS, N_HEADS, GROUP_W, GROUP_W, GROUP_W]
    offs = np.concatenate([[0], np.cumsum(col_sizes)])
    sl = lambda a, j: a[..., int(offs[j]):int(offs[j + 1])]

    tbl = jnp.transpose(rel_bias.astype(F32), (1, 2, 0)).reshape(-1) * LOG2E
    pcol = positions.astype(jnp.int32).reshape(seq, 1)
    prow = positions.astype(jnp.int32).reshape(1, seq)
    gsum = jnp.asarray(np.kron(np.eye(GROUP_W // DIFF_D), np.ones((DIFF_D, DIFF_D))), BF16)
    t_idx = np.arange(PAIR)
    same_chunk = (t_idx[:, None] // CHUNK) == (t_idx[None, :] // CHUNK)
    prefix = same_chunk & (t_idx[:, None] <= t_idx[None, :])
    total_a = np.broadcast_to(t_idx[:, None] < CHUNK, (PAIR, PAIR))
    ug = jnp.asarray(np.concatenate([prefix, total_a, ~total_a], axis=1), BF16)
    l2 = jnp.asarray(prefix.T, BF16)

    h = x.reshape(n_tok, D_MODEL)
    for i in range(depth):
        wi = w_in[i]
        wm = jnp.concatenate([_head_interleave(sl(wi, 0), sl(wi, 1)), sl(wi, 6), sl(wi, 7)], axis=-1).astype(BF16)
        wt = jnp.concatenate([sl(wi, 2), sl(wi, 3), sl(wi, 8)], axis=-1).T.astype(BF16)
        pad_rows = lambda a: jnp.zeros((N_GATE_ROWS,) + a.shape[1:], F32).at[:N_HEADS].set(a.astype(F32))
        wg = jnp.concatenate([pad_rows(sl(wi, 4).T), pad_rows(sl(wi, 5).T)], axis=0).astype(BF16)
        gate_bias = jnp.concatenate([pad_rows(b_igate[i]), pad_rows(b_fgate[i])])
        qg = jnp.tile(q_norm_g[i].astype(F32), GROUP_W // DIFF_D).reshape(1, GROUP_W) * (DIFF_D ** -0.5 * LOG2E)
        kg = jnp.tile(k_norm_g[i].astype(F32), GROUP_W // DIFF_D).reshape(1, GROUP_W)
        qk, vmt, omt, gri, grf, gci, gcf, qd, kd, vdt = _inproj(
            h, ln_mix_g[i].reshape(1, D_MODEL), wm, wt, wg, gate_bias.reshape(1, -1), gate_bias.reshape(-1, 1),
            gsum, qg, kg)

        cw = _head_interleave(mlstm_conv_w[i][:, :qk_cols], mlstm_conv_w[i][:, qk_cols:]).astype(F32)
        ng = jnp.broadcast_to(mlstm_norm_g[i].astype(F32).reshape(MLSTM_DV, 1), (MLSTM_DV, PAIR))
        hmt = _mlstm(qk, vmt, omt, gri, grf, gci, gcf, cw, ng, ug, l2, batch, seq)

        lam_init = 0.8 - 0.6 * math.exp(-0.3 * i)
        row64 = lambda a: a[i].reshape(1, DIFF_D).astype(F32)
        hd = _attention(lam_init, tbl, qd, kd, vdt, pcol, prow, row64(lam_q1), row64(lam_k1), row64(lam_q2),
                        row64(lam_k2), diff_subln_g[i].reshape(1, DIFF_DV).astype(F32), batch, seq)

        wo = w_out[i].astype(BF16)
        wu = w_up[i].astype(BF16)
        cwf = ffn_conv_w[i].astype(F32)
        cbf = ffn_conv_b[i].astype(F32).reshape(1, 2 * D_FF)
        h = _mixer(
            h, hmt, hd, p[i].reshape(n_tok, PLE_DIM),
            wo[:GROUP_W], wo[GROUP_W:], ln_ffn_g[i].reshape(1, D_MODEL),
            _ff_chunks(wu[:, :D_FF]), _ff_chunks(wu[:, D_FF:]),
            _ff_chunks(cwf[:, :D_FF]), _ff_chunks(cwf[:, D_FF:]),
            _ff_chunks(cbf[:, :D_FF]), _ff_chunks(cbf[:, D_FF:]),
            w_down[i].astype(BF16).reshape(N_FF_CHUNKS, FF_CHUNK, D_MODEL),
            ln_ple_g[i].reshape(1, D_MODEL), w_ple_gate[i].astype(BF16), w_ple_proj[i].astype(BF16),
            batch, seq)
    return h.reshape(batch, seq, D_MODEL)
```

```python
import functools
import math

import numpy as np
import jax
import jax.numpy as jnp
from jax import lax
from jax.experimental import pallas as pl
from jax.experimental.pallas import tpu as pltpu

F32 = jnp.float32
BF16 = jnp.bfloat16

D_MODEL = 1024
N_HEADS = 4
MLSTM_DK = 64
MLSTM_DV = 128
MLSTM_CONV_W = 4
CHUNK = 64
PAIR = 2 * CHUNK
DIFF_D = 64
DIFF_DV = 128
N_BUCKETS = 32
MAX_DISTANCE = 128
D_FF = 2816
FFN_CONV_W = 3
PLE_DIM = 256
EPS = 1e-6
GROUP_W = N_HEADS * 128
N_GATE_ROWS = 16

TM_IN = 512
TM_FFN = 512
FF_CHUNK = 256
N_FF_CHUNKS = D_FF // FF_CHUNK
TQ = 512
TK = 256
BIAS_ROWS = 32
BIAS_COLS = 128
NEG_BIG = -1e30
M_INIT = -1e20
ONES_ROWS = 16
V_ROWS = DIFF_DV + ONES_ROWS
LOG2E = math.log2(math.e)
VMEM_LIMIT = 56 * 1024 * 1024


def _dot(a, b):
    return jnp.dot(a, b, preferred_element_type=F32)


def _dot_nt(a, b):
    return lax.dot_general(a, b, (((1,), (1,)), ((), ())), preferred_element_type=F32)


def _dot_tn(a, b):
    return lax.dot_general(a, b, (((0,), (0,)), ((), ())), preferred_element_type=F32)


def _sigmoid(x):
    return 1.0 / (1.0 + jnp.exp(-x))


def _log_sigmoid(x):
    return jnp.minimum(x, 0.0) - jnp.log1p(jnp.exp(-jnp.abs(x)))


def _bucket_thresholds():
    max_exact = N_BUCKETS // 2
    thr = []
    for v in range(1, N_BUCKETS):
        if v <= max_exact:
            thr.append(v)
            continue
        edge = max_exact * (MAX_DISTANCE / max_exact) ** ((v - max_exact) / (N_BUCKETS - max_exact))
        assert abs(edge - round(edge)) > 1e-3, edge
        thr.append(int(math.ceil(edge)))
    assert all(a < b for a, b in zip(thr, thr[1:])), thr
    return tuple(thr)


_BUCKET_THR = _bucket_thresholds()


def _group_mean_square(z, gsum_ref):
    sq = z * z
    hi = sq.astype(BF16)
    lo = (sq - hi.astype(F32)).astype(BF16)
    return (_dot(hi, gsum_ref[...]) + _dot(lo, gsum_ref[...])) * (1.0 / DIFF_D)


def _with_ones_rows(dst_ref, vt):
    for hh in range(N_HEADS):
        dst_ref[hh * V_ROWS:hh * V_ROWS + DIFF_DV, :] = vt[hh * DIFF_DV:(hh + 1) * DIFF_DV, :]
        dst_ref[hh * V_ROWS + DIFF_DV:(hh + 1) * V_ROWS, :] = jnp.ones((ONES_ROWS, vt.shape[1]), BF16)


def _inproj_kernel(h_ref, g_ref, wm_ref, wt_ref, wg_ref, bcol_ref, brow_ref, gsum_ref, qg_ref, kg_ref,
                   qk_ref, vmt_ref, omt_ref, gri_ref, grf_ref, gci_ref, gcf_ref, qd_ref, kd_ref, vdt_ref):
    x = h_ref[...]
    ms = jnp.mean(x * x, axis=-1, keepdims=True)
    u = (x * lax.rsqrt(ms + EPS) * g_ref[...]).astype(BF16)
    w = GROUP_W
    qk_ref[...] = _dot(u, wm_ref[:, 0:w])
    zq = _dot(u, wm_ref[:, w:2 * w])
    qd_ref[...] = (zq * lax.rsqrt(_group_mean_square(zq, gsum_ref) + EPS) * qg_ref[...]).astype(BF16)
    zk = _dot(u, wm_ref[:, 2 * w:3 * w])
    kd_ref[...] = (zk * lax.rsqrt(_group_mean_square(zk, gsum_ref) + EPS) * kg_ref[...]).astype(BF16)
    _with_ones_rows(vmt_ref, _dot_nt(wt_ref[0:w, :], u).astype(BF16))
    omt_ref[...] = _dot_nt(wt_ref[w:2 * w, :], u)
    _with_ones_rows(vdt_ref, _dot_nt(wt_ref[2 * w:3 * w, :], u).astype(BF16))
    gc = _dot_nt(u, wg_ref[...]) + bcol_ref[...]
    gci_ref[...] = gc[:, 0:N_GATE_ROWS]
    gcf_ref[...] = gc[:, N_GATE_ROWS:2 * N_GATE_ROWS]
    gr = _dot_nt(wg_ref[...], u) + brow_ref[...]
    for ci in range(TM_IN // PAIR):
        gri_ref[ci] = gr[0:8, ci * PAIR:(ci + 1) * PAIR]
        grf_ref[ci] = gr[N_GATE_ROWS:N_GATE_ROWS + 8, ci * PAIR:(ci + 1) * PAIR]


def _inproj(h, g, wm, wt, wg, bcol, brow, gsum, qg, kg):
    n_tok = h.shape[0]
    grid = (n_tok // TM_IN,)
    const = lambda shape: pl.BlockSpec(shape, lambda i: (0,) * len(shape))
    tile = lambda width: pl.BlockSpec((TM_IN, width), lambda i: (i, 0))
    tile_t = lambda rows: pl.BlockSpec((rows, TM_IN), lambda i: (0, i))
    gate_rows = pl.BlockSpec((TM_IN // PAIR, 8, PAIR), lambda i: (i, 0, 0))
    out_shape = (
        jax.ShapeDtypeStruct((n_tok, GROUP_W), F32),
        jax.ShapeDtypeStruct((N_HEADS * V_ROWS, n_tok), BF16),
        jax.ShapeDtypeStruct((GROUP_W, n_tok), F32),
        jax.ShapeDtypeStruct((n_tok // PAIR, 8, PAIR), F32),
        jax.ShapeDtypeStruct((n_tok // PAIR, 8, PAIR), F32),
        jax.ShapeDtypeStruct((n_tok, N_GATE_ROWS), F32),
        jax.ShapeDtypeStruct((n_tok, N_GATE_ROWS), F32),
        jax.ShapeDtypeStruct((n_tok, GROUP_W), BF16),
        jax.ShapeDtypeStruct((n_tok, GROUP_W), BF16),
        jax.ShapeDtypeStruct((N_HEADS * V_ROWS, n_tok), BF16),
    )
    out_specs = (
        tile(GROUP_W), tile_t(N_HEADS * V_ROWS), tile_t(GROUP_W), gate_rows, gate_rows,
        tile(N_GATE_ROWS), tile(N_GATE_ROWS), tile(GROUP_W), tile(GROUP_W), tile_t(N_HEADS * V_ROWS),
    )
    in_specs = [
        tile(D_MODEL), const((1, D_MODEL)), const(wm.shape), const(wt.shape), const(wg.shape),
        const(bcol.shape), const(brow.shape), const(gsum.shape),
        const((1, GROUP_W)), const((1, GROUP_W)),
    ]
    return pl.pallas_call(
        _inproj_kernel, grid=grid, in_specs=in_specs, out_specs=out_specs, out_shape=out_shape,
        name="inproj",
        compiler_params=pltpu.CompilerParams(dimension_semantics=("arbitrary",), vmem_limit_bytes=VMEM_LIMIT),
    )(h, g, wm, wt, wg, bcol, brow, gsum, qg, kg)


def _split_hi_lo(x):
    hi = x.astype(BF16)
    return hi, (x - hi.astype(F32)).astype(BF16)


def _mlstm_kernel(qk_ref, vt_ref, ot_ref, gri_ref, grf_ref, gci_ref, gcf_ref, cw_ref, ng_ref, ug_ref, l2_ref,
                  out_ref, xpad_ref, cn_ref):
    seq = qk_ref.shape[0]
    halo = 8
    xpad_ref[0:halo, :] = jnp.zeros((halo, GROUP_W), F32)
    xpad_ref[halo:, :] = qk_ref[...]
    cn_ref[...] = jnp.zeros_like(cn_ref)

    first8 = lax.broadcasted_iota(jnp.int32, (8, PAIR), 1) < CHUNK
    first = lax.broadcasted_iota(jnp.int32, (1, PAIR), 1) < CHUNK
    upper = lax.broadcasted_iota(jnp.int32, (PAIR, PAIR), 1) >= MLSTM_DK
    key_t = lax.broadcasted_iota(jnp.int32, (PAIR, PAIR), 0)
    qry_t = lax.broadcasted_iota(jnp.int32, (PAIR, PAIR), 1)
    chunk_start = jnp.where(qry_t < CHUNK, 0, CHUNK)
    cw = cw_ref[...]
    neg_inf = jnp.float32(-jnp.inf)

    def pair_step(c2, m_prev):
        r0 = pl.multiple_of(c2 * PAIR, PAIR)
        win = xpad_ref[pl.ds(r0, PAIR + halo), :]
        conv = cw[0:1, :] * win[halo - 3:halo - 3 + PAIR, :]
        for j in range(1, MLSTM_CONV_W):
            conv = conv + cw[j:j + 1, :] * win[halo - 3 + j:halo - 3 + j + PAIR, :]
        x = conv * _sigmoid(conv)

        gi = gri_ref[c2]
        lf_hi, lf_lo = _split_hi_lo(_log_sigmoid(grf_ref[c2]))
        bg = _dot(lf_hi, ug_ref[...]) + _dot(lf_lo, ug_ref[...])
        b_r = bg[:, 0:PAIR]
        g_a = bg[:, PAIR:2 * PAIR]
        g_b = bg[:, 2 * PAIR:3 * PAIR]
        a_r = jnp.where(first8, g_a, g_b) - b_r + gi
        max_a = jnp.max(jnp.where(first8, a_r, neg_inf), axis=1, keepdims=True)
        max_b = jnp.max(jnp.where(first8, neg_inf, a_r), axis=1, keepdims=True)
        m_a = jnp.maximum(g_a + m_prev, max_a)
        m_b = jnp.maximum(g_b + m_a, max_b)
        dec_a = jnp.exp(g_a + m_prev - m_a)
        dec_b = jnp.exp(g_b + m_a - m_b)
        w_r = jnp.exp(a_r - jnp.where(first8, m_a, m_b))
        e_r = b_r + jnp.where(first8, m_prev, m_a)
        lc_hi, lc_lo = _split_hi_lo(_log_sigmoid(gcf_ref[pl.ds(r0, PAIR), :]))
        x_c = _dot(l2_ref[...], lc_hi) + _dot(l2_ref[...], lc_lo) - gci_ref[pl.ds(r0, PAIR), :]

        for hd in range(N_HEADS):
            rows = slice(hd * MLSTM_DV, (hd + 1) * MLSTM_DV)
            xh = x[:, hd * 128:(hd + 1) * 128]
            xs = pltpu.roll(xh, MLSTM_DK, 1)
            xk = jnp.where(upper, xh * (MLSTM_DK ** -0.5), 0.0).astype(BF16)
            xq = jnp.where(upper, xs, 0.0).astype(BF16)
            vt = vt_ref[hd * V_ROWS:(hd + 1) * V_ROWS, pl.ds(r0, PAIR)]
            vt32 = vt.astype(F32)
            w_h = w_r[hd:hd + 1, :]
            cn0 = cn_ref[hd]
            up_a = _dot((vt32 * jnp.where(first, w_h, 0.0)).astype(BF16), xk)
            cn1 = dec_a[hd:hd + 1, :] * cn0 + up_a
            up_b = _dot((vt32 * jnp.where(first, 0.0, w_h)).astype(BF16), xk)
            cn_ref[hd] = dec_b[hd:hd + 1, :] * cn1 + up_b
            dmat = b_r[hd:hd + 1, :] - x_c[:, hd:hd + 1]
            dmat = jnp.where(key_t <= qry_t, jnp.where(key_t >= chunk_start, dmat, neg_inf), neg_inf)
            e_h = e_r[hd:hd + 1, :]
            m_out = jnp.maximum(e_h, jnp.max(dmat, axis=0, keepdims=True))
            s_t = (_dot_nt(xk, xq) * jnp.exp(dmat - m_out)).astype(BF16)
            inter = jnp.exp(e_h - m_out)
            cq = jnp.where(first, _dot_nt(cn0.astype(BF16), xq), _dot_nt(cn1.astype(BF16), xq))
            tot = inter * cq + _dot(vt, s_t)
            den = tot[MLSTM_DV:MLSTM_DV + 1, :]
            hh = tot[0:MLSTM_DV, :] / jnp.maximum(jnp.abs(den), jnp.exp(-m_out))
            hn = hh * lax.rsqrt(jnp.mean(hh * hh, axis=0, keepdims=True) + EPS) * ng_ref[...]
            og = ot_ref[rows, pl.ds(r0, PAIR)]
            out_ref[rows, pl.ds(r0, PAIR)] = (hn * _sigmoid(og)).astype(BF16)
        return m_b

    lax.fori_loop(0, seq // PAIR, pair_step, jnp.zeros((8, PAIR), F32))


def _mlstm(qk, vmt, omt, gri, grf, gci, gcf, cw, ng, ug, l2, batch, seq):
    const = lambda arr: pl.BlockSpec(arr.shape, lambda b: (0,) * arr.ndim)
    in_specs = [
        pl.BlockSpec((seq, GROUP_W), lambda b: (b, 0)),
        pl.BlockSpec((N_HEADS * V_ROWS, seq), lambda b: (0, b)),
        pl.BlockSpec((GROUP_W, seq), lambda b: (0, b)),
        pl.BlockSpec((seq // PAIR, 8, PAIR), lambda b: (b, 0, 0)),
        pl.BlockSpec((seq // PAIR, 8, PAIR), lambda b: (b, 0, 0)),
        pl.BlockSpec((seq, N_GATE_ROWS), lambda b: (b, 0)),
        pl.BlockSpec((seq, N_GATE_ROWS), lambda b: (b, 0)),
        const(cw), const(ng), const(ug), const(l2),
    ]
    return pl.pallas_call(
        _mlstm_kernel, grid=(batch,), in_specs=in_specs,
        out_specs=pl.BlockSpec((GROUP_W, seq), lambda b: (0, b)),
        out_shape=jax.ShapeDtypeStruct((GROUP_W, batch * seq), BF16),
        scratch_shapes=[
            pltpu.VMEM((seq + 8, GROUP_W), F32),
            pltpu.VMEM((N_HEADS, V_ROWS, 128), F32),
        ],
        name="mlstm",
        compiler_params=pltpu.CompilerParams(dimension_semantics=("arbitrary",), vmem_limit_bytes=VMEM_LIMIT),
    )(qk, vmt, omt, gri, grf, gci, gcf, cw, ng, ug, l2)


def _attn_kernel(lam_init, tbl_ref, kmax_ref, qmin_ref, q_ref, k_ref, vt_ref, pcol_ref, prow_ref,
                 lq1_ref, lk1_ref, lq2_ref, lk2_ref, sg_ref, out_ref, bias_ref, qc_ref, s_ref, p_ref, acc_ref):
    hd = pl.program_id(0)
    qi = pl.program_id(1)
    b = pl.program_id(2)
    n_kv = (qi + 1) * (TQ // TK)

    @pl.when(b == 0)
    def _build_bias():
        rowi = lax.broadcasted_iota(jnp.int32, (BIAS_ROWS, BIAS_COLS), 0)
        coli = lax.broadcasted_iota(jnp.int32, (BIAS_ROWS, BIAS_COLS), 1)
        base = hd * (2 * N_BUCKETS)
        last = N_BUCKETS - 1

        def build(t, _):
            r0 = pl.multiple_of(t * BIAS_ROWS, BIAS_ROWS)
            pk = pcol_ref[pl.ds(r0, BIAS_ROWS), :]
            kmax = kmax_ref[t]
            for j in range(TQ // BIAS_COLS):
                lanes = slice(j * BIAS_COLS, (j + 1) * BIAS_COLS)
                q0 = qi * TQ + j * BIAS_COLS
                causal = rowi + r0 <= coli + q0
                flat = jnp.logical_or(qmin_ref[qi * (TQ // BIAS_COLS) + j] - kmax >= _BUCKET_THR[-1],
                                      r0 > q0 + BIAS_COLS - 1)

                @pl.when(flat)
                def _flat():
                    bias_ref[0, pl.ds(r0, BIAS_ROWS), lanes] = jnp.where(causal, tbl_ref[base + last], NEG_BIG)
                    bias_ref[1, pl.ds(r0, BIAS_ROWS), lanes] = jnp.where(
                        causal, tbl_ref[base + N_BUCKETS + last], NEG_BIG)

                @pl.when(jnp.logical_not(flat))
                def _lookup():
                    dist = jnp.maximum(prow_ref[:, lanes] - pk, 0)
                    b0 = jnp.full((BIAS_ROWS, BIAS_COLS), tbl_ref[base], F32)
                    b1 = jnp.full((BIAS_ROWS, BIAS_COLS), tbl_ref[base + N_BUCKETS], F32)
                    for v, thr in enumerate(_BUCKET_THR, start=1):
                        ge = dist >= thr
                        b0 = jnp.where(ge, tbl_ref[base + v], b0)
                        b1 = jnp.where(ge, tbl_ref[base + N_BUCKETS + v], b1)
                    bias_ref[0, pl.ds(r0, BIAS_ROWS), lanes] = jnp.where(causal, b0, NEG_BIG)
                    bias_ref[1, pl.ds(r0, BIAS_ROWS), lanes] = jnp.where(causal, b1, NEG_BIG)
            return 0

        lax.fori_loop(0, n_kv * (TK // BIAS_ROWS), build, 0)

    lane = lax.broadcasted_iota(jnp.int32, (TQ, 128), 1)
    q = q_ref[...]
    zero = jnp.zeros_like(q)
    qc_ref[0] = jnp.where(lane < DIFF_D, q, zero)
    qc_ref[1] = jnp.where(lane >= DIFF_D, q, zero)
    acc_ref[...] = jnp.zeros_like(acc_ref)

    def scores(t):
        k0 = pl.multiple_of(t * TK, TK)
        kb = k_ref[pl.ds(k0, TK), :]
        for c in range(2):
            s_ref[c] = _dot_nt(kb, qc_ref[c]) + bias_ref[c, pl.ds(k0, TK), :]

    def numerators(m_state):
        m_next, alpha_next = [], []
        for c in range(2):
            s = s_ref[c]
            m_new = jnp.maximum(m_state[c], jnp.max(s, axis=0, keepdims=True))
            alpha_next.append(jnp.exp2(m_state[c] - m_new))
            p_ref[c] = jnp.exp2(s - m_new).astype(BF16)
            m_next.append(m_new)
        return tuple(m_next), tuple(alpha_next)

    def accumulate(t, alpha):
        v0 = pl.multiple_of(t * TK, TK)
        vt = vt_ref[:, pl.ds(v0, TK)]
        for c in range(2):
            acc_ref[c] = alpha[c] * acc_ref[c] + _dot(vt, p_ref[c])

    def kv_step(t, carry):
        m_state, alpha = carry
        accumulate(t - 2, alpha)
        carry = numerators(m_state)
        scores(t)
        return carry

    assert TQ // TK >= 2
    scores(0)
    carry = numerators(tuple(jnp.full((1, TQ), M_INIT, F32) for _ in range(2)))
    scores(1)
    m_state, alpha = lax.fori_loop(2, n_kv, kv_step, carry)
    accumulate(n_kv - 2, alpha)
    _, alpha = numerators(m_state)
    accumulate(n_kv - 1, alpha)

    lam = (jnp.exp(jnp.sum(lq1_ref[...] * lk1_ref[...], axis=1, keepdims=True))
           - jnp.exp(jnp.sum(lq2_ref[...] * lk2_ref[...], axis=1, keepdims=True)) + lam_init)
    a0 = acc_ref[0]
    a1 = acc_ref[1]
    o = a0[0:DIFF_DV] / a0[DIFF_DV:DIFF_DV + 1] - lam * (a1[0:DIFF_DV] / a1[DIFF_DV:DIFF_DV + 1])
    on = o * lax.rsqrt(jnp.mean(o * o, axis=0, keepdims=True) + EPS) * sg_ref[...]
    out_ref[...] = (on * (1.0 - lam_init)).T.astype(BF16)


def _attention(lam_init, tbl, qd, kd, vdt, pcol, prow, lq1, lk1, lq2, lk2, sg, batch, seq):
    q3, k3 = (t.reshape(batch, seq, GROUP_W) for t in (qd, kd))
    vec = lambda n: pl.BlockSpec((1, n), lambda h, i, b: (0, 0))
    kmax = jnp.max(prow.reshape(seq // BIAS_ROWS, BIAS_ROWS), axis=1)
    qmin = jnp.min(prow.reshape(seq // BIAS_COLS, BIAS_COLS), axis=1)
    in_specs = [
        pl.BlockSpec(memory_space=pltpu.SMEM),
        pl.BlockSpec(memory_space=pltpu.SMEM),
        pl.BlockSpec(memory_space=pltpu.SMEM),
        pl.BlockSpec((None, TQ, 128), lambda h, i, b: (b, i, h)),
        pl.BlockSpec((None, seq, 128), lambda h, i, b: (b, 0, h)),
        pl.BlockSpec((V_ROWS, seq), lambda h, i, b: (h, b)),
        pl.BlockSpec((seq, 1), lambda h, i, b: (0, 0)),
        pl.BlockSpec((1, TQ), lambda h, i, b: (0, i)),
        vec(DIFF_D), vec(DIFF_D), vec(DIFF_D), vec(DIFF_D),
        pl.BlockSpec((DIFF_DV, TQ), lambda h, i, b: (0, 0)),
    ]
    out = pl.pallas_call(
        functools.partial(_attn_kernel, lam_init),
        grid=(N_HEADS, seq // TQ, batch), in_specs=in_specs,
        out_specs=pl.BlockSpec((None, TQ, 128), lambda h, i, b: (b, i, h)),
        out_shape=jax.ShapeDtypeStruct((batch, seq, GROUP_W), BF16),
        scratch_shapes=[
            pltpu.VMEM((2, seq, TQ), F32),
            pltpu.VMEM((2, TQ, 128), BF16),
            pltpu.VMEM((2, TK, TQ), F32),
            pltpu.VMEM((2, TK, TQ), BF16),
            pltpu.VMEM((2, V_ROWS, TQ), F32),
        ],
        name="diff_attn",
        compiler_params=pltpu.CompilerParams(
            dimension_semantics=("arbitrary", "arbitrary", "arbitrary"), vmem_limit_bytes=VMEM_LIMIT),
    )(tbl, kmax, qmin, q3, k3, vdt, pcol, prow, lq1, lk1, lq2, lk2, jnp.broadcast_to(sg.reshape(DIFF_DV, 1), (DIFF_DV, TQ)))
    return out.reshape(batch * seq, GROUP_W)


def _gelu(x):
    return 0.5 * x * (1.0 + lax.erf(x * (2.0 ** -0.5)))


def _rms(x, g):
    return x * lax.rsqrt(jnp.mean(x * x, axis=-1, keepdims=True) + EPS) * g


def _mixer_kernel(h_ref, hmt_ref, hd_ref, p_ref, woa_ref, wob_ref, gf_ref, wug_ref, wuv_ref, cwg_ref, cwv_ref,
                  cbg_ref, cbv_ref, wd_ref, gp_ref, wpg_ref, wpp_ref, out_ref,
                  sg_ref, sv_ref, cg_ref, cv_ref, acc_ref, u_ref):
    tm = h_ref.shape[0]
    halo = 8

    @pl.when(pl.program_id(1) == 0)
    def _reset_conv_history():
        cg_ref[...] = jnp.zeros_like(cg_ref)
        cv_ref[...] = jnp.zeros_like(cv_ref)

    h1 = h_ref[...] + _dot_tn(hmt_ref[...], woa_ref[...]) + _dot(hd_ref[...], wob_ref[...])
    u_ref[...] = _rms(h1, gf_ref[...]).astype(BF16)
    acc_ref[...] = jnp.zeros_like(acc_ref)

    def up_stage(j, slot):
        sg_ref[slot, halo:, :] = _dot(u_ref[...], wug_ref[j])
        sv_ref[slot, halo:, :] = _dot(u_ref[...], wuv_ref[j])

    def conv_branch(j, slot, cw_ref, cb_ref, stage_ref, hist_ref):
        stage_ref[slot, 0:halo, :] = hist_ref[j]
        hist_ref[j] = stage_ref[slot, tm:tm + halo, :]
        cw = cw_ref[j]
        return (cw[0:1, :] * stage_ref[slot, halo - 2:halo - 2 + tm, :]
                + cw[1:2, :] * stage_ref[slot, halo - 1:halo - 1 + tm, :]
                + cw[2:3, :] * stage_ref[slot, halo:, :] + cb_ref[j])

    def act_stage(j, slot):
        gate = conv_branch(j, slot, cwg_ref, cbg_ref, sg_ref, cg_ref)
        val = conv_branch(j, slot, cwv_ref, cbv_ref, sv_ref, cv_ref)
        act = (_gelu(gate) * val).astype(BF16)
        acc_ref[...] += _dot(act, wd_ref[j])

    def chunk_pair(i, _):
        j = 2 * i
        up_stage(j + 1, 1)
        act_stage(j, 0)
        up_stage(j + 2, 0)
        act_stage(j + 1, 1)
        return 0

    assert N_FF_CHUNKS % 2 == 1
    up_stage(0, 0)
    lax.fori_loop(0, N_FF_CHUNKS // 2, chunk_pair, 0)
    act_stage(N_FF_CHUNKS - 1, 0)

    h2 = h1 + acc_ref[...]
    u3 = _rms(h2, gp_ref[...]).astype(BF16)
    ple_gate = _sigmoid(_dot(u3, wpg_ref[...]))
    out_ref[...] = h2 + ple_gate * _dot(p_ref[...].astype(BF16), wpp_ref[...])


def _mixer(h, hmt, hd, p, woa, wob, gf, wug, wuv, cwg, cwv, cbg, cbv, wd, gp, wpg, wpp, batch, seq):
    tiles = seq // TM_FFN
    tile = lambda width: pl.BlockSpec((TM_FFN, width), lambda b, t: (b * tiles + t, 0))

    def const(arr):
        nd = arr.ndim
        return pl.BlockSpec(arr.shape, lambda b, t: (0,) * nd, pipeline_mode=pl.Buffered(1))

    weights = (woa, wob, gf, wug, wuv, cwg, cwv, cbg, cbv, wd, gp, wpg, wpp)
    tile_t = pl.BlockSpec((GROUP_W, TM_FFN), lambda b, t: (0, b * tiles + t))
    in_specs = [tile(D_MODEL), tile_t, tile(GROUP_W), tile(PLE_DIM)] + [const(w) for w in weights]
    return pl.pallas_call(
        _mixer_kernel, grid=(batch, tiles), in_specs=in_specs, out_specs=tile(D_MODEL),
        out_shape=jax.ShapeDtypeStruct(h.shape, F32),
        scratch_shapes=[
            pltpu.VMEM((2, TM_FFN + 8, FF_CHUNK), F32),
            pltpu.VMEM((2, TM_FFN + 8, FF_CHUNK), F32),
            pltpu.VMEM((N_FF_CHUNKS, 8, FF_CHUNK), F32),
            pltpu.VMEM((N_FF_CHUNKS, 8, FF_CHUNK), F32),
            pltpu.VMEM((TM_FFN, D_MODEL), F32),
            pltpu.VMEM((TM_FFN, D_MODEL), BF16),
        ],
        name="mixer",
        compiler_params=pltpu.CompilerParams(
            dimension_semantics=("arbitrary", "arbitrary"), vmem_limit_bytes=VMEM_LIMIT),
    )(h, hmt, hd, p, *weights)


def _head_interleave(qcols, kcols):
    lead = qcols.shape[:-1]
    qh = qcols.reshape(lead + (N_HEADS, MLSTM_DK))
    kh = kcols.reshape(lead + (N_HEADS, MLSTM_DK))
    return jnp.concatenate([qh, kh], axis=-1).reshape(lead + (GROUP_W,))


def _ff_chunks(w):
    return jnp.transpose(w.reshape(w.shape[0], N_FF_CHUNKS, FF_CHUNK), (1, 0, 2))


def kernel(x, p, positions, rel_bias, ln_mix_g, w_in, mlstm_conv_w, b_igate, b_fgate, mlstm_norm_g, q_norm_g, k_norm_g, lam_q1, lam_k1, lam_q2, lam_k2, diff_subln_g, w_out, ln_ffn_g, w_up, ffn_conv_w, ffn_conv_b, w_down, ln_ple_g, w_ple_gate, w_ple_proj):
    batch, seq, _ = x.shape
    depth = w_in.shape[0]
    n_tok = batch * seq
    qk_cols = N_HEADS * MLSTM_DK
    col_sizes = [qk_cols, qk_cols, GROUP_W, GROUP_W, N_HEADS, N_HEADS, GROUP_W, GROUP_W, GROUP_W]
    offs = np.concatenate([[0], np.cumsum(col_sizes)])
    sl = lambda a, j: a[..., int(offs[j]):int(offs[j + 1])]

    tbl = jnp.transpose(rel_bias.astype(F32), (1, 2, 0)).reshape(-1) * LOG2E
    pcol = positions.astype(jnp.int32).reshape(seq, 1)
    prow = positions.astype(jnp.int32).reshape(1, seq)
    gsum = jnp.asarray(np.kron(np.eye(GROUP_W // DIFF_D), np.ones((DIFF_D, DIFF_D))), BF16)
    t_idx = np.arange(PAIR)
    same_chunk = (t_idx[:, None] // CHUNK) == (t_idx[None, :] // CHUNK)
    prefix = same_chunk & (t_idx[:, None] <= t_idx[None, :])
    total_a = np.broadcast_to(t_idx[:, None] < CHUNK, (PAIR, PAIR))
    ug = jnp.asarray(np.concatenate([prefix, total_a, ~total_a], axis=1), BF16)
    l2 = jnp.asarray(prefix.T, BF16)

    h = x.reshape(n_tok, D_MODEL)
    for i in range(depth):
        wi = w_in[i]
        wm = jnp.concatenate([_head_interleave(sl(wi, 0), sl(wi, 1)), sl(wi, 6), sl(wi, 7)], axis=-1).astype(BF16)
        wt = jnp.concatenate([sl(wi, 2), sl(wi, 3), sl(wi, 8)], axis=-1).T.astype(BF16)
        pad_rows = lambda a: jnp.zeros((N_GATE_ROWS,) + a.shape[1:], F32).at[:N_HEADS].set(a.astype(F32))
        wg = jnp.concatenate([pad_rows(sl(wi, 4).T), pad_rows(sl(wi, 5).T)], axis=0).astype(BF16)
        gate_bias = jnp.concatenate([pad_rows(b_igate[i]), pad_rows(b_fgate[i])])
        qg = jnp.tile(q_norm_g[i].astype(F32), GROUP_W // DIFF_D).reshape(1, GROUP_W) * (DIFF_D ** -0.5 * LOG2E)
        kg = jnp.tile(k_norm_g[i].astype(F32), GROUP_W // DIFF_D).reshape(1, GROUP_W)
        qk, vmt, omt, gri, grf, gci, gcf, qd, kd, vdt = _inproj(
            h, ln_mix_g[i].reshape(1, D_MODEL), wm, wt, wg, gate_bias.reshape(1, -1), gate_bias.reshape(-1, 1),
            gsum, qg, kg)

        cw = _head_interleave(mlstm_conv_w[i][:, :qk_cols], mlstm_conv_w[i][:, qk_cols:]).astype(F32)
        ng = jnp.broadcast_to(mlstm_norm_g[i].astype(F32).reshape(MLSTM_DV, 1), (MLSTM_DV, PAIR))
        hmt = _mlstm(qk, vmt, omt, gri, grf, gci, gcf, cw, ng, ug, l2, batch, seq)

        lam_init = 0.8 - 0.6 * math.exp(-0.3 * i)
        row64 = lambda a: a[i].reshape(1, DIFF_D).astype(F32)
        hd = _attention(lam_init, tbl, qd, kd, vdt, pcol, prow, row64(lam_q1), row64(lam_k1), row64(lam_q2),
                        row64(lam_k2), diff_subln_g[i].reshape(1, DIFF_DV).astype(F32), batch, seq)

        wo = w_out[i].astype(BF16)
        wu = w_up[i].astype(BF16)
        cwf = ffn_conv_w[i].astype(F32)
        cbf = ffn_conv_b[i].astype(F32).reshape(1, 2 * D_FF)
        h = _mixer(
            h, hmt, hd, p[i].reshape(n_tok, PLE_DIM),
            wo[:GROUP_W], wo[GROUP_W:], ln_ffn_g[i].reshape(1, D_MODEL),
            _ff_chunks(wu[:, :D_FF]), _ff_chunks(wu[:, D_FF:]),
            _ff_chunks(cwf[:, :D_FF]), _ff_chunks(cwf[:, D_FF:]),
            _ff_chunks(cbf[:, :D_FF]), _ff_chunks(cbf[:, D_FF:]),
            w_down[i].astype(BF16).reshape(N_FF_CHUNKS, FF_CHUNK, D_MODEL),
            ln_ple_g[i].reshape(1, D_MODEL), w_ple_gate[i].astype(BF16), w_ple_proj[i].astype(BF16),
            batch, seq)
    return h.reshape(batch, seq, D_MODEL)
```

```python
import functools
import math

import numpy as np
import jax
import jax.numpy as jnp
from jax import lax
from jax.experimental import pallas as pl
from jax.experimental.pallas import tpu as pltpu

F32 = jnp.float32
BF16 = jnp.bfloat16

D_MODEL = 1024
N_HEADS = 4
MLSTM_DK = 64
MLSTM_DV = 128
MLSTM_CONV_W = 4
CHUNK = 64
PAIR = 2 * CHUNK
DIFF_D = 64
DIFF_DV = 128
N_BUCKETS = 32
MAX_DISTANCE = 128
D_FF = 2816
FFN_CONV_W = 3
PLE_DIM = 256
EPS = 1e-6
GROUP_W = N_HEADS * 128
N_GATE_ROWS = 16

TM_IN = 512
TM_FFN = 512
FF_CHUNK = 256
N_FF_CHUNKS = D_FF // FF_CHUNK
TQ = 512
TK = 256
BIAS_ROWS = 32
BIAS_COLS = 128
ATTN_NB = 2
NEG_BIG = -1e30
M_INIT = -1e20
ONES_ROWS = 16
V_ROWS = DIFF_DV + ONES_ROWS
LOG2E = math.log2(math.e)
VMEM_LIMIT = 56 * 1024 * 1024


def _dot(a, b):
    return jnp.dot(a, b, preferred_element_type=F32)


def _dot_nt(a, b):
    return lax.dot_general(a, b, (((1,), (1,)), ((), ())), preferred_element_type=F32)


def _dot_tn(a, b):
    return lax.dot_general(a, b, (((0,), (0,)), ((), ())), preferred_element_type=F32)


def _sigmoid(x):
    return 1.0 / (1.0 + jnp.exp(-x))


def _log_sigmoid(x):
    return jnp.minimum(x, 0.0) - jnp.log1p(jnp.exp(-jnp.abs(x)))


def _bucket_thresholds():
    max_exact = N_BUCKETS // 2
    thr = []
    for v in range(1, N_BUCKETS):
        if v <= max_exact:
            thr.append(v)
            continue
        edge = max_exact * (MAX_DISTANCE / max_exact) ** ((v - max_exact) / (N_BUCKETS - max_exact))
        assert abs(edge - round(edge)) > 1e-3, edge
        thr.append(int(math.ceil(edge)))
    assert all(a < b for a, b in zip(thr, thr[1:])), thr
    return tuple(thr)


_BUCKET_THR = _bucket_thresholds()


def _group_mean_square(z, gsum_ref):
    sq = z * z
    hi = sq.astype(BF16)
    lo = (sq - hi.astype(F32)).astype(BF16)
    return (_dot(hi, gsum_ref[...]) + _dot(lo, gsum_ref[...])) * (1.0 / DIFF_D)


def _with_ones_rows(dst_ref, vt):
    for hh in range(N_HEADS):
        dst_ref[hh * V_ROWS:hh * V_ROWS + DIFF_DV, :] = vt[hh * DIFF_DV:(hh + 1) * DIFF_DV, :]
        dst_ref[hh * V_ROWS + DIFF_DV:(hh + 1) * V_ROWS, :] = jnp.ones((ONES_ROWS, vt.shape[1]), BF16)


def _inproj_kernel(h_ref, g_ref, wm_ref, wt_ref, wg_ref, bcol_ref, brow_ref, gsum_ref, qg_ref, kg_ref,
                   qk_ref, vmt_ref, omt_ref, gri_ref, grf_ref, gci_ref, gcf_ref, qd_ref, kd_ref, vdt_ref):
    x = h_ref[...]
    ms = jnp.mean(x * x, axis=-1, keepdims=True)
    u = (x * lax.rsqrt(ms + EPS) * g_ref[...]).astype(BF16)
    w = GROUP_W
    qk_ref[...] = _dot(u, wm_ref[:, 0:w])
    zq = _dot(u, wm_ref[:, w:2 * w])
    qd_ref[...] = (zq * lax.rsqrt(_group_mean_square(zq, gsum_ref) + EPS) * qg_ref[...]).astype(BF16)
    zk = _dot(u, wm_ref[:, 2 * w:3 * w])
    kd_ref[...] = (zk * lax.rsqrt(_group_mean_square(zk, gsum_ref) + EPS) * kg_ref[...]).astype(BF16)
    _with_ones_rows(vmt_ref, _dot_nt(wt_ref[0:w, :], u).astype(BF16))
    omt_ref[...] = _dot_nt(wt_ref[w:2 * w, :], u)
    _with_ones_rows(vdt_ref, _dot_nt(wt_ref[2 * w:3 * w, :], u).astype(BF16))
    gc = _dot_nt(u, wg_ref[...]) + bcol_ref[...]
    gci_ref[...] = gc[:, 0:N_GATE_ROWS]
    gcf_ref[...] = gc[:, N_GATE_ROWS:2 * N_GATE_ROWS]
    gr = _dot_nt(wg_ref[...], u) + brow_ref[...]
    for ci in range(TM_IN // PAIR):
        gri_ref[ci] = gr[0:8, ci * PAIR:(ci + 1) * PAIR]
        grf_ref[ci] = gr[N_GATE_ROWS:N_GATE_ROWS + 8, ci * PAIR:(ci + 1) * PAIR]


def _inproj(h, g, wm, wt, wg, bcol, brow, gsum, qg, kg):
    n_tok = h.shape[0]
    grid = (n_tok // TM_IN,)
    const = lambda shape: pl.BlockSpec(shape, lambda i: (0,) * len(shape))
    tile = lambda width: pl.BlockSpec((TM_IN, width), lambda i: (i, 0))
    tile_t = lambda rows: pl.BlockSpec((rows, TM_IN), lambda i: (0, i))
    gate_rows = pl.BlockSpec((TM_IN // PAIR, 8, PAIR), lambda i: (i, 0, 0))
    out_shape = (
        jax.ShapeDtypeStruct((n_tok, GROUP_W), F32),
        jax.ShapeDtypeStruct((N_HEADS * V_ROWS, n_tok), BF16),
        jax.ShapeDtypeStruct((GROUP_W, n_tok), F32),
        jax.ShapeDtypeStruct((n_tok // PAIR, 8, PAIR), F32),
        jax.ShapeDtypeStruct((n_tok // PAIR, 8, PAIR), F32),
        jax.ShapeDtypeStruct((n_tok, N_GATE_ROWS), F32),
        jax.ShapeDtypeStruct((n_tok, N_GATE_ROWS), F32),
        jax.ShapeDtypeStruct((n_tok, GROUP_W), BF16),
        jax.ShapeDtypeStruct((n_tok, GROUP_W), BF16),
        jax.ShapeDtypeStruct((N_HEADS * V_ROWS, n_tok), BF16),
    )
    out_specs = (
        tile(GROUP_W), tile_t(N_HEADS * V_ROWS), tile_t(GROUP_W), gate_rows, gate_rows,
        tile(N_GATE_ROWS), tile(N_GATE_ROWS), tile(GROUP_W), tile(GROUP_W), tile_t(N_HEADS * V_ROWS),
    )
    in_specs = [
        tile(D_MODEL), const((1, D_MODEL)), const(wm.shape), const(wt.shape), const(wg.shape),
        const(bcol.shape), const(brow.shape), const(gsum.shape),
        const((1, GROUP_W)), const((1, GROUP_W)),
    ]
    return pl.pallas_call(
        _inproj_kernel, grid=grid, in_specs=in_specs, out_specs=out_specs, out_shape=out_shape,
        name="inproj",
        compiler_params=pltpu.CompilerParams(dimension_semantics=("arbitrary",), vmem_limit_bytes=VMEM_LIMIT),
    )(h, g, wm, wt, wg, bcol, brow, gsum, qg, kg)


def _split_hi_lo(x):
    hi = x.astype(BF16)
    return hi, (x - hi.astype(F32)).astype(BF16)


def _mlstm_kernel(qk_ref, vt_ref, ot_ref, gri_ref, grf_ref, gci_ref, gcf_ref, cw_ref, ng_ref, ug_ref, l2_ref,
                  out_ref, xpad_ref, cn_ref):
    seq = qk_ref.shape[0]
    halo = 8
    xpad_ref[0:halo, :] = jnp.zeros((halo, GROUP_W), F32)
    xpad_ref[halo:, :] = qk_ref[...]
    cn_ref[...] = jnp.zeros_like(cn_ref)

    first8 = lax.broadcasted_iota(jnp.int32, (8, PAIR), 1) < CHUNK
    first = lax.broadcasted_iota(jnp.int32, (1, PAIR), 1) < CHUNK
    upper = lax.broadcasted_iota(jnp.int32, (PAIR, PAIR), 1) >= MLSTM_DK
    key_t = lax.broadcasted_iota(jnp.int32, (PAIR, PAIR), 0)
    qry_t = lax.broadcasted_iota(jnp.int32, (PAIR, PAIR), 1)
    chunk_start = jnp.where(qry_t < CHUNK, 0, CHUNK)
    cw = cw_ref[...]
    neg_inf = jnp.float32(-jnp.inf)

    def pair_step(c2, m_prev):
        r0 = pl.multiple_of(c2 * PAIR, PAIR)
        win = xpad_ref[pl.ds(r0, PAIR + halo), :]
        conv = cw[0:1, :] * win[halo - 3:halo - 3 + PAIR, :]
        for j in range(1, MLSTM_CONV_W):
            conv = conv + cw[j:j + 1, :] * win[halo - 3 + j:halo - 3 + j + PAIR, :]
        x = conv * _sigmoid(conv)

        gi = gri_ref[c2]
        lf_hi, lf_lo = _split_hi_lo(_log_sigmoid(grf_ref[c2]))
        bg = _dot(lf_hi, ug_ref[...]) + _dot(lf_lo, ug_ref[...])
        b_r = bg[:, 0:PAIR]
        g_a = bg[:, PAIR:2 * PAIR]
        g_b = bg[:, 2 * PAIR:3 * PAIR]
        a_r = jnp.where(first8, g_a, g_b) - b_r + gi
        max_a = jnp.max(jnp.where(first8, a_r, neg_inf), axis=1, keepdims=True)
        max_b = jnp.max(jnp.where(first8, neg_inf, a_r), axis=1, keepdims=True)
        m_a = jnp.maximum(g_a + m_prev, max_a)
        m_b = jnp.maximum(g_b + m_a, max_b)
        dec_a = jnp.exp(g_a + m_prev - m_a)
        dec_b = jnp.exp(g_b + m_a - m_b)
        w_r = jnp.exp(a_r - jnp.where(first8, m_a, m_b))
        e_r = b_r + jnp.where(first8, m_prev, m_a)
        lc_hi, lc_lo = _split_hi_lo(_log_sigmoid(gcf_ref[pl.ds(r0, PAIR), :]))
        x_c = _dot(l2_ref[...], lc_hi) + _dot(l2_ref[...], lc_lo) - gci_ref[pl.ds(r0, PAIR), :]

        for hd in range(N_HEADS):
            rows = slice(hd * MLSTM_DV, (hd + 1) * MLSTM_DV)
            xh = x[:, hd * 128:(hd + 1) * 128]
            xs = pltpu.roll(xh, MLSTM_DK, 1)
            xk = jnp.where(upper, xh * (MLSTM_DK ** -0.5), 0.0).astype(BF16)
            xq = jnp.where(upper, xs, 0.0).astype(BF16)
            vt = vt_ref[hd * V_ROWS:(hd + 1) * V_ROWS, pl.ds(r0, PAIR)]
            vt32 = vt.astype(F32)
            w_h = w_r[hd:hd + 1, :]
            cn0 = cn_ref[hd]
            up_a = _dot((vt32 * jnp.where(first, w_h, 0.0)).astype(BF16), xk)
            cn1 = dec_a[hd:hd + 1, :] * cn0 + up_a
            up_b = _dot((vt32 * jnp.where(first, 0.0, w_h)).astype(BF16), xk)
            cn_ref[hd] = dec_b[hd:hd + 1, :] * cn1 + up_b
            dmat = b_r[hd:hd + 1, :] - x_c[:, hd:hd + 1]
            dmat = jnp.where(key_t <= qry_t, jnp.where(key_t >= chunk_start, dmat, neg_inf), neg_inf)
            e_h = e_r[hd:hd + 1, :]
            m_out = jnp.maximum(e_h, jnp.max(dmat, axis=0, keepdims=True))
            s_t = (_dot_nt(xk, xq) * jnp.exp(dmat - m_out)).astype(BF16)
            inter = jnp.exp(e_h - m_out)
            cq = jnp.where(first, _dot_nt(cn0.astype(BF16), xq), _dot_nt(cn1.astype(BF16), xq))
            tot = inter * cq + _dot(vt, s_t)
            den = tot[MLSTM_DV:MLSTM_DV + 1, :]
            hh = tot[0:MLSTM_DV, :] / jnp.maximum(jnp.abs(den), jnp.exp(-m_out))
            hn = hh * lax.rsqrt(jnp.mean(hh * hh, axis=0, keepdims=True) + EPS) * ng_ref[...]
            og = ot_ref[rows, pl.ds(r0, PAIR)]
            out_ref[rows, pl.ds(r0, PAIR)] = (hn * _sigmoid(og)).astype(BF16)
        return m_b

    lax.fori_loop(0, seq // PAIR, pair_step, jnp.zeros((8, PAIR), F32))


def _mlstm(qk, vmt, omt, gri, grf, gci, gcf, cw, ng, ug, l2, batch, seq):
    const = lambda arr: pl.BlockSpec(arr.shape, lambda b: (0,) * arr.ndim)
    in_specs = [
        pl.BlockSpec((seq, GROUP_W), lambda b: (b, 0)),
        pl.BlockSpec((N_HEADS * V_ROWS, seq), lambda b: (0, b)),
        pl.BlockSpec((GROUP_W, seq), lambda b: (0, b)),
        pl.BlockSpec((seq // PAIR, 8, PAIR), lambda b: (b, 0, 0)),
        pl.BlockSpec((seq // PAIR, 8, PAIR), lambda b: (b, 0, 0)),
        pl.BlockSpec((seq, N_GATE_ROWS), lambda b: (b, 0)),
        pl.BlockSpec((seq, N_GATE_ROWS), lambda b: (b, 0)),
        const(cw), const(ng), const(ug), const(l2),
    ]
    return pl.pallas_call(
        _mlstm_kernel, grid=(batch,), in_specs=in_specs,
        out_specs=pl.BlockSpec((GROUP_W, seq), lambda b: (0, b)),
        out_shape=jax.ShapeDtypeStruct((GROUP_W, batch * seq), BF16),
        scratch_shapes=[
            pltpu.VMEM((seq + 8, GROUP_W), F32),
            pltpu.VMEM((N_HEADS, V_ROWS, 128), F32),
        ],
        name="mlstm",
        compiler_params=pltpu.CompilerParams(dimension_semantics=("arbitrary",), vmem_limit_bytes=VMEM_LIMIT),
    )(qk, vmt, omt, gri, grf, gci, gcf, cw, ng, ug, l2)


def _attn_kernel(lam_init, tbl_ref, kmax_ref, qmin_ref, q_ref, k_ref, vt_ref, pcol_ref, prow_ref,
                 lq1_ref, lk1_ref, lq2_ref, lk2_ref, sg_ref, out_ref, bias_ref, qc_ref, s_ref, p_ref, acc_ref):
    hd = pl.program_id(0)
    qi = pl.program_id(1)
    b = pl.program_id(2)
    n_kv = (qi + 1) * (TQ // TK)

    @pl.when(b == 0)
    def _build_bias():
        rowi = lax.broadcasted_iota(jnp.int32, (BIAS_ROWS, BIAS_COLS), 0)
        coli = lax.broadcasted_iota(jnp.int32, (BIAS_ROWS, BIAS_COLS), 1)
        base = hd * (2 * N_BUCKETS)
        last = N_BUCKETS - 1

        def build(t, _):
            r0 = pl.multiple_of(t * BIAS_ROWS, BIAS_ROWS)
            pk = pcol_ref[pl.ds(r0, BIAS_ROWS), :]
            kmax = kmax_ref[t]
            for j in range(TQ // BIAS_COLS):
                lanes = slice(j * BIAS_COLS, (j + 1) * BIAS_COLS)
                q0 = qi * TQ + j * BIAS_COLS
                causal = rowi + r0 <= coli + q0
                flat = jnp.logical_or(qmin_ref[qi * (TQ // BIAS_COLS) + j] - kmax >= _BUCKET_THR[-1],
                                      r0 > q0 + BIAS_COLS - 1)

                @pl.when(flat)
                def _flat():
                    bias_ref[0, pl.ds(r0, BIAS_ROWS), lanes] = jnp.where(causal, tbl_ref[base + last], NEG_BIG)
                    bias_ref[1, pl.ds(r0, BIAS_ROWS), lanes] = jnp.where(
                        causal, tbl_ref[base + N_BUCKETS + last], NEG_BIG)

                @pl.when(jnp.logical_not(flat))
                def _lookup():
                    dist = jnp.maximum(prow_ref[:, lanes] - pk, 0)
                    b0 = jnp.full((BIAS_ROWS, BIAS_COLS), tbl_ref[base], F32)
                    b1 = jnp.full((BIAS_ROWS, BIAS_COLS), tbl_ref[base + N_BUCKETS], F32)
                    for v, thr in enumerate(_BUCKET_THR, start=1):
                        ge = dist >= thr
                        b0 = jnp.where(ge, tbl_ref[base + v], b0)
                        b1 = jnp.where(ge, tbl_ref[base + N_BUCKETS + v], b1)
                    bias_ref[0, pl.ds(r0, BIAS_ROWS), lanes] = jnp.where(causal, b0, NEG_BIG)
                    bias_ref[1, pl.ds(r0, BIAS_ROWS), lanes] = jnp.where(causal, b1, NEG_BIG)
            return 0

        lax.fori_loop(0, n_kv * (TK // BIAS_ROWS), build, 0)

    seq = k_ref.shape[1]
    lane = lax.broadcasted_iota(jnp.int32, (TQ, 128), 1)
    for e in range(ATTN_NB):
        q = q_ref[e]
        zero = jnp.zeros_like(q)
        qc_ref[e, 0] = jnp.where(lane < DIFF_D, q, zero)
        qc_ref[e, 1] = jnp.where(lane >= DIFF_D, q, zero)
    acc_ref[...] = jnp.zeros_like(acc_ref)

    chains = [(e, c) for e in range(ATTN_NB) for c in range(2)]

    def scores(t, slot):
        k0 = pl.multiple_of(t * TK, TK)
        for e in range(ATTN_NB):
            kb = k_ref[e, pl.ds(k0, TK), :]
            for c in range(2):
                s_ref[slot, e, c] = _dot_nt(kb, qc_ref[e, c]) + bias_ref[c, pl.ds(k0, TK), :]

    def numerators(m_state, slot):
        m_next, alpha_next = [], []
        for i, (e, c) in enumerate(chains):
            s = s_ref[slot, e, c]
            m_new = jnp.maximum(m_state[i], jnp.max(s, axis=0, keepdims=True))
            alpha_next.append(jnp.exp2(m_state[i] - m_new))
            p_ref[slot, e, c] = jnp.exp2(s - m_new).astype(BF16)
            m_next.append(m_new)
        return tuple(m_next), tuple(alpha_next)

    def accumulate(t, alpha, slot):
        for e in range(ATTN_NB):
            v0 = pl.multiple_of(e * seq + t * TK, TK)
            vt = vt_ref[:, pl.ds(v0, TK)]
            for c in range(2):
                acc_ref[e, c] = alpha[2 * e + c] * acc_ref[e, c] + _dot(vt, p_ref[slot, e, c])

    def kv_pair(i, carry):
        t = 2 * i
        m_state, alpha = carry
        accumulate(t - 2, alpha, 0)
        m_state, alpha = numerators(m_state, 1)
        scores(t, 0)
        accumulate(t - 1, alpha, 1)
        carry = numerators(m_state, 0)
        scores(t + 1, 1)
        return carry

    assert (TQ // TK) % 2 == 0
    scores(0, 0)
    carry = numerators(tuple(jnp.full((1, TQ), M_INIT, F32) for _ in chains), 0)
    scores(1, 1)
    m_state, alpha = lax.fori_loop(1, n_kv // 2, kv_pair, carry)
    accumulate(n_kv - 2, alpha, 0)
    _, alpha = numerators(m_state, 1)
    accumulate(n_kv - 1, alpha, 1)

    lam = (jnp.exp(jnp.sum(lq1_ref[...] * lk1_ref[...], axis=1, keepdims=True))
           - jnp.exp(jnp.sum(lq2_ref[...] * lk2_ref[...], axis=1, keepdims=True)) + lam_init)
    for e in range(ATTN_NB):
        a0 = acc_ref[e, 0]
        a1 = acc_ref[e, 1]
        o = a0[0:DIFF_DV] / a0[DIFF_DV:DIFF_DV + 1] - lam * (a1[0:DIFF_DV] / a1[DIFF_DV:DIFF_DV + 1])
        on = o * lax.rsqrt(jnp.mean(o * o, axis=0, keepdims=True) + EPS) * sg_ref[...]
        out_ref[e] = (on * (1.0 - lam_init)).T.astype(BF16)


def _attention(lam_init, tbl, qd, kd, vdt, pcol, prow, lq1, lk1, lq2, lk2, sg, batch, seq):
    q3, k3 = (t.reshape(batch, seq, GROUP_W) for t in (qd, kd))
    vec = lambda n: pl.BlockSpec((1, n), lambda h, i, b: (0, 0))
    kmax = jnp.max(prow.reshape(seq // BIAS_ROWS, BIAS_ROWS), axis=1)
    qmin = jnp.min(prow.reshape(seq // BIAS_COLS, BIAS_COLS), axis=1)
    in_specs = [
        pl.BlockSpec(memory_space=pltpu.SMEM),
        pl.BlockSpec(memory_space=pltpu.SMEM),
        pl.BlockSpec(memory_space=pltpu.SMEM),
        pl.BlockSpec((ATTN_NB, TQ, 128), lambda h, i, b: (b, i, h)),
        pl.BlockSpec((ATTN_NB, seq, 128), lambda h, i, b: (b, 0, h)),
        pl.BlockSpec((V_ROWS, ATTN_NB * seq), lambda h, i, b: (h, b)),
        pl.BlockSpec((seq, 1), lambda h, i, b: (0, 0)),
        pl.BlockSpec((1, TQ), lambda h, i, b: (0, i)),
        vec(DIFF_D), vec(DIFF_D), vec(DIFF_D), vec(DIFF_D),
        pl.BlockSpec((DIFF_DV, TQ), lambda h, i, b: (0, 0)),
    ]
    out = pl.pallas_call(
        functools.partial(_attn_kernel, lam_init),
        grid=(N_HEADS, seq // TQ, batch // ATTN_NB), in_specs=in_specs,
        out_specs=pl.BlockSpec((ATTN_NB, TQ, 128), lambda h, i, b: (b, i, h)),
        out_shape=jax.ShapeDtypeStruct((batch, seq, GROUP_W), BF16),
        scratch_shapes=[
            pltpu.VMEM((2, seq, TQ), F32),
            pltpu.VMEM((ATTN_NB, 2, TQ, 128), BF16),
            pltpu.VMEM((2, ATTN_NB, 2, TK, TQ), F32),
            pltpu.VMEM((2, ATTN_NB, 2, TK, TQ), BF16),
            pltpu.VMEM((ATTN_NB, 2, V_ROWS, TQ), F32),
        ],
        name="diff_attn",
        compiler_params=pltpu.CompilerParams(
            dimension_semantics=("arbitrary", "arbitrary", "arbitrary"), vmem_limit_bytes=VMEM_LIMIT),
    )(tbl, kmax, qmin, q3, k3, vdt, pcol, prow, lq1, lk1, lq2, lk2, jnp.broadcast_to(sg.reshape(DIFF_DV, 1), (DIFF_DV, TQ)))
    return out.reshape(batch * seq, GROUP_W)


def _gelu(x):
    return 0.5 * x * (1.0 + lax.erf(x * (2.0 ** -0.5)))


def _rms(x, g):
    return x * lax.rsqrt(jnp.mean(x * x, axis=-1, keepdims=True) + EPS) * g


def _mixer_kernel(h_ref, hmt_ref, hd_ref, p_ref, woa_ref, wob_ref, gf_ref, wug_ref, wuv_ref, cwg_ref, cwv_ref,
                  cbg_ref, cbv_ref, wd_ref, gp_ref, wpg_ref, wpp_ref, out_ref,
                  sg_ref, sv_ref, cg_ref, cv_ref, acc_ref, u_ref):
    tm = h_ref.shape[0]
    halo = 8

    @pl.when(pl.program_id(1) == 0)
    def _reset_conv_history():
        cg_ref[...] = jnp.zeros_like(cg_ref)
        cv_ref[...] = jnp.zeros_like(cv_ref)

    h1 = h_ref[...] + _dot_tn(hmt_ref[...], woa_ref[...]) + _dot(hd_ref[...], wob_ref[...])
    u_ref[...] = _rms(h1, gf_ref[...]).astype(BF16)
    acc_ref[...] = jnp.zeros_like(acc_ref)

    def up_stage(j, slot):
        sg_ref[slot, halo:, :] = _dot(u_ref[...], wug_ref[j])
        sv_ref[slot, halo:, :] = _dot(u_ref[...], wuv_ref[j])

    def conv_branch(j, slot, cw_ref, cb_ref, stage_ref, hist_ref):
        stage_ref[slot, 0:halo, :] = hist_ref[j]
        hist_ref[j] = stage_ref[slot, tm:tm + halo, :]
        cw = cw_ref[j]
        return (cw[0:1, :] * stage_ref[slot, halo - 2:halo - 2 + tm, :]
                + cw[1:2, :] * stage_ref[slot, halo - 1:halo - 1 + tm, :]
                + cw[2:3, :] * stage_ref[slot, halo:, :] + cb_ref[j])

    def act_stage(j, slot):
        gate = conv_branch(j, slot, cwg_ref, cbg_ref, sg_ref, cg_ref)
        val = conv_branch(j, slot, cwv_ref, cbv_ref, sv_ref, cv_ref)
        act = (_gelu(gate) * val).astype(BF16)
        acc_ref[...] += _dot(act, wd_ref[j])

    def chunk_pair(i, _):
        j = 2 * i
        up_stage(j + 1, 1)
        act_stage(j, 0)
        up_stage(j + 2, 0)
        act_stage(j + 1, 1)
        return 0

    assert N_FF_CHUNKS % 2 == 1
    up_stage(0, 0)
    lax.fori_loop(0, N_FF_CHUNKS // 2, chunk_pair, 0)
    act_stage(N_FF_CHUNKS - 1, 0)

    h2 = h1 + acc_ref[...]
    u3 = _rms(h2, gp_ref[...]).astype(BF16)
    ple_gate = _sigmoid(_dot(u3, wpg_ref[...]))
    out_ref[...] = h2 + ple_gate * _dot(p_ref[...].astype(BF16), wpp_ref[...])


def _mixer(h, hmt, hd, p, woa, wob, gf, wug, wuv, cwg, cwv, cbg, cbv, wd, gp, wpg, wpp, batch, seq):
    tiles = seq // TM_FFN
    tile = lambda width: pl.BlockSpec((TM_FFN, width), lambda b, t: (b * tiles + t, 0))

    def const(arr):
        nd = arr.ndim
        return pl.BlockSpec(arr.shape, lambda b, t: (0,) * nd, pipeline_mode=pl.Buffered(1))

    weights = (woa, wob, gf, wug, wuv, cwg, cwv, cbg, cbv, wd, gp, wpg, wpp)
    tile_t = pl.BlockSpec((GROUP_W, TM_FFN), lambda b, t: (0, b * tiles + t))
    in_specs = [tile(D_MODEL), tile_t, tile(GROUP_W), tile(PLE_DIM)] + [const(w) for w in weights]
    return pl.pallas_call(
        _mixer_kernel, grid=(batch, tiles), in_specs=in_specs, out_specs=tile(D_MODEL),
        out_shape=jax.ShapeDtypeStruct(h.shape, F32),
        scratch_shapes=[
            pltpu.VMEM((2, TM_FFN + 8, FF_CHUNK), F32),
            pltpu.VMEM((2, TM_FFN + 8, FF_CHUNK), F32),
            pltpu.VMEM((N_FF_CHUNKS, 8, FF_CHUNK), F32),
            pltpu.VMEM((N_FF_CHUNKS, 8, FF_CHUNK), F32),
            pltpu.VMEM((TM_FFN, D_MODEL), F32),
            pltpu.VMEM((TM_FFN, D_MODEL), BF16),
        ],
        name="mixer",
        compiler_params=pltpu.CompilerParams(
            dimension_semantics=("arbitrary", "arbitrary"), vmem_limit_bytes=VMEM_LIMIT),
    )(h, hmt, hd, p, *weights)


def _head_interleave(qcols, kcols):
    lead = qcols.shape[:-1]
    qh = qcols.reshape(lead + (N_HEADS, MLSTM_DK))
    kh = kcols.reshape(lead + (N_HEADS, MLSTM_DK))
    return jnp.concatenate([qh, kh], axis=-1).reshape(lead + (GROUP_W,))


def _ff_chunks(w):
    return jnp.transpose(w.reshape(w.shape[0], N_FF_CHUNKS, FF_CHUNK), (1, 0, 2))


def kernel(x, p, positions, rel_bias, ln_mix_g, w_in, mlstm_conv_w, b_igate, b_fgate, mlstm_norm_g, q_norm_g, k_norm_g, lam_q1, lam_k1, lam_q2, lam_k2, diff_subln_g, w_out, ln_ffn_g, w_up, ffn_conv_w, ffn_conv_b, w_down, ln_ple_g, w_ple_gate, w_ple_proj):
    batch, seq, _ = x.shape
    depth = w_in.shape[0]
    n_tok = batch * seq
    qk_cols = N_HEADS * MLSTM_DK
    col_sizes = [qk_cols, qk_cols, GROUP_W, GROUP_W, N_HEADS, N_HEADS, GROUP_W, GROUP_W, GROUP_W]
    offs = np.concatenate([[0], np.cumsum(col_sizes)])
    sl = lambda a, j: a[..., int(offs[j]):int(offs[j + 1])]

    tbl = jnp.transpose(rel_bias.astype(F32), (1, 2, 0)).reshape(-1) * LOG2E
    pcol = positions.astype(jnp.int32).reshape(seq, 1)
    prow = positions.astype(jnp.int32).reshape(1, seq)
    gsum = jnp.asarray(np.kron(np.eye(GROUP_W // DIFF_D), np.ones((DIFF_D, DIFF_D))), BF16)
    t_idx = np.arange(PAIR)
    same_chunk = (t_idx[:, None] // CHUNK) == (t_idx[None, :] // CHUNK)
    prefix = same_chunk & (t_idx[:, None] <= t_idx[None, :])
    total_a = np.broadcast_to(t_idx[:, None] < CHUNK, (PAIR, PAIR))
    ug = jnp.asarray(np.concatenate([prefix, total_a, ~total_a], axis=1), BF16)
    l2 = jnp.asarray(prefix.T, BF16)

    h = x.reshape(n_tok, D_MODEL)
    for i in range(depth):
        wi = w_in[i]
        wm = jnp.concatenate([_head_interleave(sl(wi, 0), sl(wi, 1)), sl(wi, 6), sl(wi, 7)], axis=-1).astype(BF16)
        wt = jnp.concatenate([sl(wi, 2), sl(wi, 3), sl(wi, 8)], axis=-1).T.astype(BF16)
        pad_rows = lambda a: jnp.zeros((N_GATE_ROWS,) + a.shape[1:], F32).at[:N_HEADS].set(a.astype(F32))
        wg = jnp.concatenate([pad_rows(sl(wi, 4).T), pad_rows(sl(wi, 5).T)], axis=0).astype(BF16)
        gate_bias = jnp.concatenate([pad_rows(b_igate[i]), pad_rows(b_fgate[i])])
        qg = jnp.tile(q_norm_g[i].astype(F32), GROUP_W // DIFF_D).reshape(1, GROUP_W) * (DIFF_D ** -0.5 * LOG2E)
        kg = jnp.tile(k_norm_g[i].astype(F32), GROUP_W // DIFF_D).reshape(1, GROUP_W)
        qk, vmt, omt, gri, grf, gci, gcf, qd, kd, vdt = _inproj(
            h, ln_mix_g[i].reshape(1, D_MODEL), wm, wt, wg, gate_bias.reshape(1, -1), gate_bias.reshape(-1, 1),
            gsum, qg, kg)

        cw = _head_interleave(mlstm_conv_w[i][:, :qk_cols], mlstm_conv_w[i][:, qk_cols:]).astype(F32)
        ng = jnp.broadcast_to(mlstm_norm_g[i].astype(F32).reshape(MLSTM_DV, 1), (MLSTM_DV, PAIR))
        hmt = _mlstm(qk, vmt, omt, gri, grf, gci, gcf, cw, ng, ug, l2, batch, seq)

        lam_init = 0.8 - 0.6 * math.exp(-0.3 * i)
        row64 = lambda a: a[i].reshape(1, DIFF_D).astype(F32)
        hd = _attention(lam_init, tbl, qd, kd, vdt, pcol, prow, row64(lam_q1), row64(lam_k1), row64(lam_q2),
                        row64(lam_k2), diff_subln_g[i].reshape(1, DIFF_DV).astype(F32), batch, seq)

        wo = w_out[i].astype(BF16)
        wu = w_up[i].astype(BF16)
        cwf = ffn_conv_w[i].astype(F32)
        cbf = ffn_conv_b[i].astype(F32).reshape(1, 2 * D_FF)
        h = _mixer(
            h, hmt, hd, p[i].reshape(n_tok, PLE_DIM),
            wo[:GROUP_W], wo[GROUP_W:], ln_ffn_g[i].reshape(1, D_MODEL),
            _ff_chunks(wu[:, :D_FF]), _ff_chunks(wu[:, D_FF:]),
            _ff_chunks(cwf[:, :D_FF]), _ff_chunks(cwf[:, D_FF:]),
            _ff_chunks(cbf[:, :D_FF]), _ff_chunks(cbf[:, D_FF:]),
            w_down[i].astype(BF16).reshape(N_FF_CHUNKS, FF_CHUNK, D_MODEL),
            ln_ple_g[i].reshape(1, D_MODEL), w_ple_gate[i].astype(BF16), w_ple_proj[i].astype(BF16),
            batch, seq)
    return h.reshape(batch, seq, D_MODEL)
```

```python
import functools
import math

import numpy as np
import jax
import jax.numpy as jnp
from jax import lax
from jax.experimental import pallas as pl
from jax.experimental.pallas import tpu as pltpu

F32 = jnp.float32
BF16 = jnp.bfloat16

D_MODEL = 1024
N_HEADS = 4
MLSTM_DK = 64
MLSTM_DV = 128
MLSTM_CONV_W = 4
CHUNK = 64
PAIR = 2 * CHUNK
DIFF_D = 64
DIFF_DV = 128
N_BUCKETS = 32
MAX_DISTANCE = 128
D_FF = 2816
FFN_CONV_W = 3
PLE_DIM = 256
EPS = 1e-6
GROUP_W = N_HEADS * 128
N_GATE_ROWS = 16

MXU_TILE = 256
TM_IN = 512
TM_FFN = 512
FF_CHUNK = 256
N_FF_CHUNKS = D_FF // FF_CHUNK
TQ = 512
TK = 256
BIAS_ROWS = 32
BIAS_COLS = 128
ATTN_NB = 2
NEG_BIG = -1e30
M_INIT = -1e20
ONES_ROWS = 16
V_ROWS = DIFF_DV + ONES_ROWS
LOG2E = math.log2(math.e)
VMEM_LIMIT = 56 * 1024 * 1024


def _dot(a, b):
    return jnp.dot(a, b, preferred_element_type=F32)


def _dot_nt(a, b):
    return lax.dot_general(a, b, (((1,), (1,)), ((), ())), preferred_element_type=F32)


def _dot_tn(a, b):
    return lax.dot_general(a, b, (((0,), (0,)), ((), ())), preferred_element_type=F32)


def _sigmoid(x):
    return 1.0 / (1.0 + jnp.exp(-x))


def _log_sigmoid(x):
    return jnp.minimum(x, 0.0) - jnp.log1p(jnp.exp(-jnp.abs(x)))


def _bucket_thresholds():
    max_exact = N_BUCKETS // 2
    thr = []
    for v in range(1, N_BUCKETS):
        if v <= max_exact:
            thr.append(v)
            continue
        edge = max_exact * (MAX_DISTANCE / max_exact) ** ((v - max_exact) / (N_BUCKETS - max_exact))
        assert abs(edge - round(edge)) > 1e-3, edge
        thr.append(int(math.ceil(edge)))
    assert all(a < b for a, b in zip(thr, thr[1:])), thr
    return tuple(thr)


_BUCKET_THR = _bucket_thresholds()


def _group_mean_square(z, gsum_ref):
    sq = z * z
    hi = sq.astype(BF16)
    lo = (sq - hi.astype(F32)).astype(BF16)
    width = gsum_ref.shape[0]
    sums = [_dot(hi[:, c0:c0 + width], gsum_ref[...]) + _dot(lo[:, c0:c0 + width], gsum_ref[...])
            for c0 in range(0, z.shape[1], width)]
    return jnp.concatenate(sums, axis=1) * (1.0 / DIFF_D)


def _with_ones_rows(dst_ref, vt):
    for hh in range(N_HEADS):
        dst_ref[hh * V_ROWS:hh * V_ROWS + DIFF_DV, :] = vt[hh * DIFF_DV:(hh + 1) * DIFF_DV, :]
        dst_ref[hh * V_ROWS + DIFF_DV:(hh + 1) * V_ROWS, :] = jnp.ones((ONES_ROWS, vt.shape[1]), BF16)


def _inproj_kernel(h_ref, g_ref, wm_ref, wt_ref, wg_ref, bcol_ref, brow_ref, gsum_ref, qg_ref, kg_ref,
                   qk_ref, vmt_ref, omt_ref, gri_ref, grf_ref, gci_ref, gcf_ref, qd_ref, kd_ref, vdt_ref):
    x = h_ref[...]
    ms = jnp.mean(x * x, axis=-1, keepdims=True)
    u = (x * lax.rsqrt(ms + EPS) * g_ref[...]).astype(BF16)
    w = GROUP_W
    qk_ref[...] = _dot(u, wm_ref[:, 0:w])
    zq = _dot(u, wm_ref[:, w:2 * w])
    qd_ref[...] = (zq * lax.rsqrt(_group_mean_square(zq, gsum_ref) + EPS) * qg_ref[...]).astype(BF16)
    zk = _dot(u, wm_ref[:, 2 * w:3 * w])
    kd_ref[...] = (zk * lax.rsqrt(_group_mean_square(zk, gsum_ref) + EPS) * kg_ref[...]).astype(BF16)
    _with_ones_rows(vmt_ref, _dot_nt(wt_ref[0:w, :], u).astype(BF16))
    omt_ref[...] = _dot_nt(wt_ref[w:2 * w, :], u)
    _with_ones_rows(vdt_ref, _dot_nt(wt_ref[2 * w:3 * w, :], u).astype(BF16))
    gc = _dot_nt(u, wg_ref[...]) + bcol_ref[...]
    gci_ref[...] = gc[:, 0:N_GATE_ROWS]
    gcf_ref[...] = gc[:, N_GATE_ROWS:2 * N_GATE_ROWS]
    gr = _dot_nt(wg_ref[...], u) + brow_ref[...]
    for ci in range(TM_IN // PAIR):
        gri_ref[ci] = gr[0:8, ci * PAIR:(ci + 1) * PAIR]
        grf_ref[ci] = gr[N_GATE_ROWS:N_GATE_ROWS + 8, ci * PAIR:(ci + 1) * PAIR]


def _inproj(h, g, wm, wt, wg, bcol, brow, gsum, qg, kg):
    n_tok = h.shape[0]
    grid = (n_tok // TM_IN,)
    const = lambda shape: pl.BlockSpec(shape, lambda i: (0,) * len(shape))
    tile = lambda width: pl.BlockSpec((TM_IN, width), lambda i: (i, 0))
    tile_t = lambda rows: pl.BlockSpec((rows, TM_IN), lambda i: (0, i))
    gate_rows = pl.BlockSpec((TM_IN // PAIR, 8, PAIR), lambda i: (i, 0, 0))
    out_shape = (
        jax.ShapeDtypeStruct((n_tok, GROUP_W), F32),
        jax.ShapeDtypeStruct((N_HEADS * V_ROWS, n_tok), BF16),
        jax.ShapeDtypeStruct((GROUP_W, n_tok), F32),
        jax.ShapeDtypeStruct((n_tok // PAIR, 8, PAIR), F32),
        jax.ShapeDtypeStruct((n_tok // PAIR, 8, PAIR), F32),
        jax.ShapeDtypeStruct((n_tok, N_GATE_ROWS), F32),
        jax.ShapeDtypeStruct((n_tok, N_GATE_ROWS), F32),
        jax.ShapeDtypeStruct((n_tok, GROUP_W), BF16),
        jax.ShapeDtypeStruct((n_tok, GROUP_W), BF16),
        jax.ShapeDtypeStruct((N_HEADS * V_ROWS, n_tok), BF16),
    )
    out_specs = (
        tile(GROUP_W), tile_t(N_HEADS * V_ROWS), tile_t(GROUP_W), gate_rows, gate_rows,
        tile(N_GATE_ROWS), tile(N_GATE_ROWS), tile(GROUP_W), tile(GROUP_W), tile_t(N_HEADS * V_ROWS),
    )
    in_specs = [
        tile(D_MODEL), const((1, D_MODEL)), const(wm.shape), const(wt.shape), const(wg.shape),
        const(bcol.shape), const(brow.shape), const(gsum.shape),
        const((1, GROUP_W)), const((1, GROUP_W)),
    ]
    return pl.pallas_call(
        _inproj_kernel, grid=grid, in_specs=in_specs, out_specs=out_specs, out_shape=out_shape,
        name="inproj",
        compiler_params=pltpu.CompilerParams(dimension_semantics=("arbitrary",), vmem_limit_bytes=VMEM_LIMIT),
    )(h, g, wm, wt, wg, bcol, brow, gsum, qg, kg)


def _split_hi_lo(x):
    hi = x.astype(BF16)
    return hi, (x - hi.astype(F32)).astype(BF16)


def _mlstm_kernel(qk_ref, vt_ref, ot_ref, gri_ref, grf_ref, gci_ref, gcf_ref, cw_ref, ng_ref, ug_ref, l2_ref,
                  out_ref, xpad_ref, cn_ref):
    seq = qk_ref.shape[0]
    halo = 8
    xpad_ref[0:halo, :] = jnp.zeros((halo, GROUP_W), F32)
    xpad_ref[halo:, :] = qk_ref[...]
    cn_ref[...] = jnp.zeros_like(cn_ref)

    first8 = lax.broadcasted_iota(jnp.int32, (8, PAIR), 1) < CHUNK
    first = lax.broadcasted_iota(jnp.int32, (1, PAIR), 1) < CHUNK
    upper = lax.broadcasted_iota(jnp.int32, (PAIR, PAIR), 1) >= MLSTM_DK
    key_t = lax.broadcasted_iota(jnp.int32, (PAIR, PAIR), 0)
    qry_t = lax.broadcasted_iota(jnp.int32, (PAIR, PAIR), 1)
    chunk_start = jnp.where(qry_t < CHUNK, 0, CHUNK)
    cw = cw_ref[...]
    neg_inf = jnp.float32(-jnp.inf)

    def pair_step(c2, m_prev):
        r0 = pl.multiple_of(c2 * PAIR, PAIR)
        win = xpad_ref[pl.ds(r0, PAIR + halo), :]
        conv = cw[0:1, :] * win[halo - 3:halo - 3 + PAIR, :]
        for j in range(1, MLSTM_CONV_W):
            conv = conv + cw[j:j + 1, :] * win[halo - 3 + j:halo - 3 + j + PAIR, :]
        x = conv * _sigmoid(conv)

        gi = gri_ref[c2]
        lf_hi, lf_lo = _split_hi_lo(_log_sigmoid(grf_ref[c2]))
        bg = _dot(lf_hi, ug_ref[...]) + _dot(lf_lo, ug_ref[...])
        b_r = bg[:, 0:PAIR]
        g_a = bg[:, PAIR:2 * PAIR]
        g_b = bg[:, 2 * PAIR:3 * PAIR]
        a_r = jnp.where(first8, g_a, g_b) - b_r + gi
        max_a = jnp.max(jnp.where(first8, a_r, neg_inf), axis=1, keepdims=True)
        max_b = jnp.max(jnp.where(first8, neg_inf, a_r), axis=1, keepdims=True)
        m_a = jnp.maximum(g_a + m_prev, max_a)
        m_b = jnp.maximum(g_b + m_a, max_b)
        dec_a = jnp.exp(g_a + m_prev - m_a)
        dec_b = jnp.exp(g_b + m_a - m_b)
        w_r = jnp.exp(a_r - jnp.where(first8, m_a, m_b))
        e_r = b_r + jnp.where(first8, m_prev, m_a)
        lc_hi, lc_lo = _split_hi_lo(_log_sigmoid(gcf_ref[pl.ds(r0, PAIR), :]))
        x_c = _dot(l2_ref[...], lc_hi) + _dot(l2_ref[...], lc_lo) - gci_ref[pl.ds(r0, PAIR), :]

        for hd in range(N_HEADS):
            rows = slice(hd * MLSTM_DV, (hd + 1) * MLSTM_DV)
            xh = x[:, hd * 128:(hd + 1) * 128]
            xs = pltpu.roll(xh, MLSTM_DK, 1)
            xk = jnp.where(upper, xh * (MLSTM_DK ** -0.5), 0.0).astype(BF16)
            xq = jnp.where(upper, xs, 0.0).astype(BF16)
            vt = vt_ref[hd * V_ROWS:(hd + 1) * V_ROWS, pl.ds(r0, PAIR)]
            vt32 = vt.astype(F32)
            w_h = w_r[hd:hd + 1, :]
            cn0 = cn_ref[hd]
            up_a = _dot((vt32 * jnp.where(first, w_h, 0.0)).astype(BF16), xk)
            cn1 = dec_a[hd:hd + 1, :] * cn0 + up_a
            up_b = _dot((vt32 * jnp.where(first, 0.0, w_h)).astype(BF16), xk)
            cn_ref[hd] = dec_b[hd:hd + 1, :] * cn1 + up_b
            dmat = b_r[hd:hd + 1, :] - x_c[:, hd:hd + 1]
            dmat = jnp.where(key_t <= qry_t, jnp.where(key_t >= chunk_start, dmat, neg_inf), neg_inf)
            e_h = e_r[hd:hd + 1, :]
            m_out = jnp.maximum(e_h, jnp.max(dmat, axis=0, keepdims=True))
            s_t = (_dot_nt(xk, xq) * jnp.exp(dmat - m_out)).astype(BF16)
            inter = jnp.exp(e_h - m_out)
            cq = jnp.where(first, _dot_nt(cn0.astype(BF16), xq), _dot_nt(cn1.astype(BF16), xq))
            tot = inter * cq + _dot(vt, s_t)
            den = tot[MLSTM_DV:MLSTM_DV + 1, :]
            hh = tot[0:MLSTM_DV, :] / jnp.maximum(jnp.abs(den), jnp.exp(-m_out))
            hn = hh * lax.rsqrt(jnp.mean(hh * hh, axis=0, keepdims=True) + EPS) * ng_ref[...]
            og = ot_ref[rows, pl.ds(r0, PAIR)]
            out_ref[rows, pl.ds(r0, PAIR)] = (hn * _sigmoid(og)).astype(BF16)
        return m_b

    lax.fori_loop(0, seq // PAIR, pair_step, jnp.zeros((8, PAIR), F32))


def _mlstm(qk, vmt, omt, gri, grf, gci, gcf, cw, ng, ug, l2, batch, seq):
    const = lambda arr: pl.BlockSpec(arr.shape, lambda b: (0,) * arr.ndim)
    in_specs = [
        pl.BlockSpec((seq, GROUP_W), lambda b: (b, 0)),
        pl.BlockSpec((N_HEADS * V_ROWS, seq), lambda b: (0, b)),
        pl.BlockSpec((GROUP_W, seq), lambda b: (0, b)),
        pl.BlockSpec((seq // PAIR, 8, PAIR), lambda b: (b, 0, 0)),
        pl.BlockSpec((seq // PAIR, 8, PAIR), lambda b: (b, 0, 0)),
        pl.BlockSpec((seq, N_GATE_ROWS), lambda b: (b, 0)),
        pl.BlockSpec((seq, N_GATE_ROWS), lambda b: (b, 0)),
        const(cw), const(ng), const(ug), const(l2),
    ]
    return pl.pallas_call(
        _mlstm_kernel, grid=(batch,), in_specs=in_specs,
        out_specs=pl.BlockSpec((GROUP_W, seq), lambda b: (0, b)),
        out_shape=jax.ShapeDtypeStruct((GROUP_W, batch * seq), BF16),
        scratch_shapes=[
            pltpu.VMEM((seq + 8, GROUP_W), F32),
            pltpu.VMEM((N_HEADS, V_ROWS, 128), F32),
        ],
        name="mlstm",
        compiler_params=pltpu.CompilerParams(dimension_semantics=("arbitrary",), vmem_limit_bytes=VMEM_LIMIT),
    )(qk, vmt, omt, gri, grf, gci, gcf, cw, ng, ug, l2)


def _attn_kernel(lam_init, tbl_ref, kmax_ref, qmin_ref, q_ref, k_ref, vt_ref, pcol_ref, prow_ref,
                 lq1_ref, lk1_ref, lq2_ref, lk2_ref, sg_ref, out_ref, bias_ref, qc_ref, s_ref, p_ref, acc_ref):
    hd = pl.program_id(0)
    qi = pl.program_id(1)
    b = pl.program_id(2)
    n_kv = (qi + 1) * (TQ // TK)

    @pl.when(b == 0)
    def _build_bias():
        rowi = lax.broadcasted_iota(jnp.int32, (BIAS_ROWS, BIAS_COLS), 0)
        coli = lax.broadcasted_iota(jnp.int32, (BIAS_ROWS, BIAS_COLS), 1)
        row = lambda c, v: tbl_ref[c * N_BUCKETS + v:c * N_BUCKETS + v + 1, :]
        last = N_BUCKETS - 1

        def build(t, _):
            r0 = pl.multiple_of(t * BIAS_ROWS, BIAS_ROWS)
            pk = pcol_ref[pl.ds(r0, BIAS_ROWS), :]
            kmax = kmax_ref[t]
            for j in range(TQ // BIAS_COLS):
                lanes = slice(j * BIAS_COLS, (j + 1) * BIAS_COLS)
                q0 = qi * TQ + j * BIAS_COLS
                causal = rowi + r0 <= coli + q0
                flat = jnp.logical_or(qmin_ref[qi * (TQ // BIAS_COLS) + j] - kmax >= _BUCKET_THR[-1],
                                      r0 > q0 + BIAS_COLS - 1)

                @pl.when(flat)
                def _flat():
                    bias_ref[0, pl.ds(r0, BIAS_ROWS), lanes] = jnp.where(causal, row(0, last), NEG_BIG)
                    bias_ref[1, pl.ds(r0, BIAS_ROWS), lanes] = jnp.where(causal, row(1, last), NEG_BIG)

                @pl.when(jnp.logical_not(flat))
                def _lookup():
                    dist = jnp.maximum(prow_ref[:, lanes] - pk, 0)
                    b0 = jnp.broadcast_to(row(0, 0), (BIAS_ROWS, BIAS_COLS))
                    b1 = jnp.broadcast_to(row(1, 0), (BIAS_ROWS, BIAS_COLS))
                    for v, thr in enumerate(_BUCKET_THR, start=1):
                        ge = dist >= thr
                        b0 = jnp.where(ge, row(0, v), b0)
                        b1 = jnp.where(ge, row(1, v), b1)
                    bias_ref[0, pl.ds(r0, BIAS_ROWS), lanes] = jnp.where(causal, b0, NEG_BIG)
                    bias_ref[1, pl.ds(r0, BIAS_ROWS), lanes] = jnp.where(causal, b1, NEG_BIG)
            return 0

        lax.fori_loop(0, n_kv * (TK // BIAS_ROWS), build, 0)

    seq = k_ref.shape[1]
    lane = lax.broadcasted_iota(jnp.int32, (TQ, 128), 1)
    for e in range(ATTN_NB):
        q = q_ref[e]
        zero = jnp.zeros_like(q)
        qc_ref[e, 0] = jnp.where(lane < DIFF_D, q, zero)
        qc_ref[e, 1] = jnp.where(lane >= DIFF_D, q, zero)
    acc_ref[...] = jnp.zeros_like(acc_ref)

    chains = [(e, c) for e in range(ATTN_NB) for c in range(2)]

    def scores(t, slot):
        k0 = pl.multiple_of(t * TK, TK)
        for e in range(ATTN_NB):
            kb = k_ref[e, pl.ds(k0, TK), :]
            for c in range(2):
                s_ref[slot, e, c] = _dot_nt(kb, qc_ref[e, c]) + bias_ref[c, pl.ds(k0, TK), :]

    def numerators(m_state, slot):
        m_next, alpha_next = [], []
        for i, (e, c) in enumerate(chains):
            s = s_ref[slot, e, c]
            m_new = jnp.maximum(m_state[i], jnp.max(s, axis=0, keepdims=True))
            alpha_next.append(jnp.exp2(m_state[i] - m_new))
            p_ref[slot, e, c] = jnp.exp2(s - m_new).astype(BF16)
            m_next.append(m_new)
        return tuple(m_next), tuple(alpha_next)

    def accumulate(t, alpha, slot):
        for e in range(ATTN_NB):
            v0 = pl.multiple_of(e * seq + t * TK, TK)
            vt = vt_ref[:, pl.ds(v0, TK)]
            for c in range(2):
                acc_ref[e, c] = alpha[2 * e + c] * acc_ref[e, c] + _dot(vt, p_ref[slot, e, c])

    def kv_pair(i, carry):
        t = 2 * i
        m_state, alpha = carry
        accumulate(t - 2, alpha, 0)
        m_state, alpha = numerators(m_state, 1)
        scores(t, 0)
        accumulate(t - 1, alpha, 1)
        carry = numerators(m_state, 0)
        scores(t + 1, 1)
        return carry

    assert (TQ // TK) % 2 == 0
    scores(0, 0)
    carry = numerators(tuple(jnp.full((1, TQ), M_INIT, F32) for _ in chains), 0)
    scores(1, 1)
    m_state, alpha = lax.fori_loop(1, n_kv // 2, kv_pair, carry)
    accumulate(n_kv - 2, alpha, 0)
    _, alpha = numerators(m_state, 1)
    accumulate(n_kv - 1, alpha, 1)

    lam = (jnp.exp(jnp.sum(lq1_ref[...] * lk1_ref[...], axis=1, keepdims=True))
           - jnp.exp(jnp.sum(lq2_ref[...] * lk2_ref[...], axis=1, keepdims=True)) + lam_init)
    for e in range(ATTN_NB):
        a0 = acc_ref[e, 0]
        a1 = acc_ref[e, 1]
        o = a0[0:DIFF_DV] / a0[DIFF_DV:DIFF_DV + 1] - lam * (a1[0:DIFF_DV] / a1[DIFF_DV:DIFF_DV + 1])
        on = o * lax.rsqrt(jnp.mean(o * o, axis=0, keepdims=True) + EPS) * sg_ref[...]
        out_ref[e] = (on * (1.0 - lam_init)).T.astype(BF16)


def _attention(lam_init, tbl, qd, kd, vdt, pcol, prow, lq1, lk1, lq2, lk2, sg, batch, seq):
    q3, k3 = (t.reshape(batch, seq, GROUP_W) for t in (qd, kd))
    vec = lambda n: pl.BlockSpec((1, n), lambda h, i, b: (0, 0))
    kmax = jnp.max(prow.reshape(seq // BIAS_ROWS, BIAS_ROWS), axis=1)
    qmin = jnp.min(prow.reshape(seq // BIAS_COLS, BIAS_COLS), axis=1)
    in_specs = [
        pl.BlockSpec((None, 2 * N_BUCKETS, BIAS_COLS), lambda h, i, b: (h, 0, 0)),
        pl.BlockSpec(memory_space=pltpu.SMEM),
        pl.BlockSpec(memory_space=pltpu.SMEM),
        pl.BlockSpec((ATTN_NB, TQ, 128), lambda h, i, b: (b, i, h)),
        pl.BlockSpec((ATTN_NB, seq, 128), lambda h, i, b: (b, 0, h)),
        pl.BlockSpec((V_ROWS, ATTN_NB * seq), lambda h, i, b: (h, b)),
        pl.BlockSpec((seq, BIAS_COLS), lambda h, i, b: (0, 0)),
        pl.BlockSpec((1, TQ), lambda h, i, b: (0, i)),
        vec(DIFF_D), vec(DIFF_D), vec(DIFF_D), vec(DIFF_D),
        pl.BlockSpec((DIFF_DV, TQ), lambda h, i, b: (0, 0)),
    ]
    out = pl.pallas_call(
        functools.partial(_attn_kernel, lam_init),
        grid=(N_HEADS, seq // TQ, batch // ATTN_NB), in_specs=in_specs,
        out_specs=pl.BlockSpec((ATTN_NB, TQ, 128), lambda h, i, b: (b, i, h)),
        out_shape=jax.ShapeDtypeStruct((batch, seq, GROUP_W), BF16),
        scratch_shapes=[
            pltpu.VMEM((2, seq, TQ), F32),
            pltpu.VMEM((ATTN_NB, 2, TQ, 128), BF16),
            pltpu.VMEM((2, ATTN_NB, 2, TK, TQ), F32),
            pltpu.VMEM((2, ATTN_NB, 2, TK, TQ), BF16),
            pltpu.VMEM((ATTN_NB, 2, V_ROWS, TQ), F32),
        ],
        name="diff_attn",
        compiler_params=pltpu.CompilerParams(
            dimension_semantics=("arbitrary", "arbitrary", "arbitrary"), vmem_limit_bytes=VMEM_LIMIT),
    )(tbl, kmax, qmin, q3, k3, vdt, pcol, prow, lq1, lk1, lq2, lk2, jnp.broadcast_to(sg.reshape(DIFF_DV, 1), (DIFF_DV, TQ)))
    return out.reshape(batch * seq, GROUP_W)


def _gelu(x):
    return 0.5 * x * (1.0 + lax.erf(x * (2.0 ** -0.5)))


def _rms(x, g):
    return x * lax.rsqrt(jnp.mean(x * x, axis=-1, keepdims=True) + EPS) * g


def _mixer_kernel(h_ref, hmt_ref, hd_ref, p_ref, woa_ref, wob_ref, gf_ref, wug_ref, wuv_ref, cwg_ref, cwv_ref,
                  cbg_ref, cbv_ref, wd_ref, gp_ref, wpg_ref, wpp_ref, out_ref,
                  sg_ref, sv_ref, cg_ref, cv_ref, acc_ref, u_ref):
    tm = h_ref.shape[0]
    halo = 8

    @pl.when(pl.program_id(1) == 0)
    def _reset_conv_history():
        cg_ref[...] = jnp.zeros_like(cg_ref)
        cv_ref[...] = jnp.zeros_like(cv_ref)

    h1 = h_ref[...] + _dot_tn(hmt_ref[...], woa_ref[...]) + _dot(hd_ref[...], wob_ref[...])
    u_ref[...] = _rms(h1, gf_ref[...]).astype(BF16)
    acc_ref[...] = jnp.zeros_like(acc_ref)

    def up_stage(j, slot):
        sg_ref[slot, halo:, :] = _dot(u_ref[...], wug_ref[j])
        sv_ref[slot, halo:, :] = _dot(u_ref[...], wuv_ref[j])

    def conv_branch(j, slot, cw_ref, cb_ref, stage_ref, hist_ref):
        stage_ref[slot, 0:halo, :] = hist_ref[j]
        hist_ref[j] = stage_ref[slot, tm:tm + halo, :]
        cw = cw_ref[j]
        return (cw[0:1, :] * stage_ref[slot, halo - 2:halo - 2 + tm, :]
                + cw[1:2, :] * stage_ref[slot, halo - 1:halo - 1 + tm, :]
                + cw[2:3, :] * stage_ref[slot, halo:, :] + cb_ref[j])

    def act_stage(j, slot):
        gate = conv_branch(j, slot, cwg_ref, cbg_ref, sg_ref, cg_ref)
        val = conv_branch(j, slot, cwv_ref, cbv_ref, sv_ref, cv_ref)
        act = (_gelu(gate) * val).astype(BF16)
        acc_ref[...] += _dot(act, wd_ref[j])

    def chunk_pair(i, _):
        j = 2 * i
        up_stage(j + 1, 1)
        act_stage(j, 0)
        up_stage(j + 2, 0)
        act_stage(j + 1, 1)
        return 0

    assert N_FF_CHUNKS % 2 == 1
    up_stage(0, 0)
    lax.fori_loop(0, N_FF_CHUNKS // 2, chunk_pair, 0)
    act_stage(N_FF_CHUNKS - 1, 0)

    h2 = h1 + acc_ref[...]
    u3 = _rms(h2, gp_ref[...]).astype(BF16)
    ple_gate = _sigmoid(_dot(u3, wpg_ref[...]))
    out_ref[...] = h2 + ple_gate * _dot(p_ref[...].astype(BF16), wpp_ref[...])


def _mixer(h, hmt, hd, p, woa, wob, gf, wug, wuv, cwg, cwv, cbg, cbv, wd, gp, wpg, wpp, batch, seq):
    tiles = seq // TM_FFN
    tile = lambda width: pl.BlockSpec((TM_FFN, width), lambda b, t: (b * tiles + t, 0))

    def const(arr):
        nd = arr.ndim
        return pl.BlockSpec(arr.shape, lambda b, t: (0,) * nd, pipeline_mode=pl.Buffered(1))

    weights = (woa, wob, gf, wug, wuv, cwg, cwv, cbg, cbv, wd, gp, wpg, wpp)
    tile_t = pl.BlockSpec((GROUP_W, TM_FFN), lambda b, t: (0, b * tiles + t))
    in_specs = [tile(D_MODEL), tile_t, tile(GROUP_W), tile(PLE_DIM)] + [const(w) for w in weights]
    return pl.pallas_call(
        _mixer_kernel, grid=(batch, tiles), in_specs=in_specs, out_specs=tile(D_MODEL),
        out_shape=jax.ShapeDtypeStruct(h.shape, F32),
        scratch_shapes=[
            pltpu.VMEM((2, TM_FFN + 8, FF_CHUNK), F32),
            pltpu.VMEM((2, TM_FFN + 8, FF_CHUNK), F32),
            pltpu.VMEM((N_FF_CHUNKS, 8, FF_CHUNK), F32),
            pltpu.VMEM((N_FF_CHUNKS, 8, FF_CHUNK), F32),
            pltpu.VMEM((TM_FFN, D_MODEL), F32),
            pltpu.VMEM((TM_FFN, D_MODEL), BF16),
        ],
        name="mixer",
        compiler_params=pltpu.CompilerParams(
            dimension_semantics=("arbitrary", "arbitrary"), vmem_limit_bytes=VMEM_LIMIT),
    )(h, hmt, hd, p, *weights)


def _head_interleave(qcols, kcols):
    lead = qcols.shape[:-1]
    qh = qcols.reshape(lead + (N_HEADS, MLSTM_DK))
    kh = kcols.reshape(lead + (N_HEADS, MLSTM_DK))
    return jnp.concatenate([qh, kh], axis=-1).reshape(lead + (GROUP_W,))


def _ff_chunks(w):
    return jnp.transpose(w.reshape(w.shape[0], N_FF_CHUNKS, FF_CHUNK), (1, 0, 2))


def kernel(x, p, positions, rel_bias, ln_mix_g, w_in, mlstm_conv_w, b_igate, b_fgate, mlstm_norm_g, q_norm_g, k_norm_g, lam_q1, lam_k1, lam_q2, lam_k2, diff_subln_g, w_out, ln_ffn_g, w_up, ffn_conv_w, ffn_conv_b, w_down, ln_ple_g, w_ple_gate, w_ple_proj):
    batch, seq, _ = x.shape
    depth = w_in.shape[0]
    n_tok = batch * seq
    qk_cols = N_HEADS * MLSTM_DK
    col_sizes = [qk_cols, qk_cols, GROUP_W, GROUP_W, N_HEADS, N_HEADS, GROUP_W, GROUP_W, GROUP_W]
    offs = np.concatenate([[0], np.cumsum(col_sizes)])
    sl = lambda a, j: a[..., int(offs[j]):int(offs[j + 1])]

    tbl = jnp.broadcast_to(
        (jnp.transpose(rel_bias.astype(F32), (1, 2, 0)) * LOG2E).reshape(N_HEADS, 2 * N_BUCKETS, 1),
        (N_HEADS, 2 * N_BUCKETS, BIAS_COLS))
    pcol = jnp.broadcast_to(positions.astype(jnp.int32).reshape(seq, 1), (seq, BIAS_COLS))
    prow = positions.astype(jnp.int32).reshape(1, seq)
    gsum = jnp.asarray(np.kron(np.eye(MXU_TILE // DIFF_D), np.ones((DIFF_D, DIFF_D))), BF16)
    t_idx = np.arange(PAIR)
    same_chunk = (t_idx[:, None] // CHUNK) == (t_idx[None, :] // CHUNK)
    prefix = same_chunk & (t_idx[:, None] <= t_idx[None, :])
    total_a = np.broadcast_to(t_idx[:, None] < CHUNK, (PAIR, PAIR))
    ug = jnp.asarray(np.concatenate([prefix, total_a, ~total_a], axis=1), BF16)
    l2 = jnp.asarray(prefix.T, BF16)

    h = x.reshape(n_tok, D_MODEL)
    for i in range(depth):
        wi = w_in[i]
        wm = jnp.concatenate([_head_interleave(sl(wi, 0), sl(wi, 1)), sl(wi, 6), sl(wi, 7)], axis=-1).astype(BF16)
        wt = jnp.concatenate([sl(wi, 2), sl(wi, 3), sl(wi, 8)], axis=-1).T.astype(BF16)
        pad_rows = lambda a: jnp.zeros((N_GATE_ROWS,) + a.shape[1:], F32).at[:N_HEADS].set(a.astype(F32))
        wg = jnp.concatenate([pad_rows(sl(wi, 4).T), pad_rows(sl(wi, 5).T)], axis=0).astype(BF16)
        gate_bias = jnp.concatenate([pad_rows(b_igate[i]), pad_rows(b_fgate[i])])
        qg = jnp.tile(q_norm_g[i].astype(F32), GROUP_W // DIFF_D).reshape(1, GROUP_W) * (DIFF_D ** -0.5 * LOG2E)
        kg = jnp.tile(k_norm_g[i].astype(F32), GROUP_W // DIFF_D).reshape(1, GROUP_W)
        qk, vmt, omt, gri, grf, gci, gcf, qd, kd, vdt = _inproj(
            h, ln_mix_g[i].reshape(1, D_MODEL), wm, wt, wg, gate_bias.reshape(1, -1), gate_bias.reshape(-1, 1),
            gsum, qg, kg)

        cw = _head_interleave(mlstm_conv_w[i][:, :qk_cols], mlstm_conv_w[i][:, qk_cols:]).astype(F32)
        ng = jnp.broadcast_to(mlstm_norm_g[i].astype(F32).reshape(MLSTM_DV, 1), (MLSTM_DV, PAIR))
        hmt = _mlstm(qk, vmt, omt, gri, grf, gci, gcf, cw, ng, ug, l2, batch, seq)

        lam_init = 0.8 - 0.6 * math.exp(-0.3 * i)
        row64 = lambda a: a[i].reshape(1, DIFF_D).astype(F32)
        hd = _attention(lam_init, tbl, qd, kd, vdt, pcol, prow, row64(lam_q1), row64(lam_k1), row64(lam_q2),
                        row64(lam_k2), diff_subln_g[i].reshape(1, DIFF_DV).astype(F32), batch, seq)

        wo = w_out[i].astype(BF16)
        wu = w_up[i].astype(BF16)
        cwf = ffn_conv_w[i].astype(F32)
        cbf = ffn_conv_b[i].astype(F32).reshape(1, 2 * D_FF)
        h = _mixer(
            h, hmt, hd, p[i].reshape(n_tok, PLE_DIM),
            wo[:GROUP_W], wo[GROUP_W:], ln_ffn_g[i].reshape(1, D_MODEL),
            _ff_chunks(wu[:, :D_FF]), _ff_chunks(wu[:, D_FF:]),
            _ff_chunks(cwf[:, :D_FF]), _ff_chunks(cwf[:, D_FF:]),
            _ff_chunks(cbf[:, :D_FF]), _ff_chunks(cbf[:, D_FF:]),
            w_down[i].astype(BF16).reshape(N_FF_CHUNKS, FF_CHUNK, D_MODEL),
            ln_ple_g[i].reshape(1, D_MODEL), w_ple_gate[i].astype(BF16), w_ple_proj[i].astype(BF16),
            batch, seq)
    return h.reshape(batch, seq, D_MODEL)
```

```python
import functools
import math

import numpy as np
import jax
import jax.numpy as jnp
from jax import lax
from jax.experimental import pallas as pl
from jax.experimental.pallas import tpu as pltpu

F32 = jnp.float32
BF16 = jnp.bfloat16

D_MODEL = 1024
N_HEADS = 4
MLSTM_DK = 64
MLSTM_DV = 128
MLSTM_CONV_W = 4
CHUNK = 64
PAIR = 2 * CHUNK
DIFF_D = 64
DIFF_DV = 128
N_BUCKETS = 32
MAX_DISTANCE = 128
D_FF = 2816
FFN_CONV_W = 3
PLE_DIM = 256
EPS = 1e-6
GROUP_W = N_HEADS * 128
N_GATE_ROWS = 16

MXU_TILE = 256
TM_IN = 512
TM_FFN = 512
FF_CHUNK = 256
N_FF_CHUNKS = D_FF // FF_CHUNK
TQ = 512
TK = 256
BIAS_ROWS = 32
BIAS_COLS = 128
ATTN_NB = 2
NEG_BIG = -1e30
M_INIT = -1e20
ONES_ROWS = 16
V_ROWS = DIFF_DV + ONES_ROWS
LOG2E = math.log2(math.e)
VMEM_LIMIT = 56 * 1024 * 1024


def _dot(a, b):
    return jnp.dot(a, b, preferred_element_type=F32)


def _dot_nt(a, b):
    return lax.dot_general(a, b, (((1,), (1,)), ((), ())), preferred_element_type=F32)


def _dot_tn(a, b):
    return lax.dot_general(a, b, (((0,), (0,)), ((), ())), preferred_element_type=F32)


def _sigmoid(x):
    return 1.0 / (1.0 + jnp.exp(-x))


def _log_sigmoid(x):
    return jnp.minimum(x, 0.0) - jnp.log1p(jnp.exp(-jnp.abs(x)))


def _bucket_thresholds():
    max_exact = N_BUCKETS // 2
    thr = []
    for v in range(1, N_BUCKETS):
        if v <= max_exact:
            thr.append(v)
            continue
        edge = max_exact * (MAX_DISTANCE / max_exact) ** ((v - max_exact) / (N_BUCKETS - max_exact))
        assert abs(edge - round(edge)) > 1e-3, edge
        thr.append(int(math.ceil(edge)))
    assert all(a < b for a, b in zip(thr, thr[1:])), thr
    return tuple(thr)


_BUCKET_THR = _bucket_thresholds()


def _group_mean_square(z, gsum_ref):
    sq = z * z
    hi = sq.astype(BF16)
    lo = (sq - hi.astype(F32)).astype(BF16)
    width = gsum_ref.shape[0]
    sums = [_dot(hi[:, c0:c0 + width], gsum_ref[...]) + _dot(lo[:, c0:c0 + width], gsum_ref[...])
            for c0 in range(0, z.shape[1], width)]
    return jnp.concatenate(sums, axis=1) * (1.0 / DIFF_D)


def _with_ones_rows(dst_ref, vt):
    for hh in range(N_HEADS):
        dst_ref[hh * V_ROWS:hh * V_ROWS + DIFF_DV, :] = vt[hh * DIFF_DV:(hh + 1) * DIFF_DV, :]
        dst_ref[hh * V_ROWS + DIFF_DV:(hh + 1) * V_ROWS, :] = jnp.ones((ONES_ROWS, vt.shape[1]), BF16)


def _inproj_kernel(h_ref, g_ref, wm_ref, wt_ref, wg_ref, bcol_ref, brow_ref, gsum_ref, qg_ref, kg_ref,
                   qk_ref, vmt_ref, omt_ref, gri_ref, grf_ref, gci_ref, gcf_ref, qd_ref, kd_ref, vdt_ref):
    x = h_ref[...]
    ms = jnp.mean(x * x, axis=-1, keepdims=True)
    u = (x * lax.rsqrt(ms + EPS) * g_ref[...]).astype(BF16)
    w = GROUP_W
    qk_ref[...] = _dot(u, wm_ref[:, 0:w])
    zq = _dot(u, wm_ref[:, w:2 * w])
    qd_ref[...] = (zq * lax.rsqrt(_group_mean_square(zq, gsum_ref) + EPS) * qg_ref[...]).astype(BF16)
    zk = _dot(u, wm_ref[:, 2 * w:3 * w])
    kd_ref[...] = (zk * lax.rsqrt(_group_mean_square(zk, gsum_ref) + EPS) * kg_ref[...]).astype(BF16)
    _with_ones_rows(vmt_ref, _dot_nt(wt_ref[0:w, :], u).astype(BF16))
    omt_ref[...] = _dot_nt(wt_ref[w:2 * w, :], u)
    _with_ones_rows(vdt_ref, _dot_nt(wt_ref[2 * w:3 * w, :], u).astype(BF16))
    gc = _dot_nt(u, wg_ref[...]) + bcol_ref[...]
    gci_ref[...] = gc[:, 0:N_GATE_ROWS]
    gcf_ref[...] = gc[:, N_GATE_ROWS:2 * N_GATE_ROWS]
    gr = _dot_nt(wg_ref[...], u) + brow_ref[...]
    for ci in range(TM_IN // PAIR):
        gri_ref[ci] = gr[0:8, ci * PAIR:(ci + 1) * PAIR]
        grf_ref[ci] = gr[N_GATE_ROWS:N_GATE_ROWS + 8, ci * PAIR:(ci + 1) * PAIR]


def _inproj(h, g, wm, wt, wg, bcol, brow, gsum, qg, kg):
    n_tok = h.shape[0]
    grid = (n_tok // TM_IN,)
    const = lambda shape: pl.BlockSpec(shape, lambda i: (0,) * len(shape))
    tile = lambda width: pl.BlockSpec((TM_IN, width), lambda i: (i, 0))
    tile_t = lambda rows: pl.BlockSpec((rows, TM_IN), lambda i: (0, i))
    gate_rows = pl.BlockSpec((TM_IN // PAIR, 8, PAIR), lambda i: (i, 0, 0))
    out_shape = (
        jax.ShapeDtypeStruct((n_tok, GROUP_W), F32),
        jax.ShapeDtypeStruct((N_HEADS * V_ROWS, n_tok), BF16),
        jax.ShapeDtypeStruct((GROUP_W, n_tok), F32),
        jax.ShapeDtypeStruct((n_tok // PAIR, 8, PAIR), F32),
        jax.ShapeDtypeStruct((n_tok // PAIR, 8, PAIR), F32),
        jax.ShapeDtypeStruct((n_tok, N_GATE_ROWS), F32),
        jax.ShapeDtypeStruct((n_tok, N_GATE_ROWS), F32),
        jax.ShapeDtypeStruct((n_tok, GROUP_W), BF16),
        jax.ShapeDtypeStruct((n_tok, GROUP_W), BF16),
        jax.ShapeDtypeStruct((N_HEADS * V_ROWS, n_tok), BF16),
    )
    out_specs = (
        tile(GROUP_W), tile_t(N_HEADS * V_ROWS), tile_t(GROUP_W), gate_rows, gate_rows,
        tile(N_GATE_ROWS), tile(N_GATE_ROWS), tile(GROUP_W), tile(GROUP_W), tile_t(N_HEADS * V_ROWS),
    )
    in_specs = [
        tile(D_MODEL), const((1, D_MODEL)), const(wm.shape), const(wt.shape), const(wg.shape),
        const(bcol.shape), const(brow.shape), const(gsum.shape),
        const((1, GROUP_W)), const((1, GROUP_W)),
    ]
    return pl.pallas_call(
        _inproj_kernel, grid=grid, in_specs=in_specs, out_specs=out_specs, out_shape=out_shape,
        name="inproj",
        compiler_params=pltpu.CompilerParams(dimension_semantics=("arbitrary",), vmem_limit_bytes=VMEM_LIMIT),
    )(h, g, wm, wt, wg, bcol, brow, gsum, qg, kg)


def _split_hi_lo(x):
    hi = x.astype(BF16)
    return hi, (x - hi.astype(F32)).astype(BF16)


def _mlstm_kernel(qk_ref, vt_ref, ot_ref, gri_ref, grf_ref, gci_ref, gcf_ref, cw_ref, ng_ref, ug_ref, l2_ref,
                  out_ref, xpad_ref, cn_ref):
    seq = qk_ref.shape[0]
    halo = 8
    xpad_ref[0:halo, :] = jnp.zeros((halo, GROUP_W), F32)
    xpad_ref[halo:, :] = qk_ref[...]
    cn_ref[...] = jnp.zeros_like(cn_ref)

    first8 = lax.broadcasted_iota(jnp.int32, (8, PAIR), 1) < CHUNK
    first = lax.broadcasted_iota(jnp.int32, (1, PAIR), 1) < CHUNK
    upper = lax.broadcasted_iota(jnp.int32, (PAIR, PAIR), 1) >= MLSTM_DK
    key_t = lax.broadcasted_iota(jnp.int32, (PAIR, PAIR), 0)
    qry_t = lax.broadcasted_iota(jnp.int32, (PAIR, PAIR), 1)
    chunk_start = jnp.where(qry_t < CHUNK, 0, CHUNK)
    cw = cw_ref[...]
    neg_inf = jnp.float32(-jnp.inf)

    def pair_step(c2, m_prev):
        r0 = pl.multiple_of(c2 * PAIR, PAIR)
        win = xpad_ref[pl.ds(r0, PAIR + halo), :]
        conv = cw[0:1, :] * win[halo - 3:halo - 3 + PAIR, :]
        for j in range(1, MLSTM_CONV_W):
            conv = conv + cw[j:j + 1, :] * win[halo - 3 + j:halo - 3 + j + PAIR, :]
        x = conv * _sigmoid(conv)

        gi = gri_ref[c2]
        lf_hi, lf_lo = _split_hi_lo(_log_sigmoid(grf_ref[c2]))
        bg = _dot(lf_hi, ug_ref[...]) + _dot(lf_lo, ug_ref[...])
        b_r = bg[:, 0:PAIR]
        g_a = bg[:, PAIR:2 * PAIR]
        g_b = bg[:, 2 * PAIR:3 * PAIR]
        a_r = jnp.where(first8, g_a, g_b) - b_r + gi
        max_a = jnp.max(jnp.where(first8, a_r, neg_inf), axis=1, keepdims=True)
        max_b = jnp.max(jnp.where(first8, neg_inf, a_r), axis=1, keepdims=True)
        m_a = jnp.maximum(g_a + m_prev, max_a)
        m_b = jnp.maximum(g_b + m_a, max_b)
        dec_a = jnp.exp(g_a + m_prev - m_a)
        dec_b = jnp.exp(g_b + m_a - m_b)
        w_r = jnp.exp(a_r - jnp.where(first8, m_a, m_b))
        e_r = b_r + jnp.where(first8, m_prev, m_a)
        lc_hi, lc_lo = _split_hi_lo(_log_sigmoid(gcf_ref[pl.ds(r0, PAIR), :]))
        x_c = _dot(l2_ref[...], lc_hi) + _dot(l2_ref[...], lc_lo) - gci_ref[pl.ds(r0, PAIR), :]

        for hd in range(N_HEADS):
            rows = slice(hd * MLSTM_DV, (hd + 1) * MLSTM_DV)
            xh = x[:, hd * 128:(hd + 1) * 128]
            xs = pltpu.roll(xh, MLSTM_DK, 1)
            xk = jnp.where(upper, xh * (MLSTM_DK ** -0.5), 0.0).astype(BF16)
            xq = jnp.where(upper, xs, 0.0).astype(BF16)
            vt = vt_ref[hd * V_ROWS:(hd + 1) * V_ROWS, pl.ds(r0, PAIR)]
            vt32 = vt.astype(F32)
            w_h = w_r[hd:hd + 1, :]
            cn0 = cn_ref[hd]
            up_a = _dot((vt32 * jnp.where(first, w_h, 0.0)).astype(BF16), xk)
            cn1 = dec_a[hd:hd + 1, :] * cn0 + up_a
            up_b = _dot((vt32 * jnp.where(first, 0.0, w_h)).astype(BF16), xk)
            cn_ref[hd] = dec_b[hd:hd + 1, :] * cn1 + up_b
            dmat = b_r[hd:hd + 1, :] - x_c[:, hd:hd + 1]
            dmat = jnp.where(key_t <= qry_t, jnp.where(key_t >= chunk_start, dmat, neg_inf), neg_inf)
            e_h = e_r[hd:hd + 1, :]
            m_out = jnp.maximum(e_h, jnp.max(dmat, axis=0, keepdims=True))
            s_t = (_dot_nt(xk, xq) * jnp.exp(dmat - m_out)).astype(BF16)
            inter = jnp.exp(e_h - m_out)
            cq = jnp.where(first, _dot_nt(cn0.astype(BF16), xq), _dot_nt(cn1.astype(BF16), xq))
            tot = inter * cq + _dot(vt, s_t)
            den = tot[MLSTM_DV:MLSTM_DV + 1, :]
            hh = tot[0:MLSTM_DV, :] / jnp.maximum(jnp.abs(den), jnp.exp(-m_out))
            hn = hh * lax.rsqrt(jnp.mean(hh * hh, axis=0, keepdims=True) + EPS) * ng_ref[...]
            og = ot_ref[rows, pl.ds(r0, PAIR)]
            out_ref[rows, pl.ds(r0, PAIR)] = (hn * _sigmoid(og)).astype(BF16)
        return m_b

    lax.fori_loop(0, seq // PAIR, pair_step, jnp.zeros((8, PAIR), F32))


def _mlstm(qk, vmt, omt, gri, grf, gci, gcf, cw, ng, ug, l2, batch, seq):
    const = lambda arr: pl.BlockSpec(arr.shape, lambda b: (0,) * arr.ndim)
    in_specs = [
        pl.BlockSpec((seq, GROUP_W), lambda b: (b, 0)),
        pl.BlockSpec((N_HEADS * V_ROWS, seq), lambda b: (0, b)),
        pl.BlockSpec((GROUP_W, seq), lambda b: (0, b)),
        pl.BlockSpec((seq // PAIR, 8, PAIR), lambda b: (b, 0, 0)),
        pl.BlockSpec((seq // PAIR, 8, PAIR), lambda b: (b, 0, 0)),
        pl.BlockSpec((seq, N_GATE_ROWS), lambda b: (b, 0)),
        pl.BlockSpec((seq, N_GATE_ROWS), lambda b: (b, 0)),
        const(cw), const(ng), const(ug), const(l2),
    ]
    return pl.pallas_call(
        _mlstm_kernel, grid=(batch,), in_specs=in_specs,
        out_specs=pl.BlockSpec((GROUP_W, seq), lambda b: (0, b)),
        out_shape=jax.ShapeDtypeStruct((GROUP_W, batch * seq), BF16),
        scratch_shapes=[
            pltpu.VMEM((seq + 8, GROUP_W), F32),
            pltpu.VMEM((N_HEADS, V_ROWS, 128), F32),
        ],
        name="mlstm",
        compiler_params=pltpu.CompilerParams(dimension_semantics=("arbitrary",), vmem_limit_bytes=VMEM_LIMIT),
    )(qk, vmt, omt, gri, grf, gci, gcf, cw, ng, ug, l2)


def _attn_kernel(lam_init, tbl_ref, kmax_ref, qmin_ref, q_ref, k_ref, vt_ref, pcol_ref, prow_ref,
                 lq1_ref, lk1_ref, lq2_ref, lk2_ref, sg_ref, out_ref, bias_ref, qc_ref, s_ref, p_ref, acc_ref):
    hd = pl.program_id(0)
    qi = pl.program_id(1)
    b = pl.program_id(2)
    n_kv = (qi + 1) * (TQ // TK)

    @pl.when(b == 0)
    def _build_bias():
        rowi = lax.broadcasted_iota(jnp.int32, (BIAS_ROWS, BIAS_COLS), 0)
        coli = lax.broadcasted_iota(jnp.int32, (BIAS_ROWS, BIAS_COLS), 1)
        row = lambda c, v: tbl_ref[c * N_BUCKETS + v:c * N_BUCKETS + v + 1, :]
        last = N_BUCKETS - 1

        def build(t, _):
            r0 = pl.multiple_of(t * BIAS_ROWS, BIAS_ROWS)
            pk = pcol_ref[pl.ds(r0, BIAS_ROWS), :]
            kmax = kmax_ref[t]
            for j in range(TQ // BIAS_COLS):
                lanes = slice(j * BIAS_COLS, (j + 1) * BIAS_COLS)
                q0 = qi * TQ + j * BIAS_COLS
                causal = rowi + r0 <= coli + q0
                flat = jnp.logical_or(qmin_ref[qi * (TQ // BIAS_COLS) + j] - kmax >= _BUCKET_THR[-1],
                                      r0 > q0 + BIAS_COLS - 1)

                @pl.when(flat)
                def _flat():
                    bias_ref[0, pl.ds(r0, BIAS_ROWS), lanes] = jnp.where(causal, row(0, last), NEG_BIG)
                    bias_ref[1, pl.ds(r0, BIAS_ROWS), lanes] = jnp.where(causal, row(1, last), NEG_BIG)

                @pl.when(jnp.logical_not(flat))
                def _lookup():
                    dist = jnp.maximum(prow_ref[:, lanes] - pk, 0)
                    b0 = jnp.broadcast_to(row(0, 0), (BIAS_ROWS, BIAS_COLS))
                    b1 = jnp.broadcast_to(row(1, 0), (BIAS_ROWS, BIAS_COLS))
                    for v, thr in enumerate(_BUCKET_THR, start=1):
                        ge = dist >= thr
                        b0 = jnp.where(ge, row(0, v), b0)
                        b1 = jnp.where(ge, row(1, v), b1)
                    bias_ref[0, pl.ds(r0, BIAS_ROWS), lanes] = jnp.where(causal, b0, NEG_BIG)
                    bias_ref[1, pl.ds(r0, BIAS_ROWS), lanes] = jnp.where(causal, b1, NEG_BIG)
            return 0

        lax.fori_loop(0, n_kv * (TK // BIAS_ROWS), build, 0)

    seq = k_ref.shape[1]
    lane = lax.broadcasted_iota(jnp.int32, (TQ, 128), 1)
    for e in range(ATTN_NB):
        q = q_ref[e]
        zero = jnp.zeros_like(q)
        qc_ref[e, 0] = jnp.where(lane < DIFF_D, q, zero)
        qc_ref[e, 1] = jnp.where(lane >= DIFF_D, q, zero)
    acc_ref[...] = jnp.zeros_like(acc_ref)

    chains = [(e, c) for e in range(ATTN_NB) for c in range(2)]

    def scores(t, slot):
        k0 = pl.multiple_of(t * TK, TK)
        for e in range(ATTN_NB):
            kb = k_ref[e, pl.ds(k0, TK), :]
            for c in range(2):
                s_ref[slot, e, c] = _dot_nt(kb, qc_ref[e, c]) + bias_ref[c, pl.ds(k0, TK), :]

    def numerators(m_state, slot):
        m_next, alpha_next = [], []
        for i, (e, c) in enumerate(chains):
            s = s_ref[slot, e, c]
            m_new = jnp.maximum(m_state[i], jnp.max(s, axis=0, keepdims=True))
            alpha_next.append(jnp.exp2(m_state[i] - m_new))
            p_ref[slot, e, c] = jnp.exp2(s - m_new).astype(BF16)
            m_next.append(m_new)
        return tuple(m_next), tuple(alpha_next)

    def accumulate(t, alpha, slot):
        for e in range(ATTN_NB):
            v0 = pl.multiple_of(e * seq + t * TK, TK)
            vt = vt_ref[:, pl.ds(v0, TK)]
            for c in range(2):
                acc_ref[e, c] = alpha[2 * e + c] * acc_ref[e, c] + _dot(vt, p_ref[slot, e, c])

    def kv_pair(i, carry):
        t = 2 * i
        m_state, alpha = carry
        accumulate(t - 2, alpha, 0)
        m_state, alpha = numerators(m_state, 1)
        scores(t, 0)
        accumulate(t - 1, alpha, 1)
        carry = numerators(m_state, 0)
        scores(t + 1, 1)
        return carry

    assert (TQ // TK) % 2 == 0
    scores(0, 0)
    carry = numerators(tuple(jnp.full((1, TQ), M_INIT, F32) for _ in chains), 0)
    scores(1, 1)
    m_state, alpha = lax.fori_loop(1, n_kv // 2, kv_pair, carry)
    accumulate(n_kv - 2, alpha, 0)
    _, alpha = numerators(m_state, 1)
    accumulate(n_kv - 1, alpha, 1)

    lam = (jnp.exp(jnp.sum(lq1_ref[...] * lk1_ref[...], axis=1, keepdims=True))
           - jnp.exp(jnp.sum(lq2_ref[...] * lk2_ref[...], axis=1, keepdims=True)) + lam_init)
    for e in range(ATTN_NB):
        a0 = acc_ref[e, 0]
        a1 = acc_ref[e, 1]
        o = a0[0:DIFF_DV] / a0[DIFF_DV:DIFF_DV + 1] - lam * (a1[0:DIFF_DV] / a1[DIFF_DV:DIFF_DV + 1])
        on = o * lax.rsqrt(jnp.mean(o * o, axis=0, keepdims=True) + EPS) * sg_ref[...]
        out_ref[e] = (on * (1.0 - lam_init)).T.astype(BF16)


def _attention(lam_init, tbl, qd, kd, vdt, pcol, prow, lq1, lk1, lq2, lk2, sg, batch, seq):
    q3, k3 = (t.reshape(batch, seq, GROUP_W) for t in (qd, kd))
    vec = lambda n: pl.BlockSpec((1, n), lambda h, i, b: (0, 0))
    kmax = jnp.max(prow.reshape(seq // BIAS_ROWS, BIAS_ROWS), axis=1)
    qmin = jnp.min(prow.reshape(seq // BIAS_COLS, BIAS_COLS), axis=1)
    in_specs = [
        pl.BlockSpec((None, 2 * N_BUCKETS, BIAS_COLS), lambda h, i, b: (h, 0, 0)),
        pl.BlockSpec(memory_space=pltpu.SMEM),
        pl.BlockSpec(memory_space=pltpu.SMEM),
        pl.BlockSpec((ATTN_NB, TQ, 128), lambda h, i, b: (b, i, h)),
        pl.BlockSpec((ATTN_NB, seq, 128), lambda h, i, b: (b, 0, h)),
        pl.BlockSpec((V_ROWS, ATTN_NB * seq), lambda h, i, b: (h, b)),
        pl.BlockSpec((seq, BIAS_COLS), lambda h, i, b: (0, 0)),
        pl.BlockSpec((1, TQ), lambda h, i, b: (0, i)),
        vec(DIFF_D), vec(DIFF_D), vec(DIFF_D), vec(DIFF_D),
        pl.BlockSpec((DIFF_DV, TQ), lambda h, i, b: (0, 0)),
    ]
    out = pl.pallas_call(
        functools.partial(_attn_kernel, lam_init),
        grid=(N_HEADS, seq // TQ, batch // ATTN_NB), in_specs=in_specs,
        out_specs=pl.BlockSpec((ATTN_NB, TQ, 128), lambda h, i, b: (b, i, h)),
        out_shape=jax.ShapeDtypeStruct((batch, seq, GROUP_W), BF16),
        scratch_shapes=[
            pltpu.VMEM((2, seq, TQ), F32),
            pltpu.VMEM((ATTN_NB, 2, TQ, 128), BF16),
            pltpu.VMEM((2, ATTN_NB, 2, TK, TQ), F32),
            pltpu.VMEM((2, ATTN_NB, 2, TK, TQ), BF16),
            pltpu.VMEM((ATTN_NB, 2, V_ROWS, TQ), F32),
        ],
        name="diff_attn",
        compiler_params=pltpu.CompilerParams(
            dimension_semantics=("arbitrary", "arbitrary", "arbitrary"), vmem_limit_bytes=VMEM_LIMIT),
    )(tbl, kmax, qmin, q3, k3, vdt, pcol, prow, lq1, lk1, lq2, lk2, jnp.broadcast_to(sg.reshape(DIFF_DV, 1), (DIFF_DV, TQ)))
    return out.reshape(batch * seq, GROUP_W)


def _gelu(x):
    return 0.5 * x * (1.0 + lax.erf(x * (2.0 ** -0.5)))


def _rms(x, g):
    return x * lax.rsqrt(jnp.mean(x * x, axis=-1, keepdims=True) + EPS) * g


def _mixer_kernel(h_ref, hmt_ref, hd_ref, p_ref, woa_ref, wob_ref, gf_ref, wu_ref, cwg_ref, cwv_ref,
                  cbg_ref, cbv_ref, wd_ref, gp_ref, wpg_ref, wpp_ref, out_ref,
                  sg_ref, sv_ref, cg_ref, cv_ref, act_ref, u_ref):
    tm = h_ref.shape[0]
    halo = 8

    @pl.when(pl.program_id(1) == 0)
    def _reset_conv_history():
        cg_ref[...] = jnp.zeros_like(cg_ref)
        cv_ref[...] = jnp.zeros_like(cv_ref)

    h1 = h_ref[...] + _dot_tn(hmt_ref[...], woa_ref[...]) + _dot(hd_ref[...], wob_ref[...])
    u_ref[...] = _rms(h1, gf_ref[...]).astype(BF16)

    def up_stage(j, slot):
        c0 = pl.multiple_of(j * FF_CHUNK, FF_CHUNK)
        sg_ref[slot, halo:, :] = _dot(u_ref[...], wu_ref[:, pl.ds(c0, FF_CHUNK)])
        sv_ref[slot, halo:, :] = _dot(u_ref[...], wu_ref[:, pl.ds(D_FF + c0, FF_CHUNK)])

    def conv_branch(j, slot, cw_ref, cb_ref, stage_ref, hist_ref):
        stage_ref[slot, 0:halo, :] = hist_ref[j]
        hist_ref[j] = stage_ref[slot, tm:tm + halo, :]
        cw = cw_ref[j]
        return (cw[0:1, :] * stage_ref[slot, halo - 2:halo - 2 + tm, :]
                + cw[1:2, :] * stage_ref[slot, halo - 1:halo - 1 + tm, :]
                + cw[2:3, :] * stage_ref[slot, halo:, :] + cb_ref[j])

    def act_stage(j, slot):
        gate = conv_branch(j, slot, cwg_ref, cbg_ref, sg_ref, cg_ref)
        val = conv_branch(j, slot, cwv_ref, cbv_ref, sv_ref, cv_ref)
        c0 = pl.multiple_of(j * FF_CHUNK, FF_CHUNK)
        act_ref[:, pl.ds(c0, FF_CHUNK)] = (_gelu(gate) * val).astype(BF16)

    def chunk_pair(i, _):
        j = 2 * i
        up_stage(j + 1, 1)
        act_stage(j, 0)
        up_stage(j + 2, 0)
        act_stage(j + 1, 1)
        return 0

    assert N_FF_CHUNKS % 2 == 1
    up_stage(0, 0)
    lax.fori_loop(0, N_FF_CHUNKS // 2, chunk_pair, 0)
    act_stage(N_FF_CHUNKS - 1, 0)

    h2 = h1 + _dot(act_ref[...], wd_ref[...])
    u3 = _rms(h2, gp_ref[...]).astype(BF16)
    ple_gate = _sigmoid(_dot(u3, wpg_ref[...]))
    out_ref[...] = h2 + ple_gate * _dot(p_ref[...].astype(BF16), wpp_ref[...])


def _mixer(h, hmt, hd, p, woa, wob, gf, wu, cwg, cwv, cbg, cbv, wd, gp, wpg, wpp, batch, seq):
    tiles = seq // TM_FFN
    tile = lambda width: pl.BlockSpec((TM_FFN, width), lambda b, t: (b * tiles + t, 0))

    def const(arr):
        nd = arr.ndim
        return pl.BlockSpec(arr.shape, lambda b, t: (0,) * nd, pipeline_mode=pl.Buffered(1))

    weights = (woa, wob, gf, wu, cwg, cwv, cbg, cbv, wd, gp, wpg, wpp)
    tile_t = pl.BlockSpec((GROUP_W, TM_FFN), lambda b, t: (0, b * tiles + t))
    in_specs = [tile(D_MODEL), tile_t, tile(GROUP_W), tile(PLE_DIM)] + [const(w) for w in weights]
    return pl.pallas_call(
        _mixer_kernel, grid=(batch, tiles), in_specs=in_specs, out_specs=tile(D_MODEL),
        out_shape=jax.ShapeDtypeStruct(h.shape, F32),
        scratch_shapes=[
            pltpu.VMEM((2, TM_FFN + 8, FF_CHUNK), F32),
            pltpu.VMEM((2, TM_FFN + 8, FF_CHUNK), F32),
            pltpu.VMEM((N_FF_CHUNKS, 8, FF_CHUNK), F32),
            pltpu.VMEM((N_FF_CHUNKS, 8, FF_CHUNK), F32),
            pltpu.VMEM((TM_FFN, D_FF), BF16),
            pltpu.VMEM((TM_FFN, D_MODEL), BF16),
        ],
        name="mixer",
        compiler_params=pltpu.CompilerParams(
            dimension_semantics=("arbitrary", "arbitrary"), vmem_limit_bytes=VMEM_LIMIT),
    )(h, hmt, hd, p, *weights)


def _head_interleave(qcols, kcols):
    lead = qcols.shape[:-1]
    qh = qcols.reshape(lead + (N_HEADS, MLSTM_DK))
    kh = kcols.reshape(lead + (N_HEADS, MLSTM_DK))
    return jnp.concatenate([qh, kh], axis=-1).reshape(lead + (GROUP_W,))


def _ff_chunks(w):
    return jnp.transpose(w.reshape(w.shape[0], N_FF_CHUNKS, FF_CHUNK), (1, 0, 2))


def kernel(x, p, positions, rel_bias, ln_mix_g, w_in, mlstm_conv_w, b_igate, b_fgate, mlstm_norm_g, q_norm_g, k_norm_g, lam_q1, lam_k1, lam_q2, lam_k2, diff_subln_g, w_out, ln_ffn_g, w_up, ffn_conv_w, ffn_conv_b, w_down, ln_ple_g, w_ple_gate, w_ple_proj):
    batch, seq, _ = x.shape
    depth = w_in.shape[0]
    n_tok = batch * seq
    qk_cols = N_HEADS * MLSTM_DK
    col_sizes = [qk_cols, qk_cols, GROUP_W, GROUP_W, N_HEADS, N_HEADS, GROUP_W, GROUP_W, GROUP_W]
    offs = np.concatenate([[0], np.cumsum(col_sizes)])
    sl = lambda a, j: a[..., int(offs[j]):int(offs[j + 1])]

    tbl = jnp.broadcast_to(
        (jnp.transpose(rel_bias.astype(F32), (1, 2, 0)) * LOG2E).reshape(N_HEADS, 2 * N_BUCKETS, 1),
        (N_HEADS, 2 * N_BUCKETS, BIAS_COLS))
    pcol = jnp.broadcast_to(positions.astype(jnp.int32).reshape(seq, 1), (seq, BIAS_COLS))
    prow = positions.astype(jnp.int32).reshape(1, seq)
    gsum = jnp.asarray(np.kron(np.eye(MXU_TILE // DIFF_D), np.ones((DIFF_D, DIFF_D))), BF16)
    t_idx = np.arange(PAIR)
    same_chunk = (t_idx[:, None] // CHUNK) == (t_idx[None, :] // CHUNK)
    prefix = same_chunk & (t_idx[:, None] <= t_idx[None, :])
    total_a = np.broadcast_to(t_idx[:, None] < CHUNK, (PAIR, PAIR))
    ug = jnp.asarray(np.concatenate([prefix, total_a, ~total_a], axis=1), BF16)
    l2 = jnp.asarray(prefix.T, BF16)

    h = x.reshape(n_tok, D_MODEL)
    for i in range(depth):
        wi = w_in[i]
        wm = jnp.concatenate([_head_interleave(sl(wi, 0), sl(wi, 1)), sl(wi, 6), sl(wi, 7)], axis=-1).astype(BF16)
        wt = jnp.concatenate([sl(wi, 2), sl(wi, 3), sl(wi, 8)], axis=-1).T.astype(BF16)
        pad_rows = lambda a: jnp.zeros((N_GATE_ROWS,) + a.shape[1:], F32).at[:N_HEADS].set(a.astype(F32))
        wg = jnp.concatenate([pad_rows(sl(wi, 4).T), pad_rows(sl(wi, 5).T)], axis=0).astype(BF16)
        gate_bias = jnp.concatenate([pad_rows(b_igate[i]), pad_rows(b_fgate[i])])
        qg = jnp.tile(q_norm_g[i].astype(F32), GROUP_W // DIFF_D).reshape(1, GROUP_W) * (DIFF_D ** -0.5 * LOG2E)
        kg = jnp.tile(k_norm_g[i].astype(F32), GROUP_W // DIFF_D).reshape(1, GROUP_W)
        qk, vmt, omt, gri, grf, gci, gcf, qd, kd, vdt = _inproj(
            h, ln_mix_g[i].reshape(1, D_MODEL), wm, wt, wg, gate_bias.reshape(1, -1), gate_bias.reshape(-1, 1),
            gsum, qg, kg)

        cw = _head_interleave(mlstm_conv_w[i][:, :qk_cols], mlstm_conv_w[i][:, qk_cols:]).astype(F32)
        ng = jnp.broadcast_to(mlstm_norm_g[i].astype(F32).reshape(MLSTM_DV, 1), (MLSTM_DV, PAIR))
        hmt = _mlstm(qk, vmt, omt, gri, grf, gci, gcf, cw, ng, ug, l2, batch, seq)

        lam_init = 0.8 - 0.6 * math.exp(-0.3 * i)
        row64 = lambda a: a[i].reshape(1, DIFF_D).astype(F32)
        hd = _attention(lam_init, tbl, qd, kd, vdt, pcol, prow, row64(lam_q1), row64(lam_k1), row64(lam_q2),
                        row64(lam_k2), diff_subln_g[i].reshape(1, DIFF_DV).astype(F32), batch, seq)

        wo = w_out[i].astype(BF16)
        wu = w_up[i].astype(BF16)
        cwf = ffn_conv_w[i].astype(F32)
        cbf = ffn_conv_b[i].astype(F32).reshape(1, 2 * D_FF)
        h = _mixer(
            h, hmt, hd, p[i].reshape(n_tok, PLE_DIM),
            wo[:GROUP_W], wo[GROUP_W:], ln_ffn_g[i].reshape(1, D_MODEL),
            wu,
            _ff_chunks(cwf[:, :D_FF]), _ff_chunks(cwf[:, D_FF:]),
            _ff_chunks(cbf[:, :D_FF]), _ff_chunks(cbf[:, D_FF:]),
            w_down[i].astype(BF16),
            ln_ple_g[i].reshape(1, D_MODEL), w_ple_gate[i].astype(BF16), w_ple_proj[i].astype(BF16),
            batch, seq)
    return h.reshape(batch, seq, D_MODEL)
```

```python
import functools
import math

import numpy as np
import jax
import jax.numpy as jnp
from jax import lax
from jax.experimental import pallas as pl
from jax.experimental.pallas import tpu as pltpu

F32 = jnp.float32
BF16 = jnp.bfloat16

D_MODEL = 1024
N_HEADS = 4
MLSTM_DK = 64
MLSTM_DV = 128
MLSTM_CONV_W = 4
CHUNK = 64
PAIR = 2 * CHUNK
DIFF_D = 64
DIFF_DV = 128
N_BUCKETS = 32
MAX_DISTANCE = 128
D_FF = 2816
FFN_CONV_W = 3
PLE_DIM = 256
EPS = 1e-6
GROUP_W = N_HEADS * 128
N_GATE_ROWS = 16

MXU_TILE = 256
TM_IN = 512
TM_FFN = 512
FF_CHUNK = 256
N_FF_CHUNKS = D_FF // FF_CHUNK
TQ = 512
TK = 256
BIAS_ROWS = 32
BIAS_COLS = 128
ATTN_NB = 2
NEG_BIG = -1e30
M_INIT = -1e20
ONES_ROWS = 16
V_ROWS = DIFF_DV + ONES_ROWS
LOG2E = math.log2(math.e)
VMEM_LIMIT = 56 * 1024 * 1024


def _dot(a, b):
    return jnp.dot(a, b, preferred_element_type=F32)


def _dot_nt(a, b):
    return lax.dot_general(a, b, (((1,), (1,)), ((), ())), preferred_element_type=F32)


def _dot_tn(a, b):
    return lax.dot_general(a, b, (((0,), (0,)), ((), ())), preferred_element_type=F32)


def _sigmoid(x):
    return 1.0 / (1.0 + jnp.exp(-x))


def _log_sigmoid(x):
    return jnp.minimum(x, 0.0) - jnp.log1p(jnp.exp(-jnp.abs(x)))


def _bucket_thresholds():
    max_exact = N_BUCKETS // 2
    thr = []
    for v in range(1, N_BUCKETS):
        if v <= max_exact:
            thr.append(v)
            continue
        edge = max_exact * (MAX_DISTANCE / max_exact) ** ((v - max_exact) / (N_BUCKETS - max_exact))
        assert abs(edge - round(edge)) > 1e-3, edge
        thr.append(int(math.ceil(edge)))
    assert all(a < b for a, b in zip(thr, thr[1:])), thr
    return tuple(thr)


_BUCKET_THR = _bucket_thresholds()


def _group_mean_square(z, gsum_ref):
    sq = z * z
    hi = sq.astype(BF16)
    lo = (sq - hi.astype(F32)).astype(BF16)
    width = gsum_ref.shape[0]
    sums = [_dot(hi[:, c0:c0 + width], gsum_ref[...]) + _dot(lo[:, c0:c0 + width], gsum_ref[...])
            for c0 in range(0, z.shape[1], width)]
    return jnp.concatenate(sums, axis=1) * (1.0 / DIFF_D)


def _with_ones_rows(dst_ref, vt):
    for hh in range(N_HEADS):
        dst_ref[hh * V_ROWS:hh * V_ROWS + DIFF_DV, :] = vt[hh * DIFF_DV:(hh + 1) * DIFF_DV, :]
        dst_ref[hh * V_ROWS + DIFF_DV:(hh + 1) * V_ROWS, :] = jnp.ones((ONES_ROWS, vt.shape[1]), BF16)


def _inproj_kernel(h_ref, g_ref, wm_ref, wt_ref, wg_ref, bcol_ref, brow_ref, gsum_ref, qg_ref, kg_ref,
                   qk_ref, vmt_ref, omt_ref, gri_ref, grf_ref, gci_ref, gcf_ref, qd_ref, kd_ref, vdt_ref):
    x = h_ref[...]
    ms = jnp.mean(x * x, axis=-1, keepdims=True)
    u = (x * lax.rsqrt(ms + EPS) * g_ref[...]).astype(BF16)
    w = GROUP_W
    qk_ref[...] = _dot(u, wm_ref[:, 0:w])
    zq = _dot(u, wm_ref[:, w:2 * w])
    qd_ref[...] = (zq * lax.rsqrt(_group_mean_square(zq, gsum_ref) + EPS) * qg_ref[...]).astype(BF16)
    zk = _dot(u, wm_ref[:, 2 * w:3 * w])
    kd_ref[...] = (zk * lax.rsqrt(_group_mean_square(zk, gsum_ref) + EPS) * kg_ref[...]).astype(BF16)
    _with_ones_rows(vmt_ref, _dot_nt(wt_ref[0:w, :], u).astype(BF16))
    omt_ref[...] = _dot_nt(wt_ref[w:2 * w, :], u)
    _with_ones_rows(vdt_ref, _dot_nt(wt_ref[2 * w:3 * w, :], u).astype(BF16))
    gc = _dot_nt(u, wg_ref[...]) + bcol_ref[...]
    gci_ref[...] = gc[:, 0:N_GATE_ROWS]
    gcf_ref[...] = gc[:, N_GATE_ROWS:2 * N_GATE_ROWS]
    gr = _dot_nt(wg_ref[...], u) + brow_ref[...]
    for ci in range(TM_IN // PAIR):
        gri_ref[ci] = gr[0:8, ci * PAIR:(ci + 1) * PAIR]
        grf_ref[ci] = gr[N_GATE_ROWS:N_GATE_ROWS + 8, ci * PAIR:(ci + 1) * PAIR]


def _inproj(h, g, wm, wt, wg, bcol, brow, gsum, qg, kg):
    n_tok = h.shape[0]
    grid = (n_tok // TM_IN,)
    const = lambda shape: pl.BlockSpec(shape, lambda i: (0,) * len(shape))
    tile = lambda width: pl.BlockSpec((TM_IN, width), lambda i: (i, 0))
    tile_t = lambda rows: pl.BlockSpec((rows, TM_IN), lambda i: (0, i))
    gate_rows = pl.BlockSpec((TM_IN // PAIR, 8, PAIR), lambda i: (i, 0, 0))
    out_shape = (
        jax.ShapeDtypeStruct((n_tok, GROUP_W), F32),
        jax.ShapeDtypeStruct((N_HEADS * V_ROWS, n_tok), BF16),
        jax.ShapeDtypeStruct((GROUP_W, n_tok), F32),
        jax.ShapeDtypeStruct((n_tok // PAIR, 8, PAIR), F32),
        jax.ShapeDtypeStruct((n_tok // PAIR, 8, PAIR), F32),
        jax.ShapeDtypeStruct((n_tok, N_GATE_ROWS), F32),
        jax.ShapeDtypeStruct((n_tok, N_GATE_ROWS), F32),
        jax.ShapeDtypeStruct((n_tok, GROUP_W), BF16),
        jax.ShapeDtypeStruct((n_tok, GROUP_W), BF16),
        jax.ShapeDtypeStruct((N_HEADS * V_ROWS, n_tok), BF16),
    )
    out_specs = (
        tile(GROUP_W), tile_t(N_HEADS * V_ROWS), tile_t(GROUP_W), gate_rows, gate_rows,
        tile(N_GATE_ROWS), tile(N_GATE_ROWS), tile(GROUP_W), tile(GROUP_W), tile_t(N_HEADS * V_ROWS),
    )
    in_specs = [
        tile(D_MODEL), const((1, D_MODEL)), const(wm.shape), const(wt.shape), const(wg.shape),
        const(bcol.shape), const(brow.shape), const(gsum.shape),
        const((1, GROUP_W)), const((1, GROUP_W)),
    ]
    return pl.pallas_call(
        _inproj_kernel, grid=grid, in_specs=in_specs, out_specs=out_specs, out_shape=out_shape,
        name="inproj",
        compiler_params=pltpu.CompilerParams(dimension_semantics=("arbitrary",), vmem_limit_bytes=VMEM_LIMIT),
    )(h, g, wm, wt, wg, bcol, brow, gsum, qg, kg)


def _split_hi_lo(x):
    hi = x.astype(BF16)
    return hi, (x - hi.astype(F32)).astype(BF16)


def _mlstm_kernel(qk_ref, vt_ref, ot_ref, gri_ref, grf_ref, gci_ref, gcf_ref, cw_ref, ng_ref, ug_ref, l2_ref,
                  out_ref, xpad_ref, cn_ref):
    seq = qk_ref.shape[0]
    halo = 8
    xpad_ref[0:halo, :] = jnp.zeros((halo, GROUP_W), F32)
    xpad_ref[halo:, :] = qk_ref[...]
    cn_ref[...] = jnp.zeros_like(cn_ref)

    first8 = lax.broadcasted_iota(jnp.int32, (8, PAIR), 1) < CHUNK
    first = lax.broadcasted_iota(jnp.int32, (1, PAIR), 1) < CHUNK
    upper = lax.broadcasted_iota(jnp.int32, (PAIR, PAIR), 1) >= MLSTM_DK
    key_t = lax.broadcasted_iota(jnp.int32, (PAIR, PAIR), 0)
    qry_t = lax.broadcasted_iota(jnp.int32, (PAIR, PAIR), 1)
    chunk_start = jnp.where(qry_t < CHUNK, 0, CHUNK)
    cw = cw_ref[...]
    neg_inf = jnp.float32(-jnp.inf)

    def pair_step(c2, m_prev):
        r0 = pl.multiple_of(c2 * PAIR, PAIR)
        win = xpad_ref[pl.ds(r0, PAIR + halo), :]
        conv = cw[0:1, :] * win[halo - 3:halo - 3 + PAIR, :]
        for j in range(1, MLSTM_CONV_W):
            conv = conv + cw[j:j + 1, :] * win[halo - 3 + j:halo - 3 + j + PAIR, :]
        x = conv * _sigmoid(conv)

        gi = gri_ref[c2]
        lf_hi, lf_lo = _split_hi_lo(_log_sigmoid(grf_ref[c2]))
        bg = _dot(lf_hi, ug_ref[...]) + _dot(lf_lo, ug_ref[...])
        b_r = bg[:, 0:PAIR]
        g_a = bg[:, PAIR:2 * PAIR]
        g_b = bg[:, 2 * PAIR:3 * PAIR]
        a_r = jnp.where(first8, g_a, g_b) - b_r + gi
        max_a = jnp.max(jnp.where(first8, a_r, neg_inf), axis=1, keepdims=True)
        max_b = jnp.max(jnp.where(first8, neg_inf, a_r), axis=1, keepdims=True)
        m_a = jnp.maximum(g_a + m_prev, max_a)
        m_b = jnp.maximum(g_b + m_a, max_b)
        dec_a = jnp.exp(g_a + m_prev - m_a)
        dec_b = jnp.exp(g_b + m_a - m_b)
        w_r = jnp.exp(a_r - jnp.where(first8, m_a, m_b))
        e_r = b_r + jnp.where(first8, m_prev, m_a)
        lc_hi, lc_lo = _split_hi_lo(_log_sigmoid(gcf_ref[pl.ds(r0, PAIR), :]))
        x_c = _dot(l2_ref[...], lc_hi) + _dot(l2_ref[...], lc_lo) - gci_ref[pl.ds(r0, PAIR), :]

        for hd in range(N_HEADS):
            rows = slice(hd * MLSTM_DV, (hd + 1) * MLSTM_DV)
            xh = x[:, hd * 128:(hd + 1) * 128]
            xs = pltpu.roll(xh, MLSTM_DK, 1)
            xk = jnp.where(upper, xh * (MLSTM_DK ** -0.5), 0.0).astype(BF16)
            xq = jnp.where(upper, xs, 0.0).astype(BF16)
            vt = vt_ref[hd * V_ROWS:(hd + 1) * V_ROWS, pl.ds(r0, PAIR)]
            vt32 = vt.astype(F32)
            w_h = w_r[hd:hd + 1, :]
            cn0 = cn_ref[hd]
            up_a = _dot((vt32 * jnp.where(first, w_h, 0.0)).astype(BF16), xk)
            cn1 = dec_a[hd:hd + 1, :] * cn0 + up_a
            up_b = _dot((vt32 * jnp.where(first, 0.0, w_h)).astype(BF16), xk)
            cn_ref[hd] = dec_b[hd:hd + 1, :] * cn1 + up_b
            dmat = b_r[hd:hd + 1, :] - x_c[:, hd:hd + 1]
            dmat = jnp.where(key_t <= qry_t, jnp.where(key_t >= chunk_start, dmat, neg_inf), neg_inf)
            e_h = e_r[hd:hd + 1, :]
            m_out = jnp.maximum(e_h, jnp.max(dmat, axis=0, keepdims=True))
            s_t = (_dot_nt(xk, xq) * jnp.exp(dmat - m_out)).astype(BF16)
            inter = jnp.exp(e_h - m_out)
            cq = jnp.where(first, _dot_nt(cn0.astype(BF16), xq), _dot_nt(cn1.astype(BF16), xq))
            tot = inter * cq + _dot(vt, s_t)
            den = tot[MLSTM_DV:MLSTM_DV + 1, :]
            hh = tot[0:MLSTM_DV, :] / jnp.maximum(jnp.abs(den), jnp.exp(-m_out))
            hn = hh * lax.rsqrt(jnp.mean(hh * hh, axis=0, keepdims=True) + EPS) * ng_ref[...]
            og = ot_ref[rows, pl.ds(r0, PAIR)]
            out_ref[rows, pl.ds(r0, PAIR)] = (hn * _sigmoid(og)).astype(BF16)
        return m_b

    lax.fori_loop(0, seq // PAIR, pair_step, jnp.zeros((8, PAIR), F32))


def _mlstm(qk, vmt, omt, gri, grf, gci, gcf, cw, ng, ug, l2, batch, seq):
    const = lambda arr: pl.BlockSpec(arr.shape, lambda b: (0,) * arr.ndim)
    in_specs = [
        pl.BlockSpec((seq, GROUP_W), lambda b: (b, 0)),
        pl.BlockSpec((N_HEADS * V_ROWS, seq), lambda b: (0, b)),
        pl.BlockSpec((GROUP_W, seq), lambda b: (0, b)),
        pl.BlockSpec((seq // PAIR, 8, PAIR), lambda b: (b, 0, 0)),
        pl.BlockSpec((seq // PAIR, 8, PAIR), lambda b: (b, 0, 0)),
        pl.BlockSpec((seq, N_GATE_ROWS), lambda b: (b, 0)),
        pl.BlockSpec((seq, N_GATE_ROWS), lambda b: (b, 0)),
        const(cw), const(ng), const(ug), const(l2),
    ]
    return pl.pallas_call(
        _mlstm_kernel, grid=(batch,), in_specs=in_specs,
        out_specs=pl.BlockSpec((GROUP_W, seq), lambda b: (0, b)),
        out_shape=jax.ShapeDtypeStruct((GROUP_W, batch * seq), BF16),
        scratch_shapes=[
            pltpu.VMEM((seq + 8, GROUP_W), F32),
            pltpu.VMEM((N_HEADS, V_ROWS, 128), F32),
        ],
        name="mlstm",
        compiler_params=pltpu.CompilerParams(dimension_semantics=("arbitrary",), vmem_limit_bytes=VMEM_LIMIT),
    )(qk, vmt, omt, gri, grf, gci, gcf, cw, ng, ug, l2)


def _attn_kernel(lam_init, tbl_ref, kmax_ref, qmin_ref, q_ref, k_ref, vt_ref, pcol_ref, prow_ref,
                 lq1_ref, lk1_ref, lq2_ref, lk2_ref, sg_ref, out_ref, bias_ref, qc_ref, s_ref, p_ref, acc_ref):
    hd = pl.program_id(0)
    qi = pl.program_id(1)
    b = pl.program_id(2)
    n_kv = (qi + 1) * (TQ // TK)

    @pl.when(b == 0)
    def _build_bias():
        rowi = lax.broadcasted_iota(jnp.int32, (BIAS_ROWS, BIAS_COLS), 0)
        coli = lax.broadcasted_iota(jnp.int32, (BIAS_ROWS, BIAS_COLS), 1)
        row = lambda c, v: tbl_ref[c * N_BUCKETS + v:c * N_BUCKETS + v + 1, :]
        last = N_BUCKETS - 1

        def build(t, _):
            r0 = pl.multiple_of(t * BIAS_ROWS, BIAS_ROWS)
            pk = pcol_ref[pl.ds(r0, BIAS_ROWS), :]
            kmax = kmax_ref[t]
            for j in range(TQ // BIAS_COLS):
                lanes = slice(j * BIAS_COLS, (j + 1) * BIAS_COLS)
                q0 = qi * TQ + j * BIAS_COLS
                causal = rowi + r0 <= coli + q0
                flat = jnp.logical_or(qmin_ref[qi * (TQ // BIAS_COLS) + j] - kmax >= _BUCKET_THR[-1],
                                      r0 > q0 + BIAS_COLS - 1)

                @pl.when(flat)
                def _flat():
                    bias_ref[0, pl.ds(r0, BIAS_ROWS), lanes] = jnp.where(causal, row(0, last), NEG_BIG)
                    bias_ref[1, pl.ds(r0, BIAS_ROWS), lanes] = jnp.where(causal, row(1, last), NEG_BIG)

                @pl.when(jnp.logical_not(flat))
                def _lookup():
                    dist = jnp.maximum(prow_ref[:, lanes] - pk, 0)
                    b0 = jnp.broadcast_to(row(0, 0), (BIAS_ROWS, BIAS_COLS))
                    b1 = jnp.broadcast_to(row(1, 0), (BIAS_ROWS, BIAS_COLS))
                    for v, thr in enumerate(_BUCKET_THR, start=1):
                        ge = dist >= thr
                        b0 = jnp.where(ge, row(0, v), b0)
                        b1 = jnp.where(ge, row(1, v), b1)
                    bias_ref[0, pl.ds(r0, BIAS_ROWS), lanes] = jnp.where(causal, b0, NEG_BIG)
                    bias_ref[1, pl.ds(r0, BIAS_ROWS), lanes] = jnp.where(causal, b1, NEG_BIG)
            return 0

        lax.fori_loop(0, n_kv * (TK // BIAS_ROWS), build, 0)

    seq = k_ref.shape[1]
    lane = lax.broadcasted_iota(jnp.int32, (TQ, 128), 1)
    for e in range(ATTN_NB):
        q = q_ref[e]
        zero = jnp.zeros_like(q)
        qc_ref[e, 0] = jnp.where(lane < DIFF_D, q, zero)
        qc_ref[e, 1] = jnp.where(lane >= DIFF_D, q, zero)
    acc_ref[...] = jnp.zeros_like(acc_ref)

    chains = [(e, c) for e in range(ATTN_NB) for c in range(2)]

    def scores(t, slot):
        k0 = pl.multiple_of(t * TK, TK)
        for e in range(ATTN_NB):
            kb = k_ref[e, pl.ds(k0, TK), :]
            for c in range(2):
                s_ref[slot, e, c] = _dot_nt(kb, qc_ref[e, c]) + bias_ref[c, pl.ds(k0, TK), :]

    def numerators(m_state, slot):
        m_next, alpha_next = [], []
        for i, (e, c) in enumerate(chains):
            s = s_ref[slot, e, c]
            m_new = jnp.maximum(m_state[i], jnp.max(s, axis=0, keepdims=True))
            alpha_next.append(jnp.exp2(m_state[i] - m_new))
            p_ref[slot, e, c] = jnp.exp2(s - m_new).astype(BF16)
            m_next.append(m_new)
        return tuple(m_next), tuple(alpha_next)

    def accumulate(t, alpha, slot):
        for e in range(ATTN_NB):
            v0 = pl.multiple_of(e * seq + t * TK, TK)
            vt = vt_ref[:, pl.ds(v0, TK)]
            for c in range(2):
                acc_ref[e, c] = alpha[2 * e + c] * acc_ref[e, c] + _dot(vt, p_ref[slot, e, c])

    def kv_pair(i, carry):
        t = 2 * i
        m_state, alpha = carry
        accumulate(t - 2, alpha, 0)
        m_state, alpha = numerators(m_state, 1)
        scores(t, 0)
        accumulate(t - 1, alpha, 1)
        carry = numerators(m_state, 0)
        scores(t + 1, 1)
        return carry

    assert (TQ // TK) % 2 == 0
    scores(0, 0)
    carry = numerators(tuple(jnp.full((1, TQ), M_INIT, F32) for _ in chains), 0)
    scores(1, 1)
    m_state, alpha = lax.fori_loop(1, n_kv // 2, kv_pair, carry)
    accumulate(n_kv - 2, alpha, 0)
    _, alpha = numerators(m_state, 1)
    accumulate(n_kv - 1, alpha, 1)

    lam = (jnp.exp(jnp.sum(lq1_ref[...] * lk1_ref[...], axis=1, keepdims=True))
           - jnp.exp(jnp.sum(lq2_ref[...] * lk2_ref[...], axis=1, keepdims=True)) + lam_init)
    for e in range(ATTN_NB):
        a0 = acc_ref[e, 0]
        a1 = acc_ref[e, 1]
        o = a0[0:DIFF_DV] / a0[DIFF_DV:DIFF_DV + 1] - lam * (a1[0:DIFF_DV] / a1[DIFF_DV:DIFF_DV + 1])
        on = o * lax.rsqrt(jnp.mean(o * o, axis=0, keepdims=True) + EPS) * sg_ref[...]
        out_ref[e] = (on * (1.0 - lam_init)).T.astype(BF16)


def _attention(lam_init, tbl, qd, kd, vdt, pcol, prow, lq1, lk1, lq2, lk2, sg, batch, seq):
    q3, k3 = (t.reshape(batch, seq, GROUP_W) for t in (qd, kd))
    vec = lambda n: pl.BlockSpec((1, n), lambda h, i, b: (0, 0))
    kmax = jnp.max(prow.reshape(seq // BIAS_ROWS, BIAS_ROWS), axis=1)
    qmin = jnp.min(prow.reshape(seq // BIAS_COLS, BIAS_COLS), axis=1)
    in_specs = [
        pl.BlockSpec((None, 2 * N_BUCKETS, BIAS_COLS), lambda h, i, b: (h, 0, 0)),
        pl.BlockSpec(memory_space=pltpu.SMEM),
        pl.BlockSpec(memory_space=pltpu.SMEM),
        pl.BlockSpec((ATTN_NB, TQ, 128), lambda h, i, b: (b, i, h)),
        pl.BlockSpec((ATTN_NB, seq, 128), lambda h, i, b: (b, 0, h)),
        pl.BlockSpec((V_ROWS, ATTN_NB * seq), lambda h, i, b: (h, b)),
        pl.BlockSpec((seq, BIAS_COLS), lambda h, i, b: (0, 0)),
        pl.BlockSpec((1, TQ), lambda h, i, b: (0, i)),
        vec(DIFF_D), vec(DIFF_D), vec(DIFF_D), vec(DIFF_D),
        pl.BlockSpec((DIFF_DV, TQ), lambda h, i, b: (0, 0)),
    ]
    out = pl.pallas_call(
        functools.partial(_attn_kernel, lam_init),
        grid=(N_HEADS, seq // TQ, batch // ATTN_NB), in_specs=in_specs,
        out_specs=pl.BlockSpec((ATTN_NB, TQ, 128), lambda h, i, b: (b, i, h)),
        out_shape=jax.ShapeDtypeStruct((batch, seq, GROUP_W), BF16),
        scratch_shapes=[
            pltpu.VMEM((2, seq, TQ), F32),
            pltpu.VMEM((ATTN_NB, 2, TQ, 128), BF16),
            pltpu.VMEM((2, ATTN_NB, 2, TK, TQ), F32),
            pltpu.VMEM((2, ATTN_NB, 2, TK, TQ), BF16),
            pltpu.VMEM((ATTN_NB, 2, V_ROWS, TQ), F32),
        ],
        name="diff_attn",
        compiler_params=pltpu.CompilerParams(
            dimension_semantics=("arbitrary", "arbitrary", "arbitrary"), vmem_limit_bytes=VMEM_LIMIT),
    )(tbl, kmax, qmin, q3, k3, vdt, pcol, prow, lq1, lk1, lq2, lk2, jnp.broadcast_to(sg.reshape(DIFF_DV, 1), (DIFF_DV, TQ)))
    return out.reshape(batch * seq, GROUP_W)


def _gelu(x):
    return 0.5 * x * (1.0 + lax.erf(x * (2.0 ** -0.5)))


def _rms(x, g):
    return x * lax.rsqrt(jnp.mean(x * x, axis=-1, keepdims=True) + EPS) * g


def _mixer_kernel(h_ref, hmt_ref, hd_ref, p_ref, wo_ref, gf_ref, wu_ref, cw_ref, cb_ref, wd_ref, gp_ref,
                  wpg_ref, wpp_ref, out_ref, sg_ref, sv_ref, cg_ref, cv_ref, acc_ref, u_ref):
    tm = h_ref.shape[0]
    halo = 8

    @pl.when(pl.program_id(1) == 0)
    def _reset_conv_history():
        cg_ref[...] = jnp.zeros_like(cg_ref)
        cv_ref[...] = jnp.zeros_like(cv_ref)

    h1 = (h_ref[...] + _dot_tn(hmt_ref[...], wo_ref[0:GROUP_W, :])
          + _dot(hd_ref[...], wo_ref[GROUP_W:2 * GROUP_W, :]))
    u_ref[...] = _rms(h1, gf_ref[...]).astype(BF16)
    acc_ref[...] = jnp.zeros_like(acc_ref)

    def up_stage(j, slot):
        c0 = pl.multiple_of(j * FF_CHUNK, FF_CHUNK)
        sg_ref[slot, halo:, :] = _dot(u_ref[...], wu_ref[:, pl.ds(c0, FF_CHUNK)])
        sv_ref[slot, halo:, :] = _dot(u_ref[...], wu_ref[:, pl.ds(D_FF + c0, FF_CHUNK)])

    def conv_branch(j, slot, c0, stage_ref, hist_ref):
        stage_ref[slot, 0:halo, :] = hist_ref[j]
        hist_ref[j] = stage_ref[slot, tm:tm + halo, :]
        cw = cw_ref[:, pl.ds(c0, FF_CHUNK)]
        return (cw[0:1, :] * stage_ref[slot, halo - 2:halo - 2 + tm, :]
                + cw[1:2, :] * stage_ref[slot, halo - 1:halo - 1 + tm, :]
                + cw[2:3, :] * stage_ref[slot, halo:, :] + cb_ref[:, pl.ds(c0, FF_CHUNK)])

    def act_stage(j, slot):
        c0 = pl.multiple_of(j * FF_CHUNK, FF_CHUNK)
        gate = conv_branch(j, slot, c0, sg_ref, cg_ref)
        val = conv_branch(j, slot, D_FF + c0, sv_ref, cv_ref)
        act = (_gelu(gate) * val).astype(BF16)
        acc_ref[...] += _dot(act, wd_ref[j])

    def chunk_pair(i, _):
        j = 2 * i
        up_stage(j + 1, 1)
        act_stage(j, 0)
        up_stage(j + 2, 0)
        act_stage(j + 1, 1)
        return 0

    assert N_FF_CHUNKS % 2 == 1
    up_stage(0, 0)
    lax.fori_loop(0, N_FF_CHUNKS // 2, chunk_pair, 0)
    act_stage(N_FF_CHUNKS - 1, 0)

    h2 = h1 + acc_ref[...]
    u3 = _rms(h2, gp_ref[...]).astype(BF16)
    ple_gate = _sigmoid(_dot(u3, wpg_ref[...]))
    out_ref[...] = h2 + ple_gate * _dot(p_ref[...].astype(BF16), wpp_ref[...])


def _mixer(layer, h, hmt, hd, p, wo, gf, wu, cw, cb, wd, gp, wpg, wpp, batch, seq):
    tiles = seq // TM_FFN
    tile = lambda width: pl.BlockSpec((TM_FFN, width), lambda b, t: (b * tiles + t, 0))

    def of_layer(arr):
        nd = arr.ndim
        return pl.BlockSpec((None,) + arr.shape[1:], lambda b, t: (layer,) + (0,) * (nd - 1),
                            pipeline_mode=pl.Buffered(1))

    weights = (wo, gf, wu, cw, cb, wd, gp, wpg, wpp)
    in_specs = [
        tile(D_MODEL),
        pl.BlockSpec((GROUP_W, TM_FFN), lambda b, t: (0, b * tiles + t)),
        tile(GROUP_W),
        pl.BlockSpec((None, TM_FFN, PLE_DIM), lambda b, t: (layer, b * tiles + t, 0)),
    ] + [of_layer(w) for w in weights]
    return pl.pallas_call(
        _mixer_kernel, grid=(batch, tiles), in_specs=in_specs, out_specs=tile(D_MODEL),
        out_shape=jax.ShapeDtypeStruct(h.shape, F32),
        scratch_shapes=[
            pltpu.VMEM((2, TM_FFN + 8, FF_CHUNK), F32),
            pltpu.VMEM((2, TM_FFN + 8, FF_CHUNK), F32),
            pltpu.VMEM((N_FF_CHUNKS, 8, FF_CHUNK), F32),
            pltpu.VMEM((N_FF_CHUNKS, 8, FF_CHUNK), F32),
            pltpu.VMEM((TM_FFN, D_MODEL), F32),
            pltpu.VMEM((TM_FFN, D_MODEL), BF16),
        ],
        name="mixer",
        compiler_params=pltpu.CompilerParams(
            dimension_semantics=("arbitrary", "arbitrary"), vmem_limit_bytes=VMEM_LIMIT),
    )(h, hmt, hd, p, *weights)


def _head_interleave(qcols, kcols):
    lead = qcols.shape[:-1]
    qh = qcols.reshape(lead + (N_HEADS, MLSTM_DK))
    kh = kcols.reshape(lead + (N_HEADS, MLSTM_DK))
    return jnp.concatenate([qh, kh], axis=-1).reshape(lead + (GROUP_W,))


def kernel(x, p, positions, rel_bias, ln_mix_g, w_in, mlstm_conv_w, b_igate, b_fgate, mlstm_norm_g, q_norm_g, k_norm_g, lam_q1, lam_k1, lam_q2, lam_k2, diff_subln_g, w_out, ln_ffn_g, w_up, ffn_conv_w, ffn_conv_b, w_down, ln_ple_g, w_ple_gate, w_ple_proj):
    batch, seq, _ = x.shape
    depth = w_in.shape[0]
    n_tok = batch * seq
    qk_cols = N_HEADS * MLSTM_DK
    col_sizes = [qk_cols, qk_cols, GROUP_W, GROUP_W, N_HEADS, N_HEADS, GROUP_W, GROUP_W, GROUP_W]
    offs = np.concatenate([[0], np.cumsum(col_sizes)])
    sl = lambda a, j: a[..., int(offs[j]):int(offs[j + 1])]

    tbl = jnp.broadcast_to(
        (jnp.transpose(rel_bias.astype(F32), (1, 2, 0)) * LOG2E).reshape(N_HEADS, 2 * N_BUCKETS, 1),
        (N_HEADS, 2 * N_BUCKETS, BIAS_COLS))
    pcol = jnp.broadcast_to(positions.astype(jnp.int32).reshape(seq, 1), (seq, BIAS_COLS))
    prow = positions.astype(jnp.int32).reshape(1, seq)
    gsum = jnp.asarray(np.kron(np.eye(MXU_TILE // DIFF_D), np.ones((DIFF_D, DIFF_D))), BF16)
    t_idx = np.arange(PAIR)
    same_chunk = (t_idx[:, None] // CHUNK) == (t_idx[None, :] // CHUNK)
    prefix = same_chunk & (t_idx[:, None] <= t_idx[None, :])
    total_a = np.broadcast_to(t_idx[:, None] < CHUNK, (PAIR, PAIR))
    ug = jnp.asarray(np.concatenate([prefix, total_a, ~total_a], axis=1), BF16)
    l2 = jnp.asarray(prefix.T, BF16)

    p_tok = p.reshape(depth, n_tok, PLE_DIM)
    row = lambda a: a.astype(F32).reshape(depth, 1, a.shape[-1])
    mixer_weights = (
        w_out.astype(BF16), row(ln_ffn_g), w_up.astype(BF16), ffn_conv_w.astype(F32), row(ffn_conv_b),
        w_down.astype(BF16).reshape(depth, N_FF_CHUNKS, FF_CHUNK, D_MODEL), row(ln_ple_g),
        w_ple_gate.astype(BF16), w_ple_proj.astype(BF16))

    h = x.reshape(n_tok, D_MODEL)
    for i in range(depth):
        wi = w_in[i]
        wm = jnp.concatenate([_head_interleave(sl(wi, 0), sl(wi, 1)), sl(wi, 6), sl(wi, 7)], axis=-1).astype(BF16)
        wt = jnp.concatenate([sl(wi, 2), sl(wi, 3), sl(wi, 8)], axis=-1).T.astype(BF16)
        pad_rows = lambda a: jnp.zeros((N_GATE_ROWS,) + a.shape[1:], F32).at[:N_HEADS].set(a.astype(F32))
        wg = jnp.concatenate([pad_rows(sl(wi, 4).T), pad_rows(sl(wi, 5).T)], axis=0).astype(BF16)
        gate_bias = jnp.concatenate([pad_rows(b_igate[i]), pad_rows(b_fgate[i])])
        qg = jnp.tile(q_norm_g[i].astype(F32), GROUP_W // DIFF_D).reshape(1, GROUP_W) * (DIFF_D ** -0.5 * LOG2E)
        kg = jnp.tile(k_norm_g[i].astype(F32), GROUP_W // DIFF_D).reshape(1, GROUP_W)
        qk, vmt, omt, gri, grf, gci, gcf, qd, kd, vdt = _inproj(
            h, ln_mix_g[i].reshape(1, D_MODEL), wm, wt, wg, gate_bias.reshape(1, -1), gate_bias.reshape(-1, 1),
            gsum, qg, kg)

        cw = _head_interleave(mlstm_conv_w[i][:, :qk_cols], mlstm_conv_w[i][:, qk_cols:]).astype(F32)
        ng = jnp.broadcast_to(mlstm_norm_g[i].astype(F32).reshape(MLSTM_DV, 1), (MLSTM_DV, PAIR))
        hmt = _mlstm(qk, vmt, omt, gri, grf, gci, gcf, cw, ng, ug, l2, batch, seq)

        lam_init = 0.8 - 0.6 * math.exp(-0.3 * i)
        row64 = lambda a: a[i].reshape(1, DIFF_D).astype(F32)
        hd = _attention(lam_init, tbl, qd, kd, vdt, pcol, prow, row64(lam_q1), row64(lam_k1), row64(lam_q2),
                        row64(lam_k2), diff_subln_g[i].reshape(1, DIFF_DV).astype(F32), batch, seq)

        h = _mixer(i, h, hmt, hd, p_tok, *mixer_weights, batch, seq)
    return h.reshape(batch, seq, D_MODEL)
```

```python
import functools
import math

import numpy as np
import jax
import jax.numpy as jnp
from jax import lax
from jax.experimental import pallas as pl
from jax.experimental.pallas import tpu as pltpu

F32 = jnp.float32
BF16 = jnp.bfloat16

D_MODEL = 1024
N_HEADS = 4
MLSTM_DK = 64
MLSTM_DV = 128
MLSTM_CONV_W = 4
CHUNK = 64
PAIR = 2 * CHUNK
DIFF_D = 64
DIFF_DV = 128
N_BUCKETS = 32
MAX_DISTANCE = 128
D_FF = 2816
FFN_CONV_W = 3
PLE_DIM = 256
EPS = 1e-6
GROUP_W = N_HEADS * 128
N_GATE_ROWS = 16

MXU_TILE = 256
TM_IN = 512
TM_FFN = 512
FF_CHUNK = 256
N_FF_CHUNKS = D_FF // FF_CHUNK
TQ = 512
TK = 256
BIAS_ROWS = 32
BIAS_COLS = 128
ATTN_NB = 2
NEG_BIG = -1e30
M_INIT = -1e20
ONES_ROWS = 16
V_ROWS = DIFF_DV + ONES_ROWS
LOG2E = math.log2(math.e)
VMEM_LIMIT = 56 * 1024 * 1024


def _dot(a, b):
    return jnp.dot(a, b, preferred_element_type=F32)


def _dot_nt(a, b):
    return lax.dot_general(a, b, (((1,), (1,)), ((), ())), preferred_element_type=F32)


def _dot_tn(a, b):
    return lax.dot_general(a, b, (((0,), (0,)), ((), ())), preferred_element_type=F32)


def _sigmoid(x):
    return 1.0 / (1.0 + jnp.exp(-x))


def _log_sigmoid(x):
    return jnp.minimum(x, 0.0) - jnp.log1p(jnp.exp(-jnp.abs(x)))


def _bucket_thresholds():
    max_exact = N_BUCKETS // 2
    thr = []
    for v in range(1, N_BUCKETS):
        if v <= max_exact:
            thr.append(v)
            continue
        edge = max_exact * (MAX_DISTANCE / max_exact) ** ((v - max_exact) / (N_BUCKETS - max_exact))
        assert abs(edge - round(edge)) > 1e-3, edge
        thr.append(int(math.ceil(edge)))
    assert all(a < b for a, b in zip(thr, thr[1:])), thr
    return tuple(thr)


_BUCKET_THR = _bucket_thresholds()


def _group_mean_square(z, gsum_ref):
    sq = z * z
    hi = sq.astype(BF16)
    lo = (sq - hi.astype(F32)).astype(BF16)
    width = gsum_ref.shape[0]
    sums = [_dot(hi[:, c0:c0 + width], gsum_ref[...]) + _dot(lo[:, c0:c0 + width], gsum_ref[...])
            for c0 in range(0, z.shape[1], width)]
    return jnp.concatenate(sums, axis=1) * (1.0 / DIFF_D)


def _with_ones_rows(dst_ref, vt):
    for hh in range(N_HEADS):
        dst_ref[hh * V_ROWS:hh * V_ROWS + DIFF_DV, :] = vt[hh * DIFF_DV:(hh + 1) * DIFF_DV, :]
        dst_ref[hh * V_ROWS + DIFF_DV:(hh + 1) * V_ROWS, :] = jnp.ones((ONES_ROWS, vt.shape[1]), BF16)


def _inproj_kernel(h_ref, g_ref, wm_ref, wt_ref, wg_ref, bcol_ref, brow_ref, gsum_ref, qg_ref, kg_ref,
                   qk_ref, vmt_ref, omt_ref, gri_ref, grf_ref, gci_ref, gcf_ref, qd_ref, kd_ref, vdt_ref):
    x = h_ref[...]
    ms = jnp.mean(x * x, axis=-1, keepdims=True)
    u = (x * lax.rsqrt(ms + EPS) * g_ref[...]).astype(BF16)
    w = GROUP_W
    qk_ref[...] = _dot(u, wm_ref[:, 0:w])
    zq = _dot(u, wm_ref[:, w:2 * w])
    qd_ref[...] = (zq * lax.rsqrt(_group_mean_square(zq, gsum_ref) + EPS) * qg_ref[...]).astype(BF16)
    zk = _dot(u, wm_ref[:, 2 * w:3 * w])
    kd_ref[...] = (zk * lax.rsqrt(_group_mean_square(zk, gsum_ref) + EPS) * kg_ref[...]).astype(BF16)
    _with_ones_rows(vmt_ref, _dot_nt(wt_ref[0:w, :], u).astype(BF16))
    omt_ref[...] = _dot_nt(wt_ref[w:2 * w, :], u)
    _with_ones_rows(vdt_ref, _dot_nt(wt_ref[2 * w:3 * w, :], u).astype(BF16))
    gc = _dot_nt(u, wg_ref[...]) + bcol_ref[...]
    gci_ref[...] = gc[:, 0:N_GATE_ROWS]
    gcf_ref[...] = gc[:, N_GATE_ROWS:2 * N_GATE_ROWS]
    gr = _dot_nt(wg_ref[...], u) + brow_ref[...]
    for ci in range(TM_IN // PAIR):
        gri_ref[ci] = gr[0:8, ci * PAIR:(ci + 1) * PAIR]
        grf_ref[ci] = gr[N_GATE_ROWS:N_GATE_ROWS + 8, ci * PAIR:(ci + 1) * PAIR]


def _inproj(h, g, wm, wt, wg, bcol, brow, gsum, qg, kg):
    n_tok = h.shape[0]
    grid = (n_tok // TM_IN,)
    const = lambda shape: pl.BlockSpec(shape, lambda i: (0,) * len(shape))
    tile = lambda width: pl.BlockSpec((TM_IN, width), lambda i: (i, 0))
    tile_t = lambda rows: pl.BlockSpec((rows, TM_IN), lambda i: (0, i))
    gate_rows = pl.BlockSpec((TM_IN // PAIR, 8, PAIR), lambda i: (i, 0, 0))
    out_shape = (
        jax.ShapeDtypeStruct((n_tok, GROUP_W), F32),
        jax.ShapeDtypeStruct((N_HEADS * V_ROWS, n_tok), BF16),
        jax.ShapeDtypeStruct((GROUP_W, n_tok), F32),
        jax.ShapeDtypeStruct((n_tok // PAIR, 8, PAIR), F32),
        jax.ShapeDtypeStruct((n_tok // PAIR, 8, PAIR), F32),
        jax.ShapeDtypeStruct((n_tok, N_GATE_ROWS), F32),
        jax.ShapeDtypeStruct((n_tok, N_GATE_ROWS), F32),
        jax.ShapeDtypeStruct((n_tok, GROUP_W), BF16),
        jax.ShapeDtypeStruct((n_tok, GROUP_W), BF16),
        jax.ShapeDtypeStruct((N_HEADS * V_ROWS, n_tok), BF16),
    )
    out_specs = (
        tile(GROUP_W), tile_t(N_HEADS * V_ROWS), tile_t(GROUP_W), gate_rows, gate_rows,
        tile(N_GATE_ROWS), tile(N_GATE_ROWS), tile(GROUP_W), tile(GROUP_W), tile_t(N_HEADS * V_ROWS),
    )
    in_specs = [
        tile(D_MODEL), const((1, D_MODEL)), const(wm.shape), const(wt.shape), const(wg.shape),
        const(bcol.shape), const(brow.shape), const(gsum.shape),
        const((1, GROUP_W)), const((1, GROUP_W)),
    ]
    return pl.pallas_call(
        _inproj_kernel, grid=grid, in_specs=in_specs, out_specs=out_specs, out_shape=out_shape,
        name="inproj",
        compiler_params=pltpu.CompilerParams(dimension_semantics=("arbitrary",), vmem_limit_bytes=VMEM_LIMIT),
    )(h, g, wm, wt, wg, bcol, brow, gsum, qg, kg)


def _split_hi_lo(x):
    hi = x.astype(BF16)
    return hi, (x - hi.astype(F32)).astype(BF16)


def _mlstm_kernel(qk_ref, vt_ref, ot_ref, gri_ref, grf_ref, gci_ref, gcf_ref, cw_ref, ng_ref, ug_ref, l2_ref,
                  out_ref, xpad_ref, cn_ref):
    seq = qk_ref.shape[0]
    halo = 8
    xpad_ref[0:halo, :] = jnp.zeros((halo, GROUP_W), F32)
    xpad_ref[halo:, :] = qk_ref[...]
    cn_ref[...] = jnp.zeros_like(cn_ref)

    first8 = lax.broadcasted_iota(jnp.int32, (8, PAIR), 1) < CHUNK
    first = lax.broadcasted_iota(jnp.int32, (1, PAIR), 1) < CHUNK
    upper = lax.broadcasted_iota(jnp.int32, (PAIR, PAIR), 1) >= MLSTM_DK
    key_t = lax.broadcasted_iota(jnp.int32, (PAIR, PAIR), 0)
    qry_t = lax.broadcasted_iota(jnp.int32, (PAIR, PAIR), 1)
    chunk_start = jnp.where(qry_t < CHUNK, 0, CHUNK)
    cw = cw_ref[...]
    neg_inf = jnp.float32(-jnp.inf)

    def pair_step(c2, m_prev):
        r0 = pl.multiple_of(c2 * PAIR, PAIR)
        win = xpad_ref[pl.ds(r0, PAIR + halo), :]
        conv = cw[0:1, :] * win[halo - 3:halo - 3 + PAIR, :]
        for j in range(1, MLSTM_CONV_W):
            conv = conv + cw[j:j + 1, :] * win[halo - 3 + j:halo - 3 + j + PAIR, :]
        x = conv * _sigmoid(conv)

        gi = gri_ref[c2]
        lf_hi, lf_lo = _split_hi_lo(_log_sigmoid(grf_ref[c2]))
        bg = _dot(lf_hi, ug_ref[...]) + _dot(lf_lo, ug_ref[...])
        b_r = bg[:, 0:PAIR]
        g_a = bg[:, PAIR:2 * PAIR]
        g_b = bg[:, 2 * PAIR:3 * PAIR]
        a_r = jnp.where(first8, g_a, g_b) - b_r + gi
        max_a = jnp.max(jnp.where(first8, a_r, neg_inf), axis=1, keepdims=True)
        max_b = jnp.max(jnp.where(first8, neg_inf, a_r), axis=1, keepdims=True)
        m_a = jnp.maximum(g_a + m_prev, max_a)
        m_b = jnp.maximum(g_b + m_a, max_b)
        dec_a = jnp.exp(g_a + m_prev - m_a)
        dec_b = jnp.exp(g_b + m_a - m_b)
        w_r = jnp.exp(a_r - jnp.where(first8, m_a, m_b))
        e_r = b_r + jnp.where(first8, m_prev, m_a)
        lc_hi, lc_lo = _split_hi_lo(_log_sigmoid(gcf_ref[pl.ds(r0, PAIR), :]))
        x_c = _dot(l2_ref[...], lc_hi) + _dot(l2_ref[...], lc_lo) - gci_ref[pl.ds(r0, PAIR), :]

        heads = range(N_HEADS)
        xk, xq, vt, cn0, up_a, kq, cq_a = [], [], [], [], [], [], []
        for hd in heads:
            xh = x[:, hd * 128:(hd + 1) * 128]
            xs = pltpu.roll(xh, MLSTM_DK, 1)
            xk.append(jnp.where(upper, xh * (MLSTM_DK ** -0.5), 0.0).astype(BF16))
            xq.append(jnp.where(upper, xs, 0.0).astype(BF16))
            vt.append(vt_ref[hd * V_ROWS:(hd + 1) * V_ROWS, pl.ds(r0, PAIR)])
            cn0.append(cn_ref[hd])
            w_a = jnp.where(first, w_r[hd:hd + 1, :], 0.0)
            up_a.append(_dot((vt[hd].astype(F32) * w_a).astype(BF16), xk[hd]))
            kq.append(_dot_nt(xk[hd], xq[hd]))
            cq_a.append(_dot_nt(cn0[hd].astype(BF16), xq[hd]))
        cn1, cq_b, up_b = [], [], []
        for hd in heads:
            cn1.append(dec_a[hd:hd + 1, :] * cn0[hd] + up_a[hd])
            w_b = jnp.where(first, 0.0, w_r[hd:hd + 1, :])
            up_b.append(_dot((vt[hd].astype(F32) * w_b).astype(BF16), xk[hd]))
            cq_b.append(_dot_nt(cn1[hd].astype(BF16), xq[hd]))
        m_out, sv = [], []
        for hd in heads:
            cn_ref[hd] = dec_b[hd:hd + 1, :] * cn1[hd] + up_b[hd]
            dmat = b_r[hd:hd + 1, :] - x_c[:, hd:hd + 1]
            dmat = jnp.where(key_t <= qry_t, jnp.where(key_t >= chunk_start, dmat, neg_inf), neg_inf)
            m_out.append(jnp.maximum(e_r[hd:hd + 1, :], jnp.max(dmat, axis=0, keepdims=True)))
            s_t = (kq[hd] * jnp.exp(dmat - m_out[hd])).astype(BF16)
            sv.append(_dot(vt[hd], s_t))
        for hd in heads:
            rows = slice(hd * MLSTM_DV, (hd + 1) * MLSTM_DV)
            inter = jnp.exp(e_r[hd:hd + 1, :] - m_out[hd])
            tot = inter * jnp.where(first, cq_a[hd], cq_b[hd]) + sv[hd]
            den = tot[MLSTM_DV:MLSTM_DV + 1, :]
            hh = tot[0:MLSTM_DV, :] / jnp.maximum(jnp.abs(den), jnp.exp(-m_out[hd]))
            hn = hh * lax.rsqrt(jnp.mean(hh * hh, axis=0, keepdims=True) + EPS) * ng_ref[...]
            og = ot_ref[rows, pl.ds(r0, PAIR)]
            out_ref[rows, pl.ds(r0, PAIR)] = (hn * _sigmoid(og)).astype(BF16)
        return m_b

    lax.fori_loop(0, seq // PAIR, pair_step, jnp.zeros((8, PAIR), F32))


def _mlstm(qk, vmt, omt, gri, grf, gci, gcf, cw, ng, ug, l2, batch, seq):
    const = lambda arr: pl.BlockSpec(arr.shape, lambda b: (0,) * arr.ndim)
    in_specs = [
        pl.BlockSpec((seq, GROUP_W), lambda b: (b, 0)),
        pl.BlockSpec((N_HEADS * V_ROWS, seq), lambda b: (0, b)),
        pl.BlockSpec((GROUP_W, seq), lambda b: (0, b)),
        pl.BlockSpec((seq // PAIR, 8, PAIR), lambda b: (b, 0, 0)),
        pl.BlockSpec((seq // PAIR, 8, PAIR), lambda b: (b, 0, 0)),
        pl.BlockSpec((seq, N_GATE_ROWS), lambda b: (b, 0)),
        pl.BlockSpec((seq, N_GATE_ROWS), lambda b: (b, 0)),
        const(cw), const(ng), const(ug), const(l2),
    ]
    return pl.pallas_call(
        _mlstm_kernel, grid=(batch,), in_specs=in_specs,
        out_specs=pl.BlockSpec((GROUP_W, seq), lambda b: (0, b)),
        out_shape=jax.ShapeDtypeStruct((GROUP_W, batch * seq), BF16),
        scratch_shapes=[
            pltpu.VMEM((seq + 8, GROUP_W), F32),
            pltpu.VMEM((N_HEADS, V_ROWS, 128), F32),
        ],
        name="mlstm",
        compiler_params=pltpu.CompilerParams(dimension_semantics=("arbitrary",), vmem_limit_bytes=VMEM_LIMIT),
    )(qk, vmt, omt, gri, grf, gci, gcf, cw, ng, ug, l2)


def _attn_kernel(lam_init, tbl_ref, kmax_ref, qmin_ref, q_ref, k_ref, vt_ref, pcol_ref, prow_ref,
                 lq1_ref, lk1_ref, lq2_ref, lk2_ref, sg_ref, out_ref, bias_ref, qc_ref, s_ref, p_ref, acc_ref):
    hd = pl.program_id(0)
    qi = pl.program_id(1)
    b = pl.program_id(2)
    n_kv = (qi + 1) * (TQ // TK)

    @pl.when(b == 0)
    def _build_bias():
        rowi = lax.broadcasted_iota(jnp.int32, (BIAS_ROWS, BIAS_COLS), 0)
        coli = lax.broadcasted_iota(jnp.int32, (BIAS_ROWS, BIAS_COLS), 1)
        row = lambda c, v: tbl_ref[c * N_BUCKETS + v:c * N_BUCKETS + v + 1, :]
        last = N_BUCKETS - 1

        def build(t, _):
            r0 = pl.multiple_of(t * BIAS_ROWS, BIAS_ROWS)
            pk = pcol_ref[pl.ds(r0, BIAS_ROWS), :]
            kmax = kmax_ref[t]
            for j in range(TQ // BIAS_COLS):
                lanes = slice(j * BIAS_COLS, (j + 1) * BIAS_COLS)
                q0 = qi * TQ + j * BIAS_COLS
                causal = rowi + r0 <= coli + q0
                flat = jnp.logical_or(qmin_ref[qi * (TQ // BIAS_COLS) + j] - kmax >= _BUCKET_THR[-1],
                                      r0 > q0 + BIAS_COLS - 1)

                @pl.when(flat)
                def _flat():
                    bias_ref[0, pl.ds(r0, BIAS_ROWS), lanes] = jnp.where(causal, row(0, last), NEG_BIG)
                    bias_ref[1, pl.ds(r0, BIAS_ROWS), lanes] = jnp.where(causal, row(1, last), NEG_BIG)

                @pl.when(jnp.logical_not(flat))
                def _lookup():
                    dist = jnp.maximum(prow_ref[:, lanes] - pk, 0)
                    b0 = jnp.broadcast_to(row(0, 0), (BIAS_ROWS, BIAS_COLS))
                    b1 = jnp.broadcast_to(row(1, 0), (BIAS_ROWS, BIAS_COLS))
                    for v, thr in enumerate(_BUCKET_THR, start=1):
                        ge = dist >= thr
                        b0 = jnp.where(ge, row(0, v), b0)
                        b1 = jnp.where(ge, row(1, v), b1)
                    bias_ref[0, pl.ds(r0, BIAS_ROWS), lanes] = jnp.where(causal, b0, NEG_BIG)
                    bias_ref[1, pl.ds(r0, BIAS_ROWS), lanes] = jnp.where(causal, b1, NEG_BIG)
            return 0

        lax.fori_loop(0, n_kv * (TK // BIAS_ROWS), build, 0)

    seq = k_ref.shape[1]
    lane = lax.broadcasted_iota(jnp.int32, (TQ, 128), 1)
    for e in range(ATTN_NB):
        q = q_ref[e]
        zero = jnp.zeros_like(q)
        qc_ref[e, 0] = jnp.where(lane < DIFF_D, q, zero)
        qc_ref[e, 1] = jnp.where(lane >= DIFF_D, q, zero)
    acc_ref[...] = jnp.zeros_like(acc_ref)

    chains = [(e, c) for e in range(ATTN_NB) for c in range(2)]

    def scores(t, slot):
        k0 = pl.multiple_of(t * TK, TK)
        for e in range(ATTN_NB):
            kb = k_ref[e, pl.ds(k0, TK), :]
            for c in range(2):
                s_ref[slot, e, c] = _dot_nt(kb, qc_ref[e, c]) + bias_ref[c, pl.ds(k0, TK), :]

    def numerators(m_state, slot):
        m_next, alpha_next = [], []
        for i, (e, c) in enumerate(chains):
            s = s_ref[slot, e, c]
            m_new = jnp.maximum(m_state[i], jnp.max(s, axis=0, keepdims=True))
            alpha_next.append(jnp.exp2(m_state[i] - m_new))
            p_ref[slot, e, c] = jnp.exp2(s - m_new).astype(BF16)
            m_next.append(m_new)
        return tuple(m_next), tuple(alpha_next)

    def accumulate(t, alpha, slot):
        for e in range(ATTN_NB):
            v0 = pl.multiple_of(e * seq + t * TK, TK)
            vt = vt_ref[:, pl.ds(v0, TK)]
            for c in range(2):
                acc_ref[e, c] = alpha[2 * e + c] * acc_ref[e, c] + _dot(vt, p_ref[slot, e, c])

    def kv_pair(i, carry):
        t = 2 * i
        m_state, alpha = carry
        accumulate(t - 2, alpha, 0)
        m_state, alpha = numerators(m_state, 1)
        scores(t, 0)
        accumulate(t - 1, alpha, 1)
        carry = numerators(m_state, 0)
        scores(t + 1, 1)
        return carry

    assert (TQ // TK) % 2 == 0
    scores(0, 0)
    carry = numerators(tuple(jnp.full((1, TQ), M_INIT, F32) for _ in chains), 0)
    scores(1, 1)
    m_state, alpha = lax.fori_loop(1, n_kv // 2, kv_pair, carry)
    accumulate(n_kv - 2, alpha, 0)
    _, alpha = numerators(m_state, 1)
    accumulate(n_kv - 1, alpha, 1)

    lam = (jnp.exp(jnp.sum(lq1_ref[...] * lk1_ref[...], axis=1, keepdims=True))
           - jnp.exp(jnp.sum(lq2_ref[...] * lk2_ref[...], axis=1, keepdims=True)) + lam_init)
    for e in range(ATTN_NB):
        a0 = acc_ref[e, 0]
        a1 = acc_ref[e, 1]
        o = a0[0:DIFF_DV] / a0[DIFF_DV:DIFF_DV + 1] - lam * (a1[0:DIFF_DV] / a1[DIFF_DV:DIFF_DV + 1])
        on = o * lax.rsqrt(jnp.mean(o * o, axis=0, keepdims=True) + EPS) * sg_ref[...]
        out_ref[e] = (on * (1.0 - lam_init)).T.astype(BF16)


def _attention(lam_init, tbl, qd, kd, vdt, pcol, prow, lq1, lk1, lq2, lk2, sg, batch, seq):
    q3, k3 = (t.reshape(batch, seq, GROUP_W) for t in (qd, kd))
    vec = lambda n: pl.BlockSpec((1, n), lambda h, i, b: (0, 0))
    kmax = jnp.max(prow.reshape(seq // BIAS_ROWS, BIAS_ROWS), axis=1)
    qmin = jnp.min(prow.reshape(seq // BIAS_COLS, BIAS_COLS), axis=1)
    in_specs = [
        pl.BlockSpec((None, 2 * N_BUCKETS, BIAS_COLS), lambda h, i, b: (h, 0, 0)),
        pl.BlockSpec(memory_space=pltpu.SMEM),
        pl.BlockSpec(memory_space=pltpu.SMEM),
        pl.BlockSpec((ATTN_NB, TQ, 128), lambda h, i, b: (b, i, h)),
        pl.BlockSpec((ATTN_NB, seq, 128), lambda h, i, b: (b, 0, h)),
        pl.BlockSpec((V_ROWS, ATTN_NB * seq), lambda h, i, b: (h, b)),
        pl.BlockSpec((seq, BIAS_COLS), lambda h, i, b: (0, 0)),
        pl.BlockSpec((1, TQ), lambda h, i, b: (0, i)),
        vec(DIFF_D), vec(DIFF_D), vec(DIFF_D), vec(DIFF_D),
        pl.BlockSpec((DIFF_DV, TQ), lambda h, i, b: (0, 0)),
    ]
    out = pl.pallas_call(
        functools.partial(_attn_kernel, lam_init),
        grid=(N_HEADS, seq // TQ, batch // ATTN_NB), in_specs=in_specs,
        out_specs=pl.BlockSpec((ATTN_NB, TQ, 128), lambda h, i, b: (b, i, h)),
        out_shape=jax.ShapeDtypeStruct((batch, seq, GROUP_W), BF16),
        scratch_shapes=[
            pltpu.VMEM((2, seq, TQ), F32),
            pltpu.VMEM((ATTN_NB, 2, TQ, 128), BF16),
            pltpu.VMEM((2, ATTN_NB, 2, TK, TQ), F32),
            pltpu.VMEM((2, ATTN_NB, 2, TK, TQ), BF16),
            pltpu.VMEM((ATTN_NB, 2, V_ROWS, TQ), F32),
        ],
        name="diff_attn",
        compiler_params=pltpu.CompilerParams(
            dimension_semantics=("arbitrary", "arbitrary", "arbitrary"), vmem_limit_bytes=VMEM_LIMIT),
    )(tbl, kmax, qmin, q3, k3, vdt, pcol, prow, lq1, lk1, lq2, lk2, jnp.broadcast_to(sg.reshape(DIFF_DV, 1), (DIFF_DV, TQ)))
    return out.reshape(batch * seq, GROUP_W)


def _gelu(x):
    return 0.5 * x * (1.0 + lax.erf(x * (2.0 ** -0.5)))


def _rms(x, g):
    return x * lax.rsqrt(jnp.mean(x * x, axis=-1, keepdims=True) + EPS) * g


def _mixer_kernel(h_ref, hmt_ref, hd_ref, p_ref, wo_ref, gf_ref, wu_ref, cw_ref, cb_ref, wd_ref, gp_ref,
                  wpg_ref, wpp_ref, out_ref, sg_ref, sv_ref, cg_ref, cv_ref, acc_ref, u_ref):
    tm = h_ref.shape[0]
    halo = 8

    @pl.when(pl.program_id(1) == 0)
    def _reset_conv_history():
        cg_ref[...] = jnp.zeros_like(cg_ref)
        cv_ref[...] = jnp.zeros_like(cv_ref)

    h1 = (h_ref[...] + _dot_tn(hmt_ref[...], wo_ref[0:GROUP_W, :])
          + _dot(hd_ref[...], wo_ref[GROUP_W:2 * GROUP_W, :]))
    u_ref[...] = _rms(h1, gf_ref[...]).astype(BF16)
    acc_ref[...] = jnp.zeros_like(acc_ref)

    def up_stage(j, slot):
        c0 = pl.multiple_of(j * FF_CHUNK, FF_CHUNK)
        sg_ref[slot, halo:, :] = _dot(u_ref[...], wu_ref[:, pl.ds(c0, FF_CHUNK)])
        sv_ref[slot, halo:, :] = _dot(u_ref[...], wu_ref[:, pl.ds(D_FF + c0, FF_CHUNK)])

    def conv_branch(j, slot, c0, stage_ref, hist_ref):
        stage_ref[slot, 0:halo, :] = hist_ref[j]
        hist_ref[j] = stage_ref[slot, tm:tm + halo, :]
        cw = cw_ref[:, pl.ds(c0, FF_CHUNK)]
        return (cw[0:1, :] * stage_ref[slot, halo - 2:halo - 2 + tm, :]
                + cw[1:2, :] * stage_ref[slot, halo - 1:halo - 1 + tm, :]
                + cw[2:3, :] * stage_ref[slot, halo:, :] + cb_ref[:, pl.ds(c0, FF_CHUNK)])

    def act_stage(j, slot):
        c0 = pl.multiple_of(j * FF_CHUNK, FF_CHUNK)
        gate = conv_branch(j, slot, c0, sg_ref, cg_ref)
        val = conv_branch(j, slot, D_FF + c0, sv_ref, cv_ref)
        act = (_gelu(gate) * val).astype(BF16)
        acc_ref[...] += _dot(act, wd_ref[j])

    def chunk_pair(i, _):
        j = 2 * i
        up_stage(j + 1, 1)
        act_stage(j, 0)
        up_stage(j + 2, 0)
        act_stage(j + 1, 1)
        return 0

    assert N_FF_CHUNKS % 2 == 1
    up_stage(0, 0)
    lax.fori_loop(0, N_FF_CHUNKS // 2, chunk_pair, 0)
    act_stage(N_FF_CHUNKS - 1, 0)

    h2 = h1 + acc_ref[...]
    u3 = _rms(h2, gp_ref[...]).astype(BF16)
    ple_gate = _sigmoid(_dot(u3, wpg_ref[...]))
    out_ref[...] = h2 + ple_gate * _dot(p_ref[...].astype(BF16), wpp_ref[...])


def _mixer(layer, h, hmt, hd, p, wo, gf, wu, cw, cb, wd, gp, wpg, wpp, batch, seq):
    tiles = seq // TM_FFN
    tile = lambda width: pl.BlockSpec((TM_FFN, width), lambda b, t: (b * tiles + t, 0))

    def of_layer(arr):
        nd = arr.ndim
        return pl.BlockSpec((None,) + arr.shape[1:], lambda b, t: (layer,) + (0,) * (nd - 1),
                            pipeline_mode=pl.Buffered(1))

    weights = (wo, gf, wu, cw, cb, wd, gp, wpg, wpp)
    in_specs = [
        tile(D_MODEL),
        pl.BlockSpec((GROUP_W, TM_FFN), lambda b, t: (0, b * tiles + t)),
        tile(GROUP_W),
        pl.BlockSpec((None, TM_FFN, PLE_DIM), lambda b, t: (layer, b * tiles + t, 0)),
    ] + [of_layer(w) for w in weights]
    return pl.pallas_call(
        _mixer_kernel, grid=(batch, tiles), in_specs=in_specs, out_specs=tile(D_MODEL),
        out_shape=jax.ShapeDtypeStruct(h.shape, F32),
        scratch_shapes=[
            pltpu.VMEM((2, TM_FFN + 8, FF_CHUNK), F32),
            pltpu.VMEM((2, TM_FFN + 8, FF_CHUNK), F32),
            pltpu.VMEM((N_FF_CHUNKS, 8, FF_CHUNK), F32),
            pltpu.VMEM((N_FF_CHUNKS, 8, FF_CHUNK), F32),
            pltpu.VMEM((TM_FFN, D_MODEL), F32),
            pltpu.VMEM((TM_FFN, D_MODEL), BF16),
        ],
        name="mixer",
        compiler_params=pltpu.CompilerParams(
            dimension_semantics=("arbitrary", "arbitrary"), vmem_limit_bytes=VMEM_LIMIT),
    )(h, hmt, hd, p, *weights)


def _head_interleave(qcols, kcols):
    lead = qcols.shape[:-1]
    qh = qcols.reshape(lead + (N_HEADS, MLSTM_DK))
    kh = kcols.reshape(lead + (N_HEADS, MLSTM_DK))
    return jnp.concatenate([qh, kh], axis=-1).reshape(lead + (GROUP_W,))


def kernel(x, p, positions, rel_bias, ln_mix_g, w_in, mlstm_conv_w, b_igate, b_fgate, mlstm_norm_g, q_norm_g, k_norm_g, lam_q1, lam_k1, lam_q2, lam_k2, diff_subln_g, w_out, ln_ffn_g, w_up, ffn_conv_w, ffn_conv_b, w_down, ln_ple_g, w_ple_gate, w_ple_proj):
    batch, seq, _ = x.shape
    depth = w_in.shape[0]
    n_tok = batch * seq
    qk_cols = N_HEADS * MLSTM_DK
    col_sizes = [qk_cols, qk_cols, GROUP_W, GROUP_W, N_HEADS, N_HEADS, GROUP_W, GROUP_W, GROUP_W]
    offs = np.concatenate([[0], np.cumsum(col_sizes)])
    sl = lambda a, j: a[..., int(offs[j]):int(offs[j + 1])]

    tbl = jnp.broadcast_to(
        (jnp.transpose(rel_bias.astype(F32), (1, 2, 0)) * LOG2E).reshape(N_HEADS, 2 * N_BUCKETS, 1),
        (N_HEADS, 2 * N_BUCKETS, BIAS_COLS))
    pcol = jnp.broadcast_to(positions.astype(jnp.int32).reshape(seq, 1), (seq, BIAS_COLS))
    prow = positions.astype(jnp.int32).reshape(1, seq)
    gsum = jnp.asarray(np.kron(np.eye(MXU_TILE // DIFF_D), np.ones((DIFF_D, DIFF_D))), BF16)
    t_idx = np.arange(PAIR)
    same_chunk = (t_idx[:, None] // CHUNK) == (t_idx[None, :] // CHUNK)
    prefix = same_chunk & (t_idx[:, None] <= t_idx[None, :])
    total_a = np.broadcast_to(t_idx[:, None] < CHUNK, (PAIR, PAIR))
    ug = jnp.asarray(np.concatenate([prefix, total_a, ~total_a], axis=1), BF16)
    l2 = jnp.asarray(prefix.T, BF16)

    p_tok = p.reshape(depth, n_tok, PLE_DIM)
    row = lambda a: a.astype(F32).reshape(depth, 1, a.shape[-1])
    mixer_weights = (
        w_out.astype(BF16), row(ln_ffn_g), w_up.astype(BF16), ffn_conv_w.astype(F32), row(ffn_conv_b),
        w_down.astype(BF16).reshape(depth, N_FF_CHUNKS, FF_CHUNK, D_MODEL), row(ln_ple_g),
        w_ple_gate.astype(BF16), w_ple_proj.astype(BF16))

    h = x.reshape(n_tok, D_MODEL)
    for i in range(depth):
        wi = w_in[i]
        wm = jnp.concatenate([_head_interleave(sl(wi, 0), sl(wi, 1)), sl(wi, 6), sl(wi, 7)], axis=-1).astype(BF16)
        wt = jnp.concatenate([sl(wi, 2), sl(wi, 3), sl(wi, 8)], axis=-1).T.astype(BF16)
        pad_rows = lambda a: jnp.zeros((N_GATE_ROWS,) + a.shape[1:], F32).at[:N_HEADS].set(a.astype(F32))
        wg = jnp.concatenate([pad_rows(sl(wi, 4).T), pad_rows(sl(wi, 5).T)], axis=0).astype(BF16)
        gate_bias = jnp.concatenate([pad_rows(b_igate[i]), pad_rows(b_fgate[i])])
        qg = jnp.tile(q_norm_g[i].astype(F32), GROUP_W // DIFF_D).reshape(1, GROUP_W) * (DIFF_D ** -0.5 * LOG2E)
        kg = jnp.tile(k_norm_g[i].astype(F32), GROUP_W // DIFF_D).reshape(1, GROUP_W)
        qk, vmt, omt, gri, grf, gci, gcf, qd, kd, vdt = _inproj(
            h, ln_mix_g[i].reshape(1, D_MODEL), wm, wt, wg, gate_bias.reshape(1, -1), gate_bias.reshape(-1, 1),
            gsum, qg, kg)

        cw = _head_interleave(mlstm_conv_w[i][:, :qk_cols], mlstm_conv_w[i][:, qk_cols:]).astype(F32)
        ng = jnp.broadcast_to(mlstm_norm_g[i].astype(F32).reshape(MLSTM_DV, 1), (MLSTM_DV, PAIR))
        hmt = _mlstm(qk, vmt, omt, gri, grf, gci, gcf, cw, ng, ug, l2, batch, seq)

        lam_init = 0.8 - 0.6 * math.exp(-0.3 * i)
        row64 = lambda a: a[i].reshape(1, DIFF_D).astype(F32)
        hd = _attention(lam_init, tbl, qd, kd, vdt, pcol, prow, row64(lam_q1), row64(lam_k1), row64(lam_q2),
                        row64(lam_k2), diff_subln_g[i].reshape(1, DIFF_DV).astype(F32), batch, seq)

        h = _mixer(i, h, hmt, hd, p_tok, *mixer_weights, batch, seq)
    return h.reshape(batch, seq, D_MODEL)
```

```python
import functools
import math

import numpy as np
import jax
import jax.numpy as jnp
from jax import lax
from jax.experimental import pallas as pl
from jax.experimental.pallas import tpu as pltpu

F32 = jnp.float32
BF16 = jnp.bfloat16

D_MODEL = 1024
N_HEADS = 4
MLSTM_DK = 64
MLSTM_DV = 128
MLSTM_CONV_W = 4
CHUNK = 64
PAIR = 2 * CHUNK
DIFF_D = 64
DIFF_DV = 128
N_BUCKETS = 32
MAX_DISTANCE = 128
D_FF = 2816
FFN_CONV_W = 3
PLE_DIM = 256
EPS = 1e-6
GROUP_W = N_HEADS * 128
N_GATE_ROWS = 16

MXU_TILE = 256
TM_IN = 512
TM_FFN = 512
FF_CHUNK = 256
N_FF_CHUNKS = D_FF // FF_CHUNK
TQ = 512
TK = 256
BIAS_ROWS = 32
BIAS_COLS = 128
ATTN_NB = 2
NEG_BIG = -1e30
M_INIT = -1e20
ONES_ROWS = 16
V_ROWS = DIFF_DV + ONES_ROWS
LOG2E = math.log2(math.e)
VMEM_LIMIT = 56 * 1024 * 1024


def _dot(a, b):
    return jnp.dot(a, b, preferred_element_type=F32)


def _dot_nt(a, b):
    return lax.dot_general(a, b, (((1,), (1,)), ((), ())), preferred_element_type=F32)


def _dot_tn(a, b):
    return lax.dot_general(a, b, (((0,), (0,)), ((), ())), preferred_element_type=F32)


def _sigmoid(x):
    return 1.0 / (1.0 + jnp.exp(-x))


def _log_sigmoid(x):
    return jnp.minimum(x, 0.0) - jnp.log1p(jnp.exp(-jnp.abs(x)))


def _bucket_thresholds():
    max_exact = N_BUCKETS // 2
    thr = []
    for v in range(1, N_BUCKETS):
        if v <= max_exact:
            thr.append(v)
            continue
        edge = max_exact * (MAX_DISTANCE / max_exact) ** ((v - max_exact) / (N_BUCKETS - max_exact))
        assert abs(edge - round(edge)) > 1e-3, edge
        thr.append(int(math.ceil(edge)))
    assert all(a < b for a, b in zip(thr, thr[1:])), thr
    return tuple(thr)


_BUCKET_THR = _bucket_thresholds()


def _group_mean_square(z, gsum_ref):
    sq = z * z
    hi = sq.astype(BF16)
    lo = (sq - hi.astype(F32)).astype(BF16)
    width = gsum_ref.shape[0]
    sums = [_dot(hi[:, c0:c0 + width], gsum_ref[...]) + _dot(lo[:, c0:c0 + width], gsum_ref[...])
            for c0 in range(0, z.shape[1], width)]
    return jnp.concatenate(sums, axis=1) * (1.0 / DIFF_D)


def _with_ones_rows(dst_ref, vt):
    for hh in range(N_HEADS):
        dst_ref[hh * V_ROWS:hh * V_ROWS + DIFF_DV, :] = vt[hh * DIFF_DV:(hh + 1) * DIFF_DV, :]
        dst_ref[hh * V_ROWS + DIFF_DV:(hh + 1) * V_ROWS, :] = jnp.ones((ONES_ROWS, vt.shape[1]), BF16)


def _inproj_kernel(h_ref, g_ref, wm_ref, wt_ref, wg_ref, bcol_ref, brow_ref, gsum_ref, qg_ref, kg_ref,
                   qk_ref, vmt_ref, omt_ref, gri_ref, grf_ref, gci_ref, gcf_ref, qd_ref, kd_ref, vdt_ref):
    x = h_ref[...]
    ms = jnp.mean(x * x, axis=-1, keepdims=True)
    u = (x * lax.rsqrt(ms + EPS) * g_ref[...]).astype(BF16)
    w = GROUP_W
    qk_ref[...] = _dot(u, wm_ref[:, 0:w])
    zq = _dot(u, wm_ref[:, w:2 * w])
    qd_ref[...] = (zq * lax.rsqrt(_group_mean_square(zq, gsum_ref) + EPS) * qg_ref[...]).astype(BF16)
    zk = _dot(u, wm_ref[:, 2 * w:3 * w])
    kd_ref[...] = (zk * lax.rsqrt(_group_mean_square(zk, gsum_ref) + EPS) * kg_ref[...]).astype(BF16)
    _with_ones_rows(vmt_ref, _dot_nt(wt_ref[0:w, :], u).astype(BF16))
    omt_ref[...] = _dot_nt(wt_ref[w:2 * w, :], u)
    _with_ones_rows(vdt_ref, _dot_nt(wt_ref[2 * w:3 * w, :], u).astype(BF16))
    gc = _dot_nt(u, wg_ref[...]) + bcol_ref[...]
    gci_ref[...] = gc[:, 0:N_GATE_ROWS]
    gcf_ref[...] = gc[:, N_GATE_ROWS:2 * N_GATE_ROWS]
    gr = _dot_nt(wg_ref[...], u) + brow_ref[...]
    for ci in range(TM_IN // PAIR):
        gri_ref[ci] = gr[0:8, ci * PAIR:(ci + 1) * PAIR]
        grf_ref[ci] = gr[N_GATE_ROWS:N_GATE_ROWS + 8, ci * PAIR:(ci + 1) * PAIR]


def _inproj(h, g, wm, wt, wg, bcol, brow, gsum, qg, kg):
    n_tok = h.shape[0]
    grid = (n_tok // TM_IN,)
    const = lambda shape: pl.BlockSpec(shape, lambda i: (0,) * len(shape))
    tile = lambda width: pl.BlockSpec((TM_IN, width), lambda i: (i, 0))
    tile_t = lambda rows: pl.BlockSpec((rows, TM_IN), lambda i: (0, i))
    gate_rows = pl.BlockSpec((TM_IN // PAIR, 8, PAIR), lambda i: (i, 0, 0))
    out_shape = (
        jax.ShapeDtypeStruct((n_tok, GROUP_W), F32),
        jax.ShapeDtypeStruct((N_HEADS * V_ROWS, n_tok), BF16),
        jax.ShapeDtypeStruct((GROUP_W, n_tok), F32),
        jax.ShapeDtypeStruct((n_tok // PAIR, 8, PAIR), F32),
        jax.ShapeDtypeStruct((n_tok // PAIR, 8, PAIR), F32),
        jax.ShapeDtypeStruct((n_tok, N_GATE_ROWS), F32),
        jax.ShapeDtypeStruct((n_tok, N_GATE_ROWS), F32),
        jax.ShapeDtypeStruct((n_tok, GROUP_W), BF16),
        jax.ShapeDtypeStruct((n_tok, GROUP_W), BF16),
        jax.ShapeDtypeStruct((N_HEADS * V_ROWS, n_tok), BF16),
    )
    out_specs = (
        tile(GROUP_W), tile_t(N_HEADS * V_ROWS), tile_t(GROUP_W), gate_rows, gate_rows,
        tile(N_GATE_ROWS), tile(N_GATE_ROWS), tile(GROUP_W), tile(GROUP_W), tile_t(N_HEADS * V_ROWS),
    )
    in_specs = [
        tile(D_MODEL), const((1, D_MODEL)), const(wm.shape), const(wt.shape), const(wg.shape),
        const(bcol.shape), const(brow.shape), const(gsum.shape),
        const((1, GROUP_W)), const((1, GROUP_W)),
    ]
    return pl.pallas_call(
        _inproj_kernel, grid=grid, in_specs=in_specs, out_specs=out_specs, out_shape=out_shape,
        name="inproj",
        compiler_params=pltpu.CompilerParams(dimension_semantics=("arbitrary",), vmem_limit_bytes=VMEM_LIMIT),
    )(h, g, wm, wt, wg, bcol, brow, gsum, qg, kg)


def _split_hi_lo(x):
    hi = x.astype(BF16)
    return hi, (x - hi.astype(F32)).astype(BF16)


def _mlstm_kernel(qk_ref, vt_ref, ot_ref, gri_ref, grf_ref, gci_ref, gcf_ref, cw_ref, ng_ref, ug_ref, l2_ref,
                  out_ref, xpad_ref, cn_ref):
    seq = qk_ref.shape[0]
    halo = 8
    xpad_ref[0:halo, :] = jnp.zeros((halo, GROUP_W), F32)
    xpad_ref[halo:, :] = qk_ref[...]
    cn_ref[...] = jnp.zeros_like(cn_ref)

    first8 = lax.broadcasted_iota(jnp.int32, (8, PAIR), 1) < CHUNK
    first = lax.broadcasted_iota(jnp.int32, (1, PAIR), 1) < CHUNK
    upper = lax.broadcasted_iota(jnp.int32, (PAIR, PAIR), 1) >= MLSTM_DK
    key_t = lax.broadcasted_iota(jnp.int32, (PAIR, PAIR), 0)
    qry_t = lax.broadcasted_iota(jnp.int32, (PAIR, PAIR), 1)
    chunk_start = jnp.where(qry_t < CHUNK, 0, CHUNK)
    cw = cw_ref[...]
    neg_inf = jnp.float32(-jnp.inf)

    def pair_step(c2, m_prev):
        r0 = pl.multiple_of(c2 * PAIR, PAIR)
        win = xpad_ref[pl.ds(r0, PAIR + halo), :]
        conv = cw[0:1, :] * win[halo - 3:halo - 3 + PAIR, :]
        for j in range(1, MLSTM_CONV_W):
            conv = conv + cw[j:j + 1, :] * win[halo - 3 + j:halo - 3 + j + PAIR, :]
        x = conv * _sigmoid(conv)

        gi = gri_ref[c2]
        lf_hi, lf_lo = _split_hi_lo(_log_sigmoid(grf_ref[c2]))
        bg = _dot(lf_hi, ug_ref[...]) + _dot(lf_lo, ug_ref[...])
        b_r = bg[:, 0:PAIR]
        g_a = bg[:, PAIR:2 * PAIR]
        g_b = bg[:, 2 * PAIR:3 * PAIR]
        a_r = jnp.where(first8, g_a, g_b) - b_r + gi
        max_a = jnp.max(jnp.where(first8, a_r, neg_inf), axis=1, keepdims=True)
        max_b = jnp.max(jnp.where(first8, neg_inf, a_r), axis=1, keepdims=True)
        m_a = jnp.maximum(g_a + m_prev, max_a)
        m_b = jnp.maximum(g_b + m_a, max_b)
        dec_a = jnp.exp(g_a + m_prev - m_a)
        dec_b = jnp.exp(g_b + m_a - m_b)
        w_r = jnp.exp(a_r - jnp.where(first8, m_a, m_b))
        e_r = b_r + jnp.where(first8, m_prev, m_a)
        lc_hi, lc_lo = _split_hi_lo(_log_sigmoid(gcf_ref[pl.ds(r0, PAIR), :]))
        x_c = _dot(l2_ref[...], lc_hi) + _dot(l2_ref[...], lc_lo) - gci_ref[pl.ds(r0, PAIR), :]

        heads = range(N_HEADS)
        xk, xq, vt, cn0, up_a, kq, cq_a, m_out, decay_t = [], [], [], [], [], [], [], [], []
        for hd in heads:
            xh = x[:, hd * 128:(hd + 1) * 128]
            xs = pltpu.roll(xh, MLSTM_DK, 1)
            xk.append(jnp.where(upper, xh * (MLSTM_DK ** -0.5), 0.0).astype(BF16))
            xq.append(jnp.where(upper, xs, 0.0).astype(BF16))
            vt.append(vt_ref[hd * V_ROWS:(hd + 1) * V_ROWS, pl.ds(r0, PAIR)])
            cn0.append(cn_ref[hd])
            w_a = jnp.where(first, w_r[hd:hd + 1, :], 0.0)
            up_a.append(_dot((vt[hd].astype(F32) * w_a).astype(BF16), xk[hd]))
            kq.append(_dot_nt(xk[hd], xq[hd]))
            cq_a.append(_dot_nt(cn0[hd].astype(BF16), xq[hd]))
            dmat = b_r[hd:hd + 1, :] - x_c[:, hd:hd + 1]
            dmat = jnp.where(key_t <= qry_t, jnp.where(key_t >= chunk_start, dmat, neg_inf), neg_inf)
            m_out.append(jnp.maximum(e_r[hd:hd + 1, :], jnp.max(dmat, axis=0, keepdims=True)))
            decay_t.append(jnp.exp(dmat - m_out[hd]))
        cn1, cq_b, up_b = [], [], []
        for hd in heads:
            cn1.append(dec_a[hd:hd + 1, :] * cn0[hd] + up_a[hd])
            w_b = jnp.where(first, 0.0, w_r[hd:hd + 1, :])
            up_b.append(_dot((vt[hd].astype(F32) * w_b).astype(BF16), xk[hd]))
            cq_b.append(_dot_nt(cn1[hd].astype(BF16), xq[hd]))
        sv = []
        for hd in heads:
            cn_ref[hd] = dec_b[hd:hd + 1, :] * cn1[hd] + up_b[hd]
            sv.append(_dot(vt[hd], (kq[hd] * decay_t[hd]).astype(BF16)))
        for hd in heads:
            rows = slice(hd * MLSTM_DV, (hd + 1) * MLSTM_DV)
            inter = jnp.exp(e_r[hd:hd + 1, :] - m_out[hd])
            tot = inter * jnp.where(first, cq_a[hd], cq_b[hd]) + sv[hd]
            den = tot[MLSTM_DV:MLSTM_DV + 1, :]
            hh = tot[0:MLSTM_DV, :] / jnp.maximum(jnp.abs(den), jnp.exp(-m_out[hd]))
            hn = hh * lax.rsqrt(jnp.mean(hh * hh, axis=0, keepdims=True) + EPS) * ng_ref[...]
            og = ot_ref[rows, pl.ds(r0, PAIR)]
            out_ref[rows, pl.ds(r0, PAIR)] = (hn * _sigmoid(og)).astype(BF16)
        return m_b

    lax.fori_loop(0, seq // PAIR, pair_step, jnp.zeros((8, PAIR), F32))


def _mlstm(qk, vmt, omt, gri, grf, gci, gcf, cw, ng, ug, l2, batch, seq):
    const = lambda arr: pl.BlockSpec(arr.shape, lambda b: (0,) * arr.ndim)
    in_specs = [
        pl.BlockSpec((seq, GROUP_W), lambda b: (b, 0)),
        pl.BlockSpec((N_HEADS * V_ROWS, seq), lambda b: (0, b)),
        pl.BlockSpec((GROUP_W, seq), lambda b: (0, b)),
        pl.BlockSpec((seq // PAIR, 8, PAIR), lambda b: (b, 0, 0)),
        pl.BlockSpec((seq // PAIR, 8, PAIR), lambda b: (b, 0, 0)),
        pl.BlockSpec((seq, N_GATE_ROWS), lambda b: (b, 0)),
        pl.BlockSpec((seq, N_GATE_ROWS), lambda b: (b, 0)),
        const(cw), const(ng), const(ug), const(l2),
    ]
    return pl.pallas_call(
        _mlstm_kernel, grid=(batch,), in_specs=in_specs,
        out_specs=pl.BlockSpec((GROUP_W, seq), lambda b: (0, b)),
        out_shape=jax.ShapeDtypeStruct((GROUP_W, batch * seq), BF16),
        scratch_shapes=[
            pltpu.VMEM((seq + 8, GROUP_W), F32),
            pltpu.VMEM((N_HEADS, V_ROWS, 128), F32),
        ],
        name="mlstm",
        compiler_params=pltpu.CompilerParams(dimension_semantics=("arbitrary",), vmem_limit_bytes=VMEM_LIMIT),
    )(qk, vmt, omt, gri, grf, gci, gcf, cw, ng, ug, l2)


def _attn_kernel(lam_init, tbl_ref, kmax_ref, qmin_ref, q_ref, k_ref, vt_ref, pcol_ref, prow_ref,
                 lq1_ref, lk1_ref, lq2_ref, lk2_ref, sg_ref, out_ref, bias_ref, qc_ref, s_ref, p_ref, acc_ref):
    hd = pl.program_id(0)
    qi = pl.program_id(1)
    b = pl.program_id(2)
    n_kv = (qi + 1) * (TQ // TK)

    @pl.when(b == 0)
    def _build_bias():
        rowi = lax.broadcasted_iota(jnp.int32, (BIAS_ROWS, BIAS_COLS), 0)
        coli = lax.broadcasted_iota(jnp.int32, (BIAS_ROWS, BIAS_COLS), 1)
        row = lambda c, v: tbl_ref[c * N_BUCKETS + v:c * N_BUCKETS + v + 1, :]
        last = N_BUCKETS - 1

        def build(t, _):
            r0 = pl.multiple_of(t * BIAS_ROWS, BIAS_ROWS)
            pk = pcol_ref[pl.ds(r0, BIAS_ROWS), :]
            kmax = kmax_ref[t]
            for j in range(TQ // BIAS_COLS):
                lanes = slice(j * BIAS_COLS, (j + 1) * BIAS_COLS)
                q0 = qi * TQ + j * BIAS_COLS
                causal = rowi + r0 <= coli + q0
                flat = jnp.logical_or(qmin_ref[qi * (TQ // BIAS_COLS) + j] - kmax >= _BUCKET_THR[-1],
                                      r0 > q0 + BIAS_COLS - 1)

                @pl.when(flat)
                def _flat():
                    bias_ref[0, pl.ds(r0, BIAS_ROWS), lanes] = jnp.where(causal, row(0, last), NEG_BIG)
                    bias_ref[1, pl.ds(r0, BIAS_ROWS), lanes] = jnp.where(causal, row(1, last), NEG_BIG)

                @pl.when(jnp.logical_not(flat))
                def _lookup():
                    dist = jnp.maximum(prow_ref[:, lanes] - pk, 0)
                    b0 = jnp.broadcast_to(row(0, 0), (BIAS_ROWS, BIAS_COLS))
                    b1 = jnp.broadcast_to(row(1, 0), (BIAS_ROWS, BIAS_COLS))
                    for v, thr in enumerate(_BUCKET_THR, start=1):
                        ge = dist >= thr
                        b0 = jnp.where(ge, row(0, v), b0)
                        b1 = jnp.where(ge, row(1, v), b1)
                    bias_ref[0, pl.ds(r0, BIAS_ROWS), lanes] = jnp.where(causal, b0, NEG_BIG)
                    bias_ref[1, pl.ds(r0, BIAS_ROWS), lanes] = jnp.where(causal, b1, NEG_BIG)
            return 0

        lax.fori_loop(0, n_kv * (TK // BIAS_ROWS), build, 0)

    seq = k_ref.shape[1]
    lane = lax.broadcasted_iota(jnp.int32, (TQ, 128), 1)
    for e in range(ATTN_NB):
        q = q_ref[e]
        zero = jnp.zeros_like(q)
        qc_ref[e, 0] = jnp.where(lane < DIFF_D, q, zero)
        qc_ref[e, 1] = jnp.where(lane >= DIFF_D, q, zero)
    acc_ref[...] = jnp.zeros_like(acc_ref)

    chains = [(e, c) for e in range(ATTN_NB) for c in range(2)]

    def scores(t, slot):
        k0 = pl.multiple_of(t * TK, TK)
        for e in range(ATTN_NB):
            kb = k_ref[e, pl.ds(k0, TK), :]
            for c in range(2):
                s_ref[slot, e, c] = _dot_nt(kb, qc_ref[e, c]) + bias_ref[c, pl.ds(k0, TK), :]

    def numerators(m_state, slot):
        m_next, alpha_next = [], []
        for i, (e, c) in enumerate(chains):
            s = s_ref[slot, e, c]
            m_new = jnp.maximum(m_state[i], jnp.max(s, axis=0, keepdims=True))
            alpha_next.append(jnp.exp2(m_state[i] - m_new))
            p_ref[slot, e, c] = jnp.exp2(s - m_new).astype(BF16)
            m_next.append(m_new)
        return tuple(m_next), tuple(alpha_next)

    def accumulate(t, alpha, slot):
        for e in range(ATTN_NB):
            v0 = pl.multiple_of(e * seq + t * TK, TK)
            vt = vt_ref[:, pl.ds(v0, TK)]
            for c in range(2):
                acc_ref[e, c] = alpha[2 * e + c] * acc_ref[e, c] + _dot(vt, p_ref[slot, e, c])

    def kv_pair(i, carry):
        t = 2 * i
        m_state, alpha = carry
        scores(t, 0)
        m_state, alpha1 = numerators(m_state, 1)
        accumulate(t - 2, alpha, 0)
        scores(t + 1, 1)
        carry = numerators(m_state, 0)
        accumulate(t - 1, alpha1, 1)
        return carry

    assert (TQ // TK) % 2 == 0
    scores(0, 0)
    scores(1, 1)
    carry = numerators(tuple(jnp.full((1, TQ), M_INIT, F32) for _ in chains), 0)
    m_state, alpha = lax.fori_loop(1, n_kv // 2, kv_pair, carry)
    _, alpha1 = numerators(m_state, 1)
    accumulate(n_kv - 2, alpha, 0)
    accumulate(n_kv - 1, alpha1, 1)

    lam = (jnp.exp(jnp.sum(lq1_ref[...] * lk1_ref[...], axis=1, keepdims=True))
           - jnp.exp(jnp.sum(lq2_ref[...] * lk2_ref[...], axis=1, keepdims=True)) + lam_init)
    for e in range(ATTN_NB):
        a0 = acc_ref[e, 0]
        a1 = acc_ref[e, 1]
        o = a0[0:DIFF_DV] / a0[DIFF_DV:DIFF_DV + 1] - lam * (a1[0:DIFF_DV] / a1[DIFF_DV:DIFF_DV + 1])
        on = o * lax.rsqrt(jnp.mean(o * o, axis=0, keepdims=True) + EPS) * sg_ref[...]
        out_ref[e] = (on * (1.0 - lam_init)).T.astype(BF16)


def _attention(lam_init, tbl, qd, kd, vdt, pcol, prow, lq1, lk1, lq2, lk2, sg, batch, seq):
    q3, k3 = (t.reshape(batch, seq, GROUP_W) for t in (qd, kd))
    vec = lambda n: pl.BlockSpec((1, n), lambda h, i, b: (0, 0))
    kmax = jnp.max(prow.reshape(seq // BIAS_ROWS, BIAS_ROWS), axis=1)
    qmin = jnp.min(prow.reshape(seq // BIAS_COLS, BIAS_COLS), axis=1)
    in_specs = [
        pl.BlockSpec((None, 2 * N_BUCKETS, BIAS_COLS), lambda h, i, b: (h, 0, 0)),
        pl.BlockSpec(memory_space=pltpu.SMEM),
        pl.BlockSpec(memory_space=pltpu.SMEM),
        pl.BlockSpec((ATTN_NB, TQ, 128), lambda h, i, b: (b, i, h)),
        pl.BlockSpec((ATTN_NB, seq, 128), lambda h, i, b: (b, 0, h)),
        pl.BlockSpec((V_ROWS, ATTN_NB * seq), lambda h, i, b: (h, b)),
        pl.BlockSpec((seq, BIAS_COLS), lambda h, i, b: (0, 0)),
        pl.BlockSpec((1, TQ), lambda h, i, b: (0, i)),
        vec(DIFF_D), vec(DIFF_D), vec(DIFF_D), vec(DIFF_D),
        pl.BlockSpec((DIFF_DV, TQ), lambda h, i, b: (0, 0)),
    ]
    out = pl.pallas_call(
        functools.partial(_attn_kernel, lam_init),
        grid=(N_HEADS, seq // TQ, batch // ATTN_NB), in_specs=in_specs,
        out_specs=pl.BlockSpec((ATTN_NB, TQ, 128), lambda h, i, b: (b, i, h)),
        out_shape=jax.ShapeDtypeStruct((batch, seq, GROUP_W), BF16),
        scratch_shapes=[
            pltpu.VMEM((2, seq, TQ), F32),
            pltpu.VMEM((ATTN_NB, 2, TQ, 128), BF16),
            pltpu.VMEM((2, ATTN_NB, 2, TK, TQ), F32),
            pltpu.VMEM((2, ATTN_NB, 2, TK, TQ), BF16),
            pltpu.VMEM((ATTN_NB, 2, V_ROWS, TQ), F32),
        ],
        name="diff_attn",
        compiler_params=pltpu.CompilerParams(
            dimension_semantics=("arbitrary", "arbitrary", "arbitrary"), vmem_limit_bytes=VMEM_LIMIT),
    )(tbl, kmax, qmin, q3, k3, vdt, pcol, prow, lq1, lk1, lq2, lk2, jnp.broadcast_to(sg.reshape(DIFF_DV, 1), (DIFF_DV, TQ)))
    return out.reshape(batch * seq, GROUP_W)


def _gelu(x):
    return 0.5 * x * (1.0 + lax.erf(x * (2.0 ** -0.5)))


def _rms(x, g):
    return x * lax.rsqrt(jnp.mean(x * x, axis=-1, keepdims=True) + EPS) * g


def _mixer_kernel(h_ref, hmt_ref, hd_ref, p_ref, wo_ref, gf_ref, wu_ref, cw_ref, cb_ref, wd_ref, gp_ref,
                  wpg_ref, wpp_ref, out_ref, sg_ref, sv_ref, cg_ref, cv_ref, acc_ref, u_ref):
    tm = h_ref.shape[0]
    halo = 8

    @pl.when(pl.program_id(1) == 0)
    def _reset_conv_history():
        cg_ref[...] = jnp.zeros_like(cg_ref)
        cv_ref[...] = jnp.zeros_like(cv_ref)

    h1 = (h_ref[...] + _dot_tn(hmt_ref[...], wo_ref[0:GROUP_W, :])
          + _dot(hd_ref[...], wo_ref[GROUP_W:2 * GROUP_W, :]))
    u_ref[...] = _rms(h1, gf_ref[...]).astype(BF16)
    acc_ref[...] = jnp.zeros_like(acc_ref)

    def up_stage(j, slot):
        c0 = pl.multiple_of(j * FF_CHUNK, FF_CHUNK)
        sg_ref[slot, halo:, :] = _dot(u_ref[...], wu_ref[:, pl.ds(c0, FF_CHUNK)])
        sv_ref[slot, halo:, :] = _dot(u_ref[...], wu_ref[:, pl.ds(D_FF + c0, FF_CHUNK)])

    def conv_branch(j, slot, c0, stage_ref, hist_ref):
        stage_ref[slot, 0:halo, :] = hist_ref[j]
        hist_ref[j] = stage_ref[slot, tm:tm + halo, :]
        cw = cw_ref[:, pl.ds(c0, FF_CHUNK)]
        return (cw[0:1, :] * stage_ref[slot, halo - 2:halo - 2 + tm, :]
                + cw[1:2, :] * stage_ref[slot, halo - 1:halo - 1 + tm, :]
                + cw[2:3, :] * stage_ref[slot, halo:, :] + cb_ref[:, pl.ds(c0, FF_CHUNK)])

    def act_stage(j, slot):
        c0 = pl.multiple_of(j * FF_CHUNK, FF_CHUNK)
        gate = conv_branch(j, slot, c0, sg_ref, cg_ref)
        val = conv_branch(j, slot, D_FF + c0, sv_ref, cv_ref)
        act = (_gelu(gate) * val).astype(BF16)
        acc_ref[...] += _dot(act, wd_ref[j])

    def chunk_pair(i, _):
        j = 2 * i
        up_stage(j + 1, 1)
        act_stage(j, 0)
        up_stage(j + 2, 0)
        act_stage(j + 1, 1)
        return 0

    assert N_FF_CHUNKS % 2 == 1
    up_stage(0, 0)
    lax.fori_loop(0, N_FF_CHUNKS // 2, chunk_pair, 0)
    act_stage(N_FF_CHUNKS - 1, 0)

    h2 = h1 + acc_ref[...]
    u3 = _rms(h2, gp_ref[...]).astype(BF16)
    ple_gate = _sigmoid(_dot(u3, wpg_ref[...]))
    out_ref[...] = h2 + ple_gate * _dot(p_ref[...].astype(BF16), wpp_ref[...])


def _mixer(layer, h, hmt, hd, p, wo, gf, wu, cw, cb, wd, gp, wpg, wpp, batch, seq):
    tiles = seq // TM_FFN
    tile = lambda width: pl.BlockSpec((TM_FFN, width), lambda b, t: (b * tiles + t, 0))

    def of_layer(arr):
        nd = arr.ndim
        return pl.BlockSpec((None,) + arr.shape[1:], lambda b, t: (layer,) + (0,) * (nd - 1),
                            pipeline_mode=pl.Buffered(1))

    weights = (wo, gf, wu, cw, cb, wd, gp, wpg, wpp)
    in_specs = [
        tile(D_MODEL),
        pl.BlockSpec((GROUP_W, TM_FFN), lambda b, t: (0, b * tiles + t)),
        tile(GROUP_W),
        pl.BlockSpec((None, TM_FFN, PLE_DIM), lambda b, t: (layer, b * tiles + t, 0)),
    ] + [of_layer(w) for w in weights]
    return pl.pallas_call(
        _mixer_kernel, grid=(batch, tiles), in_specs=in_specs, out_specs=tile(D_MODEL),
        out_shape=jax.ShapeDtypeStruct(h.shape, F32),
        scratch_shapes=[
            pltpu.VMEM((2, TM_FFN + 8, FF_CHUNK), F32),
            pltpu.VMEM((2, TM_FFN + 8, FF_CHUNK), F32),
            pltpu.VMEM((N_FF_CHUNKS, 8, FF_CHUNK), F32),
            pltpu.VMEM((N_FF_CHUNKS, 8, FF_CHUNK), F32),
            pltpu.VMEM((TM_FFN, D_MODEL), F32),
            pltpu.VMEM((TM_FFN, D_MODEL), BF16),
        ],
        name="mixer",
        compiler_params=pltpu.CompilerParams(
            dimension_semantics=("arbitrary", "arbitrary"), vmem_limit_bytes=VMEM_LIMIT),
    )(h, hmt, hd, p, *weights)


def _head_interleave(qcols, kcols):
    lead = qcols.shape[:-1]
    qh = qcols.reshape(lead + (N_HEADS, MLSTM_DK))
    kh = kcols.reshape(lead + (N_HEADS, MLSTM_DK))
    return jnp.concatenate([qh, kh], axis=-1).reshape(lead + (GROUP_W,))


def kernel(x, p, positions, rel_bias, ln_mix_g, w_in, mlstm_conv_w, b_igate, b_fgate, mlstm_norm_g, q_norm_g, k_norm_g, lam_q1, lam_k1, lam_q2, lam_k2, diff_subln_g, w_out, ln_ffn_g, w_up, ffn_conv_w, ffn_conv_b, w_down, ln_ple_g, w_ple_gate, w_ple_proj):
    batch, seq, _ = x.shape
    depth = w_in.shape[0]
    n_tok = batch * seq
    qk_cols = N_HEADS * MLSTM_DK
    col_sizes = [qk_cols, qk_cols, GROUP_W, GROUP_W, N_HEADS, N_HEADS, GROUP_W, GROUP_W, GROUP_W]
    offs = np.concatenate([[0], np.cumsum(col_sizes)])
    sl = lambda a, j: a[..., int(offs[j]):int(offs[j + 1])]

    tbl = jnp.broadcast_to(
        (jnp.transpose(rel_bias.astype(F32), (1, 2, 0)) * LOG2E).reshape(N_HEADS, 2 * N_BUCKETS, 1),
        (N_HEADS, 2 * N_BUCKETS, BIAS_COLS))
    pcol = jnp.broadcast_to(positions.astype(jnp.int32).reshape(seq, 1), (seq, BIAS_COLS))
    prow = positions.astype(jnp.int32).reshape(1, seq)
    gsum = jnp.asarray(np.kron(np.eye(MXU_TILE // DIFF_D), np.ones((DIFF_D, DIFF_D))), BF16)
    t_idx = np.arange(PAIR)
    same_chunk = (t_idx[:, None] // CHUNK) == (t_idx[None, :] // CHUNK)
    prefix = same_chunk & (t_idx[:, None] <= t_idx[None, :])
    total_a = np.broadcast_to(t_idx[:, None] < CHUNK, (PAIR, PAIR))
    ug = jnp.asarray(np.concatenate([prefix, total_a, ~total_a], axis=1), BF16)
    l2 = jnp.asarray(prefix.T, BF16)

    p_tok = p.reshape(depth, n_tok, PLE_DIM)
    row = lambda a: a.astype(F32).reshape(depth, 1, a.shape[-1])
    mixer_weights = (
        w_out.astype(BF16), row(ln_ffn_g), w_up.astype(BF16), ffn_conv_w.astype(F32), row(ffn_conv_b),
        w_down.astype(BF16).reshape(depth, N_FF_CHUNKS, FF_CHUNK, D_MODEL), row(ln_ple_g),
        w_ple_gate.astype(BF16), w_ple_proj.astype(BF16))

    h = x.reshape(n_tok, D_MODEL)
    for i in range(depth):
        wi = w_in[i]
        wm = jnp.concatenate([_head_interleave(sl(wi, 0), sl(wi, 1)), sl(wi, 6), sl(wi, 7)], axis=-1).astype(BF16)
        wt = jnp.concatenate([sl(wi, 2), sl(wi, 3), sl(wi, 8)], axis=-1).T.astype(BF16)
        pad_rows = lambda a: jnp.zeros((N_GATE_ROWS,) + a.shape[1:], F32).at[:N_HEADS].set(a.astype(F32))
        wg = jnp.concatenate([pad_rows(sl(wi, 4).T), pad_rows(sl(wi, 5).T)], axis=0).astype(BF16)
        gate_bias = jnp.concatenate([pad_rows(b_igate[i]), pad_rows(b_fgate[i])])
        qg = jnp.tile(q_norm_g[i].astype(F32), GROUP_W // DIFF_D).reshape(1, GROUP_W) * (DIFF_D ** -0.5 * LOG2E)
        kg = jnp.tile(k_norm_g[i].astype(F32), GROUP_W // DIFF_D).reshape(1, GROUP_W)
        qk, vmt, omt, gri, grf, gci, gcf, qd, kd, vdt = _inproj(
            h, ln_mix_g[i].reshape(1, D_MODEL), wm, wt, wg, gate_bias.reshape(1, -1), gate_bias.reshape(-1, 1),
            gsum, qg, kg)

        cw = _head_interleave(mlstm_conv_w[i][:, :qk_cols], mlstm_conv_w[i][:, qk_cols:]).astype(F32)
        ng = jnp.broadcast_to(mlstm_norm_g[i].astype(F32).reshape(MLSTM_DV, 1), (MLSTM_DV, PAIR))
        hmt = _mlstm(qk, vmt, omt, gri, grf, gci, gcf, cw, ng, ug, l2, batch, seq)

        lam_init = 0.8 - 0.6 * math.exp(-0.3 * i)
        row64 = lambda a: a[i].reshape(1, DIFF_D).astype(F32)
        hd = _attention(lam_init, tbl, qd, kd, vdt, pcol, prow, row64(lam_q1), row64(lam_k1), row64(lam_q2),
                        row64(lam_k2), diff_subln_g[i].reshape(1, DIFF_DV).astype(F32), batch, seq)

        h = _mixer(i, h, hmt, hd, p_tok, *mixer_weights, batch, seq)
    return h.reshape(batch, seq, D_MODEL)
```

```python
import functools
import math

import numpy as np
import jax
import jax.numpy as jnp
from jax import lax
from jax.experimental import pallas as pl
from jax.experimental.pallas import tpu as pltpu

F32 = jnp.float32
BF16 = jnp.bfloat16

D_MODEL = 1024
N_HEADS = 4
MLSTM_DK = 64
MLSTM_DV = 128
MLSTM_CONV_W = 4
CHUNK = 64
PAIR = 2 * CHUNK
DIFF_D = 64
DIFF_DV = 128
N_BUCKETS = 32
MAX_DISTANCE = 128
D_FF = 2816
FFN_CONV_W = 3
PLE_DIM = 256
EPS = 1e-6
GROUP_W = N_HEADS * 128
N_GATE_ROWS = 16

MXU_TILE = 256
TM_IN = 512
TM_FFN = 512
FF_CHUNK = 256
N_FF_CHUNKS = D_FF // FF_CHUNK
TQ = 512
TK = 256
BIAS_ROWS = 32
BIAS_COLS = 128
ATTN_NB = 2
NEG_BIG = -1e30
M_INIT = -1e20
ONES_ROWS = 16
V_ROWS = DIFF_DV + ONES_ROWS
LOG2E = math.log2(math.e)
VMEM_LIMIT = 56 * 1024 * 1024


def _dot(a, b):
    return jnp.dot(a, b, preferred_element_type=F32)


def _dot_nt(a, b):
    return lax.dot_general(a, b, (((1,), (1,)), ((), ())), preferred_element_type=F32)


def _dot_tn(a, b):
    return lax.dot_general(a, b, (((0,), (0,)), ((), ())), preferred_element_type=F32)


def _sigmoid(x):
    return 1.0 / (1.0 + jnp.exp(-x))


def _log_sigmoid(x):
    return jnp.minimum(x, 0.0) - jnp.log1p(jnp.exp(-jnp.abs(x)))


def _bucket_thresholds():
    max_exact = N_BUCKETS // 2
    thr = []
    for v in range(1, N_BUCKETS):
        if v <= max_exact:
            thr.append(v)
            continue
        edge = max_exact * (MAX_DISTANCE / max_exact) ** ((v - max_exact) / (N_BUCKETS - max_exact))
        assert abs(edge - round(edge)) > 1e-3, edge
        thr.append(int(math.ceil(edge)))
    assert all(a < b for a, b in zip(thr, thr[1:])), thr
    return tuple(thr)


_BUCKET_THR = _bucket_thresholds()


def _group_mean_square(z, gsum_ref):
    sq = (z * z).astype(BF16)
    width = gsum_ref.shape[0]
    sums = [_dot(sq[:, c0:c0 + width], gsum_ref[...]) for c0 in range(0, z.shape[1], width)]
    return jnp.concatenate(sums, axis=1) * (1.0 / DIFF_D)


def _with_ones_rows(dst_ref, vt):
    for hh in range(N_HEADS):
        dst_ref[hh * V_ROWS:hh * V_ROWS + DIFF_DV, :] = vt[hh * DIFF_DV:(hh + 1) * DIFF_DV, :]
        dst_ref[hh * V_ROWS + DIFF_DV:(hh + 1) * V_ROWS, :] = jnp.ones((ONES_ROWS, vt.shape[1]), BF16)


def _inproj_kernel(h_ref, g_ref, wm_ref, wt_ref, wg_ref, bcol_ref, brow_ref, gsum_ref, qg_ref, kg_ref,
                   qk_ref, vmt_ref, omt_ref, gri_ref, grf_ref, gci_ref, gcf_ref, qd_ref, kd_ref, vdt_ref):
    x = h_ref[...]
    ms = jnp.mean(x * x, axis=-1, keepdims=True)
    u = (x * lax.rsqrt(ms + EPS) * g_ref[...]).astype(BF16)
    w = GROUP_W
    qk_ref[...] = _dot(u, wm_ref[:, 0:w])
    zq = _dot(u, wm_ref[:, w:2 * w])
    qd_ref[...] = (zq * lax.rsqrt(_group_mean_square(zq, gsum_ref) + EPS) * qg_ref[...]).astype(BF16)
    zk = _dot(u, wm_ref[:, 2 * w:3 * w])
    kd_ref[...] = (zk * lax.rsqrt(_group_mean_square(zk, gsum_ref) + EPS) * kg_ref[...]).astype(BF16)
    _with_ones_rows(vmt_ref, _dot_nt(wt_ref[0:w, :], u).astype(BF16))
    omt_ref[...] = _dot_nt(wt_ref[w:2 * w, :], u)
    _with_ones_rows(vdt_ref, _dot_nt(wt_ref[2 * w:3 * w, :], u).astype(BF16))
    gc = _dot_nt(u, wg_ref[...]) + bcol_ref[...]
    gci_ref[...] = gc[:, 0:N_GATE_ROWS]
    gcf_ref[...] = gc[:, N_GATE_ROWS:2 * N_GATE_ROWS]
    gr = _dot_nt(wg_ref[...], u) + brow_ref[...]
    for ci in range(TM_IN // PAIR):
        gri_ref[ci] = gr[0:8, ci * PAIR:(ci + 1) * PAIR]
        grf_ref[ci] = gr[N_GATE_ROWS:N_GATE_ROWS + 8, ci * PAIR:(ci + 1) * PAIR]


def _inproj(h, g, wm, wt, wg, bcol, brow, gsum, qg, kg):
    n_tok = h.shape[0]
    grid = (n_tok // TM_IN,)
    const = lambda shape: pl.BlockSpec(shape, lambda i: (0,) * len(shape))
    tile = lambda width: pl.BlockSpec((TM_IN, width), lambda i: (i, 0))
    tile_t = lambda rows: pl.BlockSpec((rows, TM_IN), lambda i: (0, i))
    gate_rows = pl.BlockSpec((TM_IN // PAIR, 8, PAIR), lambda i: (i, 0, 0))
    out_shape = (
        jax.ShapeDtypeStruct((n_tok, GROUP_W), F32),
        jax.ShapeDtypeStruct((N_HEADS * V_ROWS, n_tok), BF16),
        jax.ShapeDtypeStruct((GROUP_W, n_tok), F32),
        jax.ShapeDtypeStruct((n_tok // PAIR, 8, PAIR), F32),
        jax.ShapeDtypeStruct((n_tok // PAIR, 8, PAIR), F32),
        jax.ShapeDtypeStruct((n_tok, N_GATE_ROWS), F32),
        jax.ShapeDtypeStruct((n_tok, N_GATE_ROWS), F32),
        jax.ShapeDtypeStruct((n_tok, GROUP_W), BF16),
        jax.ShapeDtypeStruct((n_tok, GROUP_W), BF16),
        jax.ShapeDtypeStruct((N_HEADS * V_ROWS, n_tok), BF16),
    )
    out_specs = (
        tile(GROUP_W), tile_t(N_HEADS * V_ROWS), tile_t(GROUP_W), gate_rows, gate_rows,
        tile(N_GATE_ROWS), tile(N_GATE_ROWS), tile(GROUP_W), tile(GROUP_W), tile_t(N_HEADS * V_ROWS),
    )
    in_specs = [
        tile(D_MODEL), const((1, D_MODEL)), const(wm.shape), const(wt.shape), const(wg.shape),
        const(bcol.shape), const(brow.shape), const(gsum.shape),
        const((1, GROUP_W)), const((1, GROUP_W)),
    ]
    return pl.pallas_call(
        _inproj_kernel, grid=grid, in_specs=in_specs, out_specs=out_specs, out_shape=out_shape,
        name="inproj",
        compiler_params=pltpu.CompilerParams(dimension_semantics=("arbitrary",), vmem_limit_bytes=VMEM_LIMIT),
    )(h, g, wm, wt, wg, bcol, brow, gsum, qg, kg)


def _split_hi_lo(x):
    hi = x.astype(BF16)
    return hi, (x - hi.astype(F32)).astype(BF16)


def _mlstm_kernel(qk_ref, vt_ref, ot_ref, gri_ref, grf_ref, gci_ref, gcf_ref, cw_ref, ng_ref, ug_ref, l2_ref,
                  out_ref, xpad_ref, cn_ref):
    seq = qk_ref.shape[0]
    halo = 8
    xpad_ref[0:halo, :] = jnp.zeros((halo, GROUP_W), F32)
    xpad_ref[halo:, :] = qk_ref[...]
    cn_ref[...] = jnp.zeros_like(cn_ref)

    first8 = lax.broadcasted_iota(jnp.int32, (8, PAIR), 1) < CHUNK
    first = lax.broadcasted_iota(jnp.int32, (1, PAIR), 1) < CHUNK
    upper = lax.broadcasted_iota(jnp.int32, (PAIR, PAIR), 1) >= MLSTM_DK
    key_t = lax.broadcasted_iota(jnp.int32, (PAIR, PAIR), 0)
    qry_t = lax.broadcasted_iota(jnp.int32, (PAIR, PAIR), 1)
    chunk_start = jnp.where(qry_t < CHUNK, 0, CHUNK)
    cw = cw_ref[...]
    neg_inf = jnp.float32(-jnp.inf)

    def pair_step(c2, m_prev):
        r0 = pl.multiple_of(c2 * PAIR, PAIR)
        win = xpad_ref[pl.ds(r0, PAIR + halo), :]
        conv = cw[0:1, :] * win[halo - 3:halo - 3 + PAIR, :]
        for j in range(1, MLSTM_CONV_W):
            conv = conv + cw[j:j + 1, :] * win[halo - 3 + j:halo - 3 + j + PAIR, :]
        x = conv * _sigmoid(conv)

        gi = gri_ref[c2]
        lf_hi, lf_lo = _split_hi_lo(_log_sigmoid(grf_ref[c2]))
        bg = _dot(lf_hi, ug_ref[...]) + _dot(lf_lo, ug_ref[...])
        b_r = bg[:, 0:PAIR]
        g_a = bg[:, PAIR:2 * PAIR]
        g_b = bg[:, 2 * PAIR:3 * PAIR]
        a_r = jnp.where(first8, g_a, g_b) - b_r + gi
        max_a = jnp.max(jnp.where(first8, a_r, neg_inf), axis=1, keepdims=True)
        max_b = jnp.max(jnp.where(first8, neg_inf, a_r), axis=1, keepdims=True)
        m_a = jnp.maximum(g_a + m_prev, max_a)
        m_b = jnp.maximum(g_b + m_a, max_b)
        dec_a = jnp.exp(g_a + m_prev - m_a)
        dec_b = jnp.exp(g_b + m_a - m_b)
        w_r = jnp.exp(a_r - jnp.where(first8, m_a, m_b))
        e_r = b_r + jnp.where(first8, m_prev, m_a)
        lc_hi, lc_lo = _split_hi_lo(_log_sigmoid(gcf_ref[pl.ds(r0, PAIR), :]))
        x_c = _dot(l2_ref[...], lc_hi) + _dot(l2_ref[...], lc_lo) - gci_ref[pl.ds(r0, PAIR), :]

        heads = range(N_HEADS)
        xk, xq, vt, cn0, up_a, kq, cq_a, m_out, decay_t = [], [], [], [], [], [], [], [], []
        for hd in heads:
            xh = x[:, hd * 128:(hd + 1) * 128]
            xs = pltpu.roll(xh, MLSTM_DK, 1)
            xk.append(jnp.where(upper, xh * (MLSTM_DK ** -0.5), 0.0).astype(BF16))
            xq.append(jnp.where(upper, xs, 0.0).astype(BF16))
            vt.append(vt_ref[hd * V_ROWS:(hd + 1) * V_ROWS, pl.ds(r0, PAIR)])
            cn0.append(cn_ref[hd])
            w_a = jnp.where(first, w_r[hd:hd + 1, :], 0.0)
            up_a.append(_dot((vt[hd].astype(F32) * w_a).astype(BF16), xk[hd]))
            kq.append(_dot_nt(xk[hd], xq[hd]))
            cq_a.append(_dot_nt(cn0[hd].astype(BF16), xq[hd]))
            dmat = b_r[hd:hd + 1, :] - x_c[:, hd:hd + 1]
            dmat = jnp.where(key_t <= qry_t, jnp.where(key_t >= chunk_start, dmat, neg_inf), neg_inf)
            m_out.append(jnp.maximum(e_r[hd:hd + 1, :], jnp.max(dmat, axis=0, keepdims=True)))
            decay_t.append(jnp.exp(dmat - m_out[hd]))
        cn1, cq_b, up_b = [], [], []
        for hd in heads:
            cn1.append(dec_a[hd:hd + 1, :] * cn0[hd] + up_a[hd])
            w_b = jnp.where(first, 0.0, w_r[hd:hd + 1, :])
            up_b.append(_dot((vt[hd].astype(F32) * w_b).astype(BF16), xk[hd]))
            cq_b.append(_dot_nt(cn1[hd].astype(BF16), xq[hd]))
        sv = []
        for hd in heads:
            cn_ref[hd] = dec_b[hd:hd + 1, :] * cn1[hd] + up_b[hd]
            sv.append(_dot(vt[hd], (kq[hd] * decay_t[hd]).astype(BF16)))
        for hd in heads:
            rows = slice(hd * MLSTM_DV, (hd + 1) * MLSTM_DV)
            inter = jnp.exp(e_r[hd:hd + 1, :] - m_out[hd])
            tot = inter * jnp.where(first, cq_a[hd], cq_b[hd]) + sv[hd]
            den = tot[MLSTM_DV:MLSTM_DV + 1, :]
            hh = tot[0:MLSTM_DV, :] / jnp.maximum(jnp.abs(den), jnp.exp(-m_out[hd]))
            hn = hh * lax.rsqrt(jnp.mean(hh * hh, axis=0, keepdims=True) + EPS) * ng_ref[...]
            og = ot_ref[rows, pl.ds(r0, PAIR)]
            out_ref[rows, pl.ds(r0, PAIR)] = (hn * _sigmoid(og)).astype(BF16)
        return m_b

    lax.fori_loop(0, seq // PAIR, pair_step, jnp.zeros((8, PAIR), F32))


def _mlstm(qk, vmt, omt, gri, grf, gci, gcf, cw, ng, ug, l2, batch, seq):
    const = lambda arr: pl.BlockSpec(arr.shape, lambda b: (0,) * arr.ndim)
    in_specs = [
        pl.BlockSpec((seq, GROUP_W), lambda b: (b, 0)),
        pl.BlockSpec((N_HEADS * V_ROWS, seq), lambda b: (0, b)),
        pl.BlockSpec((GROUP_W, seq), lambda b: (0, b)),
        pl.BlockSpec((seq // PAIR, 8, PAIR), lambda b: (b, 0, 0)),
        pl.BlockSpec((seq // PAIR, 8, PAIR), lambda b: (b, 0, 0)),
        pl.BlockSpec((seq, N_GATE_ROWS), lambda b: (b, 0)),
        pl.BlockSpec((seq, N_GATE_ROWS), lambda b: (b, 0)),
        const(cw), const(ng), const(ug), const(l2),
    ]
    return pl.pallas_call(
        _mlstm_kernel, grid=(batch,), in_specs=in_specs,
        out_specs=pl.BlockSpec((GROUP_W, seq), lambda b: (0, b)),
        out_shape=jax.ShapeDtypeStruct((GROUP_W, batch * seq), BF16),
        scratch_shapes=[
            pltpu.VMEM((seq + 8, GROUP_W), F32),
            pltpu.VMEM((N_HEADS, V_ROWS, 128), F32),
        ],
        name="mlstm",
        compiler_params=pltpu.CompilerParams(dimension_semantics=("arbitrary",), vmem_limit_bytes=VMEM_LIMIT),
    )(qk, vmt, omt, gri, grf, gci, gcf, cw, ng, ug, l2)


def _attn_kernel(lam_init, tbl_ref, kmax_ref, qmin_ref, q_ref, k_ref, vt_ref, pcol_ref, prow_ref,
                 lq1_ref, lk1_ref, lq2_ref, lk2_ref, sg_ref, out_ref, bias_ref, qc_ref, s_ref, p_ref, acc_ref):
    hd = pl.program_id(0)
    qi = pl.program_id(1)
    b = pl.program_id(2)
    n_kv = (qi + 1) * (TQ // TK)

    @pl.when(b == 0)
    def _build_bias():
        rowi = lax.broadcasted_iota(jnp.int32, (BIAS_ROWS, BIAS_COLS), 0)
        coli = lax.broadcasted_iota(jnp.int32, (BIAS_ROWS, BIAS_COLS), 1)
        row = lambda c, v: tbl_ref[c * N_BUCKETS + v:c * N_BUCKETS + v + 1, :]
        last = N_BUCKETS - 1

        def build(t, _):
            r0 = pl.multiple_of(t * BIAS_ROWS, BIAS_ROWS)
            pk = pcol_ref[pl.ds(r0, BIAS_ROWS), :]
            kmax = kmax_ref[t]
            for j in range(TQ // BIAS_COLS):
                lanes = slice(j * BIAS_COLS, (j + 1) * BIAS_COLS)
                q0 = qi * TQ + j * BIAS_COLS
                causal = rowi + r0 <= coli + q0
                flat = jnp.logical_or(qmin_ref[qi * (TQ // BIAS_COLS) + j] - kmax >= _BUCKET_THR[-1],
                                      r0 > q0 + BIAS_COLS - 1)

                @pl.when(flat)
                def _flat():
                    bias_ref[0, pl.ds(r0, BIAS_ROWS), lanes] = jnp.where(causal, row(0, last), NEG_BIG)
                    bias_ref[1, pl.ds(r0, BIAS_ROWS), lanes] = jnp.where(causal, row(1, last), NEG_BIG)

                @pl.when(jnp.logical_not(flat))
                def _lookup():
                    dist = jnp.maximum(prow_ref[:, lanes] - pk, 0)
                    b0 = jnp.broadcast_to(row(0, 0), (BIAS_ROWS, BIAS_COLS))
                    b1 = jnp.broadcast_to(row(1, 0), (BIAS_ROWS, BIAS_COLS))
                    for v, thr in enumerate(_BUCKET_THR, start=1):
                        ge = dist >= thr
                        b0 = jnp.where(ge, row(0, v), b0)
                        b1 = jnp.where(ge, row(1, v), b1)
                    bias_ref[0, pl.ds(r0, BIAS_ROWS), lanes] = jnp.where(causal, b0, NEG_BIG)
                    bias_ref[1, pl.ds(r0, BIAS_ROWS), lanes] = jnp.where(causal, b1, NEG_BIG)
            return 0

        lax.fori_loop(0, n_kv * (TK // BIAS_ROWS), build, 0)

    seq = k_ref.shape[1]
    lane = lax.broadcasted_iota(jnp.int32, (TQ, 128), 1)
    for e in range(ATTN_NB):
        q = q_ref[e]
        zero = jnp.zeros_like(q)
        qc_ref[e, 0] = jnp.where(lane < DIFF_D, q, zero)
        qc_ref[e, 1] = jnp.where(lane >= DIFF_D, q, zero)
    acc_ref[...] = jnp.zeros_like(acc_ref)

    chains = [(e, c) for e in range(ATTN_NB) for c in range(2)]

    def scores(t, slot):
        k0 = pl.multiple_of(t * TK, TK)
        for e in range(ATTN_NB):
            kb = k_ref[e, pl.ds(k0, TK), :]
            for c in range(2):
                s_ref[slot, e, c] = _dot_nt(kb, qc_ref[e, c]) + bias_ref[c, pl.ds(k0, TK), :]

    def numerators(m_state, slot):
        m_next, alpha_next = [], []
        for i, (e, c) in enumerate(chains):
            s = s_ref[slot, e, c]
            m_new = jnp.maximum(m_state[i], jnp.max(s, axis=0, keepdims=True))
            alpha_next.append(jnp.exp2(m_state[i] - m_new))
            p_ref[slot, e, c] = jnp.exp2(s - m_new).astype(BF16)
            m_next.append(m_new)
        return tuple(m_next), tuple(alpha_next)

    def accumulate(t, alpha, slot):
        for e in range(ATTN_NB):
            v0 = pl.multiple_of(e * seq + t * TK, TK)
            vt = vt_ref[:, pl.ds(v0, TK)]
            for c in range(2):
                acc_ref[e, c] = alpha[2 * e + c] * acc_ref[e, c] + _dot(vt, p_ref[slot, e, c])

    def kv_pair(i, carry):
        t = 2 * i
        m_state, alpha = carry
        scores(t, 0)
        m_state, alpha1 = numerators(m_state, 1)
        accumulate(t - 2, alpha, 0)
        scores(t + 1, 1)
        carry = numerators(m_state, 0)
        accumulate(t - 1, alpha1, 1)
        return carry

    assert (TQ // TK) % 2 == 0
    scores(0, 0)
    scores(1, 1)
    carry = numerators(tuple(jnp.full((1, TQ), M_INIT, F32) for _ in chains), 0)
    m_state, alpha = lax.fori_loop(1, n_kv // 2, kv_pair, carry)
    _, alpha1 = numerators(m_state, 1)
    accumulate(n_kv - 2, alpha, 0)
    accumulate(n_kv - 1, alpha1, 1)

    lam = (jnp.exp(jnp.sum(lq1_ref[...] * lk1_ref[...], axis=1, keepdims=True))
           - jnp.exp(jnp.sum(lq2_ref[...] * lk2_ref[...], axis=1, keepdims=True)) + lam_init)
    for e in range(ATTN_NB):
        a0 = acc_ref[e, 0]
        a1 = acc_ref[e, 1]
        o = a0[0:DIFF_DV] / a0[DIFF_DV:DIFF_DV + 1] - lam * (a1[0:DIFF_DV] / a1[DIFF_DV:DIFF_DV + 1])
        on = o * lax.rsqrt(jnp.mean(o * o, axis=0, keepdims=True) + EPS) * sg_ref[...]
        out_ref[e] = (on * (1.0 - lam_init)).T.astype(BF16)


def _attention(lam_init, tbl, qd, kd, vdt, pcol, prow, lq1, lk1, lq2, lk2, sg, batch, seq):
    q3, k3 = (t.reshape(batch, seq, GROUP_W) for t in (qd, kd))
    vec = lambda n: pl.BlockSpec((1, n), lambda h, i, b: (0, 0))
    kmax = jnp.max(prow.reshape(seq // BIAS_ROWS, BIAS_ROWS), axis=1)
    qmin = jnp.min(prow.reshape(seq // BIAS_COLS, BIAS_COLS), axis=1)
    in_specs = [
        pl.BlockSpec((None, 2 * N_BUCKETS, BIAS_COLS), lambda h, i, b: (h, 0, 0)),
        pl.BlockSpec(memory_space=pltpu.SMEM),
        pl.BlockSpec(memory_space=pltpu.SMEM),
        pl.BlockSpec((ATTN_NB, TQ, 128), lambda h, i, b: (b, i, h)),
        pl.BlockSpec((ATTN_NB, seq, 128), lambda h, i, b: (b, 0, h)),
        pl.BlockSpec((V_ROWS, ATTN_NB * seq), lambda h, i, b: (h, b)),
        pl.BlockSpec((seq, BIAS_COLS), lambda h, i, b: (0, 0)),
        pl.BlockSpec((1, TQ), lambda h, i, b: (0, i)),
        vec(DIFF_D), vec(DIFF_D), vec(DIFF_D), vec(DIFF_D),
        pl.BlockSpec((DIFF_DV, TQ), lambda h, i, b: (0, 0)),
    ]
    out = pl.pallas_call(
        functools.partial(_attn_kernel, lam_init),
        grid=(N_HEADS, seq // TQ, batch // ATTN_NB), in_specs=in_specs,
        out_specs=pl.BlockSpec((ATTN_NB, TQ, 128), lambda h, i, b: (b, i, h)),
        out_shape=jax.ShapeDtypeStruct((batch, seq, GROUP_W), BF16),
        scratch_shapes=[
            pltpu.VMEM((2, seq, TQ), F32),
            pltpu.VMEM((ATTN_NB, 2, TQ, 128), BF16),
            pltpu.VMEM((2, ATTN_NB, 2, TK, TQ), F32),
            pltpu.VMEM((2, ATTN_NB, 2, TK, TQ), BF16),
            pltpu.VMEM((ATTN_NB, 2, V_ROWS, TQ), F32),
        ],
        name="diff_attn",
        compiler_params=pltpu.CompilerParams(
            dimension_semantics=("arbitrary", "arbitrary", "arbitrary"), vmem_limit_bytes=VMEM_LIMIT),
    )(tbl, kmax, qmin, q3, k3, vdt, pcol, prow, lq1, lk1, lq2, lk2, jnp.broadcast_to(sg.reshape(DIFF_DV, 1), (DIFF_DV, TQ)))
    return out.reshape(batch * seq, GROUP_W)


def _gelu(x):
    return 0.5 * x * (1.0 + lax.erf(x * (2.0 ** -0.5)))


def _rms(x, g):
    return x * lax.rsqrt(jnp.mean(x * x, axis=-1, keepdims=True) + EPS) * g


def _mixer_kernel(h_ref, hmt_ref, hd_ref, p_ref, wo_ref, gf_ref, wu_ref, cw_ref, cb_ref, wd_ref, gp_ref,
                  wpg_ref, wpp_ref, out_ref, sg_ref, sv_ref, cg_ref, cv_ref, acc_ref, u_ref):
    tm = h_ref.shape[0]
    halo = 8

    @pl.when(pl.program_id(1) == 0)
    def _reset_conv_history():
        cg_ref[...] = jnp.zeros_like(cg_ref)
        cv_ref[...] = jnp.zeros_like(cv_ref)

    h1 = (h_ref[...] + _dot_tn(hmt_ref[...], wo_ref[0:GROUP_W, :])
          + _dot(hd_ref[...], wo_ref[GROUP_W:2 * GROUP_W, :]))
    u_ref[...] = _rms(h1, gf_ref[...]).astype(BF16)
    acc_ref[...] = jnp.zeros_like(acc_ref)

    def up_stage(j, slot):
        c0 = pl.multiple_of(j * FF_CHUNK, FF_CHUNK)
        sg_ref[slot, halo:, :] = _dot(u_ref[...], wu_ref[:, pl.ds(c0, FF_CHUNK)])
        sv_ref[slot, halo:, :] = _dot(u_ref[...], wu_ref[:, pl.ds(D_FF + c0, FF_CHUNK)])

    def conv_branch(j, slot, c0, stage_ref, hist_ref):
        stage_ref[slot, 0:halo, :] = hist_ref[j]
        hist_ref[j] = stage_ref[slot, tm:tm + halo, :]
        cw = cw_ref[:, pl.ds(c0, FF_CHUNK)]
        return (cw[0:1, :] * stage_ref[slot, halo - 2:halo - 2 + tm, :]
                + cw[1:2, :] * stage_ref[slot, halo - 1:halo - 1 + tm, :]
                + cw[2:3, :] * stage_ref[slot, halo:, :] + cb_ref[:, pl.ds(c0, FF_CHUNK)])

    def act_stage(j, slot):
        c0 = pl.multiple_of(j * FF_CHUNK, FF_CHUNK)
        gate = conv_branch(j, slot, c0, sg_ref, cg_ref)
        val = conv_branch(j, slot, D_FF + c0, sv_ref, cv_ref)
        act = (_gelu(gate) * val).astype(BF16)
        acc_ref[...] += _dot(act, wd_ref[j])

    def chunk_pair(i, _):
        j = 2 * i
        up_stage(j + 1, 1)
        act_stage(j, 0)
        up_stage(j + 2, 0)
        act_stage(j + 1, 1)
        return 0

    assert N_FF_CHUNKS % 2 == 1
    up_stage(0, 0)
    lax.fori_loop(0, N_FF_CHUNKS // 2, chunk_pair, 0)
    act_stage(N_FF_CHUNKS - 1, 0)

    h2 = h1 + acc_ref[...]
    u3 = _rms(h2, gp_ref[...]).astype(BF16)
    ple_gate = _sigmoid(_dot(u3, wpg_ref[...]))
    out_ref[...] = h2 + ple_gate * _dot(p_ref[...].astype(BF16), wpp_ref[...])


def _mixer(layer, h, hmt, hd, p, wo, gf, wu, cw, cb, wd, gp, wpg, wpp, batch, seq):
    tiles = seq // TM_FFN
    tile = lambda width: pl.BlockSpec((TM_FFN, width), lambda b, t: (b * tiles + t, 0))

    def of_layer(arr):
        nd = arr.ndim
        return pl.BlockSpec((None,) + arr.shape[1:], lambda b, t: (layer,) + (0,) * (nd - 1),
                            pipeline_mode=pl.Buffered(1))

    weights = (wo, gf, wu, cw, cb, wd, gp, wpg, wpp)
    in_specs = [
        tile(D_MODEL),
        pl.BlockSpec((GROUP_W, TM_FFN), lambda b, t: (0, b * tiles + t)),
        tile(GROUP_W),
        pl.BlockSpec((None, TM_FFN, PLE_DIM), lambda b, t: (layer, b * tiles + t, 0)),
    ] + [of_layer(w) for w in weights]
    return pl.pallas_call(
        _mixer_kernel, grid=(batch, tiles), in_specs=in_specs, out_specs=tile(D_MODEL),
        out_shape=jax.ShapeDtypeStruct(h.shape, F32),
        scratch_shapes=[
            pltpu.VMEM((2, TM_FFN + 8, FF_CHUNK), F32),
            pltpu.VMEM((2, TM_FFN + 8, FF_CHUNK), F32),
            pltpu.VMEM((N_FF_CHUNKS, 8, FF_CHUNK), F32),
            pltpu.VMEM((N_FF_CHUNKS, 8, FF_CHUNK), F32),
            pltpu.VMEM((TM_FFN, D_MODEL), F32),
            pltpu.VMEM((TM_FFN, D_MODEL), BF16),
        ],
        name="mixer",
        compiler_params=pltpu.CompilerParams(
            dimension_semantics=("arbitrary", "arbitrary"), vmem_limit_bytes=VMEM_LIMIT),
    )(h, hmt, hd, p, *weights)


def _head_interleave(qcols, kcols):
    lead = qcols.shape[:-1]
    qh = qcols.reshape(lead + (N_HEADS, MLSTM_DK))
    kh = kcols.reshape(lead + (N_HEADS, MLSTM_DK))
    return jnp.concatenate([qh, kh], axis=-1).reshape(lead + (GROUP_W,))


def kernel(x, p, positions, rel_bias, ln_mix_g, w_in, mlstm_conv_w, b_igate, b_fgate, mlstm_norm_g, q_norm_g, k_norm_g, lam_q1, lam_k1, lam_q2, lam_k2, diff_subln_g, w_out, ln_ffn_g, w_up, ffn_conv_w, ffn_conv_b, w_down, ln_ple_g, w_ple_gate, w_ple_proj):
    batch, seq, _ = x.shape
    depth = w_in.shape[0]
    n_tok = batch * seq
    qk_cols = N_HEADS * MLSTM_DK
    col_sizes = [qk_cols, qk_cols, GROUP_W, GROUP_W, N_HEADS, N_HEADS, GROUP_W, GROUP_W, GROUP_W]
    offs = np.concatenate([[0], np.cumsum(col_sizes)])
    sl = lambda a, j: a[..., int(offs[j]):int(offs[j + 1])]

    tbl = jnp.broadcast_to(
        (jnp.transpose(rel_bias.astype(F32), (1, 2, 0)) * LOG2E).reshape(N_HEADS, 2 * N_BUCKETS, 1),
        (N_HEADS, 2 * N_BUCKETS, BIAS_COLS))
    pcol = jnp.broadcast_to(positions.astype(jnp.int32).reshape(seq, 1), (seq, BIAS_COLS))
    prow = positions.astype(jnp.int32).reshape(1, seq)
    gsum = jnp.asarray(np.kron(np.eye(MXU_TILE // DIFF_D), np.ones((DIFF_D, DIFF_D))), BF16)
    t_idx = np.arange(PAIR)
    same_chunk = (t_idx[:, None] // CHUNK) == (t_idx[None, :] // CHUNK)
    prefix = same_chunk & (t_idx[:, None] <= t_idx[None, :])
    total_a = np.broadcast_to(t_idx[:, None] < CHUNK, (PAIR, PAIR))
    ug = jnp.asarray(np.concatenate([prefix, total_a, ~total_a], axis=1), BF16)
    l2 = jnp.asarray(prefix.T, BF16)

    p_tok = p.reshape(depth, n_tok, PLE_DIM)
    row = lambda a: a.astype(F32).reshape(depth, 1, a.shape[-1])
    mixer_weights = (
        w_out.astype(BF16), row(ln_ffn_g), w_up.astype(BF16), ffn_conv_w.astype(F32), row(ffn_conv_b),
        w_down.astype(BF16).reshape(depth, N_FF_CHUNKS, FF_CHUNK, D_MODEL), row(ln_ple_g),
        w_ple_gate.astype(BF16), w_ple_proj.astype(BF16))

    h = x.reshape(n_tok, D_MODEL)
    for i in range(depth):
        wi = w_in[i]
        wm = jnp.concatenate([_head_interleave(sl(wi, 0), sl(wi, 1)), sl(wi, 6), sl(wi, 7)], axis=-1).astype(BF16)
        wt = jnp.concatenate([sl(wi, 2), sl(wi, 3), sl(wi, 8)], axis=-1).T.astype(BF16)
        pad_rows = lambda a: jnp.zeros((N_GATE_ROWS,) + a.shape[1:], F32).at[:N_HEADS].set(a.astype(F32))
        wg = jnp.concatenate([pad_rows(sl(wi, 4).T), pad_rows(sl(wi, 5).T)], axis=0).astype(BF16)
        gate_bias = jnp.concatenate([pad_rows(b_igate[i]), pad_rows(b_fgate[i])])
        qg = jnp.tile(q_norm_g[i].astype(F32), GROUP_W // DIFF_D).reshape(1, GROUP_W) * (DIFF_D ** -0.5 * LOG2E)
        kg = jnp.tile(k_norm_g[i].astype(F32), GROUP_W // DIFF_D).reshape(1, GROUP_W)
        qk, vmt, omt, gri, grf, gci, gcf, qd, kd, vdt = _inproj(
            h, ln_mix_g[i].reshape(1, D_MODEL), wm, wt, wg, gate_bias.reshape(1, -1), gate_bias.reshape(-1, 1),
            gsum, qg, kg)

        cw = _head_interleave(mlstm_conv_w[i][:, :qk_cols], mlstm_conv_w[i][:, qk_cols:]).astype(F32)
        ng = jnp.broadcast_to(mlstm_norm_g[i].astype(F32).reshape(MLSTM_DV, 1), (MLSTM_DV, PAIR))
        hmt = _mlstm(qk, vmt, omt, gri, grf, gci, gcf, cw, ng, ug, l2, batch, seq)

        lam_init = 0.8 - 0.6 * math.exp(-0.3 * i)
        row64 = lambda a: a[i].reshape(1, DIFF_D).astype(F32)
        hd = _attention(lam_init, tbl, qd, kd, vdt, pcol, prow, row64(lam_q1), row64(lam_k1), row64(lam_q2),
                        row64(lam_k2), diff_subln_g[i].reshape(1, DIFF_DV).astype(F32), batch, seq)

        h = _mixer(i, h, hmt, hd, p_tok, *mixer_weights, batch, seq)
    return h.reshape(batch, seq, D_MODEL)
```

```python
import functools
import math

import numpy as np
import jax
import jax.numpy as jnp
from jax import lax
from jax.experimental import pallas as pl
from jax.experimental.pallas import tpu as pltpu

F32 = jnp.float32
BF16 = jnp.bfloat16

D_MODEL = 1024
N_HEADS = 4
MLSTM_DK = 64
MLSTM_DV = 128
MLSTM_CONV_W = 4
CHUNK = 64
PAIR = 2 * CHUNK
DIFF_D = 64
DIFF_DV = 128
N_BUCKETS = 32
MAX_DISTANCE = 128
D_FF = 2816
FFN_CONV_W = 3
PLE_DIM = 256
EPS = 1e-6
GROUP_W = N_HEADS * 128
N_GATE_ROWS = 16

MXU_TILE = 256
TM_IN = 512
TM_FFN = 512
FF_CHUNK = 256
N_FF_CHUNKS = D_FF // FF_CHUNK
TQ = 512
TK = 256
BIAS_ROWS = 32
BIAS_COLS = 128
ATTN_NB = 2
NEG_BIG = -1e30
M_INIT = -1e29
ONES_ROWS = 16
V_ROWS = DIFF_DV + ONES_ROWS
LOG2E = math.log2(math.e)
VMEM_LIMIT = 56 * 1024 * 1024


def _dot(a, b):
    return jnp.dot(a, b, preferred_element_type=F32)


def _dot_nt(a, b):
    return lax.dot_general(a, b, (((1,), (1,)), ((), ())), preferred_element_type=F32)


def _dot_tn(a, b):
    return lax.dot_general(a, b, (((0,), (0,)), ((), ())), preferred_element_type=F32)


def _sigmoid(x):
    return 1.0 / (1.0 + jnp.exp(-x))


def _log_sigmoid(x):
    return jnp.minimum(x, 0.0) - jnp.log1p(jnp.exp(-jnp.abs(x)))


def _bucket_thresholds():
    max_exact = N_BUCKETS // 2
    thr = []
    for v in range(1, N_BUCKETS):
        if v <= max_exact:
            thr.append(v)
            continue
        edge = max_exact * (MAX_DISTANCE / max_exact) ** ((v - max_exact) / (N_BUCKETS - max_exact))
        assert abs(edge - round(edge)) > 1e-3, edge
        thr.append(int(math.ceil(edge)))
    assert all(a < b for a, b in zip(thr, thr[1:])), thr
    return tuple(thr)


_BUCKET_THR = _bucket_thresholds()


def _group_mean_square(z, gsum_ref):
    sq = (z * z).astype(BF16)
    width = gsum_ref.shape[0]
    sums = [_dot(sq[:, c0:c0 + width], gsum_ref[...]) for c0 in range(0, z.shape[1], width)]
    return jnp.concatenate(sums, axis=1) * (1.0 / DIFF_D)


def _with_ones_rows(dst_ref, vt):
    for hh in range(N_HEADS):
        dst_ref[hh * V_ROWS:hh * V_ROWS + DIFF_DV, :] = vt[hh * DIFF_DV:(hh + 1) * DIFF_DV, :]
        dst_ref[hh * V_ROWS + DIFF_DV:(hh + 1) * V_ROWS, :] = jnp.ones((ONES_ROWS, vt.shape[1]), BF16)


def _inproj_kernel(h_ref, g_ref, wm_ref, wt_ref, wg_ref, bcol_ref, brow_ref, gsum_ref, qg_ref, kg_ref,
                   qk_ref, vmt_ref, omt_ref, gri_ref, grf_ref, gci_ref, gcf_ref, qd_ref, kd_ref, vdt_ref):
    x = h_ref[...]
    ms = jnp.mean(x * x, axis=-1, keepdims=True)
    u = (x * lax.rsqrt(ms + EPS) * g_ref[...]).astype(BF16)
    w = GROUP_W
    qk_ref[...] = _dot(u, wm_ref[:, 0:w])
    zq = _dot(u, wm_ref[:, w:2 * w])
    qd_ref[...] = (zq * lax.rsqrt(_group_mean_square(zq, gsum_ref) + EPS) * qg_ref[...]).astype(BF16)
    zk = _dot(u, wm_ref[:, 2 * w:3 * w])
    kd_ref[...] = (zk * lax.rsqrt(_group_mean_square(zk, gsum_ref) + EPS) * kg_ref[...]).astype(BF16)
    _with_ones_rows(vmt_ref, _dot_nt(wt_ref[0:w, :], u).astype(BF16))
    omt_ref[...] = _dot_nt(wt_ref[w:2 * w, :], u)
    _with_ones_rows(vdt_ref, _dot_nt(wt_ref[2 * w:3 * w, :], u).astype(BF16))
    gc = _dot_nt(u, wg_ref[...]) + bcol_ref[...]
    gci_ref[...] = gc[:, 0:N_GATE_ROWS]
    gcf_ref[...] = gc[:, N_GATE_ROWS:2 * N_GATE_ROWS]
    gr = _dot_nt(wg_ref[...], u) + brow_ref[...]
    for ci in range(TM_IN // PAIR):
        gri_ref[ci] = gr[0:8, ci * PAIR:(ci + 1) * PAIR]
        grf_ref[ci] = gr[N_GATE_ROWS:N_GATE_ROWS + 8, ci * PAIR:(ci + 1) * PAIR]


def _inproj(h, g, wm, wt, wg, bcol, brow, gsum, qg, kg):
    n_tok = h.shape[0]
    grid = (n_tok // TM_IN,)
    const = lambda shape: pl.BlockSpec(shape, lambda i: (0,) * len(shape))
    tile = lambda width: pl.BlockSpec((TM_IN, width), lambda i: (i, 0))
    tile_t = lambda rows: pl.BlockSpec((rows, TM_IN), lambda i: (0, i))
    gate_rows = pl.BlockSpec((TM_IN // PAIR, 8, PAIR), lambda i: (i, 0, 0))
    out_shape = (
        jax.ShapeDtypeStruct((n_tok, GROUP_W), F32),
        jax.ShapeDtypeStruct((N_HEADS * V_ROWS, n_tok), BF16),
        jax.ShapeDtypeStruct((GROUP_W, n_tok), F32),
        jax.ShapeDtypeStruct((n_tok // PAIR, 8, PAIR), F32),
        jax.ShapeDtypeStruct((n_tok // PAIR, 8, PAIR), F32),
        jax.ShapeDtypeStruct((n_tok, N_GATE_ROWS), F32),
        jax.ShapeDtypeStruct((n_tok, N_GATE_ROWS), F32),
        jax.ShapeDtypeStruct((n_tok, GROUP_W), BF16),
        jax.ShapeDtypeStruct((n_tok, GROUP_W), BF16),
        jax.ShapeDtypeStruct((N_HEADS * V_ROWS, n_tok), BF16),
    )
    out_specs = (
        tile(GROUP_W), tile_t(N_HEADS * V_ROWS), tile_t(GROUP_W), gate_rows, gate_rows,
        tile(N_GATE_ROWS), tile(N_GATE_ROWS), tile(GROUP_W), tile(GROUP_W), tile_t(N_HEADS * V_ROWS),
    )
    in_specs = [
        tile(D_MODEL), const((1, D_MODEL)), const(wm.shape), const(wt.shape), const(wg.shape),
        const(bcol.shape), const(brow.shape), const(gsum.shape),
        const((1, GROUP_W)), const((1, GROUP_W)),
    ]
    return pl.pallas_call(
        _inproj_kernel, grid=grid, in_specs=in_specs, out_specs=out_specs, out_shape=out_shape,
        name="inproj",
        compiler_params=pltpu.CompilerParams(dimension_semantics=("arbitrary",), vmem_limit_bytes=VMEM_LIMIT),
    )(h, g, wm, wt, wg, bcol, brow, gsum, qg, kg)


def _split_hi_lo(x):
    hi = x.astype(BF16)
    return hi, (x - hi.astype(F32)).astype(BF16)


def _mlstm_kernel(qk_ref, vt_ref, ot_ref, gri_ref, grf_ref, gci_ref, gcf_ref, cw_ref, ng_ref, ug_ref, l2_ref,
                  out_ref, xpad_ref, cn_ref):
    seq = qk_ref.shape[0]
    halo = 8
    xpad_ref[0:halo, :] = jnp.zeros((halo, GROUP_W), F32)
    xpad_ref[halo:, :] = qk_ref[...]
    cn_ref[...] = jnp.zeros_like(cn_ref)

    first8 = lax.broadcasted_iota(jnp.int32, (8, PAIR), 1) < CHUNK
    first = lax.broadcasted_iota(jnp.int32, (1, PAIR), 1) < CHUNK
    upper = lax.broadcasted_iota(jnp.int32, (PAIR, PAIR), 1) >= MLSTM_DK
    key_t = lax.broadcasted_iota(jnp.int32, (PAIR, PAIR), 0)
    qry_t = lax.broadcasted_iota(jnp.int32, (PAIR, PAIR), 1)
    chunk_start = jnp.where(qry_t < CHUNK, 0, CHUNK)
    cw = cw_ref[...]
    neg_inf = jnp.float32(-jnp.inf)

    def pair_step(c2, m_prev):
        r0 = pl.multiple_of(c2 * PAIR, PAIR)
        win = xpad_ref[pl.ds(r0, PAIR + halo), :]
        conv = cw[0:1, :] * win[halo - 3:halo - 3 + PAIR, :]
        for j in range(1, MLSTM_CONV_W):
            conv = conv + cw[j:j + 1, :] * win[halo - 3 + j:halo - 3 + j + PAIR, :]
        x = conv * _sigmoid(conv)

        gi = gri_ref[c2]
        lf_hi, lf_lo = _split_hi_lo(_log_sigmoid(grf_ref[c2]))
        bg = _dot(lf_hi, ug_ref[...]) + _dot(lf_lo, ug_ref[...])
        b_r = bg[:, 0:PAIR]
        g_a = bg[:, PAIR:2 * PAIR]
        g_b = bg[:, 2 * PAIR:3 * PAIR]
        a_r = jnp.where(first8, g_a, g_b) - b_r + gi
        max_a = jnp.max(jnp.where(first8, a_r, neg_inf), axis=1, keepdims=True)
        max_b = jnp.max(jnp.where(first8, neg_inf, a_r), axis=1, keepdims=True)
        m_a = jnp.maximum(g_a + m_prev, max_a)
        m_b = jnp.maximum(g_b + m_a, max_b)
        dec_a = jnp.exp(g_a + m_prev - m_a)
        dec_b = jnp.exp(g_b + m_a - m_b)
        w_r = jnp.exp(a_r - jnp.where(first8, m_a, m_b))
        e_r = b_r + jnp.where(first8, m_prev, m_a)
        lc_hi, lc_lo = _split_hi_lo(_log_sigmoid(gcf_ref[pl.ds(r0, PAIR), :]))
        x_c = _dot(l2_ref[...], lc_hi) + _dot(l2_ref[...], lc_lo) - gci_ref[pl.ds(r0, PAIR), :]

        heads = range(N_HEADS)
        xk, xq, vt, cn0, up_a, kq, cq_a, m_out, decay_t = [], [], [], [], [], [], [], [], []
        for hd in heads:
            xh = x[:, hd * 128:(hd + 1) * 128]
            xs = pltpu.roll(xh, MLSTM_DK, 1)
            xk.append(jnp.where(upper, xh * (MLSTM_DK ** -0.5), 0.0).astype(BF16))
            xq.append(jnp.where(upper, xs, 0.0).astype(BF16))
            vt.append(vt_ref[hd * V_ROWS:(hd + 1) * V_ROWS, pl.ds(r0, PAIR)])
            cn0.append(cn_ref[hd])
            w_a = jnp.where(first, w_r[hd:hd + 1, :], 0.0)
            up_a.append(_dot((vt[hd].astype(F32) * w_a).astype(BF16), xk[hd]))
            kq.append(_dot_nt(xk[hd], xq[hd]))
            cq_a.append(_dot_nt(cn0[hd].astype(BF16), xq[hd]))
            dmat = b_r[hd:hd + 1, :] - x_c[:, hd:hd + 1]
            dmat = jnp.where(key_t <= qry_t, jnp.where(key_t >= chunk_start, dmat, neg_inf), neg_inf)
            m_out.append(jnp.maximum(e_r[hd:hd + 1, :], jnp.max(dmat, axis=0, keepdims=True)))
            decay_t.append(jnp.exp(dmat - m_out[hd]))
        cn1, cq_b, up_b = [], [], []
        for hd in heads:
            cn1.append(dec_a[hd:hd + 1, :] * cn0[hd] + up_a[hd])
            w_b = jnp.where(first, 0.0, w_r[hd:hd + 1, :])
            up_b.append(_dot((vt[hd].astype(F32) * w_b).astype(BF16), xk[hd]))
            cq_b.append(_dot_nt(cn1[hd].astype(BF16), xq[hd]))
        sv = []
        for hd in heads:
            cn_ref[hd] = dec_b[hd:hd + 1, :] * cn1[hd] + up_b[hd]
            sv.append(_dot(vt[hd], (kq[hd] * decay_t[hd]).astype(BF16)))
        for hd in heads:
            rows = slice(hd * MLSTM_DV, (hd + 1) * MLSTM_DV)
            inter = jnp.exp(e_r[hd:hd + 1, :] - m_out[hd])
            tot = inter * jnp.where(first, cq_a[hd], cq_b[hd]) + sv[hd]
            den = tot[MLSTM_DV:MLSTM_DV + 1, :]
            hh = tot[0:MLSTM_DV, :] / jnp.maximum(jnp.abs(den), jnp.exp(-m_out[hd]))
            hn = hh * lax.rsqrt(jnp.mean(hh * hh, axis=0, keepdims=True) + EPS) * ng_ref[...]
            og = ot_ref[rows, pl.ds(r0, PAIR)]
            out_ref[rows, pl.ds(r0, PAIR)] = (hn * _sigmoid(og)).astype(BF16)
        return m_b

    lax.fori_loop(0, seq // PAIR, pair_step, jnp.zeros((8, PAIR), F32))


def _mlstm(qk, vmt, omt, gri, grf, gci, gcf, cw, ng, ug, l2, batch, seq):
    const = lambda arr: pl.BlockSpec(arr.shape, lambda b: (0,) * arr.ndim)
    in_specs = [
        pl.BlockSpec((seq, GROUP_W), lambda b: (b, 0)),
        pl.BlockSpec((N_HEADS * V_ROWS, seq), lambda b: (0, b)),
        pl.BlockSpec((GROUP_W, seq), lambda b: (0, b)),
        pl.BlockSpec((seq // PAIR, 8, PAIR), lambda b: (b, 0, 0)),
        pl.BlockSpec((seq // PAIR, 8, PAIR), lambda b: (b, 0, 0)),
        pl.BlockSpec((seq, N_GATE_ROWS), lambda b: (b, 0)),
        pl.BlockSpec((seq, N_GATE_ROWS), lambda b: (b, 0)),
        const(cw), const(ng), const(ug), const(l2),
    ]
    return pl.pallas_call(
        _mlstm_kernel, grid=(batch,), in_specs=in_specs,
        out_specs=pl.BlockSpec((GROUP_W, seq), lambda b: (0, b)),
        out_shape=jax.ShapeDtypeStruct((GROUP_W, batch * seq), BF16),
        scratch_shapes=[
            pltpu.VMEM((seq + 8, GROUP_W), F32),
            pltpu.VMEM((N_HEADS, V_ROWS, 128), F32),
        ],
        name="mlstm",
        compiler_params=pltpu.CompilerParams(dimension_semantics=("arbitrary",), vmem_limit_bytes=VMEM_LIMIT),
    )(qk, vmt, omt, gri, grf, gci, gcf, cw, ng, ug, l2)


def _attn_kernel(lam_init, tbl_ref, kmax_ref, qmin_ref, q_ref, k_ref, vt_ref, pcol_ref, prow_ref,
                 lq1_ref, lk1_ref, lq2_ref, lk2_ref, sg_ref, out_ref, bias_ref, qc_ref, s_ref, p_ref, acc_ref):
    hd = pl.program_id(0)
    qi = pl.program_id(1)
    b = pl.program_id(2)
    n_kv = (qi + 1) * (TQ // TK)

    @pl.when(b == 0)
    def _build_bias():
        rowi = lax.broadcasted_iota(jnp.int32, (BIAS_ROWS, BIAS_COLS), 0)
        coli = lax.broadcasted_iota(jnp.int32, (BIAS_ROWS, BIAS_COLS), 1)
        row = lambda c, v: tbl_ref[c * N_BUCKETS + v:c * N_BUCKETS + v + 1, :]
        last = N_BUCKETS - 1

        def build(t, _):
            r0 = pl.multiple_of(t * BIAS_ROWS, BIAS_ROWS)
            pk = pcol_ref[pl.ds(r0, BIAS_ROWS), :]
            kmax = kmax_ref[t]
            for j in range(TQ // BIAS_COLS):
                lanes = slice(j * BIAS_COLS, (j + 1) * BIAS_COLS)
                q0 = qi * TQ + j * BIAS_COLS
                causal = rowi + r0 <= coli + q0
                flat = jnp.logical_or(qmin_ref[qi * (TQ // BIAS_COLS) + j] - kmax >= _BUCKET_THR[-1],
                                      r0 > q0 + BIAS_COLS - 1)

                @pl.when(flat)
                def _flat():
                    bias_ref[0, pl.ds(r0, BIAS_ROWS), lanes] = jnp.where(causal, row(0, last), NEG_BIG)
                    bias_ref[1, pl.ds(r0, BIAS_ROWS), lanes] = jnp.where(causal, row(1, last), NEG_BIG)

                @pl.when(jnp.logical_not(flat))
                def _lookup():
                    dist = jnp.maximum(prow_ref[:, lanes] - pk, 0)
                    b0 = jnp.broadcast_to(row(0, 0), (BIAS_ROWS, BIAS_COLS))
                    b1 = jnp.broadcast_to(row(1, 0), (BIAS_ROWS, BIAS_COLS))
                    for v, thr in enumerate(_BUCKET_THR, start=1):
                        ge = dist >= thr
                        b0 = jnp.where(ge, row(0, v), b0)
                        b1 = jnp.where(ge, row(1, v), b1)
                    bias_ref[0, pl.ds(r0, BIAS_ROWS), lanes] = jnp.where(causal, b0, NEG_BIG)
                    bias_ref[1, pl.ds(r0, BIAS_ROWS), lanes] = jnp.where(causal, b1, NEG_BIG)
            return 0

        lax.fori_loop(0, n_kv * (TK // BIAS_ROWS), build, 0)

    seq = k_ref.shape[1]
    lane = lax.broadcasted_iota(jnp.int32, (TQ, 128), 1)
    for e in range(ATTN_NB):
        q = q_ref[e]
        zero = jnp.zeros_like(q)
        qc_ref[e, 0] = jnp.where(lane < DIFF_D, q, zero)
        qc_ref[e, 1] = jnp.where(lane >= DIFF_D, q, zero)
    acc_ref[...] = jnp.zeros_like(acc_ref)

    chains = [(e, c) for e in range(ATTN_NB) for c in range(2)]

    def scores(t, slot):
        k0 = pl.multiple_of(t * TK, TK)
        for e in range(ATTN_NB):
            kb = k_ref[e, pl.ds(k0, TK), :]
            for c in range(2):
                s_ref[slot, e, c] = _dot_nt(kb, qc_ref[e, c]) + bias_ref[c, pl.ds(k0, TK), :]

    def numerators(m_state, slot, lanes=slice(None)):
        m_next, alpha_next = [], []
        for i, (e, c) in enumerate(chains):
            s = s_ref[slot, e, c, :, lanes]
            m_prev = m_state[i][:, lanes]
            m_new = jnp.maximum(m_prev, jnp.max(s, axis=0, keepdims=True))
            alpha_next.append(jnp.exp2(m_prev - m_new))
            p_ref[slot, e, c, :, lanes] = jnp.exp2(s - m_new).astype(BF16)
            m_next.append(m_new)
        return tuple(m_next), tuple(alpha_next)

    def accumulate(t, alpha, slot, lanes=slice(None)):
        for e in range(ATTN_NB):
            v0 = pl.multiple_of(e * seq + t * TK, TK)
            vt = vt_ref[:, pl.ds(v0, TK)]
            for c in range(2):
                acc_ref[e, c, :, lanes] = (alpha[2 * e + c] * acc_ref[e, c, :, lanes]
                                           + _dot(vt, p_ref[slot, e, c, :, lanes]))

    def kv_pair(i, carry):
        t = 2 * i
        m_state, alpha = carry
        scores(t, 0)
        m_state, alpha1 = numerators(m_state, 1)
        accumulate(t - 2, alpha, 0)
        scores(t + 1, 1)
        carry = numerators(m_state, 0)
        accumulate(t - 1, alpha1, 1)
        return carry

    assert (TQ // TK) % 2 == 0
    scores(0, 0)
    scores(1, 1)
    carry = numerators(tuple(jnp.full((1, TQ), M_INIT, F32) for _ in chains), 0)
    m_state, alpha = lax.fori_loop(1, n_kv // 2, kv_pair, carry)
    live = slice(TQ - TK, TQ)
    _, alpha1 = numerators(m_state, 1, live)
    accumulate(n_kv - 2, alpha, 0)
    accumulate(n_kv - 1, alpha1, 1, live)

    lam = (jnp.exp(jnp.sum(lq1_ref[...] * lk1_ref[...], axis=1, keepdims=True))
           - jnp.exp(jnp.sum(lq2_ref[...] * lk2_ref[...], axis=1, keepdims=True)) + lam_init)
    for e in range(ATTN_NB):
        a0 = acc_ref[e, 0]
        a1 = acc_ref[e, 1]
        o = a0[0:DIFF_DV] / a0[DIFF_DV:DIFF_DV + 1] - lam * (a1[0:DIFF_DV] / a1[DIFF_DV:DIFF_DV + 1])
        on = o * lax.rsqrt(jnp.mean(o * o, axis=0, keepdims=True) + EPS) * sg_ref[...]
        out_ref[e] = (on * (1.0 - lam_init)).T.astype(BF16)


def _attention(lam_init, tbl, qd, kd, vdt, pcol, prow, lq1, lk1, lq2, lk2, sg, batch, seq):
    q3, k3 = (t.reshape(batch, seq, GROUP_W) for t in (qd, kd))
    vec = lambda n: pl.BlockSpec((1, n), lambda h, i, b: (0, 0))
    kmax = jnp.max(prow.reshape(seq // BIAS_ROWS, BIAS_ROWS), axis=1)
    qmin = jnp.min(prow.reshape(seq // BIAS_COLS, BIAS_COLS), axis=1)
    in_specs = [
        pl.BlockSpec((None, 2 * N_BUCKETS, BIAS_COLS), lambda h, i, b: (h, 0, 0)),
        pl.BlockSpec(memory_space=pltpu.SMEM),
        pl.BlockSpec(memory_space=pltpu.SMEM),
        pl.BlockSpec((ATTN_NB, TQ, 128), lambda h, i, b: (b, i, h)),
        pl.BlockSpec((ATTN_NB, seq, 128), lambda h, i, b: (b, 0, h)),
        pl.BlockSpec((V_ROWS, ATTN_NB * seq), lambda h, i, b: (h, b)),
        pl.BlockSpec((seq, BIAS_COLS), lambda h, i, b: (0, 0)),
        pl.BlockSpec((1, TQ), lambda h, i, b: (0, i)),
        vec(DIFF_D), vec(DIFF_D), vec(DIFF_D), vec(DIFF_D),
        pl.BlockSpec((DIFF_DV, TQ), lambda h, i, b: (0, 0)),
    ]
    out = pl.pallas_call(
        functools.partial(_attn_kernel, lam_init),
        grid=(N_HEADS, seq // TQ, batch // ATTN_NB), in_specs=in_specs,
        out_specs=pl.BlockSpec((ATTN_NB, TQ, 128), lambda h, i, b: (b, i, h)),
        out_shape=jax.ShapeDtypeStruct((batch, seq, GROUP_W), BF16),
        scratch_shapes=[
            pltpu.VMEM((2, seq, TQ), F32),
            pltpu.VMEM((ATTN_NB, 2, TQ, 128), BF16),
            pltpu.VMEM((2, ATTN_NB, 2, TK, TQ), F32),
            pltpu.VMEM((2, ATTN_NB, 2, TK, TQ), BF16),
            pltpu.VMEM((ATTN_NB, 2, V_ROWS, TQ), F32),
        ],
        name="diff_attn",
        compiler_params=pltpu.CompilerParams(
            dimension_semantics=("arbitrary", "arbitrary", "arbitrary"), vmem_limit_bytes=VMEM_LIMIT),
    )(tbl, kmax, qmin, q3, k3, vdt, pcol, prow, lq1, lk1, lq2, lk2, jnp.broadcast_to(sg.reshape(DIFF_DV, 1), (DIFF_DV, TQ)))
    return out.reshape(batch * seq, GROUP_W)


def _gelu(x):
    return 0.5 * x * (1.0 + lax.erf(x * (2.0 ** -0.5)))


def _rms(x, g):
    return x * lax.rsqrt(jnp.mean(x * x, axis=-1, keepdims=True) + EPS) * g


def _mixer_kernel(h_ref, hmt_ref, hd_ref, p_ref, wo_ref, gf_ref, wu_ref, cw_ref, cb_ref, wd_ref, gp_ref,
                  wpg_ref, wpp_ref, out_ref, sg_ref, sv_ref, cg_ref, cv_ref, acc_ref, u_ref):
    tm = h_ref.shape[0]
    halo = 8

    @pl.when(pl.program_id(1) == 0)
    def _reset_conv_history():
        cg_ref[...] = jnp.zeros_like(cg_ref)
        cv_ref[...] = jnp.zeros_like(cv_ref)

    h1 = (h_ref[...] + _dot_tn(hmt_ref[...], wo_ref[0:GROUP_W, :])
          + _dot(hd_ref[...], wo_ref[GROUP_W:2 * GROUP_W, :]))
    u_ref[...] = _rms(h1, gf_ref[...]).astype(BF16)
    acc_ref[...] = jnp.zeros_like(acc_ref)

    def up_stage(j, slot):
        c0 = pl.multiple_of(j * FF_CHUNK, FF_CHUNK)
        sg_ref[slot, halo:, :] = _dot(u_ref[...], wu_ref[:, pl.ds(c0, FF_CHUNK)])
        sv_ref[slot, halo:, :] = _dot(u_ref[...], wu_ref[:, pl.ds(D_FF + c0, FF_CHUNK)])

    def conv_branch(j, slot, c0, stage_ref, hist_ref):
        stage_ref[slot, 0:halo, :] = hist_ref[j]
        hist_ref[j] = stage_ref[slot, tm:tm + halo, :]
        cw = cw_ref[:, pl.ds(c0, FF_CHUNK)]
        return (cw[0:1, :] * stage_ref[slot, halo - 2:halo - 2 + tm, :]
                + cw[1:2, :] * stage_ref[slot, halo - 1:halo - 1 + tm, :]
                + cw[2:3, :] * stage_ref[slot, halo:, :] + cb_ref[:, pl.ds(c0, FF_CHUNK)])

    def act_stage(j, slot):
        c0 = pl.multiple_of(j * FF_CHUNK, FF_CHUNK)
        gate = conv_branch(j, slot, c0, sg_ref, cg_ref)
        val = conv_branch(j, slot, D_FF + c0, sv_ref, cv_ref)
        act = (_gelu(gate) * val).astype(BF16)
        acc_ref[...] += _dot(act, wd_ref[j])

    def chunk_pair(i, _):
        j = 2 * i
        up_stage(j + 1, 1)
        act_stage(j, 0)
        up_stage(j + 2, 0)
        act_stage(j + 1, 1)
        return 0

    assert N_FF_CHUNKS % 2 == 1
    up_stage(0, 0)
    lax.fori_loop(0, N_FF_CHUNKS // 2, chunk_pair, 0)
    act_stage(N_FF_CHUNKS - 1, 0)

    h2 = h1 + acc_ref[...]
    u3 = _rms(h2, gp_ref[...]).astype(BF16)
    ple_gate = _sigmoid(_dot(u3, wpg_ref[...]))
    out_ref[...] = h2 + ple_gate * _dot(p_ref[...].astype(BF16), wpp_ref[...])


def _mixer(layer, h, hmt, hd, p, wo, gf, wu, cw, cb, wd, gp, wpg, wpp, batch, seq):
    tiles = seq // TM_FFN
    tile = lambda width: pl.BlockSpec((TM_FFN, width), lambda b, t: (b * tiles + t, 0))

    def of_layer(arr):
        nd = arr.ndim
        return pl.BlockSpec((None,) + arr.shape[1:], lambda b, t: (layer,) + (0,) * (nd - 1),
                            pipeline_mode=pl.Buffered(1))

    weights = (wo, gf, wu, cw, cb, wd, gp, wpg, wpp)
    in_specs = [
        tile(D_MODEL),
        pl.BlockSpec((GROUP_W, TM_FFN), lambda b, t: (0, b * tiles + t)),
        tile(GROUP_W),
        pl.BlockSpec((None, TM_FFN, PLE_DIM), lambda b, t: (layer, b * tiles + t, 0)),
    ] + [of_layer(w) for w in weights]
    return pl.pallas_call(
        _mixer_kernel, grid=(batch, tiles), in_specs=in_specs, out_specs=tile(D_MODEL),
        out_shape=jax.ShapeDtypeStruct(h.shape, F32),
        scratch_shapes=[
            pltpu.VMEM((2, TM_FFN + 8, FF_CHUNK), F32),
            pltpu.VMEM((2, TM_FFN + 8, FF_CHUNK), F32),
            pltpu.VMEM((N_FF_CHUNKS, 8, FF_CHUNK), F32),
            pltpu.VMEM((N_FF_CHUNKS, 8, FF_CHUNK), F32),
            pltpu.VMEM((TM_FFN, D_MODEL), F32),
            pltpu.VMEM((TM_FFN, D_MODEL), BF16),
        ],
        name="mixer",
        compiler_params=pltpu.CompilerParams(
            dimension_semantics=("arbitrary", "arbitrary"), vmem_limit_bytes=VMEM_LIMIT),
    )(h, hmt, hd, p, *weights)


def _head_interleave(qcols, kcols):
    lead = qcols.shape[:-1]
    qh = qcols.reshape(lead + (N_HEADS, MLSTM_DK))
    kh = kcols.reshape(lead + (N_HEADS, MLSTM_DK))
    return jnp.concatenate([qh, kh], axis=-1).reshape(lead + (GROUP_W,))


def kernel(x, p, positions, rel_bias, ln_mix_g, w_in, mlstm_conv_w, b_igate, b_fgate, mlstm_norm_g, q_norm_g, k_norm_g, lam_q1, lam_k1, lam_q2, lam_k2, diff_subln_g, w_out, ln_ffn_g, w_up, ffn_conv_w, ffn_conv_b, w_down, ln_ple_g, w_ple_gate, w_ple_proj):
    batch, seq, _ = x.shape
    depth = w_in.shape[0]
    n_tok = batch * seq
    qk_cols = N_HEADS * MLSTM_DK
    col_sizes = [qk_cols, qk_cols, GROUP_W, GROUP_W, N_HEADS, N_HEADS, GROUP_W, GROUP_W, GROUP_W]
    offs = np.concatenate([[0], np.cumsum(col_sizes)])
    sl = lambda a, j: a[..., int(offs[j]):int(offs[j + 1])]

    tbl = jnp.broadcast_to(
        (jnp.transpose(rel_bias.astype(F32), (1, 2, 0)) * LOG2E).reshape(N_HEADS, 2 * N_BUCKETS, 1),
        (N_HEADS, 2 * N_BUCKETS, BIAS_COLS))
    pcol = jnp.broadcast_to(positions.astype(jnp.int32).reshape(seq, 1), (seq, BIAS_COLS))
    prow = positions.astype(jnp.int32).reshape(1, seq)
    gsum = jnp.asarray(np.kron(np.eye(MXU_TILE // DIFF_D), np.ones((DIFF_D, DIFF_D))), BF16)
    t_idx = np.arange(PAIR)
    same_chunk = (t_idx[:, None] // CHUNK) == (t_idx[None, :] // CHUNK)
    prefix = same_chunk & (t_idx[:, None] <= t_idx[None, :])
    total_a = np.broadcast_to(t_idx[:, None] < CHUNK, (PAIR, PAIR))
    ug = jnp.asarray(np.concatenate([prefix, total_a, ~total_a], axis=1), BF16)
    l2 = jnp.asarray(prefix.T, BF16)

    p_tok = p.reshape(depth, n_tok, PLE_DIM)
    row = lambda a: a.astype(F32).reshape(depth, 1, a.shape[-1])
    mixer_weights = (
        w_out.astype(BF16), row(ln_ffn_g), w_up.astype(BF16), ffn_conv_w.astype(F32), row(ffn_conv_b),
        w_down.astype(BF16).reshape(depth, N_FF_CHUNKS, FF_CHUNK, D_MODEL), row(ln_ple_g),
        w_ple_gate.astype(BF16), w_ple_proj.astype(BF16))

    h = x.reshape(n_tok, D_MODEL)
    for i in range(depth):
        wi = w_in[i]
        wm = jnp.concatenate([_head_interleave(sl(wi, 0), sl(wi, 1)), sl(wi, 6), sl(wi, 7)], axis=-1).astype(BF16)
        wt = jnp.concatenate([sl(wi, 2), sl(wi, 3), sl(wi, 8)], axis=-1).T.astype(BF16)
        pad_rows = lambda a: jnp.zeros((N_GATE_ROWS,) + a.shape[1:], F32).at[:N_HEADS].set(a.astype(F32))
        wg = jnp.concatenate([pad_rows(sl(wi, 4).T), pad_rows(sl(wi, 5).T)], axis=0).astype(BF16)
        gate_bias = jnp.concatenate([pad_rows(b_igate[i]), pad_rows(b_fgate[i])])
        qg = jnp.tile(q_norm_g[i].astype(F32), GROUP_W // DIFF_D).reshape(1, GROUP_W) * (DIFF_D ** -0.5 * LOG2E)
        kg = jnp.tile(k_norm_g[i].astype(F32), GROUP_W // DIFF_D).reshape(1, GROUP_W)
        qk, vmt, omt, gri, grf, gci, gcf, qd, kd, vdt = _inproj(
            h, ln_mix_g[i].reshape(1, D_MODEL), wm, wt, wg, gate_bias.reshape(1, -1), gate_bias.reshape(-1, 1),
            gsum, qg, kg)

        cw = _head_interleave(mlstm_conv_w[i][:, :qk_cols], mlstm_conv_w[i][:, qk_cols:]).astype(F32)
        ng = jnp.broadcast_to(mlstm_norm_g[i].astype(F32).reshape(MLSTM_DV, 1), (MLSTM_DV, PAIR))
        hmt = _mlstm(qk, vmt, omt, gri, grf, gci, gcf, cw, ng, ug, l2, batch, seq)

        lam_init = 0.8 - 0.6 * math.exp(-0.3 * i)
        row64 = lambda a: a[i].reshape(1, DIFF_D).astype(F32)
        hd = _attention(lam_init, tbl, qd, kd, vdt, pcol, prow, row64(lam_q1), row64(lam_k1), row64(lam_q2),
                        row64(lam_k2), diff_subln_g[i].reshape(1, DIFF_DV).astype(F32), batch, seq)

        h = _mixer(i, h, hmt, hd, p_tok, *mixer_weights, batch, seq)
    return h.reshape(batch, seq, D_MODEL)
```

```python
import functools
import math

import numpy as np
import jax
import jax.numpy as jnp
from jax import lax
from jax.experimental import pallas as pl
from jax.experimental.pallas import tpu as pltpu

F32 = jnp.float32
BF16 = jnp.bfloat16

D_MODEL = 1024
N_HEADS = 4
MLSTM_DK = 64
MLSTM_DV = 128
MLSTM_CONV_W = 4
CHUNK = 64
PAIR = 2 * CHUNK
DIFF_D = 64
DIFF_DV = 128
N_BUCKETS = 32
MAX_DISTANCE = 128
D_FF = 2816
FFN_CONV_W = 3
PLE_DIM = 256
EPS = 1e-6
GROUP_W = N_HEADS * 128
N_GATE_ROWS = 16

MXU_TILE = 256
TM_IN = 512
TM_FFN = 512
FF_CHUNK = 256
N_FF_CHUNKS = D_FF // FF_CHUNK
TQ = 512
TK = 256
BIAS_ROWS = 32
BIAS_COLS = 128
ATTN_NB = 2
NEG_BIG = -1e30
M_INIT = -1e29
ONES_ROWS = 16
V_ROWS = DIFF_DV + ONES_ROWS
LOG2E = math.log2(math.e)
VMEM_LIMIT = 56 * 1024 * 1024


def _dot(a, b):
    return jnp.dot(a, b, preferred_element_type=F32)


def _dot_nt(a, b):
    return lax.dot_general(a, b, (((1,), (1,)), ((), ())), preferred_element_type=F32)


def _dot_tn(a, b):
    return lax.dot_general(a, b, (((0,), (0,)), ((), ())), preferred_element_type=F32)


def _sigmoid(x):
    return 1.0 / (1.0 + jnp.exp(-x))


def _log_sigmoid(x):
    return jnp.minimum(x, 0.0) - jnp.log1p(jnp.exp(-jnp.abs(x)))


def _bucket_thresholds():
    max_exact = N_BUCKETS // 2
    thr = []
    for v in range(1, N_BUCKETS):
        if v <= max_exact:
            thr.append(v)
            continue
        edge = max_exact * (MAX_DISTANCE / max_exact) ** ((v - max_exact) / (N_BUCKETS - max_exact))
        assert abs(edge - round(edge)) > 1e-3, edge
        thr.append(int(math.ceil(edge)))
    assert all(a < b for a, b in zip(thr, thr[1:])), thr
    return tuple(thr)


_BUCKET_THR = _bucket_thresholds()


def _group_mean_square(z, gsum_ref):
    sq = (z * z).astype(BF16)
    width = gsum_ref.shape[0]
    sums = [_dot(sq[:, c0:c0 + width], gsum_ref[...]) for c0 in range(0, z.shape[1], width)]
    return jnp.concatenate(sums, axis=1) * (1.0 / DIFF_D)


def _with_ones_rows(dst_ref, vt):
    for hh in range(N_HEADS):
        dst_ref[hh * V_ROWS:hh * V_ROWS + DIFF_DV, :] = vt[hh * DIFF_DV:(hh + 1) * DIFF_DV, :]
        dst_ref[hh * V_ROWS + DIFF_DV:(hh + 1) * V_ROWS, :] = jnp.ones((ONES_ROWS, vt.shape[1]), BF16)


def _inproj_kernel(h_ref, g_ref, wm_ref, wt_ref, wg_ref, bcol_ref, brow_ref, gsum_ref, qg_ref, kg_ref,
                   qk_ref, vmt_ref, omt_ref, gri_ref, grf_ref, gci_ref, gcf_ref, qd_ref, kd_ref, vdt_ref):
    x = h_ref[...]
    ms = jnp.mean(x * x, axis=-1, keepdims=True)
    u = (x * lax.rsqrt(ms + EPS) * g_ref[...]).astype(BF16)
    w = GROUP_W
    qk_ref[...] = _dot(u, wm_ref[:, 0:w])
    zq = _dot(u, wm_ref[:, w:2 * w])
    qd_ref[...] = (zq * lax.rsqrt(_group_mean_square(zq, gsum_ref) + EPS) * qg_ref[...]).astype(BF16)
    zk = _dot(u, wm_ref[:, 2 * w:3 * w])
    kd_ref[...] = (zk * lax.rsqrt(_group_mean_square(zk, gsum_ref) + EPS) * kg_ref[...]).astype(BF16)
    _with_ones_rows(vmt_ref, _dot_nt(wt_ref[0:w, :], u).astype(BF16))
    omt_ref[...] = _dot_nt(wt_ref[w:2 * w, :], u)
    _with_ones_rows(vdt_ref, _dot_nt(wt_ref[2 * w:3 * w, :], u).astype(BF16))
    gc = _dot_nt(u, wg_ref[...]) + bcol_ref[...]
    gci_ref[...] = gc[:, 0:N_GATE_ROWS]
    gcf_ref[...] = gc[:, N_GATE_ROWS:2 * N_GATE_ROWS]
    gr = _dot_nt(wg_ref[...], u) + brow_ref[...]
    for ci in range(TM_IN // PAIR):
        gri_ref[ci] = gr[0:8, ci * PAIR:(ci + 1) * PAIR]
        grf_ref[ci] = gr[N_GATE_ROWS:N_GATE_ROWS + 8, ci * PAIR:(ci + 1) * PAIR]


def _inproj(h, g, wm, wt, wg, bcol, brow, gsum, qg, kg):
    n_tok = h.shape[0]
    grid = (n_tok // TM_IN,)
    const = lambda shape: pl.BlockSpec(shape, lambda i: (0,) * len(shape))
    tile = lambda width: pl.BlockSpec((TM_IN, width), lambda i: (i, 0))
    tile_t = lambda rows: pl.BlockSpec((rows, TM_IN), lambda i: (0, i))
    gate_rows = pl.BlockSpec((TM_IN // PAIR, 8, PAIR), lambda i: (i, 0, 0))
    out_shape = (
        jax.ShapeDtypeStruct((n_tok, GROUP_W), F32),
        jax.ShapeDtypeStruct((N_HEADS * V_ROWS, n_tok), BF16),
        jax.ShapeDtypeStruct((GROUP_W, n_tok), F32),
        jax.ShapeDtypeStruct((n_tok // PAIR, 8, PAIR), F32),
        jax.ShapeDtypeStruct((n_tok // PAIR, 8, PAIR), F32),
        jax.ShapeDtypeStruct((n_tok, N_GATE_ROWS), F32),
        jax.ShapeDtypeStruct((n_tok, N_GATE_ROWS), F32),
        jax.ShapeDtypeStruct((n_tok, GROUP_W), BF16),
        jax.ShapeDtypeStruct((n_tok, GROUP_W), BF16),
        jax.ShapeDtypeStruct((N_HEADS * V_ROWS, n_tok), BF16),
    )
    out_specs = (
        tile(GROUP_W), tile_t(N_HEADS * V_ROWS), tile_t(GROUP_W), gate_rows, gate_rows,
        tile(N_GATE_ROWS), tile(N_GATE_ROWS), tile(GROUP_W), tile(GROUP_W), tile_t(N_HEADS * V_ROWS),
    )
    in_specs = [
        tile(D_MODEL), const((1, D_MODEL)), const(wm.shape), const(wt.shape), const(wg.shape),
        const(bcol.shape), const(brow.shape), const(gsum.shape),
        const((1, GROUP_W)), const((1, GROUP_W)),
    ]
    return pl.pallas_call(
        _inproj_kernel, grid=grid, in_specs=in_specs, out_specs=out_specs, out_shape=out_shape,
        name="inproj",
        compiler_params=pltpu.CompilerParams(dimension_semantics=("arbitrary",), vmem_limit_bytes=VMEM_LIMIT),
    )(h, g, wm, wt, wg, bcol, brow, gsum, qg, kg)


def _split_hi_lo(x):
    hi = x.astype(BF16)
    return hi, (x - hi.astype(F32)).astype(BF16)


def _mlstm_kernel(qk_ref, vt_ref, ot_ref, gri_ref, grf_ref, gci_ref, gcf_ref, cw_ref, ng_ref, ug_ref, l2_ref,
                  out_ref, xpad_ref, cn_ref):
    seq = qk_ref.shape[0]
    halo = 8
    xpad_ref[0:halo, :] = jnp.zeros((halo, GROUP_W), F32)
    xpad_ref[halo:, :] = qk_ref[...]
    cn_ref[...] = jnp.zeros_like(cn_ref)

    first8 = lax.broadcasted_iota(jnp.int32, (8, PAIR), 1) < CHUNK
    first = lax.broadcasted_iota(jnp.int32, (1, PAIR), 1) < CHUNK
    upper = lax.broadcasted_iota(jnp.int32, (PAIR, PAIR), 1) >= MLSTM_DK
    key_t = lax.broadcasted_iota(jnp.int32, (PAIR, PAIR), 0)
    qry_t = lax.broadcasted_iota(jnp.int32, (PAIR, PAIR), 1)
    chunk_start = jnp.where(qry_t < CHUNK, 0, CHUNK)
    cw = cw_ref[...]
    neg_inf = jnp.float32(-jnp.inf)

    def pair_step(c2, m_prev):
        r0 = pl.multiple_of(c2 * PAIR, PAIR)
        win = xpad_ref[pl.ds(r0, PAIR + halo), :]
        conv = cw[0:1, :] * win[halo - 3:halo - 3 + PAIR, :]
        for j in range(1, MLSTM_CONV_W):
            conv = conv + cw[j:j + 1, :] * win[halo - 3 + j:halo - 3 + j + PAIR, :]
        x = conv * _sigmoid(conv)

        gi = gri_ref[c2]
        lf_hi, lf_lo = _split_hi_lo(_log_sigmoid(grf_ref[c2]))
        bg = _dot(lf_hi, ug_ref[...]) + _dot(lf_lo, ug_ref[...])
        b_r = bg[:, 0:PAIR]
        g_a = bg[:, PAIR:2 * PAIR]
        g_b = bg[:, 2 * PAIR:3 * PAIR]
        a_r = jnp.where(first8, g_a, g_b) - b_r + gi
        max_a = jnp.max(jnp.where(first8, a_r, neg_inf), axis=1, keepdims=True)
        max_b = jnp.max(jnp.where(first8, neg_inf, a_r), axis=1, keepdims=True)
        m_a = jnp.maximum(g_a + m_prev, max_a)
        m_b = jnp.maximum(g_b + m_a, max_b)
        dec_a = jnp.exp(g_a + m_prev - m_a)
        dec_b = jnp.exp(g_b + m_a - m_b)
        w_r = jnp.exp(a_r - jnp.where(first8, m_a, m_b))
        e_r = b_r + jnp.where(first8, m_prev, m_a)
        lc_hi, lc_lo = _split_hi_lo(_log_sigmoid(gcf_ref[pl.ds(r0, PAIR), :]))
        x_c = _dot(l2_ref[...], lc_hi) + _dot(l2_ref[...], lc_lo) - gci_ref[pl.ds(r0, PAIR), :]

        heads = range(N_HEADS)
        xk, xq, vt, cn0, up_a, kq, cq_a, m_out, decay_t = [], [], [], [], [], [], [], [], []
        for hd in heads:
            xh = x[:, hd * 128:(hd + 1) * 128]
            xs = pltpu.roll(xh, MLSTM_DK, 1)
            xk.append(jnp.where(upper, xh * (MLSTM_DK ** -0.5), 0.0).astype(BF16))
            xq.append(jnp.where(upper, xs, 0.0).astype(BF16))
            vt.append(vt_ref[hd * V_ROWS:(hd + 1) * V_ROWS, pl.ds(r0, PAIR)])
            cn0.append(cn_ref[hd])
            w_a = jnp.where(first, w_r[hd:hd + 1, :], 0.0)
            up_a.append(_dot((vt[hd].astype(F32) * w_a).astype(BF16), xk[hd]))
            kq.append(_dot_nt(xk[hd], xq[hd]))
            cq_a.append(_dot_nt(cn0[hd].astype(BF16), xq[hd]))
            dmat = b_r[hd:hd + 1, :] - x_c[:, hd:hd + 1]
            dmat = jnp.where(key_t <= qry_t, jnp.where(key_t >= chunk_start, dmat, neg_inf), neg_inf)
            m_out.append(jnp.maximum(e_r[hd:hd + 1, :], jnp.max(dmat, axis=0, keepdims=True)))
            decay_t.append(jnp.exp(dmat - m_out[hd]))
        cn1, cq_b, up_b = [], [], []
        for hd in heads:
            cn1.append(dec_a[hd:hd + 1, :] * cn0[hd] + up_a[hd])
            w_b = jnp.where(first, 0.0, w_r[hd:hd + 1, :])
            up_b.append(_dot((vt[hd].astype(F32) * w_b).astype(BF16), xk[hd]))
            cq_b.append(_dot_nt(cn1[hd].astype(BF16), xq[hd]))
        sv = []
        for hd in heads:
            cn_ref[hd] = dec_b[hd:hd + 1, :] * cn1[hd] + up_b[hd]
            sv.append(_dot(vt[hd], (kq[hd] * decay_t[hd]).astype(BF16)))
        for hd in heads:
            rows = slice(hd * MLSTM_DV, (hd + 1) * MLSTM_DV)
            inter = jnp.exp(e_r[hd:hd + 1, :] - m_out[hd])
            tot = inter * jnp.where(first, cq_a[hd], cq_b[hd]) + sv[hd]
            den = tot[MLSTM_DV:MLSTM_DV + 1, :]
            hh = tot[0:MLSTM_DV, :] / jnp.maximum(jnp.abs(den), jnp.exp(-m_out[hd]))
            hn = hh * lax.rsqrt(jnp.mean(hh * hh, axis=0, keepdims=True) + EPS) * ng_ref[...]
            og = ot_ref[rows, pl.ds(r0, PAIR)]
            out_ref[rows, pl.ds(r0, PAIR)] = (hn * _sigmoid(og)).astype(BF16)
        return m_b

    lax.fori_loop(0, seq // PAIR, pair_step, jnp.zeros((8, PAIR), F32))


def _mlstm(qk, vmt, omt, gri, grf, gci, gcf, cw, ng, ug, l2, batch, seq):
    const = lambda arr: pl.BlockSpec(arr.shape, lambda b: (0,) * arr.ndim)
    in_specs = [
        pl.BlockSpec((seq, GROUP_W), lambda b: (b, 0)),
        pl.BlockSpec((N_HEADS * V_ROWS, seq), lambda b: (0, b)),
        pl.BlockSpec((GROUP_W, seq), lambda b: (0, b)),
        pl.BlockSpec((seq // PAIR, 8, PAIR), lambda b: (b, 0, 0)),
        pl.BlockSpec((seq // PAIR, 8, PAIR), lambda b: (b, 0, 0)),
        pl.BlockSpec((seq, N_GATE_ROWS), lambda b: (b, 0)),
        pl.BlockSpec((seq, N_GATE_ROWS), lambda b: (b, 0)),
        const(cw), const(ng), const(ug), const(l2),
    ]
    return pl.pallas_call(
        _mlstm_kernel, grid=(batch,), in_specs=in_specs,
        out_specs=pl.BlockSpec((GROUP_W, seq), lambda b: (0, b)),
        out_shape=jax.ShapeDtypeStruct((GROUP_W, batch * seq), BF16),
        scratch_shapes=[
            pltpu.VMEM((seq + 8, GROUP_W), F32),
            pltpu.VMEM((N_HEADS, V_ROWS, 128), F32),
        ],
        name="mlstm",
        compiler_params=pltpu.CompilerParams(dimension_semantics=("arbitrary",), vmem_limit_bytes=VMEM_LIMIT),
    )(qk, vmt, omt, gri, grf, gci, gcf, cw, ng, ug, l2)


def _attn_kernel(lam_init, tbl_ref, kmax_ref, qmin_ref, q_ref, k_ref, vt_ref, pcol_ref, prow_ref,
                 lq1_ref, lk1_ref, lq2_ref, lk2_ref, sg_ref, out_ref, bias_ref, qc_ref, s_ref, p_ref, acc_ref):
    hd = pl.program_id(0)
    qi = pl.program_id(1)
    b = pl.program_id(2)
    n_kv = (qi + 1) * (TQ // TK)

    @pl.when(b == 0)
    def _build_bias():
        rowi = lax.broadcasted_iota(jnp.int32, (BIAS_ROWS, BIAS_COLS), 0)
        coli = lax.broadcasted_iota(jnp.int32, (BIAS_ROWS, BIAS_COLS), 1)
        row = lambda c, v: tbl_ref[c * N_BUCKETS + v:c * N_BUCKETS + v + 1, :]
        last = N_BUCKETS - 1

        def build(t, _):
            r0 = pl.multiple_of(t * BIAS_ROWS, BIAS_ROWS)
            pk = pcol_ref[pl.ds(r0, BIAS_ROWS), :]
            kmax = kmax_ref[t]
            for j in range(TQ // BIAS_COLS):
                lanes = slice(j * BIAS_COLS, (j + 1) * BIAS_COLS)
                q0 = qi * TQ + j * BIAS_COLS
                causal = rowi + r0 <= coli + q0
                flat = jnp.logical_or(qmin_ref[qi * (TQ // BIAS_COLS) + j] - kmax >= _BUCKET_THR[-1],
                                      r0 > q0 + BIAS_COLS - 1)

                @pl.when(flat)
                def _flat():
                    bias_ref[0, pl.ds(r0, BIAS_ROWS), lanes] = jnp.where(causal, row(0, last), NEG_BIG)
                    bias_ref[1, pl.ds(r0, BIAS_ROWS), lanes] = jnp.where(causal, row(1, last), NEG_BIG)

                @pl.when(jnp.logical_not(flat))
                def _lookup():
                    dist = jnp.maximum(prow_ref[:, lanes] - pk, 0)
                    b0 = jnp.broadcast_to(row(0, 0), (BIAS_ROWS, BIAS_COLS))
                    b1 = jnp.broadcast_to(row(1, 0), (BIAS_ROWS, BIAS_COLS))
                    for v, thr in enumerate(_BUCKET_THR, start=1):
                        ge = dist >= thr
                        b0 = jnp.where(ge, row(0, v), b0)
                        b1 = jnp.where(ge, row(1, v), b1)
                    bias_ref[0, pl.ds(r0, BIAS_ROWS), lanes] = jnp.where(causal, b0, NEG_BIG)
                    bias_ref[1, pl.ds(r0, BIAS_ROWS), lanes] = jnp.where(causal, b1, NEG_BIG)
            return 0

        lax.fori_loop(0, n_kv * (TK // BIAS_ROWS), build, 0)

    seq = k_ref.shape[1]
    lane = lax.broadcasted_iota(jnp.int32, (TQ, 128), 1)
    for e in range(ATTN_NB):
        q = q_ref[e]
        zero = jnp.zeros_like(q)
        qc_ref[e, 0] = jnp.where(lane < DIFF_D, q, zero)
        qc_ref[e, 1] = jnp.where(lane >= DIFF_D, q, zero)
    acc_ref[...] = jnp.zeros_like(acc_ref)

    chains = [(e, c) for e in range(ATTN_NB) for c in range(2)]

    def scores(t, slot):
        k0 = pl.multiple_of(t * TK, TK)
        for e in range(ATTN_NB):
            kb = k_ref[e, pl.ds(k0, TK), :]
            for c in range(2):
                s_ref[slot, e, c] = _dot_nt(kb, qc_ref[e, c]) + bias_ref[c, pl.ds(k0, TK), :]

    def numerators(m_state, slot, lanes=slice(None)):
        m_next, alpha_next = [], []
        for i, (e, c) in enumerate(chains):
            s = s_ref[slot, e, c, :, lanes]
            m_prev = m_state[i][:, lanes]
            m_new = jnp.maximum(m_prev, jnp.max(s, axis=0, keepdims=True))
            alpha_next.append(jnp.exp2(m_prev - m_new))
            p_ref[slot, e, c, :, lanes] = jnp.exp2(s - m_new).astype(BF16)
            m_next.append(m_new)
        return tuple(m_next), tuple(alpha_next)

    def accumulate(t, alpha, slot, lanes=slice(None)):
        for e in range(ATTN_NB):
            v0 = pl.multiple_of(e * seq + t * TK, TK)
            vt = vt_ref[:, pl.ds(v0, TK)]
            for c in range(2):
                acc_ref[e, c, :, lanes] = (alpha[2 * e + c] * acc_ref[e, c, :, lanes]
                                           + _dot(vt, p_ref[slot, e, c, :, lanes]))

    def kv_pair(i, carry):
        t = 2 * i
        m_state, alpha = carry
        scores(t, 0)
        m_state, alpha1 = numerators(m_state, 1)
        accumulate(t - 2, alpha, 0)
        scores(t + 1, 1)
        carry = numerators(m_state, 0)
        accumulate(t - 1, alpha1, 1)
        return carry

    assert (TQ // TK) % 2 == 0
    scores(0, 0)
    scores(1, 1)
    carry = numerators(tuple(jnp.full((1, TQ), M_INIT, F32) for _ in chains), 0)
    m_state, alpha = lax.fori_loop(1, n_kv // 2, kv_pair, carry)
    live = slice(TQ - TK, TQ)
    _, alpha1 = numerators(m_state, 1, live)
    accumulate(n_kv - 2, alpha, 0)
    accumulate(n_kv - 1, alpha1, 1, live)

    lam = (jnp.exp(jnp.sum(lq1_ref[...] * lk1_ref[...], axis=1, keepdims=True))
           - jnp.exp(jnp.sum(lq2_ref[...] * lk2_ref[...], axis=1, keepdims=True)) + lam_init)
    for e in range(ATTN_NB):
        a0 = acc_ref[e, 0]
        a1 = acc_ref[e, 1]
        o = a0[0:DIFF_DV] / a0[DIFF_DV:DIFF_DV + 1] - lam * (a1[0:DIFF_DV] / a1[DIFF_DV:DIFF_DV + 1])
        on = o * lax.rsqrt(jnp.mean(o * o, axis=0, keepdims=True) + EPS) * sg_ref[...]
        out_ref[e] = (on * (1.0 - lam_init)).astype(BF16)


def _attention(lam_init, tbl, qd, kd, vdt, pcol, prow, lq1, lk1, lq2, lk2, sg, batch, seq):
    q3, k3 = (t.reshape(batch, seq, GROUP_W) for t in (qd, kd))
    vec = lambda n: pl.BlockSpec((1, n), lambda h, i, b: (0, 0))
    kmax = jnp.max(prow.reshape(seq // BIAS_ROWS, BIAS_ROWS), axis=1)
    qmin = jnp.min(prow.reshape(seq // BIAS_COLS, BIAS_COLS), axis=1)
    in_specs = [
        pl.BlockSpec((None, 2 * N_BUCKETS, BIAS_COLS), lambda h, i, b: (h, 0, 0)),
        pl.BlockSpec(memory_space=pltpu.SMEM),
        pl.BlockSpec(memory_space=pltpu.SMEM),
        pl.BlockSpec((ATTN_NB, TQ, 128), lambda h, i, b: (b, i, h)),
        pl.BlockSpec((ATTN_NB, seq, 128), lambda h, i, b: (b, 0, h)),
        pl.BlockSpec((V_ROWS, ATTN_NB * seq), lambda h, i, b: (h, b)),
        pl.BlockSpec((seq, BIAS_COLS), lambda h, i, b: (0, 0)),
        pl.BlockSpec((1, TQ), lambda h, i, b: (0, i)),
        vec(DIFF_D), vec(DIFF_D), vec(DIFF_D), vec(DIFF_D),
        pl.BlockSpec((DIFF_DV, TQ), lambda h, i, b: (0, 0)),
    ]
    out = pl.pallas_call(
        functools.partial(_attn_kernel, lam_init),
        grid=(N_HEADS, seq // TQ, batch // ATTN_NB), in_specs=in_specs,
        out_specs=pl.BlockSpec((ATTN_NB, DIFF_DV, TQ), lambda h, i, b: (b, h, i)),
        out_shape=jax.ShapeDtypeStruct((batch, GROUP_W, seq), BF16),
        scratch_shapes=[
            pltpu.VMEM((2, seq, TQ), F32),
            pltpu.VMEM((ATTN_NB, 2, TQ, 128), BF16),
            pltpu.VMEM((2, ATTN_NB, 2, TK, TQ), F32),
            pltpu.VMEM((2, ATTN_NB, 2, TK, TQ), BF16),
            pltpu.VMEM((ATTN_NB, 2, V_ROWS, TQ), F32),
        ],
        name="diff_attn",
        compiler_params=pltpu.CompilerParams(
            dimension_semantics=("arbitrary", "arbitrary", "arbitrary"), vmem_limit_bytes=VMEM_LIMIT),
    )(tbl, kmax, qmin, q3, k3, vdt, pcol, prow, lq1, lk1, lq2, lk2, jnp.broadcast_to(sg.reshape(DIFF_DV, 1), (DIFF_DV, TQ)))
    return out


def _gelu(x):
    return 0.5 * x * (1.0 + lax.erf(x * (2.0 ** -0.5)))


def _rms(x, g):
    return x * lax.rsqrt(jnp.mean(x * x, axis=-1, keepdims=True) + EPS) * g


def _mixer_kernel(h_ref, hmt_ref, hdt_ref, p_ref, wo_ref, gf_ref, wu_ref, cw_ref, cb_ref, wd_ref, gp_ref,
                  wpg_ref, wpp_ref, out_ref, sg_ref, sv_ref, cg_ref, cv_ref, acc_ref, u_ref):
    tm = h_ref.shape[0]
    halo = 8

    @pl.when(pl.program_id(1) == 0)
    def _reset_conv_history():
        cg_ref[...] = jnp.zeros_like(cg_ref)
        cv_ref[...] = jnp.zeros_like(cv_ref)

    h1 = h_ref[...] + _dot_tn(jnp.concatenate([hmt_ref[...], hdt_ref[...]], axis=0), wo_ref[...])
    u_ref[...] = _rms(h1, gf_ref[...]).astype(BF16)
    acc_ref[...] = jnp.zeros_like(acc_ref)

    def up_stage(j, slot):
        c0 = pl.multiple_of(j * FF_CHUNK, FF_CHUNK)
        sg_ref[slot, halo:, :] = _dot(u_ref[...], wu_ref[:, pl.ds(c0, FF_CHUNK)])
        sv_ref[slot, halo:, :] = _dot(u_ref[...], wu_ref[:, pl.ds(D_FF + c0, FF_CHUNK)])

    def conv_branch(j, slot, c0, stage_ref, hist_ref):
        stage_ref[slot, 0:halo, :] = hist_ref[j]
        hist_ref[j] = stage_ref[slot, tm:tm + halo, :]
        cw = cw_ref[:, pl.ds(c0, FF_CHUNK)]
        return (cw[0:1, :] * stage_ref[slot, halo - 2:halo - 2 + tm, :]
                + cw[1:2, :] * stage_ref[slot, halo - 1:halo - 1 + tm, :]
                + cw[2:3, :] * stage_ref[slot, halo:, :] + cb_ref[:, pl.ds(c0, FF_CHUNK)])

    def act_stage(j, slot):
        c0 = pl.multiple_of(j * FF_CHUNK, FF_CHUNK)
        gate = conv_branch(j, slot, c0, sg_ref, cg_ref)
        val = conv_branch(j, slot, D_FF + c0, sv_ref, cv_ref)
        act = (_gelu(gate) * val).astype(BF16)
        acc_ref[...] += _dot(act, wd_ref[j])

    def chunk_pair(i, _):
        j = 2 * i
        up_stage(j + 1, 1)
        act_stage(j, 0)
        up_stage(j + 2, 0)
        act_stage(j + 1, 1)
        return 0

    assert N_FF_CHUNKS % 2 == 1
    up_stage(0, 0)
    lax.fori_loop(0, N_FF_CHUNKS // 2, chunk_pair, 0)
    act_stage(N_FF_CHUNKS - 1, 0)

    h2 = h1 + acc_ref[...]
    u3 = _rms(h2, gp_ref[...]).astype(BF16)
    ple_gate = _sigmoid(_dot(u3, wpg_ref[...]))
    out_ref[...] = h2 + ple_gate * _dot(p_ref[...].astype(BF16), wpp_ref[...])


def _mixer(layer, h, hmt, hdt, p, wo, gf, wu, cw, cb, wd, gp, wpg, wpp, batch, seq):
    tiles = seq // TM_FFN
    tile = lambda width: pl.BlockSpec((TM_FFN, width), lambda b, t: (b * tiles + t, 0))

    def of_layer(arr):
        nd = arr.ndim
        return pl.BlockSpec((None,) + arr.shape[1:], lambda b, t: (layer,) + (0,) * (nd - 1),
                            pipeline_mode=pl.Buffered(1))

    weights = (wo, gf, wu, cw, cb, wd, gp, wpg, wpp)
    in_specs = [
        tile(D_MODEL),
        pl.BlockSpec((GROUP_W, TM_FFN), lambda b, t: (0, b * tiles + t)),
        pl.BlockSpec((None, GROUP_W, TM_FFN), lambda b, t: (b, 0, t)),
        pl.BlockSpec((None, TM_FFN, PLE_DIM), lambda b, t: (layer, b * tiles + t, 0)),
    ] + [of_layer(w) for w in weights]
    return pl.pallas_call(
        _mixer_kernel, grid=(batch, tiles), in_specs=in_specs, out_specs=tile(D_MODEL),
        out_shape=jax.ShapeDtypeStruct(h.shape, F32),
        scratch_shapes=[
            pltpu.VMEM((2, TM_FFN + 8, FF_CHUNK), F32),
            pltpu.VMEM((2, TM_FFN + 8, FF_CHUNK), F32),
            pltpu.VMEM((N_FF_CHUNKS, 8, FF_CHUNK), F32),
            pltpu.VMEM((N_FF_CHUNKS, 8, FF_CHUNK), F32),
            pltpu.VMEM((TM_FFN, D_MODEL), F32),
            pltpu.VMEM((TM_FFN, D_MODEL), BF16),
        ],
        name="mixer",
        compiler_params=pltpu.CompilerParams(
            dimension_semantics=("arbitrary", "arbitrary"), vmem_limit_bytes=VMEM_LIMIT),
    )(h, hmt, hdt, p, *weights)


def _head_interleave(qcols, kcols):
    lead = qcols.shape[:-1]
    qh = qcols.reshape(lead + (N_HEADS, MLSTM_DK))
    kh = kcols.reshape(lead + (N_HEADS, MLSTM_DK))
    return jnp.concatenate([qh, kh], axis=-1).reshape(lead + (GROUP_W,))


def kernel(x, p, positions, rel_bias, ln_mix_g, w_in, mlstm_conv_w, b_igate, b_fgate, mlstm_norm_g, q_norm_g, k_norm_g, lam_q1, lam_k1, lam_q2, lam_k2, diff_subln_g, w_out, ln_ffn_g, w_up, ffn_conv_w, ffn_conv_b, w_down, ln_ple_g, w_ple_gate, w_ple_proj):
    batch, seq, _ = x.shape
    depth = w_in.shape[0]
    n_tok = batch * seq
    qk_cols = N_HEADS * MLSTM_DK
    col_sizes = [qk_cols, qk_cols, GROUP_W, GROUP_W, N_HEADS, N_HEADS, GROUP_W, GROUP_W, GROUP_W]
    offs = np.concatenate([[0], np.cumsum(col_sizes)])
    sl = lambda a, j: a[..., int(offs[j]):int(offs[j + 1])]

    tbl = jnp.broadcast_to(
        (jnp.transpose(rel_bias.astype(F32), (1, 2, 0)) * LOG2E).reshape(N_HEADS, 2 * N_BUCKETS, 1),
        (N_HEADS, 2 * N_BUCKETS, BIAS_COLS))
    pcol = jnp.broadcast_to(positions.astype(jnp.int32).reshape(seq, 1), (seq, BIAS_COLS))
    prow = positions.astype(jnp.int32).reshape(1, seq)
    gsum = jnp.asarray(np.kron(np.eye(MXU_TILE // DIFF_D), np.ones((DIFF_D, DIFF_D))), BF16)
    t_idx = np.arange(PAIR)
    same_chunk = (t_idx[:, None] // CHUNK) == (t_idx[None, :] // CHUNK)
    prefix = same_chunk & (t_idx[:, None] <= t_idx[None, :])
    total_a = np.broadcast_to(t_idx[:, None] < CHUNK, (PAIR, PAIR))
    ug = jnp.asarray(np.concatenate([prefix, total_a, ~total_a], axis=1), BF16)
    l2 = jnp.asarray(prefix.T, BF16)

    p_tok = p.reshape(depth, n_tok, PLE_DIM)
    row = lambda a: a.astype(F32).reshape(depth, 1, a.shape[-1])
    mixer_weights = (
        w_out.astype(BF16), row(ln_ffn_g), w_up.astype(BF16), ffn_conv_w.astype(F32), row(ffn_conv_b),
        w_down.astype(BF16).reshape(depth, N_FF_CHUNKS, FF_CHUNK, D_MODEL), row(ln_ple_g),
        w_ple_gate.astype(BF16), w_ple_proj.astype(BF16))

    h = x.reshape(n_tok, D_MODEL)
    for i in range(depth):
        wi = w_in[i]
        wm = jnp.concatenate([_head_interleave(sl(wi, 0), sl(wi, 1)), sl(wi, 6), sl(wi, 7)], axis=-1).astype(BF16)
        wt = jnp.concatenate([sl(wi, 2), sl(wi, 3), sl(wi, 8)], axis=-1).T.astype(BF16)
        pad_rows = lambda a: jnp.zeros((N_GATE_ROWS,) + a.shape[1:], F32).at[:N_HEADS].set(a.astype(F32))
        wg = jnp.concatenate([pad_rows(sl(wi, 4).T), pad_rows(sl(wi, 5).T)], axis=0).astype(BF16)
        gate_bias = jnp.concatenate([pad_rows(b_igate[i]), pad_rows(b_fgate[i])])
        qg = jnp.tile(q_norm_g[i].astype(F32), GROUP_W // DIFF_D).reshape(1, GROUP_W) * (DIFF_D ** -0.5 * LOG2E)
        kg = jnp.tile(k_norm_g[i].astype(F32), GROUP_W // DIFF_D).reshape(1, GROUP_W)
        qk, vmt, omt, gri, grf, gci, gcf, qd, kd, vdt = _inproj(
            h, ln_mix_g[i].reshape(1, D_MODEL), wm, wt, wg, gate_bias.reshape(1, -1), gate_bias.reshape(-1, 1),
            gsum, qg, kg)

        cw = _head_interleave(mlstm_conv_w[i][:, :qk_cols], mlstm_conv_w[i][:, qk_cols:]).astype(F32)
        ng = jnp.broadcast_to(mlstm_norm_g[i].astype(F32).reshape(MLSTM_DV, 1), (MLSTM_DV, PAIR))
        hmt = _mlstm(qk, vmt, omt, gri, grf, gci, gcf, cw, ng, ug, l2, batch, seq)

        lam_init = 0.8 - 0.6 * math.exp(-0.3 * i)
        row64 = lambda a: a[i].reshape(1, DIFF_D).astype(F32)
        hdt = _attention(lam_init, tbl, qd, kd, vdt, pcol, prow, row64(lam_q1), row64(lam_k1), row64(lam_q2),
                        row64(lam_k2), diff_subln_g[i].reshape(1, DIFF_DV).astype(F32), batch, seq)

        h = _mixer(i, h, hmt, hdt, p_tok, *mixer_weights, batch, seq)
    return h.reshape(batch, seq, D_MODEL)
```

```python
import functools
import math

import numpy as np
import jax
import jax.numpy as jnp
from jax import lax
from jax.experimental import pallas as pl
from jax.experimental.pallas import tpu as pltpu

F32 = jnp.float32
BF16 = jnp.bfloat16

LANES = 128
SUBLANES = 8
MXU_TILE = 256
V7X_VMEM_BYTES = 64 * 1024 * 1024

D_MODEL = 1024
N_HEADS = 4
MLSTM_DK = 64
MLSTM_DV = 128
MLSTM_CONV_W = 4
CHUNK = 64
PAIR = 2 * CHUNK
DIFF_D = 64
DIFF_DV = 128
N_BUCKETS = 32
MAX_DISTANCE = 128
D_FF = 2816
FFN_CONV_W = 3
PLE_DIM = 256
EPS = 1e-6
GROUP_W = N_HEADS * LANES
N_GATE_ROWS = 16

TM_IN = 512
TM_FFN = 512
FF_CHUNK = 256
N_FF_CHUNKS = D_FF // FF_CHUNK
TQ = 512
TK = 256
BIAS_ROWS = 32
BIAS_COLS = LANES
ATTN_NB = 2
NEG_BIG = -1e30
M_INIT = -1e29
ONES_ROWS = 16
V_ROWS = DIFF_DV + ONES_ROWS
LOG2E = math.log2(math.e)
VMEM_LIMIT = V7X_VMEM_BYTES * 7 // 8


def _dot(a, b):
    return jnp.dot(a, b, preferred_element_type=F32)


def _dot_nt(a, b):
    return lax.dot_general(a, b, (((1,), (1,)), ((), ())), preferred_element_type=F32)


def _dot_tn(a, b):
    return lax.dot_general(a, b, (((0,), (0,)), ((), ())), preferred_element_type=F32)


def _sigmoid(x):
    return 1.0 / (1.0 + jnp.exp(-x))


def _log_sigmoid(x):
    return jnp.minimum(x, 0.0) - jnp.log1p(jnp.exp(-jnp.abs(x)))


def _bucket_thresholds():
    max_exact = N_BUCKETS // 2
    thr = []
    for v in range(1, N_BUCKETS):
        if v <= max_exact:
            thr.append(v)
            continue
        edge = max_exact * (MAX_DISTANCE / max_exact) ** ((v - max_exact) / (N_BUCKETS - max_exact))
        assert abs(edge - round(edge)) > 1e-3, edge
        thr.append(int(math.ceil(edge)))
    assert all(a < b for a, b in zip(thr, thr[1:])), thr
    return tuple(thr)


_BUCKET_THR = _bucket_thresholds()


def _group_mean_square(z, gsum_ref):
    sq = (z * z).astype(BF16)
    width = gsum_ref.shape[0]
    sums = [_dot(sq[:, c0:c0 + width], gsum_ref[...]) for c0 in range(0, z.shape[1], width)]
    return jnp.concatenate(sums, axis=1) * (1.0 / DIFF_D)


def _with_ones_rows(dst_ref, vt):
    for hh in range(N_HEADS):
        dst_ref[hh * V_ROWS:hh * V_ROWS + DIFF_DV, :] = vt[hh * DIFF_DV:(hh + 1) * DIFF_DV, :]
        dst_ref[hh * V_ROWS + DIFF_DV:(hh + 1) * V_ROWS, :] = jnp.ones((ONES_ROWS, vt.shape[1]), BF16)


def _inproj_kernel(h_ref, g_ref, wm_ref, wt_ref, wg_ref, bcol_ref, brow_ref, gsum_ref, qg_ref, kg_ref,
                   qk_ref, vmt_ref, omt_ref, gri_ref, grf_ref, gci_ref, gcf_ref, qd_ref, kd_ref, vdt_ref):
    x = h_ref[...]
    ms = jnp.mean(x * x, axis=-1, keepdims=True)
    u = (x * lax.rsqrt(ms + EPS) * g_ref[...]).astype(BF16)
    w = GROUP_W
    qk_ref[...] = _dot(u, wm_ref[:, 0:w])
    zq = _dot(u, wm_ref[:, w:2 * w])
    qd_ref[...] = (zq * lax.rsqrt(_group_mean_square(zq, gsum_ref) + EPS) * qg_ref[...]).astype(BF16)
    zk = _dot(u, wm_ref[:, 2 * w:3 * w])
    kd_ref[...] = (zk * lax.rsqrt(_group_mean_square(zk, gsum_ref) + EPS) * kg_ref[...]).astype(BF16)
    _with_ones_rows(vmt_ref, _dot_nt(wt_ref[0:w, :], u).astype(BF16))
    omt_ref[...] = _dot_nt(wt_ref[w:2 * w, :], u)
    _with_ones_rows(vdt_ref, _dot_nt(wt_ref[2 * w:3 * w, :], u).astype(BF16))
    gc = _dot_nt(u, wg_ref[...]) + bcol_ref[...]
    gci_ref[...] = gc[:, 0:N_GATE_ROWS]
    gcf_ref[...] = gc[:, N_GATE_ROWS:2 * N_GATE_ROWS]
    gr = _dot_nt(wg_ref[...], u) + brow_ref[...]
    for ci in range(TM_IN // PAIR):
        gri_ref[ci] = gr[0:SUBLANES, ci * PAIR:(ci + 1) * PAIR]
        grf_ref[ci] = gr[N_GATE_ROWS:N_GATE_ROWS + SUBLANES, ci * PAIR:(ci + 1) * PAIR]


def _inproj(h, g, wm, wt, wg, bcol, brow, gsum, qg, kg):
    n_tok = h.shape[0]
    grid = (n_tok // TM_IN,)
    const = lambda shape: pl.BlockSpec(shape, lambda i: (0,) * len(shape))
    tile = lambda width: pl.BlockSpec((TM_IN, width), lambda i: (i, 0))
    tile_t = lambda rows: pl.BlockSpec((rows, TM_IN), lambda i: (0, i))
    gate_rows = pl.BlockSpec((TM_IN // PAIR, SUBLANES, PAIR), lambda i: (i, 0, 0))
    out_shape = (
        jax.ShapeDtypeStruct((n_tok, GROUP_W), F32),
        jax.ShapeDtypeStruct((N_HEADS * V_ROWS, n_tok), BF16),
        jax.ShapeDtypeStruct((GROUP_W, n_tok), F32),
        jax.ShapeDtypeStruct((n_tok // PAIR, SUBLANES, PAIR), F32),
        jax.ShapeDtypeStruct((n_tok // PAIR, SUBLANES, PAIR), F32),
        jax.ShapeDtypeStruct((n_tok, N_GATE_ROWS), F32),
        jax.ShapeDtypeStruct((n_tok, N_GATE_ROWS), F32),
        jax.ShapeDtypeStruct((n_tok, GROUP_W), BF16),
        jax.ShapeDtypeStruct((n_tok, GROUP_W), BF16),
        jax.ShapeDtypeStruct((N_HEADS * V_ROWS, n_tok), BF16),
    )
    out_specs = (
        tile(GROUP_W), tile_t(N_HEADS * V_ROWS), tile_t(GROUP_W), gate_rows, gate_rows,
        tile(N_GATE_ROWS), tile(N_GATE_ROWS), tile(GROUP_W), tile(GROUP_W), tile_t(N_HEADS * V_ROWS),
    )
    in_specs = [
        tile(D_MODEL), const((1, D_MODEL)), const(wm.shape), const(wt.shape), const(wg.shape),
        const(bcol.shape), const(brow.shape), const(gsum.shape),
        const((1, GROUP_W)), const((1, GROUP_W)),
    ]
    return pl.pallas_call(
        _inproj_kernel, grid=grid, in_specs=in_specs, out_specs=out_specs, out_shape=out_shape,
        name="inproj",
        compiler_params=pltpu.CompilerParams(dimension_semantics=("arbitrary",), vmem_limit_bytes=VMEM_LIMIT),
    )(h, g, wm, wt, wg, bcol, brow, gsum, qg, kg)


def _split_hi_lo(x):
    hi = x.astype(BF16)
    return hi, (x - hi.astype(F32)).astype(BF16)


def _mlstm_kernel(qk_ref, vt_ref, ot_ref, gri_ref, grf_ref, gci_ref, gcf_ref, cw_ref, ng_ref, ug_ref, l2_ref,
                  out_ref, xpad_ref, cn_ref):
    seq = qk_ref.shape[0]
    halo = SUBLANES
    xpad_ref[0:halo, :] = jnp.zeros((halo, GROUP_W), F32)
    xpad_ref[halo:, :] = qk_ref[...]
    cn_ref[...] = jnp.zeros_like(cn_ref)

    first8 = lax.broadcasted_iota(jnp.int32, (SUBLANES, PAIR), 1) < CHUNK
    first = lax.broadcasted_iota(jnp.int32, (1, PAIR), 1) < CHUNK
    upper = lax.broadcasted_iota(jnp.int32, (PAIR, PAIR), 1) >= MLSTM_DK
    key_t = lax.broadcasted_iota(jnp.int32, (PAIR, PAIR), 0)
    qry_t = lax.broadcasted_iota(jnp.int32, (PAIR, PAIR), 1)
    chunk_start = jnp.where(qry_t < CHUNK, 0, CHUNK)
    cw = cw_ref[...]
    neg_inf = jnp.float32(-jnp.inf)

    def pair_step(c2, m_prev):
        r0 = pl.multiple_of(c2 * PAIR, PAIR)
        win = xpad_ref[pl.ds(r0, PAIR + halo), :]
        conv = cw[0:1, :] * win[halo - 3:halo - 3 + PAIR, :]
        for j in range(1, MLSTM_CONV_W):
            conv = conv + cw[j:j + 1, :] * win[halo - 3 + j:halo - 3 + j + PAIR, :]
        x = conv * _sigmoid(conv)

        gi = gri_ref[c2]
        lf_hi, lf_lo = _split_hi_lo(_log_sigmoid(grf_ref[c2]))
        bg = _dot(lf_hi, ug_ref[...]) + _dot(lf_lo, ug_ref[...])
        b_r = bg[:, 0:PAIR]
        g_a = bg[:, PAIR:2 * PAIR]
        g_b = bg[:, 2 * PAIR:3 * PAIR]
        a_r = jnp.where(first8, g_a, g_b) - b_r + gi
        max_a = jnp.max(jnp.where(first8, a_r, neg_inf), axis=1, keepdims=True)
        max_b = jnp.max(jnp.where(first8, neg_inf, a_r), axis=1, keepdims=True)
        m_a = jnp.maximum(g_a + m_prev, max_a)
        m_b = jnp.maximum(g_b + m_a, max_b)
        dec_a = jnp.exp(g_a + m_prev - m_a)
        dec_b = jnp.exp(g_b + m_a - m_b)
        w_r = jnp.exp(a_r - jnp.where(first8, m_a, m_b))
        e_r = b_r + jnp.where(first8, m_prev, m_a)
        lc_hi, lc_lo = _split_hi_lo(_log_sigmoid(gcf_ref[pl.ds(r0, PAIR), :]))
        x_c = _dot(l2_ref[...], lc_hi) + _dot(l2_ref[...], lc_lo) - gci_ref[pl.ds(r0, PAIR), :]

        heads = range(N_HEADS)
        xk, xq, vt, cn0, up_a, kq, cq_a, m_out, decay_t = [], [], [], [], [], [], [], [], []
        for hd in heads:
            xh = x[:, hd * LANES:(hd + 1) * LANES]
            xs = pltpu.roll(xh, MLSTM_DK, 1)
            xk.append(jnp.where(upper, xh * (MLSTM_DK ** -0.5), 0.0).astype(BF16))
            xq.append(jnp.where(upper, xs, 0.0).astype(BF16))
            vt.append(vt_ref[hd * V_ROWS:(hd + 1) * V_ROWS, pl.ds(r0, PAIR)])
            cn0.append(cn_ref[hd])
            w_a = jnp.where(first, w_r[hd:hd + 1, :], 0.0)
            up_a.append(_dot((vt[hd].astype(F32) * w_a).astype(BF16), xk[hd]))
            kq.append(_dot_nt(xk[hd], xq[hd]))
            cq_a.append(_dot_nt(cn0[hd].astype(BF16), xq[hd]))
            dmat = b_r[hd:hd + 1, :] - x_c[:, hd:hd + 1]
            dmat = jnp.where(key_t <= qry_t, jnp.where(key_t >= chunk_start, dmat, neg_inf), neg_inf)
            m_out.append(jnp.maximum(e_r[hd:hd + 1, :], jnp.max(dmat, axis=0, keepdims=True)))
            decay_t.append(jnp.exp(dmat - m_out[hd]))
        cn1, cq_b, up_b = [], [], []
        for hd in heads:
            cn1.append(dec_a[hd:hd + 1, :] * cn0[hd] + up_a[hd])
            w_b = jnp.where(first, 0.0, w_r[hd:hd + 1, :])
            up_b.append(_dot((vt[hd].astype(F32) * w_b).astype(BF16), xk[hd]))
            cq_b.append(_dot_nt(cn1[hd].astype(BF16), xq[hd]))
        sv = []
        for hd in heads:
            cn_ref[hd] = dec_b[hd:hd + 1, :] * cn1[hd] + up_b[hd]
            sv.append(_dot(vt[hd], (kq[hd] * decay_t[hd]).astype(BF16)))
        for hd in heads:
            rows = slice(hd * MLSTM_DV, (hd + 1) * MLSTM_DV)
            inter = jnp.exp(e_r[hd:hd + 1, :] - m_out[hd])
            tot = inter * jnp.where(first, cq_a[hd], cq_b[hd]) + sv[hd]
            den = tot[MLSTM_DV:MLSTM_DV + 1, :]
            hh = tot[0:MLSTM_DV, :] / jnp.maximum(jnp.abs(den), jnp.exp(-m_out[hd]))
            hn = hh * lax.rsqrt(jnp.mean(hh * hh, axis=0, keepdims=True) + EPS) * ng_ref[...]
            og = ot_ref[rows, pl.ds(r0, PAIR)]
            out_ref[rows, pl.ds(r0, PAIR)] = (hn * _sigmoid(og)).astype(BF16)
        return m_b

    lax.fori_loop(0, seq // PAIR, pair_step, jnp.zeros((SUBLANES, PAIR), F32))


def _mlstm(qk, vmt, omt, gri, grf, gci, gcf, cw, ng, ug, l2, batch, seq):
    const = lambda arr: pl.BlockSpec(arr.shape, lambda b: (0,) * arr.ndim)
    in_specs = [
        pl.BlockSpec((seq, GROUP_W), lambda b: (b, 0)),
        pl.BlockSpec((N_HEADS * V_ROWS, seq), lambda b: (0, b)),
        pl.BlockSpec((GROUP_W, seq), lambda b: (0, b)),
        pl.BlockSpec((seq // PAIR, SUBLANES, PAIR), lambda b: (b, 0, 0)),
        pl.BlockSpec((seq // PAIR, SUBLANES, PAIR), lambda b: (b, 0, 0)),
        pl.BlockSpec((seq, N_GATE_ROWS), lambda b: (b, 0)),
        pl.BlockSpec((seq, N_GATE_ROWS), lambda b: (b, 0)),
        const(cw), const(ng), const(ug), const(l2),
    ]
    return pl.pallas_call(
        _mlstm_kernel, grid=(batch,), in_specs=in_specs,
        out_specs=pl.BlockSpec((GROUP_W, seq), lambda b: (0, b)),
        out_shape=jax.ShapeDtypeStruct((GROUP_W, batch * seq), BF16),
        scratch_shapes=[
            pltpu.VMEM((seq + SUBLANES, GROUP_W), F32),
            pltpu.VMEM((N_HEADS, V_ROWS, LANES), F32),
        ],
        name="mlstm",
        compiler_params=pltpu.CompilerParams(dimension_semantics=("arbitrary",), vmem_limit_bytes=VMEM_LIMIT),
    )(qk, vmt, omt, gri, grf, gci, gcf, cw, ng, ug, l2)


def _attn_kernel(lam_init, tbl_ref, kmax_ref, qmin_ref, q_ref, k_ref, vt_ref, pcol_ref, prow_ref,
                 lq1_ref, lk1_ref, lq2_ref, lk2_ref, sg_ref, out_ref, bias_ref, qc_ref, s_ref, p_ref, acc_ref):
    hd = pl.program_id(0)
    qi = pl.program_id(1)
    b = pl.program_id(2)
    n_kv = (qi + 1) * (TQ // TK)

    @pl.when(b == 0)
    def _build_bias():
        rowi = lax.broadcasted_iota(jnp.int32, (BIAS_ROWS, BIAS_COLS), 0)
        coli = lax.broadcasted_iota(jnp.int32, (BIAS_ROWS, BIAS_COLS), 1)
        row = lambda c, v: tbl_ref[c * N_BUCKETS + v:c * N_BUCKETS + v + 1, :]
        last = N_BUCKETS - 1

        def build(t, _):
            r0 = pl.multiple_of(t * BIAS_ROWS, BIAS_ROWS)
            pk = pcol_ref[pl.ds(r0, BIAS_ROWS), :]
            kmax = kmax_ref[t]
            for j in range(TQ // BIAS_COLS):
                lanes = slice(j * BIAS_COLS, (j + 1) * BIAS_COLS)
                q0 = qi * TQ + j * BIAS_COLS
                causal = rowi + r0 <= coli + q0
                flat = jnp.logical_or(qmin_ref[qi * (TQ // BIAS_COLS) + j] - kmax >= _BUCKET_THR[-1],
                                      r0 > q0 + BIAS_COLS - 1)

                @pl.when(flat)
                def _flat():
                    bias_ref[0, pl.ds(r0, BIAS_ROWS), lanes] = jnp.where(causal, row(0, last), NEG_BIG)
                    bias_ref[1, pl.ds(r0, BIAS_ROWS), lanes] = jnp.where(causal, row(1, last), NEG_BIG)

                @pl.when(jnp.logical_not(flat))
                def _lookup():
                    dist = jnp.maximum(prow_ref[:, lanes] - pk, 0)
                    b0 = jnp.broadcast_to(row(0, 0), (BIAS_ROWS, BIAS_COLS))
                    b1 = jnp.broadcast_to(row(1, 0), (BIAS_ROWS, BIAS_COLS))
                    for v, thr in enumerate(_BUCKET_THR, start=1):
                        ge = dist >= thr
                        b0 = jnp.where(ge, row(0, v), b0)
                        b1 = jnp.where(ge, row(1, v), b1)
                    bias_ref[0, pl.ds(r0, BIAS_ROWS), lanes] = jnp.where(causal, b0, NEG_BIG)
                    bias_ref[1, pl.ds(r0, BIAS_ROWS), lanes] = jnp.where(causal, b1, NEG_BIG)
            return 0

        lax.fori_loop(0, n_kv * (TK // BIAS_ROWS), build, 0)

    seq = k_ref.shape[1]
    lane = lax.broadcasted_iota(jnp.int32, (TQ, LANES), 1)
    for e in range(ATTN_NB):
        q = q_ref[e]
        zero = jnp.zeros_like(q)
        qc_ref[e, 0] = jnp.where(lane < DIFF_D, q, zero)
        qc_ref[e, 1] = jnp.where(lane >= DIFF_D, q, zero)
    acc_ref[...] = jnp.zeros_like(acc_ref)

    chains = [(e, c) for e in range(ATTN_NB) for c in range(2)]

    def scores(t, slot):
        k0 = pl.multiple_of(t * TK, TK)
        for e in range(ATTN_NB):
            kb = k_ref[e, pl.ds(k0, TK), :]
            for c in range(2):
                s_ref[slot, e, c] = _dot_nt(kb, qc_ref[e, c]) + bias_ref[c, pl.ds(k0, TK), :]

    def numerators(m_state, slot, lanes=slice(None)):
        m_next, alpha_next = [], []
        for i, (e, c) in enumerate(chains):
            m_prev = m_state[i][:, lanes]
            m_new = jnp.maximum(m_prev, jnp.max(s_ref[slot, e, c, :, lanes], axis=0, keepdims=True))
            alpha_next.append(jnp.exp2(m_prev - m_new))
            m_next.append(m_new)
        for i, (e, c) in enumerate(chains):
            p_ref[slot, e, c, :, lanes] = jnp.exp2(s_ref[slot, e, c, :, lanes] - m_next[i]).astype(BF16)
        return tuple(m_next), tuple(alpha_next)

    def accumulate(t, alpha, slot, lanes=slice(None)):
        for e in range(ATTN_NB):
            v0 = pl.multiple_of(e * seq + t * TK, TK)
            vt = vt_ref[:, pl.ds(v0, TK)]
            for c in range(2):
                acc_ref[e, c, :, lanes] = (alpha[2 * e + c] * acc_ref[e, c, :, lanes]
                                           + _dot(vt, p_ref[slot, e, c, :, lanes]))

    def kv_pair(i, carry):
        t = 2 * i
        m_state, alpha = carry
        scores(t, 0)
        m_state, alpha1 = numerators(m_state, 1)
        accumulate(t - 2, alpha, 0)
        scores(t + 1, 1)
        carry = numerators(m_state, 0)
        accumulate(t - 1, alpha1, 1)
        return carry

    assert (TQ // TK) % 2 == 0
    scores(0, 0)
    scores(1, 1)
    carry = numerators(tuple(jnp.full((1, TQ), M_INIT, F32) for _ in chains), 0)
    m_state, alpha = lax.fori_loop(1, n_kv // 2, kv_pair, carry)
    live = slice(TQ - TK, TQ)
    _, alpha1 = numerators(m_state, 1, live)
    accumulate(n_kv - 2, alpha, 0)
    accumulate(n_kv - 1, alpha1, 1, live)

    lam = (jnp.exp(jnp.sum(lq1_ref[...] * lk1_ref[...], axis=1, keepdims=True))
           - jnp.exp(jnp.sum(lq2_ref[...] * lk2_ref[...], axis=1, keepdims=True)) + lam_init)
    for e in range(ATTN_NB):
        a0 = acc_ref[e, 0]
        a1 = acc_ref[e, 1]
        o = a0[0:DIFF_DV] / a0[DIFF_DV:DIFF_DV + 1] - lam * (a1[0:DIFF_DV] / a1[DIFF_DV:DIFF_DV + 1])
        on = o * lax.rsqrt(jnp.mean(o * o, axis=0, keepdims=True) + EPS) * sg_ref[...]
        out_ref[e] = (on * (1.0 - lam_init)).astype(BF16)


def _attention(lam_init, tbl, qd, kd, vdt, pcol, prow, lq1, lk1, lq2, lk2, sg, batch, seq):
    q3, k3 = (t.reshape(batch, seq, GROUP_W) for t in (qd, kd))
    vec = lambda n: pl.BlockSpec((1, n), lambda h, i, b: (0, 0))
    kmax = jnp.max(prow.reshape(seq // BIAS_ROWS, BIAS_ROWS), axis=1)
    qmin = jnp.min(prow.reshape(seq // BIAS_COLS, BIAS_COLS), axis=1)
    in_specs = [
        pl.BlockSpec((None, 2 * N_BUCKETS, BIAS_COLS), lambda h, i, b: (h, 0, 0)),
        pl.BlockSpec(memory_space=pltpu.SMEM),
        pl.BlockSpec(memory_space=pltpu.SMEM),
        pl.BlockSpec((ATTN_NB, TQ, LANES), lambda h, i, b: (b, i, h)),
        pl.BlockSpec((ATTN_NB, seq, LANES), lambda h, i, b: (b, 0, h)),
        pl.BlockSpec((V_ROWS, ATTN_NB * seq), lambda h, i, b: (h, b)),
        pl.BlockSpec((seq, BIAS_COLS), lambda h, i, b: (0, 0)),
        pl.BlockSpec((1, TQ), lambda h, i, b: (0, i)),
        vec(DIFF_D), vec(DIFF_D), vec(DIFF_D), vec(DIFF_D),
        pl.BlockSpec((DIFF_DV, TQ), lambda h, i, b: (0, 0)),
    ]
    out = pl.pallas_call(
        functools.partial(_attn_kernel, lam_init),
        grid=(N_HEADS, seq // TQ, batch // ATTN_NB), in_specs=in_specs,
        out_specs=pl.BlockSpec((ATTN_NB, DIFF_DV, TQ), lambda h, i, b: (b, h, i)),
        out_shape=jax.ShapeDtypeStruct((batch, GROUP_W, seq), BF16),
        scratch_shapes=[
            pltpu.VMEM((2, seq, TQ), F32),
            pltpu.VMEM((ATTN_NB, 2, TQ, LANES), BF16),
            pltpu.VMEM((2, ATTN_NB, 2, TK, TQ), F32),
            pltpu.VMEM((2, ATTN_NB, 2, TK, TQ), BF16),
            pltpu.VMEM((ATTN_NB, 2, V_ROWS, TQ), F32),
        ],
        name="diff_attn",
        compiler_params=pltpu.CompilerParams(
            dimension_semantics=("arbitrary", "arbitrary", "arbitrary"), vmem_limit_bytes=VMEM_LIMIT),
    )(tbl, kmax, qmin, q3, k3, vdt, pcol, prow, lq1, lk1, lq2, lk2, jnp.broadcast_to(sg.reshape(DIFF_DV, 1), (DIFF_DV, TQ)))
    return out


def _gelu(x):
    return 0.5 * x * (1.0 + lax.erf(x * (2.0 ** -0.5)))


def _rms(x, g):
    return x * lax.rsqrt(jnp.mean(x * x, axis=-1, keepdims=True) + EPS) * g


def _mixer_kernel(h_ref, hmt_ref, hdt_ref, p_ref, wo_ref, gf_ref, wu_ref, cw_ref, cb_ref, wd_ref, gp_ref,
                  wpg_ref, wpp_ref, out_ref, sg_ref, sv_ref, cg_ref, cv_ref, acc_ref, u_ref):
    tm = h_ref.shape[0]
    halo = SUBLANES

    @pl.when(pl.program_id(1) == 0)
    def _reset_conv_history():
        cg_ref[...] = jnp.zeros_like(cg_ref)
        cv_ref[...] = jnp.zeros_like(cv_ref)

    h1 = h_ref[...] + _dot_tn(jnp.concatenate([hmt_ref[...], hdt_ref[...]], axis=0), wo_ref[...])
    u_ref[...] = _rms(h1, gf_ref[...]).astype(BF16)
    acc_ref[...] = jnp.zeros_like(acc_ref)

    def up_stage(j, slot):
        c0 = pl.multiple_of(j * FF_CHUNK, FF_CHUNK)
        sg_ref[slot, halo:, :] = _dot(u_ref[...], wu_ref[:, pl.ds(c0, FF_CHUNK)])
        sv_ref[slot, halo:, :] = _dot(u_ref[...], wu_ref[:, pl.ds(D_FF + c0, FF_CHUNK)])

    def conv_branch(j, slot, c0, stage_ref, hist_ref):
        stage_ref[slot, 0:halo, :] = hist_ref[j]
        hist_ref[j] = stage_ref[slot, tm:tm + halo, :]
        cw = cw_ref[:, pl.ds(c0, FF_CHUNK)]
        return (cw[0:1, :] * stage_ref[slot, halo - 2:halo - 2 + tm, :]
                + cw[1:2, :] * stage_ref[slot, halo - 1:halo - 1 + tm, :]
                + cw[2:3, :] * stage_ref[slot, halo:, :] + cb_ref[:, pl.ds(c0, FF_CHUNK)])

    def act_stage(j, slot):
        c0 = pl.multiple_of(j * FF_CHUNK, FF_CHUNK)
        gate = conv_branch(j, slot, c0, sg_ref, cg_ref)
        val = conv_branch(j, slot, D_FF + c0, sv_ref, cv_ref)
        act = (_gelu(gate) * val).astype(BF16)
        acc_ref[...] += _dot(act, wd_ref[j])

    def chunk_pair(i, _):
        j = 2 * i
        up_stage(j + 1, 1)
        act_stage(j, 0)
        up_stage(j + 2, 0)
        act_stage(j + 1, 1)
        return 0

    assert N_FF_CHUNKS % 2 == 1
    up_stage(0, 0)
    lax.fori_loop(0, N_FF_CHUNKS // 2, chunk_pair, 0)
    act_stage(N_FF_CHUNKS - 1, 0)

    h2 = h1 + acc_ref[...]
    u3 = _rms(h2, gp_ref[...]).astype(BF16)
    ple_gate = _sigmoid(_dot(u3, wpg_ref[...]))
    out_ref[...] = h2 + ple_gate * _dot(p_ref[...].astype(BF16), wpp_ref[...])


def _mixer(layer, h, hmt, hdt, p, wo, gf, wu, cw, cb, wd, gp, wpg, wpp, batch, seq):
    tiles = seq // TM_FFN
    tile = lambda width: pl.BlockSpec((TM_FFN, width), lambda b, t: (b * tiles + t, 0))

    def of_layer(arr):
        nd = arr.ndim
        return pl.BlockSpec((None,) + arr.shape[1:], lambda b, t: (layer,) + (0,) * (nd - 1),
                            pipeline_mode=pl.Buffered(1))

    weights = (wo, gf, wu, cw, cb, wd, gp, wpg, wpp)
    in_specs = [
        tile(D_MODEL),
        pl.BlockSpec((GROUP_W, TM_FFN), lambda b, t: (0, b * tiles + t)),
        pl.BlockSpec((None, GROUP_W, TM_FFN), lambda b, t: (b, 0, t)),
        pl.BlockSpec((None, TM_FFN, PLE_DIM), lambda b, t: (layer, b * tiles + t, 0)),
    ] + [of_layer(w) for w in weights]
    return pl.pallas_call(
        _mixer_kernel, grid=(batch, tiles), in_specs=in_specs, out_specs=tile(D_MODEL),
        out_shape=jax.ShapeDtypeStruct(h.shape, F32),
        scratch_shapes=[
            pltpu.VMEM((2, TM_FFN + SUBLANES, FF_CHUNK), F32),
            pltpu.VMEM((2, TM_FFN + SUBLANES, FF_CHUNK), F32),
            pltpu.VMEM((N_FF_CHUNKS, SUBLANES, FF_CHUNK), F32),
            pltpu.VMEM((N_FF_CHUNKS, SUBLANES, FF_CHUNK), F32),
            pltpu.VMEM((TM_FFN, D_MODEL), F32),
            pltpu.VMEM((TM_FFN, D_MODEL), BF16),
        ],
        name="mixer",
        compiler_params=pltpu.CompilerParams(
            dimension_semantics=("arbitrary", "arbitrary"), vmem_limit_bytes=VMEM_LIMIT),
    )(h, hmt, hdt, p, *weights)


def _head_interleave(qcols, kcols):
    lead = qcols.shape[:-1]
    qh = qcols.reshape(lead + (N_HEADS, MLSTM_DK))
    kh = kcols.reshape(lead + (N_HEADS, MLSTM_DK))
    return jnp.concatenate([qh, kh], axis=-1).reshape(lead + (GROUP_W,))


def kernel(x, p, positions, rel_bias, ln_mix_g, w_in, mlstm_conv_w, b_igate, b_fgate, mlstm_norm_g, q_norm_g, k_norm_g, lam_q1, lam_k1, lam_q2, lam_k2, diff_subln_g, w_out, ln_ffn_g, w_up, ffn_conv_w, ffn_conv_b, w_down, ln_ple_g, w_ple_gate, w_ple_proj):
    batch, seq, _ = x.shape
    depth = w_in.shape[0]
    n_tok = batch * seq
    qk_cols = N_HEADS * MLSTM_DK
    col_sizes = [qk_cols, qk_cols, GROUP_W, GROUP_W, N_HEADS, N_HEADS, GROUP_W, GROUP_W, GROUP_W]
    offs = np.concatenate([[0], np.cumsum(col_sizes)])
    sl = lambda a, j: a[..., int(offs[j]):int(offs[j + 1])]

    tbl = jnp.broadcast_to(
        (jnp.transpose(rel_bias.astype(F32), (1, 2, 0)) * LOG2E).reshape(N_HEADS, 2 * N_BUCKETS, 1),
        (N_HEADS, 2 * N_BUCKETS, BIAS_COLS))
    pcol = jnp.broadcast_to(positions.astype(jnp.int32).reshape(seq, 1), (seq, BIAS_COLS))
    prow = positions.astype(jnp.int32).reshape(1, seq)
    gsum = jnp.asarray(np.kron(np.eye(MXU_TILE // DIFF_D), np.ones((DIFF_D, DIFF_D))), BF16)
    t_idx = np.arange(PAIR)
    same_chunk = (t_idx[:, None] // CHUNK) == (t_idx[None, :] // CHUNK)
    prefix = same_chunk & (t_idx[:, None] <= t_idx[None, :])
    total_a = np.broadcast_to(t_idx[:, None] < CHUNK, (PAIR, PAIR))
    ug = jnp.asarray(np.concatenate([prefix, total_a, ~total_a], axis=1), BF16)
    l2 = jnp.asarray(prefix.T, BF16)

    p_tok = p.reshape(depth, n_tok, PLE_DIM)
    row = lambda a: a.astype(F32).reshape(depth, 1, a.shape[-1])
    mixer_weights = (
        w_out.astype(BF16), row(ln_ffn_g), w_up.astype(BF16), ffn_conv_w.astype(F32), row(ffn_conv_b),
        w_down.astype(BF16).reshape(depth, N_FF_CHUNKS, FF_CHUNK, D_MODEL), row(ln_ple_g),
        w_ple_gate.astype(BF16), w_ple_proj.astype(BF16))

    h = x.reshape(n_tok, D_MODEL)
    for i in range(depth):
        wi = w_in[i]
        wm = jnp.concatenate([_head_interleave(sl(wi, 0), sl(wi, 1)), sl(wi, 6), sl(wi, 7)], axis=-1).astype(BF16)
        wt = jnp.concatenate([sl(wi, 2), sl(wi, 3), sl(wi, 8)], axis=-1).T.astype(BF16)
        pad_rows = lambda a: jnp.zeros((N_GATE_ROWS,) + a.shape[1:], F32).at[:N_HEADS].set(a.astype(F32))
        wg = jnp.concatenate([pad_rows(sl(wi, 4).T), pad_rows(sl(wi, 5).T)], axis=0).astype(BF16)
        gate_bias = jnp.concatenate([pad_rows(b_igate[i]), pad_rows(b_fgate[i])])
        qg = jnp.tile(q_norm_g[i].astype(F32), GROUP_W // DIFF_D).reshape(1, GROUP_W) * (DIFF_D ** -0.5 * LOG2E)
        kg = jnp.tile(k_norm_g[i].astype(F32), GROUP_W // DIFF_D).reshape(1, GROUP_W)
        qk, vmt, omt, gri, grf, gci, gcf, qd, kd, vdt = _inproj(
            h, ln_mix_g[i].reshape(1, D_MODEL), wm, wt, wg, gate_bias.reshape(1, -1), gate_bias.reshape(-1, 1),
            gsum, qg, kg)

        cw = _head_interleave(mlstm_conv_w[i][:, :qk_cols], mlstm_conv_w[i][:, qk_cols:]).astype(F32)
        ng = jnp.broadcast_to(mlstm_norm_g[i].astype(F32).reshape(MLSTM_DV, 1), (MLSTM_DV, PAIR))
        hmt = _mlstm(qk, vmt, omt, gri, grf, gci, gcf, cw, ng, ug, l2, batch, seq)

        lam_init = 0.8 - 0.6 * math.exp(-0.3 * i)
        row64 = lambda a: a[i].reshape(1, DIFF_D).astype(F32)
        hdt = _attention(lam_init, tbl, qd, kd, vdt, pcol, prow, row64(lam_q1), row64(lam_k1), row64(lam_q2),
                        row64(lam_k2), diff_subln_g[i].reshape(1, DIFF_DV).astype(F32), batch, seq)

        h = _mixer(i, h, hmt, hdt, p_tok, *mixer_weights, batch, seq)
    return h.reshape(batch, seq, D_MODEL)
```

```python
import functools
import math

import numpy as np
import jax
import jax.numpy as jnp
from jax import lax
from jax.experimental import pallas as pl
from jax.experimental.pallas import tpu as pltpu

F32 = jnp.float32
BF16 = jnp.bfloat16

LANES = 128
SUBLANES = 8
MXU_TILE = 256
V7X_VMEM_BYTES = 64 * 1024 * 1024

D_MODEL = 1024
N_HEADS = 4
MLSTM_DK = 64
MLSTM_DV = 128
MLSTM_CONV_W = 4
CHUNK = 64
PAIR = 2 * CHUNK
DIFF_D = 64
DIFF_DV = 128
N_BUCKETS = 32
MAX_DISTANCE = 128
D_FF = 2816
FFN_CONV_W = 3
PLE_DIM = 256
EPS = 1e-6
GROUP_W = N_HEADS * LANES
N_GATE_ROWS = 16

TM_IN = 512
TM_FFN = 512
FF_CHUNK = 256
N_FF_CHUNKS = D_FF // FF_CHUNK
TQ = 512
TK = 256
BIAS_ROWS = 32
BIAS_COLS = LANES
ATTN_NB = 2
NEG_BIG = -1e30
M_INIT = -1e29
ONES_ROWS = 16
V_ROWS = DIFF_DV + ONES_ROWS
LOG2E = math.log2(math.e)
VMEM_LIMIT = V7X_VMEM_BYTES * 7 // 8


def _dot(a, b):
    return jnp.dot(a, b, preferred_element_type=F32)


def _dot_nt(a, b):
    return lax.dot_general(a, b, (((1,), (1,)), ((), ())), preferred_element_type=F32)


def _dot_tn(a, b):
    return lax.dot_general(a, b, (((0,), (0,)), ((), ())), preferred_element_type=F32)


def _sigmoid(x):
    return 1.0 / (1.0 + jnp.exp(-x))


def _log_sigmoid(x):
    return jnp.minimum(x, 0.0) - jnp.log1p(jnp.exp(-jnp.abs(x)))


def _bucket_thresholds():
    max_exact = N_BUCKETS // 2
    thr = []
    for v in range(1, N_BUCKETS):
        if v <= max_exact:
            thr.append(v)
            continue
        edge = max_exact * (MAX_DISTANCE / max_exact) ** ((v - max_exact) / (N_BUCKETS - max_exact))
        assert abs(edge - round(edge)) > 1e-3, edge
        thr.append(int(math.ceil(edge)))
    assert all(a < b for a, b in zip(thr, thr[1:])), thr
    return tuple(thr)


_BUCKET_THR = _bucket_thresholds()


def _group_mean_square(z, gsum_ref):
    sq = (z * z).astype(BF16)
    width = gsum_ref.shape[0]
    sums = [_dot(sq[:, c0:c0 + width], gsum_ref[...]) for c0 in range(0, z.shape[1], width)]
    return jnp.concatenate(sums, axis=1) * (1.0 / DIFF_D)


def _with_ones_rows(dst_ref, vt):
    for hh in range(N_HEADS):
        dst_ref[hh * V_ROWS:hh * V_ROWS + DIFF_DV, :] = vt[hh * DIFF_DV:(hh + 1) * DIFF_DV, :]
        dst_ref[hh * V_ROWS + DIFF_DV:(hh + 1) * V_ROWS, :] = jnp.ones((ONES_ROWS, vt.shape[1]), BF16)


def _inproj_kernel(h_ref, g_ref, wm_ref, wt_ref, wg_ref, bcol_ref, brow_ref, gsum_ref, qg_ref, kg_ref,
                   qk_ref, vmt_ref, omt_ref, gri_ref, grf_ref, gci_ref, gcf_ref, qd_ref, kd_ref, vdt_ref):
    x = h_ref[...]
    ms = jnp.mean(x * x, axis=-1, keepdims=True)
    u = (x * lax.rsqrt(ms + EPS) * g_ref[...]).astype(BF16)
    w = GROUP_W
    qk_ref[...] = _dot(u, wm_ref[:, 0:w])
    zq = _dot(u, wm_ref[:, w:2 * w])
    qd_ref[...] = (zq * lax.rsqrt(_group_mean_square(zq, gsum_ref) + EPS) * qg_ref[...]).astype(BF16)
    zk = _dot(u, wm_ref[:, 2 * w:3 * w])
    kd_ref[...] = (zk * lax.rsqrt(_group_mean_square(zk, gsum_ref) + EPS) * kg_ref[...]).astype(BF16)
    _with_ones_rows(vmt_ref, _dot_nt(wt_ref[0:w, :], u).astype(BF16))
    omt_ref[...] = _dot_nt(wt_ref[w:2 * w, :], u)
    _with_ones_rows(vdt_ref, _dot_nt(wt_ref[2 * w:3 * w, :], u).astype(BF16))
    gc = _dot_nt(u, wg_ref[...]) + bcol_ref[...]
    gci_ref[...] = gc[:, 0:N_GATE_ROWS]
    gcf_ref[...] = gc[:, N_GATE_ROWS:2 * N_GATE_ROWS]
    gr = _dot_nt(wg_ref[...], u) + brow_ref[...]
    for ci in range(TM_IN // PAIR):
        gri_ref[ci] = gr[0:SUBLANES, ci * PAIR:(ci + 1) * PAIR]
        grf_ref[ci] = gr[N_GATE_ROWS:N_GATE_ROWS + SUBLANES, ci * PAIR:(ci + 1) * PAIR]


def _inproj(h, g, wm, wt, wg, bcol, brow, gsum, qg, kg):
    n_tok = h.shape[0]
    grid = (n_tok // TM_IN,)
    const = lambda shape: pl.BlockSpec(shape, lambda i: (0,) * len(shape))
    tile = lambda width: pl.BlockSpec((TM_IN, width), lambda i: (i, 0))
    tile_t = lambda rows: pl.BlockSpec((rows, TM_IN), lambda i: (0, i))
    gate_rows = pl.BlockSpec((TM_IN // PAIR, SUBLANES, PAIR), lambda i: (i, 0, 0))
    out_shape = (
        jax.ShapeDtypeStruct((n_tok, GROUP_W), F32),
        jax.ShapeDtypeStruct((N_HEADS * V_ROWS, n_tok), BF16),
        jax.ShapeDtypeStruct((GROUP_W, n_tok), F32),
        jax.ShapeDtypeStruct((n_tok // PAIR, SUBLANES, PAIR), F32),
        jax.ShapeDtypeStruct((n_tok // PAIR, SUBLANES, PAIR), F32),
        jax.ShapeDtypeStruct((n_tok, N_GATE_ROWS), F32),
        jax.ShapeDtypeStruct((n_tok, N_GATE_ROWS), F32),
        jax.ShapeDtypeStruct((n_tok, GROUP_W), BF16),
        jax.ShapeDtypeStruct((n_tok, GROUP_W), BF16),
        jax.ShapeDtypeStruct((N_HEADS * V_ROWS, n_tok), BF16),
    )
    out_specs = (
        tile(GROUP_W), tile_t(N_HEADS * V_ROWS), tile_t(GROUP_W), gate_rows, gate_rows,
        tile(N_GATE_ROWS), tile(N_GATE_ROWS), tile(GROUP_W), tile(GROUP_W), tile_t(N_HEADS * V_ROWS),
    )
    in_specs = [
        tile(D_MODEL), const((1, D_MODEL)), const(wm.shape), const(wt.shape), const(wg.shape),
        const(bcol.shape), const(brow.shape), const(gsum.shape),
        const((1, GROUP_W)), const((1, GROUP_W)),
    ]
    return pl.pallas_call(
        _inproj_kernel, grid=grid, in_specs=in_specs, out_specs=out_specs, out_shape=out_shape,
        name="inproj",
        compiler_params=pltpu.CompilerParams(dimension_semantics=("arbitrary",), vmem_limit_bytes=VMEM_LIMIT),
    )(h, g, wm, wt, wg, bcol, brow, gsum, qg, kg)


def _split_hi_lo(x):
    hi = x.astype(BF16)
    return hi, (x - hi.astype(F32)).astype(BF16)


def _mlstm_kernel(qk_ref, vt_ref, ot_ref, gri_ref, grf_ref, gci_ref, gcf_ref, cw_ref, ng_ref, ug_ref, l2_ref,
                  out_ref, xpad_ref, cn_ref):
    seq = qk_ref.shape[0]
    halo = SUBLANES
    xpad_ref[0:halo, :] = jnp.zeros((halo, GROUP_W), F32)
    xpad_ref[halo:, :] = qk_ref[...]
    cn_ref[...] = jnp.zeros_like(cn_ref)

    first8 = lax.broadcasted_iota(jnp.int32, (SUBLANES, PAIR), 1) < CHUNK
    first = lax.broadcasted_iota(jnp.int32, (1, PAIR), 1) < CHUNK
    upper = lax.broadcasted_iota(jnp.int32, (PAIR, PAIR), 1) >= MLSTM_DK
    key_t = lax.broadcasted_iota(jnp.int32, (PAIR, PAIR), 0)
    qry_t = lax.broadcasted_iota(jnp.int32, (PAIR, PAIR), 1)
    chunk_start = jnp.where(qry_t < CHUNK, 0, CHUNK)
    cw = cw_ref[...]
    neg_inf = jnp.float32(-jnp.inf)

    def pair_step(c2, m_prev):
        r0 = pl.multiple_of(c2 * PAIR, PAIR)
        win = xpad_ref[pl.ds(r0, PAIR + halo), :]
        conv = cw[0:1, :] * win[halo - 3:halo - 3 + PAIR, :]
        for j in range(1, MLSTM_CONV_W):
            conv = conv + cw[j:j + 1, :] * win[halo - 3 + j:halo - 3 + j + PAIR, :]
        x = conv * _sigmoid(conv)

        gi = gri_ref[c2]
        lf_hi, lf_lo = _split_hi_lo(_log_sigmoid(grf_ref[c2]))
        bg = _dot(lf_hi, ug_ref[...]) + _dot(lf_lo, ug_ref[...])
        b_r = bg[:, 0:PAIR]
        g_a = bg[:, PAIR:2 * PAIR]
        g_b = bg[:, 2 * PAIR:3 * PAIR]
        a_r = jnp.where(first8, g_a, g_b) - b_r + gi
        max_a = jnp.max(jnp.where(first8, a_r, neg_inf), axis=1, keepdims=True)
        max_b = jnp.max(jnp.where(first8, neg_inf, a_r), axis=1, keepdims=True)
        m_a = jnp.maximum(g_a + m_prev, max_a)
        m_b = jnp.maximum(g_b + m_a, max_b)
        dec_a = jnp.exp(g_a + m_prev - m_a)
        dec_b = jnp.exp(g_b + m_a - m_b)
        w_r = jnp.exp(a_r - jnp.where(first8, m_a, m_b))
        e_r = b_r + jnp.where(first8, m_prev, m_a)
        lc_hi, lc_lo = _split_hi_lo(_log_sigmoid(gcf_ref[pl.ds(r0, PAIR), :]))
        x_c = _dot(l2_ref[...], lc_hi) + _dot(l2_ref[...], lc_lo) - gci_ref[pl.ds(r0, PAIR), :]

        heads = range(N_HEADS)
        xk, xq, vt, cn0, up_a, kq, cq_a, m_out, decay_t = [], [], [], [], [], [], [], [], []
        for hd in heads:
            xh = x[:, hd * LANES:(hd + 1) * LANES]
            xs = pltpu.roll(xh, MLSTM_DK, 1)
            xk.append(jnp.where(upper, xh * (MLSTM_DK ** -0.5), 0.0).astype(BF16))
            xq.append(jnp.where(upper, xs, 0.0).astype(BF16))
            vt.append(vt_ref[hd * V_ROWS:(hd + 1) * V_ROWS, pl.ds(r0, PAIR)])
            cn0.append(cn_ref[hd])
            w_a = jnp.where(first, w_r[hd:hd + 1, :], 0.0)
            up_a.append(_dot((vt[hd].astype(F32) * w_a).astype(BF16), xk[hd]))
            kq.append(_dot_nt(xk[hd], xq[hd]))
            cq_a.append(_dot_nt(cn0[hd].astype(BF16), xq[hd]))
            dmat = b_r[hd:hd + 1, :] - x_c[:, hd:hd + 1]
            dmat = jnp.where(key_t <= qry_t, jnp.where(key_t >= chunk_start, dmat, neg_inf), neg_inf)
            m_out.append(jnp.maximum(e_r[hd:hd + 1, :], jnp.max(dmat, axis=0, keepdims=True)))
            decay_t.append(jnp.exp(dmat - m_out[hd]))
        cn1, cq_b, up_b = [], [], []
        for hd in heads:
            cn1.append(dec_a[hd:hd + 1, :] * cn0[hd] + up_a[hd])
            w_b = jnp.where(first, 0.0, w_r[hd:hd + 1, :])
            up_b.append(_dot((vt[hd].astype(F32) * w_b).astype(BF16), xk[hd]))
            cq_b.append(_dot_nt(cn1[hd].astype(BF16), xq[hd]))
        sv = []
        for hd in heads:
            cn_ref[hd] = dec_b[hd:hd + 1, :] * cn1[hd] + up_b[hd]
            sv.append(_dot(vt[hd], (kq[hd] * decay_t[hd]).astype(BF16)))
        for hd in heads:
            rows = slice(hd * MLSTM_DV, (hd + 1) * MLSTM_DV)
            inter = jnp.exp(e_r[hd:hd + 1, :] - m_out[hd])
            tot = inter * jnp.where(first, cq_a[hd], cq_b[hd]) + sv[hd]
            den = tot[MLSTM_DV:MLSTM_DV + 1, :]
            hh = tot[0:MLSTM_DV, :] / jnp.maximum(jnp.abs(den), jnp.exp(-m_out[hd]))
            hn = hh * lax.rsqrt(jnp.mean(hh * hh, axis=0, keepdims=True) + EPS) * ng_ref[...]
            og = ot_ref[rows, pl.ds(r0, PAIR)]
            out_ref[rows, pl.ds(r0, PAIR)] = (hn * _sigmoid(og)).astype(BF16)
        return m_b

    lax.fori_loop(0, seq // PAIR, pair_step, jnp.zeros((SUBLANES, PAIR), F32))


def _mlstm(qk, vmt, omt, gri, grf, gci, gcf, cw, ng, ug, l2, batch, seq):
    const = lambda arr: pl.BlockSpec(arr.shape, lambda b: (0,) * arr.ndim)
    in_specs = [
        pl.BlockSpec((seq, GROUP_W), lambda b: (b, 0)),
        pl.BlockSpec((N_HEADS * V_ROWS, seq), lambda b: (0, b)),
        pl.BlockSpec((GROUP_W, seq), lambda b: (0, b)),
        pl.BlockSpec((seq // PAIR, SUBLANES, PAIR), lambda b: (b, 0, 0)),
        pl.BlockSpec((seq // PAIR, SUBLANES, PAIR), lambda b: (b, 0, 0)),
        pl.BlockSpec((seq, N_GATE_ROWS), lambda b: (b, 0)),
        pl.BlockSpec((seq, N_GATE_ROWS), lambda b: (b, 0)),
        const(cw), const(ng), const(ug), const(l2),
    ]
    return pl.pallas_call(
        _mlstm_kernel, grid=(batch,), in_specs=in_specs,
        out_specs=pl.BlockSpec((GROUP_W, seq), lambda b: (0, b)),
        out_shape=jax.ShapeDtypeStruct((GROUP_W, batch * seq), BF16),
        scratch_shapes=[
            pltpu.VMEM((seq + SUBLANES, GROUP_W), F32),
            pltpu.VMEM((N_HEADS, V_ROWS, LANES), F32),
        ],
        name="mlstm",
        compiler_params=pltpu.CompilerParams(dimension_semantics=("arbitrary",), vmem_limit_bytes=VMEM_LIMIT),
    )(qk, vmt, omt, gri, grf, gci, gcf, cw, ng, ug, l2)


def _attn_kernel(lam_init, tbl_ref, kmax_ref, qmin_ref, q_ref, k_ref, vt_ref, pcol_ref, prow_ref,
                 lq1_ref, lk1_ref, lq2_ref, lk2_ref, sg_ref, out_ref, bias_ref, qc_ref, s_ref, p_ref, acc_ref):
    hd = pl.program_id(0)
    qi = pl.program_id(1)
    b = pl.program_id(2)
    n_kv = (qi + 1) * (TQ // TK)

    @pl.when(b == 0)
    def _build_bias():
        rowi = lax.broadcasted_iota(jnp.int32, (BIAS_ROWS, BIAS_COLS), 0)
        coli = lax.broadcasted_iota(jnp.int32, (BIAS_ROWS, BIAS_COLS), 1)
        row = lambda c, v: tbl_ref[c * N_BUCKETS + v:c * N_BUCKETS + v + 1, :]
        last = N_BUCKETS - 1

        def build(t, _):
            r0 = pl.multiple_of(t * BIAS_ROWS, BIAS_ROWS)
            pk = pcol_ref[pl.ds(r0, BIAS_ROWS), :]
            kmax = kmax_ref[t]
            for j in range(TQ // BIAS_COLS):
                lanes = slice(j * BIAS_COLS, (j + 1) * BIAS_COLS)
                q0 = qi * TQ + j * BIAS_COLS
                causal = rowi + r0 <= coli + q0
                flat = jnp.logical_or(qmin_ref[qi * (TQ // BIAS_COLS) + j] - kmax >= _BUCKET_THR[-1],
                                      r0 > q0 + BIAS_COLS - 1)

                @pl.when(flat)
                def _flat():
                    bias_ref[0, pl.ds(r0, BIAS_ROWS), lanes] = jnp.where(causal, row(0, last), NEG_BIG)
                    bias_ref[1, pl.ds(r0, BIAS_ROWS), lanes] = jnp.where(causal, row(1, last), NEG_BIG)

                @pl.when(jnp.logical_not(flat))
                def _lookup():
                    dist = jnp.maximum(prow_ref[:, lanes] - pk, 0)
                    b0 = jnp.broadcast_to(row(0, 0), (BIAS_ROWS, BIAS_COLS))
                    b1 = jnp.broadcast_to(row(1, 0), (BIAS_ROWS, BIAS_COLS))
                    for v, thr in enumerate(_BUCKET_THR, start=1):
                        ge = dist >= thr
                        b0 = jnp.where(ge, row(0, v), b0)
                        b1 = jnp.where(ge, row(1, v), b1)
                    bias_ref[0, pl.ds(r0, BIAS_ROWS), lanes] = jnp.where(causal, b0, NEG_BIG)
                    bias_ref[1, pl.ds(r0, BIAS_ROWS), lanes] = jnp.where(causal, b1, NEG_BIG)
            return 0

        lax.fori_loop(0, n_kv * (TK // BIAS_ROWS), build, 0)

    seq = k_ref.shape[1]
    lane = lax.broadcasted_iota(jnp.int32, (TQ, LANES), 1)
    for e in range(ATTN_NB):
        q = q_ref[e]
        zero = jnp.zeros_like(q)
        qc_ref[e, 0] = jnp.where(lane < DIFF_D, q, zero)
        qc_ref[e, 1] = jnp.where(lane >= DIFF_D, q, zero)
    acc_ref[...] = jnp.zeros_like(acc_ref)

    chains = [(e, c) for e in range(ATTN_NB) for c in range(2)]

    def scores(t, slot):
        k0 = pl.multiple_of(t * TK, TK)
        for e in range(ATTN_NB):
            kb = k_ref[e, pl.ds(k0, TK), :]
            for c in range(2):
                s_ref[slot, e, c] = _dot_nt(kb, qc_ref[e, c]) + bias_ref[c, pl.ds(k0, TK), :]

    def numerators(m_state, slot, lanes=slice(None)):
        m_next, alpha_next = [], []
        for i, (e, c) in enumerate(chains):
            m_prev = m_state[i][:, lanes]
            m_new = jnp.maximum(m_prev, jnp.max(s_ref[slot, e, c, :, lanes], axis=0, keepdims=True))
            alpha_next.append(jnp.exp2(m_prev - m_new))
            m_next.append(m_new)
        for i, (e, c) in enumerate(chains):
            p_ref[slot, e, c, :, lanes] = jnp.exp2(s_ref[slot, e, c, :, lanes] - m_next[i]).astype(BF16)
        return tuple(m_next), tuple(alpha_next)

    def accumulate(t, alpha, slot, lanes=slice(None)):
        for e in range(ATTN_NB):
            v0 = pl.multiple_of(e * seq + t * TK, TK)
            vt = vt_ref[:, pl.ds(v0, TK)]
            for c in range(2):
                acc_ref[e, c, :, lanes] = (alpha[2 * e + c] * acc_ref[e, c, :, lanes]
                                           + _dot(vt, p_ref[slot, e, c, :, lanes]))

    def kv_pair(i, carry):
        t = 2 * i
        m_state, alpha = carry
        scores(t, 0)
        m_state, alpha1 = numerators(m_state, 1)
        accumulate(t - 2, alpha, 0)
        scores(t + 1, 1)
        carry = numerators(m_state, 0)
        accumulate(t - 1, alpha1, 1)
        return carry

    assert (TQ // TK) % 2 == 0
    scores(0, 0)
    scores(1, 1)
    carry = numerators(tuple(jnp.full((1, TQ), M_INIT, F32) for _ in chains), 0)
    m_state, alpha = lax.fori_loop(1, n_kv // 2, kv_pair, carry)
    live = slice(TQ - TK, TQ)
    _, alpha1 = numerators(m_state, 1, live)
    accumulate(n_kv - 2, alpha, 0)
    accumulate(n_kv - 1, alpha1, 1, live)

    lam = (jnp.exp(jnp.sum(lq1_ref[...] * lk1_ref[...], axis=1, keepdims=True))
           - jnp.exp(jnp.sum(lq2_ref[...] * lk2_ref[...], axis=1, keepdims=True)) + lam_init)
    for e in range(ATTN_NB):
        a0 = acc_ref[e, 0]
        a1 = acc_ref[e, 1]
        o = a0[0:DIFF_DV] / a0[DIFF_DV:DIFF_DV + 1] - lam * (a1[0:DIFF_DV] / a1[DIFF_DV:DIFF_DV + 1])
        on = o * lax.rsqrt(jnp.mean(o * o, axis=0, keepdims=True) + EPS) * sg_ref[...]
        out_ref[e] = (on * (1.0 - lam_init)).astype(BF16)


def _attention(lam_init, tbl, qd, kd, vdt, pcol, prow, lq1, lk1, lq2, lk2, sg, batch, seq):
    q3, k3 = (t.reshape(batch, seq, GROUP_W) for t in (qd, kd))
    vec = lambda n: pl.BlockSpec((1, n), lambda h, i, b: (0, 0))
    kmax = jnp.max(prow.reshape(seq // BIAS_ROWS, BIAS_ROWS), axis=1)
    qmin = jnp.min(prow.reshape(seq // BIAS_COLS, BIAS_COLS), axis=1)
    in_specs = [
        pl.BlockSpec((None, 2 * N_BUCKETS, BIAS_COLS), lambda h, i, b: (h, 0, 0)),
        pl.BlockSpec(memory_space=pltpu.SMEM),
        pl.BlockSpec(memory_space=pltpu.SMEM),
        pl.BlockSpec((ATTN_NB, TQ, LANES), lambda h, i, b: (b, i, h)),
        pl.BlockSpec((ATTN_NB, seq, LANES), lambda h, i, b: (b, 0, h)),
        pl.BlockSpec((V_ROWS, ATTN_NB * seq), lambda h, i, b: (h, b)),
        pl.BlockSpec((seq, BIAS_COLS), lambda h, i, b: (0, 0)),
        pl.BlockSpec((1, TQ), lambda h, i, b: (0, i)),
        vec(DIFF_D), vec(DIFF_D), vec(DIFF_D), vec(DIFF_D),
        pl.BlockSpec((DIFF_DV, TQ), lambda h, i, b: (0, 0)),
    ]
    out = pl.pallas_call(
        functools.partial(_attn_kernel, lam_init),
        grid=(N_HEADS, seq // TQ, batch // ATTN_NB), in_specs=in_specs,
        out_specs=pl.BlockSpec((ATTN_NB, DIFF_DV, TQ), lambda h, i, b: (b, h, i)),
        out_shape=jax.ShapeDtypeStruct((batch, GROUP_W, seq), BF16),
        scratch_shapes=[
            pltpu.VMEM((2, seq, TQ), F32),
            pltpu.VMEM((ATTN_NB, 2, TQ, LANES), BF16),
            pltpu.VMEM((2, ATTN_NB, 2, TK, TQ), F32),
            pltpu.VMEM((2, ATTN_NB, 2, TK, TQ), BF16),
            pltpu.VMEM((ATTN_NB, 2, V_ROWS, TQ), F32),
        ],
        name="diff_attn",
        compiler_params=pltpu.CompilerParams(
            dimension_semantics=("arbitrary", "arbitrary", "arbitrary"), vmem_limit_bytes=VMEM_LIMIT),
    )(tbl, kmax, qmin, q3, k3, vdt, pcol, prow, lq1, lk1, lq2, lk2, jnp.broadcast_to(sg.reshape(DIFF_DV, 1), (DIFF_DV, TQ)))
    return out


def _gelu(x):
    return 0.5 * x * (1.0 + lax.erf(x * (2.0 ** -0.5)))


def _rms(x, g):
    return x * lax.rsqrt(jnp.mean(x * x, axis=-1, keepdims=True) + EPS) * g


def _mixer_kernel(h_ref, hmt_ref, hdt_ref, p_ref, wo_ref, gf_ref, wu_ref, cw_ref, cb_ref, wd_ref, gp_ref,
                  wpg_ref, wpp_ref, out_ref, sg_ref, sv_ref, cg_ref, cv_ref, acc_ref, u_ref):
    tm = h_ref.shape[0]
    halo = SUBLANES

    @pl.when(pl.program_id(1) == 0)
    def _reset_conv_history():
        cg_ref[...] = jnp.zeros_like(cg_ref)
        cv_ref[...] = jnp.zeros_like(cv_ref)

    h1 = h_ref[...] + _dot_tn(jnp.concatenate([hmt_ref[...], hdt_ref[...]], axis=0), wo_ref[...])
    u_ref[...] = _rms(h1, gf_ref[...]).astype(BF16)
    acc_ref[...] = jnp.zeros_like(acc_ref)

    def up_stage(j, slot):
        c0 = pl.multiple_of(j * FF_CHUNK, FF_CHUNK)
        sg_ref[slot, halo:, :] = _dot(u_ref[...], wu_ref[:, pl.ds(c0, FF_CHUNK)])
        sv_ref[slot, halo:, :] = _dot(u_ref[...], wu_ref[:, pl.ds(D_FF + c0, FF_CHUNK)])

    def conv_branch(j, slot, c0, stage_ref, hist_ref):
        stage_ref[slot, 0:halo, :] = hist_ref[j]
        hist_ref[j] = stage_ref[slot, tm:tm + halo, :]
        cw = cw_ref[:, pl.ds(c0, FF_CHUNK)]
        return (cw[0:1, :] * stage_ref[slot, halo - 2:halo - 2 + tm, :]
                + cw[1:2, :] * stage_ref[slot, halo - 1:halo - 1 + tm, :]
                + cw[2:3, :] * stage_ref[slot, halo:, :] + cb_ref[:, pl.ds(c0, FF_CHUNK)])

    def act_stage(j, slot):
        c0 = pl.multiple_of(j * FF_CHUNK, FF_CHUNK)
        gate = conv_branch(j, slot, c0, sg_ref, cg_ref)
        val = conv_branch(j, slot, D_FF + c0, sv_ref, cv_ref)
        act = (_gelu(gate) * val).astype(BF16)
        acc_ref[...] += _dot(act, wd_ref[j])

    def chunk_pair(i, _):
        j = 2 * i
        up_stage(j + 1, 1)
        act_stage(j, 0)
        up_stage(j + 2, 0)
        act_stage(j + 1, 1)
        return 0

    assert N_FF_CHUNKS % 2 == 1
    up_stage(0, 0)
    for i in range(N_FF_CHUNKS // 2):
        chunk_pair(i, 0)
    act_stage(N_FF_CHUNKS - 1, 0)

    h2 = h1 + acc_ref[...]
    u3 = _rms(h2, gp_ref[...]).astype(BF16)
    ple_gate = _sigmoid(_dot(u3, wpg_ref[...]))
    out_ref[...] = h2 + ple_gate * _dot(p_ref[...].astype(BF16), wpp_ref[...])


def _mixer(layer, h, hmt, hdt, p, wo, gf, wu, cw, cb, wd, gp, wpg, wpp, batch, seq):
    tiles = seq // TM_FFN
    tile = lambda width: pl.BlockSpec((TM_FFN, width), lambda b, t: (b * tiles + t, 0))

    def of_layer(arr):
        nd = arr.ndim
        return pl.BlockSpec((None,) + arr.shape[1:], lambda b, t: (layer,) + (0,) * (nd - 1),
                            pipeline_mode=pl.Buffered(1))

    weights = (wo, gf, wu, cw, cb, wd, gp, wpg, wpp)
    in_specs = [
        tile(D_MODEL),
        pl.BlockSpec((GROUP_W, TM_FFN), lambda b, t: (0, b * tiles + t)),
        pl.BlockSpec((None, GROUP_W, TM_FFN), lambda b, t: (b, 0, t)),
        pl.BlockSpec((None, TM_FFN, PLE_DIM), lambda b, t: (layer, b * tiles + t, 0)),
    ] + [of_layer(w) for w in weights]
    return pl.pallas_call(
        _mixer_kernel, grid=(batch, tiles), in_specs=in_specs, out_specs=tile(D_MODEL),
        out_shape=jax.ShapeDtypeStruct(h.shape, F32),
        scratch_shapes=[
            pltpu.VMEM((2, TM_FFN + SUBLANES, FF_CHUNK), F32),
            pltpu.VMEM((2, TM_FFN + SUBLANES, FF_CHUNK), F32),
            pltpu.VMEM((N_FF_CHUNKS, SUBLANES, FF_CHUNK), F32),
            pltpu.VMEM((N_FF_CHUNKS, SUBLANES, FF_CHUNK), F32),
            pltpu.VMEM((TM_FFN, D_MODEL), F32),
            pltpu.VMEM((TM_FFN, D_MODEL), BF16),
        ],
        name="mixer",
        compiler_params=pltpu.CompilerParams(
            dimension_semantics=("arbitrary", "arbitrary"), vmem_limit_bytes=VMEM_LIMIT),
    )(h, hmt, hdt, p, *weights)


def _head_interleave(qcols, kcols):
    lead = qcols.shape[:-1]
    qh = qcols.reshape(lead + (N_HEADS, MLSTM_DK))
    kh = kcols.reshape(lead + (N_HEADS, MLSTM_DK))
    return jnp.concatenate([qh, kh], axis=-1).reshape(lead + (GROUP_W,))


def kernel(x, p, positions, rel_bias, ln_mix_g, w_in, mlstm_conv_w, b_igate, b_fgate, mlstm_norm_g, q_norm_g, k_norm_g, lam_q1, lam_k1, lam_q2, lam_k2, diff_subln_g, w_out, ln_ffn_g, w_up, ffn_conv_w, ffn_conv_b, w_down, ln_ple_g, w_ple_gate, w_ple_proj):
    batch, seq, _ = x.shape
    depth = w_in.shape[0]
    n_tok = batch * seq
    qk_cols = N_HEADS * MLSTM_DK
    col_sizes = [qk_cols, qk_cols, GROUP_W, GROUP_W, N_HEADS, N_HEADS, GROUP_W, GROUP_W, GROUP_W]
    offs = np.concatenate([[0], np.cumsum(col_sizes)])
    sl = lambda a, j: a[..., int(offs[j]):int(offs[j + 1])]

    tbl = jnp.broadcast_to(
        (jnp.transpose(rel_bias.astype(F32), (1, 2, 0)) * LOG2E).reshape(N_HEADS, 2 * N_BUCKETS, 1),
        (N_HEADS, 2 * N_BUCKETS, BIAS_COLS))
    pcol = jnp.broadcast_to(positions.astype(jnp.int32).reshape(seq, 1), (seq, BIAS_COLS))
    prow = positions.astype(jnp.int32).reshape(1, seq)
    gsum = jnp.asarray(np.kron(np.eye(MXU_TILE // DIFF_D), np.ones((DIFF_D, DIFF_D))), BF16)
    t_idx = np.arange(PAIR)
    same_chunk = (t_idx[:, None] // CHUNK) == (t_idx[None, :] // CHUNK)
    prefix = same_chunk & (t_idx[:, None] <= t_idx[None, :])
    total_a = np.broadcast_to(t_idx[:, None] < CHUNK, (PAIR, PAIR))
    ug = jnp.asarray(np.concatenate([prefix, total_a, ~total_a], axis=1), BF16)
    l2 = jnp.asarray(prefix.T, BF16)

    p_tok = p.reshape(depth, n_tok, PLE_DIM)
    row = lambda a: a.astype(F32).reshape(depth, 1, a.shape[-1])
    mixer_weights = (
        w_out.astype(BF16), row(ln_ffn_g), w_up.astype(BF16), ffn_conv_w.astype(F32), row(ffn_conv_b),
        w_down.astype(BF16).reshape(depth, N_FF_CHUNKS, FF_CHUNK, D_MODEL), row(ln_ple_g),
        w_ple_gate.astype(BF16), w_ple_proj.astype(BF16))

    h = x.reshape(n_tok, D_MODEL)
    for i in range(depth):
        wi = w_in[i]
        wm = jnp.concatenate([_head_interleave(sl(wi, 0), sl(wi, 1)), sl(wi, 6), sl(wi, 7)], axis=-1).astype(BF16)
        wt = jnp.concatenate([sl(wi, 2), sl(wi, 3), sl(wi, 8)], axis=-1).T.astype(BF16)
        pad_rows = lambda a: jnp.zeros((N_GATE_ROWS,) + a.shape[1:], F32).at[:N_HEADS].set(a.astype(F32))
        wg = jnp.concatenate([pad_rows(sl(wi, 4).T), pad_rows(sl(wi, 5).T)], axis=0).astype(BF16)
        gate_bias = jnp.concatenate([pad_rows(b_igate[i]), pad_rows(b_fgate[i])])
        qg = jnp.tile(q_norm_g[i].astype(F32), GROUP_W // DIFF_D).reshape(1, GROUP_W) * (DIFF_D ** -0.5 * LOG2E)
        kg = jnp.tile(k_norm_g[i].astype(F32), GROUP_W // DIFF_D).reshape(1, GROUP_W)
        qk, vmt, omt, gri, grf, gci, gcf, qd, kd, vdt = _inproj(
            h, ln_mix_g[i].reshape(1, D_MODEL), wm, wt, wg, gate_bias.reshape(1, -1), gate_bias.reshape(-1, 1),
            gsum, qg, kg)

        cw = _head_interleave(mlstm_conv_w[i][:, :qk_cols], mlstm_conv_w[i][:, qk_cols:]).astype(F32)
        ng = jnp.broadcast_to(mlstm_norm_g[i].astype(F32).reshape(MLSTM_DV, 1), (MLSTM_DV, PAIR))
        hmt = _mlstm(qk, vmt, omt, gri, grf, gci, gcf, cw, ng, ug, l2, batch, seq)

        lam_init = 0.8 - 0.6 * math.exp(-0.3 * i)
        row64 = lambda a: a[i].reshape(1, DIFF_D).astype(F32)
        hdt = _attention(lam_init, tbl, qd, kd, vdt, pcol, prow, row64(lam_q1), row64(lam_k1), row64(lam_q2),
                        row64(lam_k2), diff_subln_g[i].reshape(1, DIFF_DV).astype(F32), batch, seq)

        h = _mixer(i, h, hmt, hdt, p_tok, *mixer_weights, batch, seq)
    return h.reshape(batch, seq, D_MODEL)
```

```python
import functools
import math

import numpy as np
import jax
import jax.numpy as jnp
from jax import lax
from jax.experimental import pallas as pl
from jax.experimental.pallas import tpu as pltpu

F32 = jnp.float32
BF16 = jnp.bfloat16

LANES = 128
SUBLANES = 8
MXU_TILE = 256
V7X_VMEM_BYTES = 64 * 1024 * 1024

D_MODEL = 1024
N_HEADS = 4
MLSTM_DK = 64
MLSTM_DV = 128
MLSTM_CONV_W = 4
CHUNK = 64
PAIR = 2 * CHUNK
DIFF_D = 64
DIFF_DV = 128
N_BUCKETS = 32
MAX_DISTANCE = 128
D_FF = 2816
FFN_CONV_W = 3
PLE_DIM = 256
EPS = 1e-6
GROUP_W = N_HEADS * LANES
N_GATE_ROWS = 16

TM_IN = 512
TM_FFN = 512
FF_CHUNK = 256
N_FF_CHUNKS = D_FF // FF_CHUNK
TQ = 512
TK = 256
BIAS_ROWS = 32
BIAS_COLS = LANES
ATTN_NB = 2
NEG_BIG = -1e30
M_INIT = -1e29
ONES_ROWS = 16
V_ROWS = DIFF_DV + ONES_ROWS
LOG2E = math.log2(math.e)
VMEM_LIMIT = V7X_VMEM_BYTES * 7 // 8


def _dot(a, b):
    return jnp.dot(a, b, preferred_element_type=F32)


def _dot_nt(a, b):
    return lax.dot_general(a, b, (((1,), (1,)), ((), ())), preferred_element_type=F32)


def _dot_tn(a, b):
    return lax.dot_general(a, b, (((0,), (0,)), ((), ())), preferred_element_type=F32)


def _sigmoid(x):
    return 1.0 / (1.0 + jnp.exp(-x))


def _log_sigmoid(x):
    return jnp.minimum(x, 0.0) - jnp.log1p(jnp.exp(-jnp.abs(x)))


def _bucket_thresholds():
    max_exact = N_BUCKETS // 2
    thr = []
    for v in range(1, N_BUCKETS):
        if v <= max_exact:
            thr.append(v)
            continue
        edge = max_exact * (MAX_DISTANCE / max_exact) ** ((v - max_exact) / (N_BUCKETS - max_exact))
        assert abs(edge - round(edge)) > 1e-3, edge
        thr.append(int(math.ceil(edge)))
    assert all(a < b for a, b in zip(thr, thr[1:])), thr
    return tuple(thr)


_BUCKET_THR = _bucket_thresholds()


def _group_mean_square(z, gsum_ref):
    sq = (z * z).astype(BF16)
    width = gsum_ref.shape[0]
    sums = [_dot(sq[:, c0:c0 + width], gsum_ref[...]) for c0 in range(0, z.shape[1], width)]
    return jnp.concatenate(sums, axis=1) * (1.0 / DIFF_D)


def _with_ones_rows(dst_ref, vt):
    for hh in range(N_HEADS):
        dst_ref[hh * V_ROWS:hh * V_ROWS + DIFF_DV, :] = vt[hh * DIFF_DV:(hh + 1) * DIFF_DV, :]
        dst_ref[hh * V_ROWS + DIFF_DV:(hh + 1) * V_ROWS, :] = jnp.ones((ONES_ROWS, vt.shape[1]), BF16)


def _inproj_kernel(h_ref, g_ref, wm_ref, wt_ref, wg_ref, bcol_ref, brow_ref, gsum_ref, qg_ref, kg_ref,
                   qk_ref, vmt_ref, omt_ref, gri_ref, grf_ref, gci_ref, gcf_ref, qd_ref, kd_ref, vdt_ref):
    x = h_ref[...]
    ms = jnp.mean(x * x, axis=-1, keepdims=True)
    u = (x * lax.rsqrt(ms + EPS) * g_ref[...]).astype(BF16)
    w = GROUP_W
    qk_ref[...] = _dot(u, wm_ref[:, 0:w])
    zq = _dot(u, wm_ref[:, w:2 * w])
    qd_ref[...] = (zq * lax.rsqrt(_group_mean_square(zq, gsum_ref) + EPS) * qg_ref[...]).astype(BF16)
    zk = _dot(u, wm_ref[:, 2 * w:3 * w])
    kd_ref[...] = (zk * lax.rsqrt(_group_mean_square(zk, gsum_ref) + EPS) * kg_ref[...]).astype(BF16)
    _with_ones_rows(vmt_ref, _dot_nt(wt_ref[0:w, :], u).astype(BF16))
    omt_ref[...] = _dot_nt(wt_ref[w:2 * w, :], u)
    _with_ones_rows(vdt_ref, _dot_nt(wt_ref[2 * w:3 * w, :], u).astype(BF16))
    gc = _dot_nt(u, wg_ref[...]) + bcol_ref[...]
    gci_ref[...] = gc[:, 0:N_GATE_ROWS]
    gcf_ref[...] = gc[:, N_GATE_ROWS:2 * N_GATE_ROWS]
    gr = _dot_nt(wg_ref[...], u) + brow_ref[...]
    for ci in range(TM_IN // PAIR):
        gri_ref[ci] = gr[0:SUBLANES, ci * PAIR:(ci + 1) * PAIR]
        grf_ref[ci] = gr[N_GATE_ROWS:N_GATE_ROWS + SUBLANES, ci * PAIR:(ci + 1) * PAIR]


def _inproj(h, g, wm, wt, wg, bcol, brow, gsum, qg, kg):
    n_tok = h.shape[0]
    grid = (n_tok // TM_IN,)
    const = lambda shape: pl.BlockSpec(shape, lambda i: (0,) * len(shape))
    tile = lambda width: pl.BlockSpec((TM_IN, width), lambda i: (i, 0))
    tile_t = lambda rows: pl.BlockSpec((rows, TM_IN), lambda i: (0, i))
    gate_rows = pl.BlockSpec((TM_IN // PAIR, SUBLANES, PAIR), lambda i: (i, 0, 0))
    out_shape = (
        jax.ShapeDtypeStruct((n_tok, GROUP_W), F32),
        jax.ShapeDtypeStruct((N_HEADS * V_ROWS, n_tok), BF16),
        jax.ShapeDtypeStruct((GROUP_W, n_tok), F32),
        jax.ShapeDtypeStruct((n_tok // PAIR, SUBLANES, PAIR), F32),
        jax.ShapeDtypeStruct((n_tok // PAIR, SUBLANES, PAIR), F32),
        jax.ShapeDtypeStruct((n_tok, N_GATE_ROWS), F32),
        jax.ShapeDtypeStruct((n_tok, N_GATE_ROWS), F32),
        jax.ShapeDtypeStruct((n_tok, GROUP_W), BF16),
        jax.ShapeDtypeStruct((n_tok, GROUP_W), BF16),
        jax.ShapeDtypeStruct((N_HEADS * V_ROWS, n_tok), BF16),
    )
    out_specs = (
        tile(GROUP_W), tile_t(N_HEADS * V_ROWS), tile_t(GROUP_W), gate_rows, gate_rows,
        tile(N_GATE_ROWS), tile(N_GATE_ROWS), tile(GROUP_W), tile(GROUP_W), tile_t(N_HEADS * V_ROWS),
    )
    in_specs = [
        tile(D_MODEL), const((1, D_MODEL)), const(wm.shape), const(wt.shape), const(wg.shape),
        const(bcol.shape), const(brow.shape), const(gsum.shape),
        const((1, GROUP_W)), const((1, GROUP_W)),
    ]
    return pl.pallas_call(
        _inproj_kernel, grid=grid, in_specs=in_specs, out_specs=out_specs, out_shape=out_shape,
        name="inproj",
        compiler_params=pltpu.CompilerParams(dimension_semantics=("arbitrary",), vmem_limit_bytes=VMEM_LIMIT),
    )(h, g, wm, wt, wg, bcol, brow, gsum, qg, kg)


def _split_hi_lo(x):
    hi = x.astype(BF16)
    return hi, (x - hi.astype(F32)).astype(BF16)


def _mlstm_kernel(qk_ref, vt_ref, ot_ref, gri_ref, grf_ref, gci_ref, gcf_ref, cw_ref, ng_ref, ug_ref, l2_ref,
                  out_ref, xpad_ref, cn_ref):
    seq = qk_ref.shape[0]
    halo = SUBLANES
    xpad_ref[0:halo, :] = jnp.zeros((halo, GROUP_W), F32)
    xpad_ref[halo:, :] = qk_ref[...]
    cn_ref[...] = jnp.zeros_like(cn_ref)

    first8 = lax.broadcasted_iota(jnp.int32, (SUBLANES, PAIR), 1) < CHUNK
    first = lax.broadcasted_iota(jnp.int32, (1, PAIR), 1) < CHUNK
    upper = lax.broadcasted_iota(jnp.int32, (PAIR, PAIR), 1) >= MLSTM_DK
    key_t = lax.broadcasted_iota(jnp.int32, (PAIR, PAIR), 0)
    qry_t = lax.broadcasted_iota(jnp.int32, (PAIR, PAIR), 1)
    chunk_start = jnp.where(qry_t < CHUNK, 0, CHUNK)
    cw = cw_ref[...]
    neg_inf = jnp.float32(-jnp.inf)

    def pair_step(c2, m_prev):
        r0 = pl.multiple_of(c2 * PAIR, PAIR)
        win = xpad_ref[pl.ds(r0, PAIR + halo), :]
        conv = cw[0:1, :] * win[halo - 3:halo - 3 + PAIR, :]
        for j in range(1, MLSTM_CONV_W):
            conv = conv + cw[j:j + 1, :] * win[halo - 3 + j:halo - 3 + j + PAIR, :]
        x = conv * _sigmoid(conv)

        gi = gri_ref[c2]
        lf_hi, lf_lo = _split_hi_lo(_log_sigmoid(grf_ref[c2]))
        bg = _dot(lf_hi, ug_ref[...]) + _dot(lf_lo, ug_ref[...])
        b_r = bg[:, 0:PAIR]
        g_a = bg[:, PAIR:2 * PAIR]
        g_b = bg[:, 2 * PAIR:3 * PAIR]
        a_r = jnp.where(first8, g_a, g_b) - b_r + gi
        max_a = jnp.max(jnp.where(first8, a_r, neg_inf), axis=1, keepdims=True)
        max_b = jnp.max(jnp.where(first8, neg_inf, a_r), axis=1, keepdims=True)
        m_a = jnp.maximum(g_a + m_prev, max_a)
        m_b = jnp.maximum(g_b + m_a, max_b)
        dec_a = jnp.exp(g_a + m_prev - m_a)
        dec_b = jnp.exp(g_b + m_a - m_b)
        w_r = jnp.exp(a_r - jnp.where(first8, m_a, m_b))
        e_r = b_r + jnp.where(first8, m_prev, m_a)
        lc_hi, lc_lo = _split_hi_lo(_log_sigmoid(gcf_ref[pl.ds(r0, PAIR), :]))
        x_c = _dot(l2_ref[...], lc_hi) + _dot(l2_ref[...], lc_lo) - gci_ref[pl.ds(r0, PAIR), :]

        heads = range(N_HEADS)
        xk, xq, vt, cn0, up_a, kq, cq_a, m_out, decay_t = [], [], [], [], [], [], [], [], []
        for hd in heads:
            xh = x[:, hd * LANES:(hd + 1) * LANES]
            xs = pltpu.roll(xh, MLSTM_DK, 1)
            xk.append(jnp.where(upper, xh * (MLSTM_DK ** -0.5), 0.0).astype(BF16))
            xq.append(jnp.where(upper, xs, 0.0).astype(BF16))
            vt.append(vt_ref[hd * V_ROWS:(hd + 1) * V_ROWS, pl.ds(r0, PAIR)])
            cn0.append(cn_ref[hd])
            w_a = jnp.where(first, w_r[hd:hd + 1, :], 0.0)
            up_a.append(_dot((vt[hd].astype(F32) * w_a).astype(BF16), xk[hd]))
            kq.append(_dot_nt(xk[hd], xq[hd]))
            cq_a.append(_dot_nt(cn0[hd].astype(BF16), xq[hd]))
            dmat = b_r[hd:hd + 1, :] - x_c[:, hd:hd + 1]
            dmat = jnp.where(key_t <= qry_t, jnp.where(key_t >= chunk_start, dmat, neg_inf), neg_inf)
            m_out.append(jnp.maximum(e_r[hd:hd + 1, :], jnp.max(dmat, axis=0, keepdims=True)))
            decay_t.append(jnp.exp(dmat - m_out[hd]))
        cn1, cq_b, up_b = [], [], []
        for hd in heads:
            cn1.append(dec_a[hd:hd + 1, :] * cn0[hd] + up_a[hd])
            w_b = jnp.where(first, 0.0, w_r[hd:hd + 1, :])
            up_b.append(_dot((vt[hd].astype(F32) * w_b).astype(BF16), xk[hd]))
            cq_b.append(_dot_nt(cn1[hd].astype(BF16), xq[hd]))
        sv = []
        for hd in heads:
            cn_ref[hd] = dec_b[hd:hd + 1, :] * cn1[hd] + up_b[hd]
            sv.append(_dot(vt[hd], (kq[hd] * decay_t[hd]).astype(BF16)))
        for hd in heads:
            rows = slice(hd * MLSTM_DV, (hd + 1) * MLSTM_DV)
            inter = jnp.exp(e_r[hd:hd + 1, :] - m_out[hd])
            tot = inter * jnp.where(first, cq_a[hd], cq_b[hd]) + sv[hd]
            den = tot[MLSTM_DV:MLSTM_DV + 1, :]
            hh = tot[0:MLSTM_DV, :] / jnp.maximum(jnp.abs(den), jnp.exp(-m_out[hd]))
            hn = hh * lax.rsqrt(jnp.mean(hh * hh, axis=0, keepdims=True) + EPS) * ng_ref[...]
            og = ot_ref[rows, pl.ds(r0, PAIR)]
            out_ref[rows, pl.ds(r0, PAIR)] = (hn * _sigmoid(og)).astype(BF16)
        return m_b

    lax.fori_loop(0, seq // PAIR, pair_step, jnp.zeros((SUBLANES, PAIR), F32), unroll=2)


def _mlstm(qk, vmt, omt, gri, grf, gci, gcf, cw, ng, ug, l2, batch, seq):
    const = lambda arr: pl.BlockSpec(arr.shape, lambda b: (0,) * arr.ndim)
    in_specs = [
        pl.BlockSpec((seq, GROUP_W), lambda b: (b, 0)),
        pl.BlockSpec((N_HEADS * V_ROWS, seq), lambda b: (0, b)),
        pl.BlockSpec((GROUP_W, seq), lambda b: (0, b)),
        pl.BlockSpec((seq // PAIR, SUBLANES, PAIR), lambda b: (b, 0, 0)),
        pl.BlockSpec((seq // PAIR, SUBLANES, PAIR), lambda b: (b, 0, 0)),
        pl.BlockSpec((seq, N_GATE_ROWS), lambda b: (b, 0)),
        pl.BlockSpec((seq, N_GATE_ROWS), lambda b: (b, 0)),
        const(cw), const(ng), const(ug), const(l2),
    ]
    return pl.pallas_call(
        _mlstm_kernel, grid=(batch,), in_specs=in_specs,
        out_specs=pl.BlockSpec((GROUP_W, seq), lambda b: (0, b)),
        out_shape=jax.ShapeDtypeStruct((GROUP_W, batch * seq), BF16),
        scratch_shapes=[
            pltpu.VMEM((seq + SUBLANES, GROUP_W), F32),
            pltpu.VMEM((N_HEADS, V_ROWS, LANES), F32),
        ],
        name="mlstm",
        compiler_params=pltpu.CompilerParams(dimension_semantics=("arbitrary",), vmem_limit_bytes=VMEM_LIMIT),
    )(qk, vmt, omt, gri, grf, gci, gcf, cw, ng, ug, l2)


def _attn_kernel(lam_init, tbl_ref, kmax_ref, qmin_ref, q_ref, k_ref, vt_ref, pcol_ref, prow_ref,
                 lq1_ref, lk1_ref, lq2_ref, lk2_ref, sg_ref, out_ref, bias_ref, qc_ref, s_ref, p_ref, acc_ref):
    hd = pl.program_id(0)
    qi = pl.program_id(1)
    b = pl.program_id(2)
    n_kv = (qi + 1) * (TQ // TK)

    @pl.when(b == 0)
    def _build_bias():
        rowi = lax.broadcasted_iota(jnp.int32, (BIAS_ROWS, BIAS_COLS), 0)
        coli = lax.broadcasted_iota(jnp.int32, (BIAS_ROWS, BIAS_COLS), 1)
        row = lambda c, v: tbl_ref[c * N_BUCKETS + v:c * N_BUCKETS + v + 1, :]
        last = N_BUCKETS - 1

        def build(t, _):
            r0 = pl.multiple_of(t * BIAS_ROWS, BIAS_ROWS)
            pk = pcol_ref[pl.ds(r0, BIAS_ROWS), :]
            kmax = kmax_ref[t]
            for j in range(TQ // BIAS_COLS):
                lanes = slice(j * BIAS_COLS, (j + 1) * BIAS_COLS)
                q0 = qi * TQ + j * BIAS_COLS
                causal = rowi + r0 <= coli + q0
                flat = jnp.logical_or(qmin_ref[qi * (TQ // BIAS_COLS) + j] - kmax >= _BUCKET_THR[-1],
                                      r0 > q0 + BIAS_COLS - 1)

                @pl.when(flat)
                def _flat():
                    bias_ref[0, pl.ds(r0, BIAS_ROWS), lanes] = jnp.where(causal, row(0, last), NEG_BIG)
                    bias_ref[1, pl.ds(r0, BIAS_ROWS), lanes] = jnp.where(causal, row(1, last), NEG_BIG)

                @pl.when(jnp.logical_not(flat))
                def _lookup():
                    dist = jnp.maximum(prow_ref[:, lanes] - pk, 0)
                    b0 = jnp.broadcast_to(row(0, 0), (BIAS_ROWS, BIAS_COLS))
                    b1 = jnp.broadcast_to(row(1, 0), (BIAS_ROWS, BIAS_COLS))
                    for v, thr in enumerate(_BUCKET_THR, start=1):
                        ge = dist >= thr
                        b0 = jnp.where(ge, row(0, v), b0)
                        b1 = jnp.where(ge, row(1, v), b1)
                    bias_ref[0, pl.ds(r0, BIAS_ROWS), lanes] = jnp.where(causal, b0, NEG_BIG)
                    bias_ref[1, pl.ds(r0, BIAS_ROWS), lanes] = jnp.where(causal, b1, NEG_BIG)
            return 0

        lax.fori_loop(0, n_kv * (TK // BIAS_ROWS), build, 0)

    seq = k_ref.shape[1]
    lane = lax.broadcasted_iota(jnp.int32, (TQ, LANES), 1)
    for e in range(ATTN_NB):
        q = q_ref[e]
        zero = jnp.zeros_like(q)
        qc_ref[e, 0] = jnp.where(lane < DIFF_D, q, zero)
        qc_ref[e, 1] = jnp.where(lane >= DIFF_D, q, zero)
    acc_ref[...] = jnp.zeros_like(acc_ref)

    chains = [(e, c) for e in range(ATTN_NB) for c in range(2)]

    def scores(t, slot):
        k0 = pl.multiple_of(t * TK, TK)
        for e in range(ATTN_NB):
            kb = k_ref[e, pl.ds(k0, TK), :]
            for c in range(2):
                s_ref[slot, e, c] = _dot_nt(kb, qc_ref[e, c]) + bias_ref[c, pl.ds(k0, TK), :]

    def numerators(m_state, slot, lanes=slice(None)):
        m_next, alpha_next = [], []
        for i, (e, c) in enumerate(chains):
            m_prev = m_state[i][:, lanes]
            m_new = jnp.maximum(m_prev, jnp.max(s_ref[slot, e, c, :, lanes], axis=0, keepdims=True))
            alpha_next.append(jnp.exp2(m_prev - m_new))
            m_next.append(m_new)
        for i, (e, c) in enumerate(chains):
            p_ref[slot, e, c, :, lanes] = jnp.exp2(s_ref[slot, e, c, :, lanes] - m_next[i]).astype(BF16)
        return tuple(m_next), tuple(alpha_next)

    def accumulate(t, alpha, slot, lanes=slice(None)):
        for e in range(ATTN_NB):
            v0 = pl.multiple_of(e * seq + t * TK, TK)
            vt = vt_ref[:, pl.ds(v0, TK)]
            for c in range(2):
                acc_ref[e, c, :, lanes] = (alpha[2 * e + c] * acc_ref[e, c, :, lanes]
                                           + _dot(vt, p_ref[slot, e, c, :, lanes]))

    def kv_pair(i, carry):
        t = 2 * i
        m_state, alpha = carry
        scores(t, 0)
        m_state, alpha1 = numerators(m_state, 1)
        accumulate(t - 2, alpha, 0)
        scores(t + 1, 1)
        carry = numerators(m_state, 0)
        accumulate(t - 1, alpha1, 1)
        return carry

    assert (TQ // TK) % 2 == 0
    scores(0, 0)
    scores(1, 1)
    carry = numerators(tuple(jnp.full((1, TQ), M_INIT, F32) for _ in chains), 0)
    m_state, alpha = lax.fori_loop(1, n_kv // 2, kv_pair, carry)
    live = slice(TQ - TK, TQ)
    _, alpha1 = numerators(m_state, 1, live)
    accumulate(n_kv - 2, alpha, 0)
    accumulate(n_kv - 1, alpha1, 1, live)

    lam = (jnp.exp(jnp.sum(lq1_ref[...] * lk1_ref[...], axis=1, keepdims=True))
           - jnp.exp(jnp.sum(lq2_ref[...] * lk2_ref[...], axis=1, keepdims=True)) + lam_init)
    for e in range(ATTN_NB):
        a0 = acc_ref[e, 0]
        a1 = acc_ref[e, 1]
        o = a0[0:DIFF_DV] / a0[DIFF_DV:DIFF_DV + 1] - lam * (a1[0:DIFF_DV] / a1[DIFF_DV:DIFF_DV + 1])
        on = o * lax.rsqrt(jnp.mean(o * o, axis=0, keepdims=True) + EPS) * sg_ref[...]
        out_ref[e] = (on * (1.0 - lam_init)).astype(BF16)


def _attention(lam_init, tbl, qd, kd, vdt, pcol, prow, lq1, lk1, lq2, lk2, sg, batch, seq):
    q3, k3 = (t.reshape(batch, seq, GROUP_W) for t in (qd, kd))
    vec = lambda n: pl.BlockSpec((1, n), lambda h, i, b: (0, 0))
    kmax = jnp.max(prow.reshape(seq // BIAS_ROWS, BIAS_ROWS), axis=1)
    qmin = jnp.min(prow.reshape(seq // BIAS_COLS, BIAS_COLS), axis=1)
    in_specs = [
        pl.BlockSpec((None, 2 * N_BUCKETS, BIAS_COLS), lambda h, i, b: (h, 0, 0)),
        pl.BlockSpec(memory_space=pltpu.SMEM),
        pl.BlockSpec(memory_space=pltpu.SMEM),
        pl.BlockSpec((ATTN_NB, TQ, LANES), lambda h, i, b: (b, i, h)),
        pl.BlockSpec((ATTN_NB, seq, LANES), lambda h, i, b: (b, 0, h)),
        pl.BlockSpec((V_ROWS, ATTN_NB * seq), lambda h, i, b: (h, b)),
        pl.BlockSpec((seq, BIAS_COLS), lambda h, i, b: (0, 0)),
        pl.BlockSpec((1, TQ), lambda h, i, b: (0, i)),
        vec(DIFF_D), vec(DIFF_D), vec(DIFF_D), vec(DIFF_D),
        pl.BlockSpec((DIFF_DV, TQ), lambda h, i, b: (0, 0)),
    ]
    out = pl.pallas_call(
        functools.partial(_attn_kernel, lam_init),
        grid=(N_HEADS, seq // TQ, batch // ATTN_NB), in_specs=in_specs,
        out_specs=pl.BlockSpec((ATTN_NB, DIFF_DV, TQ), lambda h, i, b: (b, h, i)),
        out_shape=jax.ShapeDtypeStruct((batch, GROUP_W, seq), BF16),
        scratch_shapes=[
            pltpu.VMEM((2, seq, TQ), F32),
            pltpu.VMEM((ATTN_NB, 2, TQ, LANES), BF16),
            pltpu.VMEM((2, ATTN_NB, 2, TK, TQ), F32),
            pltpu.VMEM((2, ATTN_NB, 2, TK, TQ), BF16),
            pltpu.VMEM((ATTN_NB, 2, V_ROWS, TQ), F32),
        ],
        name="diff_attn",
        compiler_params=pltpu.CompilerParams(
            dimension_semantics=("arbitrary", "arbitrary", "arbitrary"), vmem_limit_bytes=VMEM_LIMIT),
    )(tbl, kmax, qmin, q3, k3, vdt, pcol, prow, lq1, lk1, lq2, lk2, jnp.broadcast_to(sg.reshape(DIFF_DV, 1), (DIFF_DV, TQ)))
    return out


def _gelu(x):
    return 0.5 * x * (1.0 + lax.erf(x * (2.0 ** -0.5)))


def _rms(x, g):
    return x * lax.rsqrt(jnp.mean(x * x, axis=-1, keepdims=True) + EPS) * g


def _mixer_kernel(h_ref, hmt_ref, hdt_ref, p_ref, wo_ref, gf_ref, wu_ref, cw_ref, cb_ref, wd_ref, gp_ref,
                  wpg_ref, wpp_ref, out_ref, sg_ref, sv_ref, cg_ref, cv_ref, acc_ref, u_ref):
    tm = h_ref.shape[0]
    halo = SUBLANES

    @pl.when(pl.program_id(1) == 0)
    def _reset_conv_history():
        cg_ref[...] = jnp.zeros_like(cg_ref)
        cv_ref[...] = jnp.zeros_like(cv_ref)

    h1 = h_ref[...] + _dot_tn(jnp.concatenate([hmt_ref[...], hdt_ref[...]], axis=0), wo_ref[...])
    u_ref[...] = _rms(h1, gf_ref[...]).astype(BF16)
    acc_ref[...] = jnp.zeros_like(acc_ref)

    def up_stage(j, slot):
        c0 = pl.multiple_of(j * FF_CHUNK, FF_CHUNK)
        sg_ref[slot, halo:, :] = _dot(u_ref[...], wu_ref[:, pl.ds(c0, FF_CHUNK)])
        sv_ref[slot, halo:, :] = _dot(u_ref[...], wu_ref[:, pl.ds(D_FF + c0, FF_CHUNK)])

    def conv_branch(j, slot, c0, stage_ref, hist_ref):
        stage_ref[slot, 0:halo, :] = hist_ref[j]
        hist_ref[j] = stage_ref[slot, tm:tm + halo, :]
        cw = cw_ref[:, pl.ds(c0, FF_CHUNK)]
        return (cw[0:1, :] * stage_ref[slot, halo - 2:halo - 2 + tm, :]
                + cw[1:2, :] * stage_ref[slot, halo - 1:halo - 1 + tm, :]
                + cw[2:3, :] * stage_ref[slot, halo:, :] + cb_ref[:, pl.ds(c0, FF_CHUNK)])

    def act_stage(j, slot):
        c0 = pl.multiple_of(j * FF_CHUNK, FF_CHUNK)
        gate = conv_branch(j, slot, c0, sg_ref, cg_ref)
        val = conv_branch(j, slot, D_FF + c0, sv_ref, cv_ref)
        act = (_gelu(gate) * val).astype(BF16)
        acc_ref[...] += _dot(act, wd_ref[j])

    def chunk_pair(i, _):
        j = 2 * i
        up_stage(j + 1, 1)
        act_stage(j, 0)
        up_stage(j + 2, 0)
        act_stage(j + 1, 1)
        return 0

    assert N_FF_CHUNKS % 2 == 1
    up_stage(0, 0)
    for i in range(N_FF_CHUNKS // 2):
        chunk_pair(i, 0)
    act_stage(N_FF_CHUNKS - 1, 0)

    h2 = h1 + acc_ref[...]
    u3 = _rms(h2, gp_ref[...]).astype(BF16)
    ple_gate = _sigmoid(_dot(u3, wpg_ref[...]))
    out_ref[...] = h2 + ple_gate * _dot(p_ref[...].astype(BF16), wpp_ref[...])


def _mixer(layer, h, hmt, hdt, p, wo, gf, wu, cw, cb, wd, gp, wpg, wpp, batch, seq):
    tiles = seq // TM_FFN
    tile = lambda width: pl.BlockSpec((TM_FFN, width), lambda b, t: (b * tiles + t, 0))

    def of_layer(arr):
        nd = arr.ndim
        return pl.BlockSpec((None,) + arr.shape[1:], lambda b, t: (layer,) + (0,) * (nd - 1),
                            pipeline_mode=pl.Buffered(1))

    weights = (wo, gf, wu, cw, cb, wd, gp, wpg, wpp)
    in_specs = [
        tile(D_MODEL),
        pl.BlockSpec((GROUP_W, TM_FFN), lambda b, t: (0, b * tiles + t)),
        pl.BlockSpec((None, GROUP_W, TM_FFN), lambda b, t: (b, 0, t)),
        pl.BlockSpec((None, TM_FFN, PLE_DIM), lambda b, t: (layer, b * tiles + t, 0)),
    ] + [of_layer(w) for w in weights]
    return pl.pallas_call(
        _mixer_kernel, grid=(batch, tiles), in_specs=in_specs, out_specs=tile(D_MODEL),
        out_shape=jax.ShapeDtypeStruct(h.shape, F32),
        scratch_shapes=[
            pltpu.VMEM((2, TM_FFN + SUBLANES, FF_CHUNK), F32),
            pltpu.VMEM((2, TM_FFN + SUBLANES, FF_CHUNK), F32),
            pltpu.VMEM((N_FF_CHUNKS, SUBLANES, FF_CHUNK), F32),
            pltpu.VMEM((N_FF_CHUNKS, SUBLANES, FF_CHUNK), F32),
            pltpu.VMEM((TM_FFN, D_MODEL), F32),
            pltpu.VMEM((TM_FFN, D_MODEL), BF16),
        ],
        name="mixer",
        compiler_params=pltpu.CompilerParams(
            dimension_semantics=("arbitrary", "arbitrary"), vmem_limit_bytes=VMEM_LIMIT),
    )(h, hmt, hdt, p, *weights)


def _head_interleave(qcols, kcols):
    lead = qcols.shape[:-1]
    qh = qcols.reshape(lead + (N_HEADS, MLSTM_DK))
    kh = kcols.reshape(lead + (N_HEADS, MLSTM_DK))
    return jnp.concatenate([qh, kh], axis=-1).reshape(lead + (GROUP_W,))


def kernel(x, p, positions, rel_bias, ln_mix_g, w_in, mlstm_conv_w, b_igate, b_fgate, mlstm_norm_g, q_norm_g, k_norm_g, lam_q1, lam_k1, lam_q2, lam_k2, diff_subln_g, w_out, ln_ffn_g, w_up, ffn_conv_w, ffn_conv_b, w_down, ln_ple_g, w_ple_gate, w_ple_proj):
    batch, seq, _ = x.shape
    depth = w_in.shape[0]
    n_tok = batch * seq
    qk_cols = N_HEADS * MLSTM_DK
    col_sizes = [qk_cols, qk_cols, GROUP_W, GROUP_W, N_HEADS, N_HEADS, GROUP_W, GROUP_W, GROUP_W]
    offs = np.concatenate([[0], np.cumsum(col_sizes)])
    sl = lambda a, j: a[..., int(offs[j]):int(offs[j + 1])]

    tbl = jnp.broadcast_to(
        (jnp.transpose(rel_bias.astype(F32), (1, 2, 0)) * LOG2E).reshape(N_HEADS, 2 * N_BUCKETS, 1),
        (N_HEADS, 2 * N_BUCKETS, BIAS_COLS))
    pcol = jnp.broadcast_to(positions.astype(jnp.int32).reshape(seq, 1), (seq, BIAS_COLS))
    prow = positions.astype(jnp.int32).reshape(1, seq)
    gsum = jnp.asarray(np.kron(np.eye(MXU_TILE // DIFF_D), np.ones((DIFF_D, DIFF_D))), BF16)
    t_idx = np.arange(PAIR)
    same_chunk = (t_idx[:, None] // CHUNK) == (t_idx[None, :] // CHUNK)
    prefix = same_chunk & (t_idx[:, None] <= t_idx[None, :])
    total_a = np.broadcast_to(t_idx[:, None] < CHUNK, (PAIR, PAIR))
    ug = jnp.asarray(np.concatenate([prefix, total_a, ~total_a], axis=1), BF16)
    l2 = jnp.asarray(prefix.T, BF16)

    p_tok = p.reshape(depth, n_tok, PLE_DIM)
    row = lambda a: a.astype(F32).reshape(depth, 1, a.shape[-1])
    mixer_weights = (
        w_out.astype(BF16), row(ln_ffn_g), w_up.astype(BF16), ffn_conv_w.astype(F32), row(ffn_conv_b),
        w_down.astype(BF16).reshape(depth, N_FF_CHUNKS, FF_CHUNK, D_MODEL), row(ln_ple_g),
        w_ple_gate.astype(BF16), w_ple_proj.astype(BF16))

    h = x.reshape(n_tok, D_MODEL)
    for i in range(depth):
        wi = w_in[i]
        wm = jnp.concatenate([_head_interleave(sl(wi, 0), sl(wi, 1)), sl(wi, 6), sl(wi, 7)], axis=-1).astype(BF16)
        wt = jnp.concatenate([sl(wi, 2), sl(wi, 3), sl(wi, 8)], axis=-1).T.astype(BF16)
        pad_rows = lambda a: jnp.zeros((N_GATE_ROWS,) + a.shape[1:], F32).at[:N_HEADS].set(a.astype(F32))
        wg = jnp.concatenate([pad_rows(sl(wi, 4).T), pad_rows(sl(wi, 5).T)], axis=0).astype(BF16)
        gate_bias = jnp.concatenate([pad_rows(b_igate[i]), pad_rows(b_fgate[i])])
        qg = jnp.tile(q_norm_g[i].astype(F32), GROUP_W // DIFF_D).reshape(1, GROUP_W) * (DIFF_D ** -0.5 * LOG2E)
        kg = jnp.tile(k_norm_g[i].astype(F32), GROUP_W // DIFF_D).reshape(1, GROUP_W)
        qk, vmt, omt, gri, grf, gci, gcf, qd, kd, vdt = _inproj(
            h, ln_mix_g[i].reshape(1, D_MODEL), wm, wt, wg, gate_bias.reshape(1, -1), gate_bias.reshape(-1, 1),
            gsum, qg, kg)

        cw = _head_interleave(mlstm_conv_w[i][:, :qk_cols], mlstm_conv_w[i][:, qk_cols:]).astype(F32)
        ng = jnp.broadcast_to(mlstm_norm_g[i].astype(F32).reshape(MLSTM_DV, 1), (MLSTM_DV, PAIR))
        hmt = _mlstm(qk, vmt, omt, gri, grf, gci, gcf, cw, ng, ug, l2, batch, seq)

        lam_init = 0.8 - 0.6 * math.exp(-0.3 * i)
        row64 = lambda a: a[i].reshape(1, DIFF_D).astype(F32)
        hdt = _attention(lam_init, tbl, qd, kd, vdt, pcol, prow, row64(lam_q1), row64(lam_k1), row64(lam_q2),
                        row64(lam_k2), diff_subln_g[i].reshape(1, DIFF_DV).astype(F32), batch, seq)

        h = _mixer(i, h, hmt, hdt, p_tok, *mixer_weights, batch, seq)
    return h.reshape(batch, seq, D_MODEL)
```

```python
import functools
import math

import numpy as np
import jax
import jax.numpy as jnp
from jax import lax
from jax.experimental import pallas as pl
from jax.experimental.pallas import tpu as pltpu

F32 = jnp.float32
BF16 = jnp.bfloat16

LANES = 128
SUBLANES = 8
MXU_TILE = 256
V7X_VMEM_BYTES = 64 * 1024 * 1024

D_MODEL = 1024
N_HEADS = 4
MLSTM_DK = 64
MLSTM_DV = 128
MLSTM_CONV_W = 4
CHUNK = 64
PAIR = 2 * CHUNK
DIFF_D = 64
DIFF_DV = 128
N_BUCKETS = 32
MAX_DISTANCE = 128
D_FF = 2816
FFN_CONV_W = 3
PLE_DIM = 256
EPS = 1e-6
GROUP_W = N_HEADS * LANES
N_GATE_ROWS = 16

TM_IN = 512
TM_FFN = 512
FF_CHUNK = 256
N_FF_CHUNKS = D_FF // FF_CHUNK
TQ = 512
TK = 256
BIAS_ROWS = 32
BIAS_COLS = LANES
ATTN_NB = 2
NEG_BIG = -1e30
M_INIT = -1e29
ONES_ROWS = 16
V_ROWS = DIFF_DV + ONES_ROWS
LOG2E = math.log2(math.e)
VMEM_LIMIT = V7X_VMEM_BYTES * 7 // 8


def _dot(a, b):
    return jnp.dot(a, b, preferred_element_type=F32)


def _dot_nt(a, b):
    return lax.dot_general(a, b, (((1,), (1,)), ((), ())), preferred_element_type=F32)


def _dot_tn(a, b):
    return lax.dot_general(a, b, (((0,), (0,)), ((), ())), preferred_element_type=F32)


def _sigmoid(x):
    return 1.0 / (1.0 + jnp.exp(-x))


def _log_sigmoid(x):
    return jnp.minimum(x, 0.0) - jnp.log1p(jnp.exp(-jnp.abs(x)))


def _bucket_thresholds():
    max_exact = N_BUCKETS // 2
    thr = []
    for v in range(1, N_BUCKETS):
        if v <= max_exact:
            thr.append(v)
            continue
        edge = max_exact * (MAX_DISTANCE / max_exact) ** ((v - max_exact) / (N_BUCKETS - max_exact))
        assert abs(edge - round(edge)) > 1e-3, edge
        thr.append(int(math.ceil(edge)))
    assert all(a < b for a, b in zip(thr, thr[1:])), thr
    return tuple(thr)


_BUCKET_THR = _bucket_thresholds()


def _group_mean_square(z, gsum_ref):
    sq = (z * z).astype(BF16)
    width = gsum_ref.shape[0]
    sums = [_dot(sq[:, c0:c0 + width], gsum_ref[...]) for c0 in range(0, z.shape[1], width)]
    return jnp.concatenate(sums, axis=1) * (1.0 / DIFF_D)


def _with_ones_rows(dst_ref, vt):
    for hh in range(N_HEADS):
        dst_ref[hh * V_ROWS:hh * V_ROWS + DIFF_DV, :] = vt[hh * DIFF_DV:(hh + 1) * DIFF_DV, :]
        dst_ref[hh * V_ROWS + DIFF_DV:(hh + 1) * V_ROWS, :] = jnp.ones((ONES_ROWS, vt.shape[1]), BF16)


def _inproj_kernel(h_ref, g_ref, wm_ref, wt_ref, wg_ref, bcol_ref, brow_ref, gsum_ref, qg_ref, kg_ref,
                   qk_ref, vmt_ref, omt_ref, gri_ref, grf_ref, gci_ref, gcf_ref, qd_ref, kd_ref, vdt_ref):
    x = h_ref[...]
    ms = jnp.mean(x * x, axis=-1, keepdims=True)
    u = (x * lax.rsqrt(ms + EPS) * g_ref[...]).astype(BF16)
    w = GROUP_W
    qk_ref[...] = _dot(u, wm_ref[:, 0:w])
    zq = _dot(u, wm_ref[:, w:2 * w])
    qd_ref[...] = (zq * lax.rsqrt(_group_mean_square(zq, gsum_ref) + EPS) * qg_ref[...]).astype(BF16)
    zk = _dot(u, wm_ref[:, 2 * w:3 * w])
    kd_ref[...] = (zk * lax.rsqrt(_group_mean_square(zk, gsum_ref) + EPS) * kg_ref[...]).astype(BF16)
    _with_ones_rows(vmt_ref, _dot_nt(wt_ref[0:w, :], u).astype(BF16))
    omt_ref[...] = _dot_nt(wt_ref[w:2 * w, :], u)
    _with_ones_rows(vdt_ref, _dot_nt(wt_ref[2 * w:3 * w, :], u).astype(BF16))
    gc = _dot_nt(u, wg_ref[...]) + bcol_ref[...]
    gci_ref[...] = gc[:, 0:N_GATE_ROWS]
    gcf_ref[...] = gc[:, N_GATE_ROWS:2 * N_GATE_ROWS]
    gr = _dot_nt(wg_ref[...], u) + brow_ref[...]
    for ci in range(TM_IN // PAIR):
        gri_ref[ci] = gr[0:SUBLANES, ci * PAIR:(ci + 1) * PAIR]
        grf_ref[ci] = gr[N_GATE_ROWS:N_GATE_ROWS + SUBLANES, ci * PAIR:(ci + 1) * PAIR]


def _inproj(h, g, wm, wt, wg, bcol, brow, gsum, qg, kg):
    n_tok = h.shape[0]
    grid = (n_tok // TM_IN,)
    const = lambda shape: pl.BlockSpec(shape, lambda i: (0,) * len(shape))
    tile = lambda width: pl.BlockSpec((TM_IN, width), lambda i: (i, 0))
    tile_t = lambda rows: pl.BlockSpec((rows, TM_IN), lambda i: (0, i))
    gate_rows = pl.BlockSpec((TM_IN // PAIR, SUBLANES, PAIR), lambda i: (i, 0, 0))
    out_shape = (
        jax.ShapeDtypeStruct((n_tok, GROUP_W), F32),
        jax.ShapeDtypeStruct((N_HEADS * V_ROWS, n_tok), BF16),
        jax.ShapeDtypeStruct((GROUP_W, n_tok), F32),
        jax.ShapeDtypeStruct((n_tok // PAIR, SUBLANES, PAIR), F32),
        jax.ShapeDtypeStruct((n_tok // PAIR, SUBLANES, PAIR), F32),
        jax.ShapeDtypeStruct((n_tok, N_GATE_ROWS), F32),
        jax.ShapeDtypeStruct((n_tok, N_GATE_ROWS), F32),
        jax.ShapeDtypeStruct((n_tok, GROUP_W), BF16),
        jax.ShapeDtypeStruct((n_tok, GROUP_W), BF16),
        jax.ShapeDtypeStruct((N_HEADS * V_ROWS, n_tok), BF16),
    )
    out_specs = (
        tile(GROUP_W), tile_t(N_HEADS * V_ROWS), tile_t(GROUP_W), gate_rows, gate_rows,
        tile(N_GATE_ROWS), tile(N_GATE_ROWS), tile(GROUP_W), tile(GROUP_W), tile_t(N_HEADS * V_ROWS),
    )
    in_specs = [
        tile(D_MODEL), const((1, D_MODEL)), const(wm.shape), const(wt.shape), const(wg.shape),
        const(bcol.shape), const(brow.shape), const(gsum.shape),
        const((1, GROUP_W)), const((1, GROUP_W)),
    ]
    return pl.pallas_call(
        _inproj_kernel, grid=grid, in_specs=in_specs, out_specs=out_specs, out_shape=out_shape,
        name="inproj",
        compiler_params=pltpu.CompilerParams(dimension_semantics=("arbitrary",), vmem_limit_bytes=VMEM_LIMIT),
    )(h, g, wm, wt, wg, bcol, brow, gsum, qg, kg)


def _split_hi_lo(x):
    hi = x.astype(BF16)
    return hi, (x - hi.astype(F32)).astype(BF16)


def _mlstm_kernel(qk_ref, vt_ref, ot_ref, gri_ref, grf_ref, gci_ref, gcf_ref, cw_ref, ng_ref, ug_ref, l2_ref,
                  out_ref, xpad_ref, cn_ref):
    seq = qk_ref.shape[0]
    halo = SUBLANES
    xpad_ref[0:halo, :] = jnp.zeros((halo, GROUP_W), F32)
    xpad_ref[halo:, :] = qk_ref[...]
    cn_ref[...] = jnp.zeros_like(cn_ref)

    first8 = lax.broadcasted_iota(jnp.int32, (SUBLANES, PAIR), 1) < CHUNK
    first = lax.broadcasted_iota(jnp.int32, (1, PAIR), 1) < CHUNK
    upper = lax.broadcasted_iota(jnp.int32, (PAIR, PAIR), 1) >= MLSTM_DK
    key_t = lax.broadcasted_iota(jnp.int32, (PAIR, PAIR), 0)
    qry_t = lax.broadcasted_iota(jnp.int32, (PAIR, PAIR), 1)
    chunk_start = jnp.where(qry_t < CHUNK, 0, CHUNK)
    cw = cw_ref[...]
    neg_inf = jnp.float32(-jnp.inf)

    def pair_step(c2, m_prev):
        r0 = pl.multiple_of(c2 * PAIR, PAIR)
        win = xpad_ref[pl.ds(r0, PAIR + halo), :]
        conv = cw[0:1, :] * win[halo - 3:halo - 3 + PAIR, :]
        for j in range(1, MLSTM_CONV_W):
            conv = conv + cw[j:j + 1, :] * win[halo - 3 + j:halo - 3 + j + PAIR, :]
        x = conv * _sigmoid(conv)

        gi = gri_ref[c2]
        lf_hi, lf_lo = _split_hi_lo(_log_sigmoid(grf_ref[c2]))
        bg = _dot(lf_hi, ug_ref[...]) + _dot(lf_lo, ug_ref[...])
        b_r = bg[:, 0:PAIR]
        g_a = bg[:, PAIR:2 * PAIR]
        g_b = bg[:, 2 * PAIR:3 * PAIR]
        a_r = jnp.where(first8, g_a, g_b) - b_r + gi
        max_a = jnp.max(jnp.where(first8, a_r, neg_inf), axis=1, keepdims=True)
        max_b = jnp.max(jnp.where(first8, neg_inf, a_r), axis=1, keepdims=True)
        m_a = jnp.maximum(g_a + m_prev, max_a)
        m_b = jnp.maximum(g_b + m_a, max_b)
        dec_a = jnp.exp(g_a + m_prev - m_a)
        dec_b = jnp.exp(g_b + m_a - m_b)
        w_r = jnp.exp(a_r - jnp.where(first8, m_a, m_b))
        e_r = b_r + jnp.where(first8, m_prev, m_a)
        lc_hi, lc_lo = _split_hi_lo(_log_sigmoid(gcf_ref[pl.ds(r0, PAIR), :]))
        x_c = _dot(l2_ref[...], lc_hi) + _dot(l2_ref[...], lc_lo) - gci_ref[pl.ds(r0, PAIR), :]

        heads = range(N_HEADS)
        xk, xq, vt, cn0, up_a, kq, cq_a, m_out, decay_t = [], [], [], [], [], [], [], [], []
        for hd in heads:
            xh = x[:, hd * LANES:(hd + 1) * LANES]
            xs = pltpu.roll(xh, MLSTM_DK, 1)
            xk.append(jnp.where(upper, xh * (MLSTM_DK ** -0.5), 0.0).astype(BF16))
            xq.append(jnp.where(upper, xs, 0.0).astype(BF16))
            vt.append(vt_ref[hd * V_ROWS:(hd + 1) * V_ROWS, pl.ds(r0, PAIR)])
            cn0.append(cn_ref[hd])
            w_a = jnp.where(first, w_r[hd:hd + 1, :], 0.0)
            up_a.append(_dot((vt[hd].astype(F32) * w_a).astype(BF16), xk[hd]))
            kq.append(_dot_nt(xk[hd], xq[hd]))
            cq_a.append(_dot_nt(cn0[hd].astype(BF16), xq[hd]))
            dmat = b_r[hd:hd + 1, :] - x_c[:, hd:hd + 1]
            dmat = jnp.where(key_t <= qry_t, jnp.where(key_t >= chunk_start, dmat, neg_inf), neg_inf)
            m_out.append(jnp.maximum(e_r[hd:hd + 1, :], jnp.max(dmat, axis=0, keepdims=True)))
            decay_t.append(jnp.exp(dmat - m_out[hd]))
        cn1, cq_b, up_b = [], [], []
        for hd in heads:
            cn1.append(dec_a[hd:hd + 1, :] * cn0[hd] + up_a[hd])
            w_b = jnp.where(first, 0.0, w_r[hd:hd + 1, :])
            up_b.append(_dot((vt[hd].astype(F32) * w_b).astype(BF16), xk[hd]))
            cq_b.append(_dot_nt(cn1[hd].astype(BF16), xq[hd]))
        sv = []
        for hd in heads:
            cn_ref[hd] = dec_b[hd:hd + 1, :] * cn1[hd] + up_b[hd]
            sv.append(_dot(vt[hd], (kq[hd] * decay_t[hd]).astype(BF16)))
        for hd in heads:
            rows = slice(hd * MLSTM_DV, (hd + 1) * MLSTM_DV)
            inter = jnp.exp(e_r[hd:hd + 1, :] - m_out[hd])
            tot = inter * jnp.where(first, cq_a[hd], cq_b[hd]) + sv[hd]
            den = tot[MLSTM_DV:MLSTM_DV + 1, :]
            hh = tot[0:MLSTM_DV, :] / jnp.maximum(jnp.abs(den), jnp.exp(-m_out[hd]))
            hn = hh * lax.rsqrt(jnp.mean(hh * hh, axis=0, keepdims=True) + EPS) * ng_ref[...]
            og = ot_ref[rows, pl.ds(r0, PAIR)]
            out_ref[rows, pl.ds(r0, PAIR)] = (hn * _sigmoid(og)).astype(BF16)
        return m_b

    lax.fori_loop(0, seq // PAIR, pair_step, jnp.zeros((SUBLANES, PAIR), F32), unroll=2)


def _mlstm(qk, vmt, omt, gri, grf, gci, gcf, cw, ng, ug, l2, batch, seq):
    const = lambda arr: pl.BlockSpec(arr.shape, lambda b: (0,) * arr.ndim)
    in_specs = [
        pl.BlockSpec((seq, GROUP_W), lambda b: (b, 0)),
        pl.BlockSpec((N_HEADS * V_ROWS, seq), lambda b: (0, b)),
        pl.BlockSpec((GROUP_W, seq), lambda b: (0, b)),
        pl.BlockSpec((seq // PAIR, SUBLANES, PAIR), lambda b: (b, 0, 0)),
        pl.BlockSpec((seq // PAIR, SUBLANES, PAIR), lambda b: (b, 0, 0)),
        pl.BlockSpec((seq, N_GATE_ROWS), lambda b: (b, 0)),
        pl.BlockSpec((seq, N_GATE_ROWS), lambda b: (b, 0)),
        const(cw), const(ng), const(ug), const(l2),
    ]
    return pl.pallas_call(
        _mlstm_kernel, grid=(batch,), in_specs=in_specs,
        out_specs=pl.BlockSpec((GROUP_W, seq), lambda b: (0, b)),
        out_shape=jax.ShapeDtypeStruct((GROUP_W, batch * seq), BF16),
        scratch_shapes=[
            pltpu.VMEM((seq + SUBLANES, GROUP_W), F32),
            pltpu.VMEM((N_HEADS, V_ROWS, LANES), F32),
        ],
        name="mlstm",
        compiler_params=pltpu.CompilerParams(dimension_semantics=("arbitrary",), vmem_limit_bytes=VMEM_LIMIT),
    )(qk, vmt, omt, gri, grf, gci, gcf, cw, ng, ug, l2)


def _attn_kernel(lam_init, tbl_ref, kmax_ref, qmin_ref, q_ref, k_ref, vt_ref, pcol_ref, prow_ref,
                 lq1_ref, lk1_ref, lq2_ref, lk2_ref, sg_ref, out_ref, bias_ref, qc_ref, s_ref, p_ref, acc_ref):
    hd = pl.program_id(0)
    qi = pl.program_id(1)
    b = pl.program_id(2)
    n_kv = (qi + 1) * (TQ // TK)

    @pl.when(b == 0)
    def _build_bias():
        rowi = lax.broadcasted_iota(jnp.int32, (BIAS_ROWS, BIAS_COLS), 0)
        coli = lax.broadcasted_iota(jnp.int32, (BIAS_ROWS, BIAS_COLS), 1)
        row = lambda c, v: tbl_ref[c * N_BUCKETS + v:c * N_BUCKETS + v + 1, :]
        last = N_BUCKETS - 1

        def build(t, _):
            r0 = pl.multiple_of(t * BIAS_ROWS, BIAS_ROWS)
            pk = pcol_ref[pl.ds(r0, BIAS_ROWS), :]
            kmax = kmax_ref[t]
            for j in range(TQ // BIAS_COLS):
                lanes = slice(j * BIAS_COLS, (j + 1) * BIAS_COLS)
                q0 = qi * TQ + j * BIAS_COLS
                causal = rowi + r0 <= coli + q0
                flat = jnp.logical_or(qmin_ref[qi * (TQ // BIAS_COLS) + j] - kmax >= _BUCKET_THR[-1],
                                      r0 > q0 + BIAS_COLS - 1)

                @pl.when(flat)
                def _flat():
                    bias_ref[0, pl.ds(r0, BIAS_ROWS), lanes] = jnp.where(causal, row(0, last), NEG_BIG)
                    bias_ref[1, pl.ds(r0, BIAS_ROWS), lanes] = jnp.where(causal, row(1, last), NEG_BIG)

                @pl.when(jnp.logical_not(flat))
                def _lookup():
                    dist = jnp.maximum(prow_ref[:, lanes] - pk, 0)
                    b0 = jnp.broadcast_to(row(0, 0), (BIAS_ROWS, BIAS_COLS))
                    b1 = jnp.broadcast_to(row(1, 0), (BIAS_ROWS, BIAS_COLS))
                    for v, thr in enumerate(_BUCKET_THR, start=1):
                        ge = dist >= thr
                        b0 = jnp.where(ge, row(0, v), b0)
                        b1 = jnp.where(ge, row(1, v), b1)
                    bias_ref[0, pl.ds(r0, BIAS_ROWS), lanes] = jnp.where(causal, b0, NEG_BIG)
                    bias_ref[1, pl.ds(r0, BIAS_ROWS), lanes] = jnp.where(causal, b1, NEG_BIG)
            return 0

        lax.fori_loop(0, n_kv * (TK // BIAS_ROWS), build, 0)

    seq = k_ref.shape[1]
    lane = lax.broadcasted_iota(jnp.int32, (TQ, LANES), 1)
    for e in range(ATTN_NB):
        q = q_ref[e]
        zero = jnp.zeros_like(q)
        qc_ref[e, 0] = jnp.where(lane < DIFF_D, q, zero)
        qc_ref[e, 1] = jnp.where(lane >= DIFF_D, q, zero)
    acc_ref[...] = jnp.zeros_like(acc_ref)

    chains = [(e, c) for e in range(ATTN_NB) for c in range(2)]

    def scores(t, slot):
        k0 = t * TK
        for e in range(ATTN_NB):
            kb = k_ref[e, pl.ds(k0, TK), :]
            for c in range(2):
                s_ref[slot, e, c] = _dot_nt(kb, qc_ref[e, c]) + bias_ref[c, pl.ds(k0, TK), :]

    def numerators(m_state, slot, lanes=slice(None)):
        m_next, alpha_next = [], []
        for i, (e, c) in enumerate(chains):
            m_prev = m_state[i][:, lanes]
            m_new = jnp.maximum(m_prev, jnp.max(s_ref[slot, e, c, :, lanes], axis=0, keepdims=True))
            alpha_next.append(jnp.exp2(m_prev - m_new))
            m_next.append(m_new)
        for i, (e, c) in enumerate(chains):
            p_ref[slot, e, c, :, lanes] = jnp.exp2(s_ref[slot, e, c, :, lanes] - m_next[i]).astype(BF16)
        return tuple(m_next), tuple(alpha_next)

    def accumulate(t, alpha, slot, lanes=slice(None)):
        for e in range(ATTN_NB):
            v0 = e * seq + t * TK
            vt = vt_ref[:, pl.ds(v0, TK)]
            for c in range(2):
                acc_ref[e, c, :, lanes] = (alpha[2 * e + c] * acc_ref[e, c, :, lanes]
                                           + _dot(vt, p_ref[slot, e, c, :, lanes]))

    def run(n_tiles):
        scores(0, 0)
        scores(1, 1)
        m_state, alpha = numerators(tuple(jnp.full((1, TQ), M_INIT, F32) for _ in chains), 0)
        for t in range(2, n_tiles, 2):
            scores(t, 0)
            m_state, alpha1 = numerators(m_state, 1)
            accumulate(t - 2, alpha, 0)
            scores(t + 1, 1)
            m_state, alpha = numerators(m_state, 0)
            accumulate(t - 1, alpha1, 1)
        live = slice(TQ - TK, TQ)
        _, alpha1 = numerators(m_state, 1, live)
        accumulate(n_tiles - 2, alpha, 0)
        accumulate(n_tiles - 1, alpha1, 1, live)

        lam = (jnp.exp(jnp.sum(lq1_ref[...] * lk1_ref[...], axis=1, keepdims=True))
               - jnp.exp(jnp.sum(lq2_ref[...] * lk2_ref[...], axis=1, keepdims=True)) + lam_init)
        for e in range(ATTN_NB):
            a0 = acc_ref[e, 0]
            a1 = acc_ref[e, 1]
            o = a0[0:DIFF_DV] / a0[DIFF_DV:DIFF_DV + 1] - lam * (a1[0:DIFF_DV] / a1[DIFF_DV:DIFF_DV + 1])
            on = o * lax.rsqrt(jnp.mean(o * o, axis=0, keepdims=True) + EPS) * sg_ref[...]
            out_ref[e] = (on * (1.0 - lam_init)).astype(BF16)

    assert (TQ // TK) % 2 == 0
    for k in range(seq // TQ):
        pl.when(qi == k)(functools.partial(run, (k + 1) * (TQ // TK)))


def _attention(lam_init, tbl, qd, kd, vdt, pcol, prow, lq1, lk1, lq2, lk2, sg, batch, seq):
    q3, k3 = (t.reshape(batch, seq, GROUP_W) for t in (qd, kd))
    vec = lambda n: pl.BlockSpec((1, n), lambda h, i, b: (0, 0))
    kmax = jnp.max(prow.reshape(seq // BIAS_ROWS, BIAS_ROWS), axis=1)
    qmin = jnp.min(prow.reshape(seq // BIAS_COLS, BIAS_COLS), axis=1)
    in_specs = [
        pl.BlockSpec((None, 2 * N_BUCKETS, BIAS_COLS), lambda h, i, b: (h, 0, 0)),
        pl.BlockSpec(memory_space=pltpu.SMEM),
        pl.BlockSpec(memory_space=pltpu.SMEM),
        pl.BlockSpec((ATTN_NB, TQ, LANES), lambda h, i, b: (b, i, h)),
        pl.BlockSpec((ATTN_NB, seq, LANES), lambda h, i, b: (b, 0, h)),
        pl.BlockSpec((V_ROWS, ATTN_NB * seq), lambda h, i, b: (h, b)),
        pl.BlockSpec((seq, BIAS_COLS), lambda h, i, b: (0, 0)),
        pl.BlockSpec((1, TQ), lambda h, i, b: (0, i)),
        vec(DIFF_D), vec(DIFF_D), vec(DIFF_D), vec(DIFF_D),
        pl.BlockSpec((DIFF_DV, TQ), lambda h, i, b: (0, 0)),
    ]
    out = pl.pallas_call(
        functools.partial(_attn_kernel, lam_init),
        grid=(N_HEADS, seq // TQ, batch // ATTN_NB), in_specs=in_specs,
        out_specs=pl.BlockSpec((ATTN_NB, DIFF_DV, TQ), lambda h, i, b: (b, h, i)),
        out_shape=jax.ShapeDtypeStruct((batch, GROUP_W, seq), BF16),
        scratch_shapes=[
            pltpu.VMEM((2, seq, TQ), F32),
            pltpu.VMEM((ATTN_NB, 2, TQ, LANES), BF16),
            pltpu.VMEM((2, ATTN_NB, 2, TK, TQ), F32),
            pltpu.VMEM((2, ATTN_NB, 2, TK, TQ), BF16),
            pltpu.VMEM((ATTN_NB, 2, V_ROWS, TQ), F32),
        ],
        name="diff_attn",
        compiler_params=pltpu.CompilerParams(
            dimension_semantics=("arbitrary", "arbitrary", "arbitrary"), vmem_limit_bytes=VMEM_LIMIT),
    )(tbl, kmax, qmin, q3, k3, vdt, pcol, prow, lq1, lk1, lq2, lk2, jnp.broadcast_to(sg.reshape(DIFF_DV, 1), (DIFF_DV, TQ)))
    return out


def _gelu(x):
    return 0.5 * x * (1.0 + lax.erf(x * (2.0 ** -0.5)))


def _rms(x, g):
    return x * lax.rsqrt(jnp.mean(x * x, axis=-1, keepdims=True) + EPS) * g


def _mixer_kernel(h_ref, hmt_ref, hdt_ref, p_ref, wo_ref, gf_ref, wu_ref, cw_ref, cb_ref, wd_ref, gp_ref,
                  wpg_ref, wpp_ref, out_ref, sg_ref, sv_ref, cg_ref, cv_ref, acc_ref, u_ref):
    tm = h_ref.shape[0]
    halo = SUBLANES

    @pl.when(pl.program_id(1) == 0)
    def _reset_conv_history():
        cg_ref[...] = jnp.zeros_like(cg_ref)
        cv_ref[...] = jnp.zeros_like(cv_ref)

    h1 = h_ref[...] + _dot_tn(jnp.concatenate([hmt_ref[...], hdt_ref[...]], axis=0), wo_ref[...])
    u_ref[...] = _rms(h1, gf_ref[...]).astype(BF16)
    acc_ref[...] = jnp.zeros_like(acc_ref)

    def up_stage(j, slot):
        c0 = pl.multiple_of(j * FF_CHUNK, FF_CHUNK)
        sg_ref[slot, halo:, :] = _dot(u_ref[...], wu_ref[:, pl.ds(c0, FF_CHUNK)])
        sv_ref[slot, halo:, :] = _dot(u_ref[...], wu_ref[:, pl.ds(D_FF + c0, FF_CHUNK)])

    def conv_branch(j, slot, c0, stage_ref, hist_ref):
        stage_ref[slot, 0:halo, :] = hist_ref[j]
        hist_ref[j] = stage_ref[slot, tm:tm + halo, :]
        cw = cw_ref[:, pl.ds(c0, FF_CHUNK)]
        return (cw[0:1, :] * stage_ref[slot, halo - 2:halo - 2 + tm, :]
                + cw[1:2, :] * stage_ref[slot, halo - 1:halo - 1 + tm, :]
                + cw[2:3, :] * stage_ref[slot, halo:, :] + cb_ref[:, pl.ds(c0, FF_CHUNK)])

    def act_stage(j, slot):
        c0 = pl.multiple_of(j * FF_CHUNK, FF_CHUNK)
        gate = conv_branch(j, slot, c0, sg_ref, cg_ref)
        val = conv_branch(j, slot, D_FF + c0, sv_ref, cv_ref)
        act = (_gelu(gate) * val).astype(BF16)
        acc_ref[...] += _dot(act, wd_ref[j])

    def chunk_pair(i, _):
        j = 2 * i
        up_stage(j + 1, 1)
        act_stage(j, 0)
        up_stage(j + 2, 0)
        act_stage(j + 1, 1)
        return 0

    assert N_FF_CHUNKS % 2 == 1
    up_stage(0, 0)
    for i in range(N_FF_CHUNKS // 2):
        chunk_pair(i, 0)
    act_stage(N_FF_CHUNKS - 1, 0)

    h2 = h1 + acc_ref[...]
    u3 = _rms(h2, gp_ref[...]).astype(BF16)
    ple_gate = _sigmoid(_dot(u3, wpg_ref[...]))
    out_ref[...] = h2 + ple_gate * _dot(p_ref[...].astype(BF16), wpp_ref[...])


def _mixer(layer, h, hmt, hdt, p, wo, gf, wu, cw, cb, wd, gp, wpg, wpp, batch, seq):
    tiles = seq // TM_FFN
    tile = lambda width: pl.BlockSpec((TM_FFN, width), lambda b, t: (b * tiles + t, 0))

    def of_layer(arr):
        nd = arr.ndim
        return pl.BlockSpec((None,) + arr.shape[1:], lambda b, t: (layer,) + (0,) * (nd - 1),
                            pipeline_mode=pl.Buffered(1))

    weights = (wo, gf, wu, cw, cb, wd, gp, wpg, wpp)
    in_specs = [
        tile(D_MODEL),
        pl.BlockSpec((GROUP_W, TM_FFN), lambda b, t: (0, b * tiles + t)),
        pl.BlockSpec((None, GROUP_W, TM_FFN), lambda b, t: (b, 0, t)),
        pl.BlockSpec((None, TM_FFN, PLE_DIM), lambda b, t: (layer, b * tiles + t, 0)),
    ] + [of_layer(w) for w in weights]
    return pl.pallas_call(
        _mixer_kernel, grid=(batch, tiles), in_specs=in_specs, out_specs=tile(D_MODEL),
        out_shape=jax.ShapeDtypeStruct(h.shape, F32),
        scratch_shapes=[
            pltpu.VMEM((2, TM_FFN + SUBLANES, FF_CHUNK), F32),
            pltpu.VMEM((2, TM_FFN + SUBLANES, FF_CHUNK), F32),
            pltpu.VMEM((N_FF_CHUNKS, SUBLANES, FF_CHUNK), F32),
            pltpu.VMEM((N_FF_CHUNKS, SUBLANES, FF_CHUNK), F32),
            pltpu.VMEM((TM_FFN, D_MODEL), F32),
            pltpu.VMEM((TM_FFN, D_MODEL), BF16),
        ],
        name="mixer",
        compiler_params=pltpu.CompilerParams(
            dimension_semantics=("arbitrary", "arbitrary"), vmem_limit_bytes=VMEM_LIMIT),
    )(h, hmt, hdt, p, *weights)


def _head_interleave(qcols, kcols):
    lead = qcols.shape[:-1]
    qh = qcols.reshape(lead + (N_HEADS, MLSTM_DK))
    kh = kcols.reshape(lead + (N_HEADS, MLSTM_DK))
    return jnp.concatenate([qh, kh], axis=-1).reshape(lead + (GROUP_W,))


def kernel(x, p, positions, rel_bias, ln_mix_g, w_in, mlstm_conv_w, b_igate, b_fgate, mlstm_norm_g, q_norm_g, k_norm_g, lam_q1, lam_k1, lam_q2, lam_k2, diff_subln_g, w_out, ln_ffn_g, w_up, ffn_conv_w, ffn_conv_b, w_down, ln_ple_g, w_ple_gate, w_ple_proj):
    batch, seq, _ = x.shape
    depth = w_in.shape[0]
    n_tok = batch * seq
    qk_cols = N_HEADS * MLSTM_DK
    col_sizes = [qk_cols, qk_cols, GROUP_W, GROUP_W, N_HEADS, N_HEADS, GROUP_W, GROUP_W, GROUP_W]
    offs = np.concatenate([[0], np.cumsum(col_sizes)])
    sl = lambda a, j: a[..., int(offs[j]):int(offs[j + 1])]

    tbl = jnp.broadcast_to(
        (jnp.transpose(rel_bias.astype(F32), (1, 2, 0)) * LOG2E).reshape(N_HEADS, 2 * N_BUCKETS, 1),
        (N_HEADS, 2 * N_BUCKETS, BIAS_COLS))
    pcol = jnp.broadcast_to(positions.astype(jnp.int32).reshape(seq, 1), (seq, BIAS_COLS))
    prow = positions.astype(jnp.int32).reshape(1, seq)
    gsum = jnp.asarray(np.kron(np.eye(MXU_TILE // DIFF_D), np.ones((DIFF_D, DIFF_D))), BF16)
    t_idx = np.arange(PAIR)
    same_chunk = (t_idx[:, None] // CHUNK) == (t_idx[None, :] // CHUNK)
    prefix = same_chunk & (t_idx[:, None] <= t_idx[None, :])
    total_a = np.broadcast_to(t_idx[:, None] < CHUNK, (PAIR, PAIR))
    ug = jnp.asarray(np.concatenate([prefix, total_a, ~total_a], axis=1), BF16)
    l2 = jnp.asarray(prefix.T, BF16)

    p_tok = p.reshape(depth, n_tok, PLE_DIM)
    row = lambda a: a.astype(F32).reshape(depth, 1, a.shape[-1])
    mixer_weights = (
        w_out.astype(BF16), row(ln_ffn_g), w_up.astype(BF16), ffn_conv_w.astype(F32), row(ffn_conv_b),
        w_down.astype(BF16).reshape(depth, N_FF_CHUNKS, FF_CHUNK, D_MODEL), row(ln_ple_g),
        w_ple_gate.astype(BF16), w_ple_proj.astype(BF16))

    h = x.reshape(n_tok, D_MODEL)
    for i in range(depth):
        wi = w_in[i]
        wm = jnp.concatenate([_head_interleave(sl(wi, 0), sl(wi, 1)), sl(wi, 6), sl(wi, 7)], axis=-1).astype(BF16)
        wt = jnp.concatenate([sl(wi, 2), sl(wi, 3), sl(wi, 8)], axis=-1).T.astype(BF16)
        pad_rows = lambda a: jnp.zeros((N_GATE_ROWS,) + a.shape[1:], F32).at[:N_HEADS].set(a.astype(F32))
        wg = jnp.concatenate([pad_rows(sl(wi, 4).T), pad_rows(sl(wi, 5).T)], axis=0).astype(BF16)
        gate_bias = jnp.concatenate([pad_rows(b_igate[i]), pad_rows(b_fgate[i])])
        qg = jnp.tile(q_norm_g[i].astype(F32), GROUP_W // DIFF_D).reshape(1, GROUP_W) * (DIFF_D ** -0.5 * LOG2E)
        kg = jnp.tile(k_norm_g[i].astype(F32), GROUP_W // DIFF_D).reshape(1, GROUP_W)
        qk, vmt, omt, gri, grf, gci, gcf, qd, kd, vdt = _inproj(
            h, ln_mix_g[i].reshape(1, D_MODEL), wm, wt, wg, gate_bias.reshape(1, -1), gate_bias.reshape(-1, 1),
            gsum, qg, kg)

        cw = _head_interleave(mlstm_conv_w[i][:, :qk_cols], mlstm_conv_w[i][:, qk_cols:]).astype(F32)
        ng = jnp.broadcast_to(mlstm_norm_g[i].astype(F32).reshape(MLSTM_DV, 1), (MLSTM_DV, PAIR))
        hmt = _mlstm(qk, vmt, omt, gri, grf, gci, gcf, cw, ng, ug, l2, batch, seq)

        lam_init = 0.8 - 0.6 * math.exp(-0.3 * i)
        row64 = lambda a: a[i].reshape(1, DIFF_D).astype(F32)
        hdt = _attention(lam_init, tbl, qd, kd, vdt, pcol, prow, row64(lam_q1), row64(lam_k1), row64(lam_q2),
                        row64(lam_k2), diff_subln_g[i].reshape(1, DIFF_DV).astype(F32), batch, seq)

        h = _mixer(i, h, hmt, hdt, p_tok, *mixer_weights, batch, seq)
    return h.reshape(batch, seq, D_MODEL)
```

```python
import functools
import math

import numpy as np
import jax
import jax.numpy as jnp
from jax import lax
from jax.experimental import pallas as pl
from jax.experimental.pallas import tpu as pltpu

F32 = jnp.float32
BF16 = jnp.bfloat16

LANES = 128
SUBLANES = 8
MXU_TILE = 256
V7X_VMEM_BYTES = 64 * 1024 * 1024

D_MODEL = 1024
N_HEADS = 4
MLSTM_DK = 64
MLSTM_DV = 128
MLSTM_CONV_W = 4
CHUNK = 64
PAIR = 2 * CHUNK
DIFF_D = 64
DIFF_DV = 128
N_BUCKETS = 32
MAX_DISTANCE = 128
D_FF = 2816
FFN_CONV_W = 3
PLE_DIM = 256
EPS = 1e-6
GROUP_W = N_HEADS * LANES
N_GATE_ROWS = 16

TM_IN = 512
TM_FFN = 512
FF_CHUNK = 256
N_FF_CHUNKS = D_FF // FF_CHUNK
TQ = 512
TK = 256
BIAS_ROWS = 32
BIAS_COLS = LANES
ATTN_NB = 2
NEG_BIG = -1e30
M_INIT = -1e29
ONES_ROWS = 16
V_ROWS = DIFF_DV + ONES_ROWS
LOG2E = math.log2(math.e)
VMEM_LIMIT = V7X_VMEM_BYTES * 7 // 8


def _dot(a, b):
    return jnp.dot(a, b, preferred_element_type=F32)


def _dot_nt(a, b):
    return lax.dot_general(a, b, (((1,), (1,)), ((), ())), preferred_element_type=F32)


def _dot_tn(a, b):
    return lax.dot_general(a, b, (((0,), (0,)), ((), ())), preferred_element_type=F32)


def _sigmoid(x):
    return 1.0 / (1.0 + jnp.exp(-x))


def _log_sigmoid(x):
    return jnp.minimum(x, 0.0) - jnp.log1p(jnp.exp(-jnp.abs(x)))


def _bucket_thresholds():
    max_exact = N_BUCKETS // 2
    thr = []
    for v in range(1, N_BUCKETS):
        if v <= max_exact:
            thr.append(v)
            continue
        edge = max_exact * (MAX_DISTANCE / max_exact) ** ((v - max_exact) / (N_BUCKETS - max_exact))
        assert abs(edge - round(edge)) > 1e-3, edge
        thr.append(int(math.ceil(edge)))
    assert all(a < b for a, b in zip(thr, thr[1:])), thr
    return tuple(thr)


_BUCKET_THR = _bucket_thresholds()


def _group_mean_square(z, gsum_ref):
    sq = (z * z).astype(BF16)
    width = gsum_ref.shape[0]
    sums = [_dot(sq[:, c0:c0 + width], gsum_ref[...]) for c0 in range(0, z.shape[1], width)]
    return jnp.concatenate(sums, axis=1) * (1.0 / DIFF_D)


def _with_ones_rows(dst_ref, vt):
    for hh in range(N_HEADS):
        dst_ref[hh * V_ROWS:hh * V_ROWS + DIFF_DV, :] = vt[hh * DIFF_DV:(hh + 1) * DIFF_DV, :]
        dst_ref[hh * V_ROWS + DIFF_DV:(hh + 1) * V_ROWS, :] = jnp.ones((ONES_ROWS, vt.shape[1]), BF16)


def _inproj_kernel(h_ref, g_ref, wm_ref, wt_ref, wg_ref, bcol_ref, brow_ref, gsum_ref, qg_ref, kg_ref,
                   qk_ref, vmt_ref, omt_ref, gri_ref, grf_ref, gci_ref, gcf_ref, qd_ref, kd_ref, vdt_ref):
    x = h_ref[...]
    ms = jnp.mean(x * x, axis=-1, keepdims=True)
    u = (x * lax.rsqrt(ms + EPS) * g_ref[...]).astype(BF16)
    w = GROUP_W
    qk_ref[...] = _dot(u, wm_ref[:, 0:w])
    zq = _dot(u, wm_ref[:, w:2 * w])
    qd_ref[...] = (zq * lax.rsqrt(_group_mean_square(zq, gsum_ref) + EPS) * qg_ref[...]).astype(BF16)
    zk = _dot(u, wm_ref[:, 2 * w:3 * w])
    kd_ref[...] = (zk * lax.rsqrt(_group_mean_square(zk, gsum_ref) + EPS) * kg_ref[...]).astype(BF16)
    _with_ones_rows(vmt_ref, _dot_nt(wt_ref[0:w, :], u).astype(BF16))
    omt_ref[...] = _dot_nt(wt_ref[w:2 * w, :], u)
    _with_ones_rows(vdt_ref, _dot_nt(wt_ref[2 * w:3 * w, :], u).astype(BF16))
    gc = _dot_nt(u, wg_ref[...]) + bcol_ref[...]
    gci_ref[...] = gc[:, 0:N_GATE_ROWS]
    gcf_ref[...] = gc[:, N_GATE_ROWS:2 * N_GATE_ROWS]
    gr = _dot_nt(wg_ref[...], u) + brow_ref[...]
    for ci in range(TM_IN // PAIR):
        gri_ref[ci] = gr[0:SUBLANES, ci * PAIR:(ci + 1) * PAIR]
        grf_ref[ci] = gr[N_GATE_ROWS:N_GATE_ROWS + SUBLANES, ci * PAIR:(ci + 1) * PAIR]


def _inproj(h, g, wm, wt, wg, bcol, brow, gsum, qg, kg):
    n_tok = h.shape[0]
    grid = (n_tok // TM_IN,)
    const = lambda shape: pl.BlockSpec(shape, lambda i: (0,) * len(shape))
    tile = lambda width: pl.BlockSpec((TM_IN, width), lambda i: (i, 0))
    tile_t = lambda rows: pl.BlockSpec((rows, TM_IN), lambda i: (0, i))
    gate_rows = pl.BlockSpec((TM_IN // PAIR, SUBLANES, PAIR), lambda i: (i, 0, 0))
    out_shape = (
        jax.ShapeDtypeStruct((n_tok, GROUP_W), F32),
        jax.ShapeDtypeStruct((N_HEADS * V_ROWS, n_tok), BF16),
        jax.ShapeDtypeStruct((GROUP_W, n_tok), F32),
        jax.ShapeDtypeStruct((n_tok // PAIR, SUBLANES, PAIR), F32),
        jax.ShapeDtypeStruct((n_tok // PAIR, SUBLANES, PAIR), F32),
        jax.ShapeDtypeStruct((n_tok, N_GATE_ROWS), F32),
        jax.ShapeDtypeStruct((n_tok, N_GATE_ROWS), F32),
        jax.ShapeDtypeStruct((n_tok, GROUP_W), BF16),
        jax.ShapeDtypeStruct((n_tok, GROUP_W), BF16),
        jax.ShapeDtypeStruct((N_HEADS * V_ROWS, n_tok), BF16),
    )
    out_specs = (
        tile(GROUP_W), tile_t(N_HEADS * V_ROWS), tile_t(GROUP_W), gate_rows, gate_rows,
        tile(N_GATE_ROWS), tile(N_GATE_ROWS), tile(GROUP_W), tile(GROUP_W), tile_t(N_HEADS * V_ROWS),
    )
    in_specs = [
        tile(D_MODEL), const((1, D_MODEL)), const(wm.shape), const(wt.shape), const(wg.shape),
        const(bcol.shape), const(brow.shape), const(gsum.shape),
        const((1, GROUP_W)), const((1, GROUP_W)),
    ]
    return pl.pallas_call(
        _inproj_kernel, grid=grid, in_specs=in_specs, out_specs=out_specs, out_shape=out_shape,
        name="inproj",
        compiler_params=pltpu.CompilerParams(dimension_semantics=("arbitrary",), vmem_limit_bytes=VMEM_LIMIT),
    )(h, g, wm, wt, wg, bcol, brow, gsum, qg, kg)


def _split_hi_lo(x):
    hi = x.astype(BF16)
    return hi, (x - hi.astype(F32)).astype(BF16)


def _mlstm_kernel(qk_ref, vt_ref, ot_ref, gri_ref, grf_ref, gci_ref, gcf_ref, cw_ref, ng_ref, ug_ref, l2_ref,
                  out_ref, xpad_ref, cn_ref):
    seq = qk_ref.shape[0]
    halo = SUBLANES
    xpad_ref[0:halo, :] = jnp.zeros((halo, GROUP_W), F32)
    xpad_ref[halo:, :] = qk_ref[...]
    cn_ref[...] = jnp.zeros_like(cn_ref)

    first8 = lax.broadcasted_iota(jnp.int32, (SUBLANES, PAIR), 1) < CHUNK
    first = lax.broadcasted_iota(jnp.int32, (1, PAIR), 1) < CHUNK
    upper = lax.broadcasted_iota(jnp.int32, (PAIR, PAIR), 1) >= MLSTM_DK
    key_t = lax.broadcasted_iota(jnp.int32, (PAIR, PAIR), 0)
    qry_t = lax.broadcasted_iota(jnp.int32, (PAIR, PAIR), 1)
    chunk_start = jnp.where(qry_t < CHUNK, 0, CHUNK)
    cw = cw_ref[...]
    neg_inf = jnp.float32(-jnp.inf)

    def pair_step(c2, m_prev):
        r0 = pl.multiple_of(c2 * PAIR, PAIR)
        win = xpad_ref[pl.ds(r0, PAIR + halo), :]
        conv = cw[0:1, :] * win[halo - 3:halo - 3 + PAIR, :]
        for j in range(1, MLSTM_CONV_W):
            conv = conv + cw[j:j + 1, :] * win[halo - 3 + j:halo - 3 + j + PAIR, :]
        x = conv * _sigmoid(conv)

        gi = gri_ref[c2]
        lf_hi, lf_lo = _split_hi_lo(_log_sigmoid(grf_ref[c2]))
        bg = _dot(lf_hi, ug_ref[...]) + _dot(lf_lo, ug_ref[...])
        b_r = bg[:, 0:PAIR]
        g_a = bg[:, PAIR:2 * PAIR]
        g_b = bg[:, 2 * PAIR:3 * PAIR]
        a_r = jnp.where(first8, g_a, g_b) - b_r + gi
        max_a = jnp.max(jnp.where(first8, a_r, neg_inf), axis=1, keepdims=True)
        max_b = jnp.max(jnp.where(first8, neg_inf, a_r), axis=1, keepdims=True)
        m_a = jnp.maximum(g_a + m_prev, max_a)
        m_b = jnp.maximum(g_b + m_a, max_b)
        dec_a = jnp.exp(g_a + m_prev - m_a)
        dec_b = jnp.exp(g_b + m_a - m_b)
        w_r = jnp.exp(a_r - jnp.where(first8, m_a, m_b))
        e_r = b_r + jnp.where(first8, m_prev, m_a)
        lc_hi, lc_lo = _split_hi_lo(_log_sigmoid(gcf_ref[pl.ds(r0, PAIR), :]))
        x_c = _dot(l2_ref[...], lc_hi) + _dot(l2_ref[...], lc_lo) - gci_ref[pl.ds(r0, PAIR), :]

        heads = range(N_HEADS)
        xk, xq, vt, cn0, up_a, kq, cq_a, m_out, decay_t = [], [], [], [], [], [], [], [], []
        for hd in heads:
            xh = x[:, hd * LANES:(hd + 1) * LANES]
            xs = pltpu.roll(xh, MLSTM_DK, 1)
            xk.append(jnp.where(upper, xh * (MLSTM_DK ** -0.5), 0.0).astype(BF16))
            xq.append(jnp.where(upper, xs, 0.0).astype(BF16))
            vt.append(vt_ref[hd * V_ROWS:(hd + 1) * V_ROWS, pl.ds(r0, PAIR)])
            cn0.append(cn_ref[hd])
            w_a = jnp.where(first, w_r[hd:hd + 1, :], 0.0)
            up_a.append(_dot((vt[hd].astype(F32) * w_a).astype(BF16), xk[hd]))
            kq.append(_dot_nt(xk[hd], xq[hd]))
            cq_a.append(_dot_nt(cn0[hd].astype(BF16), xq[hd]))
            dmat = b_r[hd:hd + 1, :] - x_c[:, hd:hd + 1]
            dmat = jnp.where(key_t <= qry_t, jnp.where(key_t >= chunk_start, dmat, neg_inf), neg_inf)
            m_out.append(jnp.maximum(e_r[hd:hd + 1, :], jnp.max(dmat, axis=0, keepdims=True)))
            decay_t.append(jnp.exp(dmat - m_out[hd]))
        cn1, cq_b, up_b = [], [], []
        for hd in heads:
            cn1.append(dec_a[hd:hd + 1, :] * cn0[hd] + up_a[hd])
            w_b = jnp.where(first, 0.0, w_r[hd:hd + 1, :])
            up_b.append(_dot((vt[hd].astype(F32) * w_b).astype(BF16), xk[hd]))
            cq_b.append(_dot_nt(cn1[hd].astype(BF16), xq[hd]))
        sv = []
        for hd in heads:
            cn_ref[hd] = dec_b[hd:hd + 1, :] * cn1[hd] + up_b[hd]
            sv.append(_dot(vt[hd], (kq[hd] * decay_t[hd]).astype(BF16)))
        for hd in heads:
            rows = slice(hd * MLSTM_DV, (hd + 1) * MLSTM_DV)
            inter = jnp.exp(e_r[hd:hd + 1, :] - m_out[hd])
            tot = inter * jnp.where(first, cq_a[hd], cq_b[hd]) + sv[hd]
            den = tot[MLSTM_DV:MLSTM_DV + 1, :]
            hh = tot[0:MLSTM_DV, :] / jnp.maximum(jnp.abs(den), jnp.exp(-m_out[hd]))
            hn = hh * lax.rsqrt(jnp.mean(hh * hh, axis=0, keepdims=True) + EPS) * ng_ref[...]
            og = ot_ref[rows, pl.ds(r0, PAIR)]
            out_ref[rows, pl.ds(r0, PAIR)] = (hn * _sigmoid(og)).astype(BF16)
        return m_b

    lax.fori_loop(0, seq // PAIR, pair_step, jnp.zeros((SUBLANES, PAIR), F32), unroll=2)


def _mlstm(qk, vmt, omt, gri, grf, gci, gcf, cw, ng, ug, l2, batch, seq):
    const = lambda arr: pl.BlockSpec(arr.shape, lambda b: (0,) * arr.ndim)
    in_specs = [
        pl.BlockSpec((seq, GROUP_W), lambda b: (b, 0)),
        pl.BlockSpec((N_HEADS * V_ROWS, seq), lambda b: (0, b)),
        pl.BlockSpec((GROUP_W, seq), lambda b: (0, b)),
        pl.BlockSpec((seq // PAIR, SUBLANES, PAIR), lambda b: (b, 0, 0)),
        pl.BlockSpec((seq // PAIR, SUBLANES, PAIR), lambda b: (b, 0, 0)),
        pl.BlockSpec((seq, N_GATE_ROWS), lambda b: (b, 0)),
        pl.BlockSpec((seq, N_GATE_ROWS), lambda b: (b, 0)),
        const(cw), const(ng), const(ug), const(l2),
    ]
    return pl.pallas_call(
        _mlstm_kernel, grid=(batch,), in_specs=in_specs,
        out_specs=pl.BlockSpec((GROUP_W, seq), lambda b: (0, b)),
        out_shape=jax.ShapeDtypeStruct((GROUP_W, batch * seq), BF16),
        scratch_shapes=[
            pltpu.VMEM((seq + SUBLANES, GROUP_W), F32),
            pltpu.VMEM((N_HEADS, V_ROWS, LANES), F32),
        ],
        name="mlstm",
        compiler_params=pltpu.CompilerParams(dimension_semantics=("arbitrary",), vmem_limit_bytes=VMEM_LIMIT),
    )(qk, vmt, omt, gri, grf, gci, gcf, cw, ng, ug, l2)


def _attn_kernel(lam_init, tbl_ref, kmax_ref, qmin_ref, q_ref, k_ref, vt_ref, pcol_ref, prow_ref,
                 lq1_ref, lk1_ref, lq2_ref, lk2_ref, sg_ref, out_ref, bias_ref, qc_ref, s_ref, p_ref, acc_ref):
    hd = pl.program_id(0)
    qi = pl.program_id(1)
    b = pl.program_id(2)
    n_kv = (qi + 1) * (TQ // TK)

    @pl.when(b == 0)
    def _build_bias():
        rowi = lax.broadcasted_iota(jnp.int32, (BIAS_ROWS, BIAS_COLS), 0)
        coli = lax.broadcasted_iota(jnp.int32, (BIAS_ROWS, BIAS_COLS), 1)
        row = lambda c, v: tbl_ref[c * N_BUCKETS + v:c * N_BUCKETS + v + 1, :]
        last = N_BUCKETS - 1

        def build(t, _):
            r0 = pl.multiple_of(t * BIAS_ROWS, BIAS_ROWS)
            pk = pcol_ref[pl.ds(r0, BIAS_ROWS), :]
            kmax = kmax_ref[t]
            for j in range(TQ // BIAS_COLS):
                lanes = slice(j * BIAS_COLS, (j + 1) * BIAS_COLS)
                q0 = qi * TQ + j * BIAS_COLS
                causal = rowi + r0 <= coli + q0
                flat = jnp.logical_or(qmin_ref[qi * (TQ // BIAS_COLS) + j] - kmax >= _BUCKET_THR[-1],
                                      r0 > q0 + BIAS_COLS - 1)

                @pl.when(flat)
                def _flat():
                    bias_ref[0, pl.ds(r0, BIAS_ROWS), lanes] = jnp.where(causal, row(0, last), NEG_BIG)
                    bias_ref[1, pl.ds(r0, BIAS_ROWS), lanes] = jnp.where(causal, row(1, last), NEG_BIG)

                @pl.when(jnp.logical_not(flat))
                def _lookup():
                    dist = jnp.maximum(prow_ref[:, lanes] - pk, 0)
                    b0 = jnp.broadcast_to(row(0, 0), (BIAS_ROWS, BIAS_COLS))
                    b1 = jnp.broadcast_to(row(1, 0), (BIAS_ROWS, BIAS_COLS))
                    for v, thr in enumerate(_BUCKET_THR, start=1):
                        ge = dist >= thr
                        b0 = jnp.where(ge, row(0, v), b0)
                        b1 = jnp.where(ge, row(1, v), b1)
                    bias_ref[0, pl.ds(r0, BIAS_ROWS), lanes] = jnp.where(causal, b0, NEG_BIG)
                    bias_ref[1, pl.ds(r0, BIAS_ROWS), lanes] = jnp.where(causal, b1, NEG_BIG)
            return 0

        lax.fori_loop(0, n_kv * (TK // BIAS_ROWS), build, 0)

    seq = k_ref.shape[1]
    lane = lax.broadcasted_iota(jnp.int32, (TQ, LANES), 1)
    for e in range(ATTN_NB):
        q = q_ref[e]
        zero = jnp.zeros_like(q)
        qc_ref[e, 0] = jnp.where(lane < DIFF_D, q, zero)
        qc_ref[e, 1] = jnp.where(lane >= DIFF_D, q, zero)
    acc_ref[...] = jnp.zeros_like(acc_ref)

    chains = [(e, c) for e in range(ATTN_NB) for c in range(2)]

    def scores(t, slot):
        k0 = t * TK
        for e in range(ATTN_NB):
            kb = k_ref[e, pl.ds(k0, TK), :]
            for c in range(2):
                s_ref[slot, e, c] = _dot_nt(kb, qc_ref[e, c]) + bias_ref[c, pl.ds(k0, TK), :]

    def numerators(m_state, slot, lanes=slice(None)):
        m_next, alpha_next = [], []
        for i, (e, c) in enumerate(chains):
            m_prev = m_state[i][:, lanes]
            m_new = jnp.maximum(m_prev, jnp.max(s_ref[slot, e, c, :, lanes], axis=0, keepdims=True))
            alpha_next.append(jnp.exp2(m_prev - m_new))
            m_next.append(m_new)
        for i, (e, c) in enumerate(chains):
            p_ref[slot, e, c, :, lanes] = jnp.maximum(jnp.exp2(s_ref[slot, e, c, :, lanes] - m_next[i]), 0.0).astype(BF16)
        return tuple(m_next), tuple(alpha_next)

    def accumulate(t, alpha, slot, lanes=slice(None)):
        for e in range(ATTN_NB):
            v0 = e * seq + t * TK
            vt = vt_ref[:, pl.ds(v0, TK)]
            for c in range(2):
                acc_ref[e, c, :, lanes] = (alpha[2 * e + c] * acc_ref[e, c, :, lanes]
                                           + _dot(vt, p_ref[slot, e, c, :, lanes]))

    def run(n_tiles):
        scores(0, 0)
        scores(1, 1)
        m_state, alpha = numerators(tuple(jnp.full((1, TQ), M_INIT, F32) for _ in chains), 0)
        for t in range(2, n_tiles, 2):
            scores(t, 0)
            m_state, alpha1 = numerators(m_state, 1)
            accumulate(t - 2, alpha, 0)
            scores(t + 1, 1)
            m_state, alpha = numerators(m_state, 0)
            accumulate(t - 1, alpha1, 1)
        live = slice(TQ - TK, TQ)
        _, alpha1 = numerators(m_state, 1, live)
        accumulate(n_tiles - 2, alpha, 0)
        accumulate(n_tiles - 1, alpha1, 1, live)

        lam = (jnp.exp(jnp.sum(lq1_ref[...] * lk1_ref[...], axis=1, keepdims=True))
               - jnp.exp(jnp.sum(lq2_ref[...] * lk2_ref[...], axis=1, keepdims=True)) + lam_init)
        for e in range(ATTN_NB):
            a0 = acc_ref[e, 0]
            a1 = acc_ref[e, 1]
            o = a0[0:DIFF_DV] / a0[DIFF_DV:DIFF_DV + 1] - lam * (a1[0:DIFF_DV] / a1[DIFF_DV:DIFF_DV + 1])
            on = o * lax.rsqrt(jnp.mean(o * o, axis=0, keepdims=True) + EPS) * sg_ref[...]
            out_ref[e] = (on * (1.0 - lam_init)).astype(BF16)

    assert (TQ // TK) % 2 == 0
    for k in range(seq // TQ):
        pl.when(qi == k)(functools.partial(run, (k + 1) * (TQ // TK)))


def _attention(lam_init, tbl, qd, kd, vdt, pcol, prow, lq1, lk1, lq2, lk2, sg, batch, seq):
    q3, k3 = (t.reshape(batch, seq, GROUP_W) for t in (qd, kd))
    vec = lambda n: pl.BlockSpec((1, n), lambda h, i, b: (0, 0))
    kmax = jnp.max(prow.reshape(seq // BIAS_ROWS, BIAS_ROWS), axis=1)
    qmin = jnp.min(prow.reshape(seq // BIAS_COLS, BIAS_COLS), axis=1)
    in_specs = [
        pl.BlockSpec((None, 2 * N_BUCKETS, BIAS_COLS), lambda h, i, b: (h, 0, 0)),
        pl.BlockSpec(memory_space=pltpu.SMEM),
        pl.BlockSpec(memory_space=pltpu.SMEM),
        pl.BlockSpec((ATTN_NB, TQ, LANES), lambda h, i, b: (b, i, h)),
        pl.BlockSpec((ATTN_NB, seq, LANES), lambda h, i, b: (b, 0, h)),
        pl.BlockSpec((V_ROWS, ATTN_NB * seq), lambda h, i, b: (h, b)),
        pl.BlockSpec((seq, BIAS_COLS), lambda h, i, b: (0, 0)),
        pl.BlockSpec((1, TQ), lambda h, i, b: (0, i)),
        vec(DIFF_D), vec(DIFF_D), vec(DIFF_D), vec(DIFF_D),
        pl.BlockSpec((DIFF_DV, TQ), lambda h, i, b: (0, 0)),
    ]
    out = pl.pallas_call(
        functools.partial(_attn_kernel, lam_init),
        grid=(N_HEADS, seq // TQ, batch // ATTN_NB), in_specs=in_specs,
        out_specs=pl.BlockSpec((ATTN_NB, DIFF_DV, TQ), lambda h, i, b: (b, h, i)),
        out_shape=jax.ShapeDtypeStruct((batch, GROUP_W, seq), BF16),
        scratch_shapes=[
            pltpu.VMEM((2, seq, TQ), F32),
            pltpu.VMEM((ATTN_NB, 2, TQ, LANES), BF16),
            pltpu.VMEM((2, ATTN_NB, 2, TK, TQ), F32),
            pltpu.VMEM((2, ATTN_NB, 2, TK, TQ), BF16),
            pltpu.VMEM((ATTN_NB, 2, V_ROWS, TQ), F32),
        ],
        name="diff_attn",
        compiler_params=pltpu.CompilerParams(
            dimension_semantics=("arbitrary", "arbitrary", "arbitrary"), vmem_limit_bytes=VMEM_LIMIT),
    )(tbl, kmax, qmin, q3, k3, vdt, pcol, prow, lq1, lk1, lq2, lk2, jnp.broadcast_to(sg.reshape(DIFF_DV, 1), (DIFF_DV, TQ)))
    return out


def _gelu(x):
    return 0.5 * x * (1.0 + lax.erf(x * (2.0 ** -0.5)))


def _rms(x, g):
    return x * lax.rsqrt(jnp.mean(x * x, axis=-1, keepdims=True) + EPS) * g


def _mixer_kernel(h_ref, hmt_ref, hdt_ref, p_ref, wo_ref, gf_ref, wu_ref, cw_ref, cb_ref, wd_ref, gp_ref,
                  wpg_ref, wpp_ref, out_ref, sg_ref, sv_ref, cg_ref, cv_ref, acc_ref, u_ref):
    tm = h_ref.shape[0]
    halo = SUBLANES

    @pl.when(pl.program_id(1) == 0)
    def _reset_conv_history():
        cg_ref[...] = jnp.zeros_like(cg_ref)
        cv_ref[...] = jnp.zeros_like(cv_ref)

    h1 = h_ref[...] + _dot_tn(jnp.concatenate([hmt_ref[...], hdt_ref[...]], axis=0), wo_ref[...])
    u_ref[...] = _rms(h1, gf_ref[...]).astype(BF16)
    acc_ref[...] = jnp.zeros_like(acc_ref)

    def up_stage(j, slot):
        c0 = pl.multiple_of(j * FF_CHUNK, FF_CHUNK)
        sg_ref[slot, halo:, :] = _dot(u_ref[...], wu_ref[:, pl.ds(c0, FF_CHUNK)])
        sv_ref[slot, halo:, :] = _dot(u_ref[...], wu_ref[:, pl.ds(D_FF + c0, FF_CHUNK)])

    def conv_branch(j, slot, c0, stage_ref, hist_ref):
        stage_ref[slot, 0:halo, :] = hist_ref[j]
        hist_ref[j] = stage_ref[slot, tm:tm + halo, :]
        cw = cw_ref[:, pl.ds(c0, FF_CHUNK)]
        return (cw[0:1, :] * stage_ref[slot, halo - 2:halo - 2 + tm, :]
                + cw[1:2, :] * stage_ref[slot, halo - 1:halo - 1 + tm, :]
                + cw[2:3, :] * stage_ref[slot, halo:, :] + cb_ref[:, pl.ds(c0, FF_CHUNK)])

    def act_stage(j, slot):
        c0 = pl.multiple_of(j * FF_CHUNK, FF_CHUNK)
        gate = conv_branch(j, slot, c0, sg_ref, cg_ref)
        val = conv_branch(j, slot, D_FF + c0, sv_ref, cv_ref)
        act = (_gelu(gate) * val).astype(BF16)
        acc_ref[...] += _dot(act, wd_ref[j])

    def chunk_pair(i, _):
        j = 2 * i
        up_stage(j + 1, 1)
        act_stage(j, 0)
        up_stage(j + 2, 0)
        act_stage(j + 1, 1)
        return 0

    assert N_FF_CHUNKS % 2 == 1
    up_stage(0, 0)
    for i in range(N_FF_CHUNKS // 2):
        chunk_pair(i, 0)
    act_stage(N_FF_CHUNKS - 1, 0)

    h2 = h1 + acc_ref[...]
    u3 = _rms(h2, gp_ref[...]).astype(BF16)
    ple_gate = _sigmoid(_dot(u3, wpg_ref[...]))
    out_ref[...] = h2 + ple_gate * _dot(p_ref[...].astype(BF16), wpp_ref[...])


def _mixer(layer, h, hmt, hdt, p, wo, gf, wu, cw, cb, wd, gp, wpg, wpp, batch, seq):
    tiles = seq // TM_FFN
    tile = lambda width: pl.BlockSpec((TM_FFN, width), lambda b, t: (b * tiles + t, 0))

    def of_layer(arr):
        nd = arr.ndim
        return pl.BlockSpec((None,) + arr.shape[1:], lambda b, t: (layer,) + (0,) * (nd - 1),
                            pipeline_mode=pl.Buffered(1))

    weights = (wo, gf, wu, cw, cb, wd, gp, wpg, wpp)
    in_specs = [
        tile(D_MODEL),
        pl.BlockSpec((GROUP_W, TM_FFN), lambda b, t: (0, b * tiles + t)),
        pl.BlockSpec((None, GROUP_W, TM_FFN), lambda b, t: (b, 0, t)),
        pl.BlockSpec((None, TM_FFN, PLE_DIM), lambda b, t: (layer, b * tiles + t, 0)),
    ] + [of_layer(w) for w in weights]
    return pl.pallas_call(
        _mixer_kernel, grid=(batch, tiles), in_specs=in_specs, out_specs=tile(D_MODEL),
        out_shape=jax.ShapeDtypeStruct(h.shape, F32),
        scratch_shapes=[
            pltpu.VMEM((2, TM_FFN + SUBLANES, FF_CHUNK), F32),
            pltpu.VMEM((2, TM_FFN + SUBLANES, FF_CHUNK), F32),
            pltpu.VMEM((N_FF_CHUNKS, SUBLANES, FF_CHUNK), F32),
            pltpu.VMEM((N_FF_CHUNKS, SUBLANES, FF_CHUNK), F32),
            pltpu.VMEM((TM_FFN, D_MODEL), F32),
            pltpu.VMEM((TM_FFN, D_MODEL), BF16),
        ],
        name="mixer",
        compiler_params=pltpu.CompilerParams(
            dimension_semantics=("arbitrary", "arbitrary"), vmem_limit_bytes=VMEM_LIMIT),
    )(h, hmt, hdt, p, *weights)


def _head_interleave(qcols, kcols):
    lead = qcols.shape[:-1]
    qh = qcols.reshape(lead + (N_HEADS, MLSTM_DK))
    kh = kcols.reshape(lead + (N_HEADS, MLSTM_DK))
    return jnp.concatenate([qh, kh], axis=-1).reshape(lead + (GROUP_W,))


def kernel(x, p, positions, rel_bias, ln_mix_g, w_in, mlstm_conv_w, b_igate, b_fgate, mlstm_norm_g, q_norm_g, k_norm_g, lam_q1, lam_k1, lam_q2, lam_k2, diff_subln_g, w_out, ln_ffn_g, w_up, ffn_conv_w, ffn_conv_b, w_down, ln_ple_g, w_ple_gate, w_ple_proj):
    batch, seq, _ = x.shape
    depth = w_in.shape[0]
    n_tok = batch * seq
    qk_cols = N_HEADS * MLSTM_DK
    col_sizes = [qk_cols, qk_cols, GROUP_W, GROUP_W, N_HEADS, N_HEADS, GROUP_W, GROUP_W, GROUP_W]
    offs = np.concatenate([[0], np.cumsum(col_sizes)])
    sl = lambda a, j: a[..., int(offs[j]):int(offs[j + 1])]

    tbl = jnp.broadcast_to(
        (jnp.transpose(rel_bias.astype(F32), (1, 2, 0)) * LOG2E).reshape(N_HEADS, 2 * N_BUCKETS, 1),
        (N_HEADS, 2 * N_BUCKETS, BIAS_COLS))
    pcol = jnp.broadcast_to(positions.astype(jnp.int32).reshape(seq, 1), (seq, BIAS_COLS))
    prow = positions.astype(jnp.int32).reshape(1, seq)
    gsum = jnp.asarray(np.kron(np.eye(MXU_TILE // DIFF_D), np.ones((DIFF_D, DIFF_D))), BF16)
    t_idx = np.arange(PAIR)
    same_chunk = (t_idx[:, None] // CHUNK) == (t_idx[None, :] // CHUNK)
    prefix = same_chunk & (t_idx[:, None] <= t_idx[None, :])
    total_a = np.broadcast_to(t_idx[:, None] < CHUNK, (PAIR, PAIR))
    ug = jnp.asarray(np.concatenate([prefix, total_a, ~total_a], axis=1), BF16)
    l2 = jnp.asarray(prefix.T, BF16)

    p_tok = p.reshape(depth, n_tok, PLE_DIM)
    row = lambda a: a.astype(F32).reshape(depth, 1, a.shape[-1])
    mixer_weights = (
        w_out.astype(BF16), row(ln_ffn_g), w_up.astype(BF16), ffn_conv_w.astype(F32), row(ffn_conv_b),
        w_down.astype(BF16).reshape(depth, N_FF_CHUNKS, FF_CHUNK, D_MODEL), row(ln_ple_g),
        w_ple_gate.astype(BF16), w_ple_proj.astype(BF16))

    h = x.reshape(n_tok, D_MODEL)
    for i in range(depth):
        wi = w_in[i]
        wm = jnp.concatenate([_head_interleave(sl(wi, 0), sl(wi, 1)), sl(wi, 6), sl(wi, 7)], axis=-1).astype(BF16)
        wt = jnp.concatenate([sl(wi, 2), sl(wi, 3), sl(wi, 8)], axis=-1).T.astype(BF16)
        pad_rows = lambda a: jnp.zeros((N_GATE_ROWS,) + a.shape[1:], F32).at[:N_HEADS].set(a.astype(F32))
        wg = jnp.concatenate([pad_rows(sl(wi, 4).T), pad_rows(sl(wi, 5).T)], axis=0).astype(BF16)
        gate_bias = jnp.concatenate([pad_rows(b_igate[i]), pad_rows(b_fgate[i])])
        qg = jnp.tile(q_norm_g[i].astype(F32), GROUP_W // DIFF_D).reshape(1, GROUP_W) * (DIFF_D ** -0.5 * LOG2E)
        kg = jnp.tile(k_norm_g[i].astype(F32), GROUP_W // DIFF_D).reshape(1, GROUP_W)
        qk, vmt, omt, gri, grf, gci, gcf, qd, kd, vdt = _inproj(
            h, ln_mix_g[i].reshape(1, D_MODEL), wm, wt, wg, gate_bias.reshape(1, -1), gate_bias.reshape(-1, 1),
            gsum, qg, kg)

        cw = _head_interleave(mlstm_conv_w[i][:, :qk_cols], mlstm_conv_w[i][:, qk_cols:]).astype(F32)
        ng = jnp.broadcast_to(mlstm_norm_g[i].astype(F32).reshape(MLSTM_DV, 1), (MLSTM_DV, PAIR))
        hmt = _mlstm(qk, vmt, omt, gri, grf, gci, gcf, cw, ng, ug, l2, batch, seq)

        lam_init = 0.8 - 0.6 * math.exp(-0.3 * i)
        row64 = lambda a: a[i].reshape(1, DIFF_D).astype(F32)
        hdt = _attention(lam_init, tbl, qd, kd, vdt, pcol, prow, row64(lam_q1), row64(lam_k1), row64(lam_q2),
                        row64(lam_k2), diff_subln_g[i].reshape(1, DIFF_DV).astype(F32), batch, seq)

        h = _mixer(i, h, hmt, hdt, p_tok, *mixer_weights, batch, seq)
    return h.reshape(batch, seq, D_MODEL)
```
